```python
import math
import jax, jax.numpy as jnp
from jax import lax
import numpy as np

D_MODEL = 1024
BATCH = 2
SEQ = 8192
DEPTH = 1
DEC_BATCH = 32
DEC_SEQ = 8
PAST_LEN = 8192
PAGE_SIZE = 128

HEAD_DIM = 64
ATTN_GROUPS = ((128, 1), (512, 4), (2048, 16))
HEADS_PER_GROUP = 4
N_ATTN_HEADS = HEADS_PER_GROUP * len(ATTN_GROUPS)
ATTN_WIDTH = N_ATTN_HEADS * HEAD_DIM
ATTN_OUT_WIDTH = HEADS_PER_GROUP * HEAD_DIM
ROPE_THETA = 10000.0
SSD_D_INNER = D_MODEL
SSD_HEAD_DIM = 64
SSD_HEADS = SSD_D_INNER // SSD_HEAD_DIM
SSD_GROUPS = 2
SSD_HEADS_PER_GROUP = SSD_HEADS // SSD_GROUPS
SSD_STATE = 128
SSD_CONV = 4
SSD_CHUNK = 128
SSD_CONV_DIM = SSD_D_INNER + 2 * SSD_GROUPS * SSD_STATE
N_EXPERT_GROUPS = 4
EXPERTS_PER_GROUP = 8
N_EXPERTS = N_EXPERT_GROUPS * EXPERTS_PER_GROUP
TOP_K = 2
D_FF_EXPERT = 512
MOE_BLOCK = 128
NORM_EPS = 1e-6
IN_SPLITS = (ATTN_WIDTH, ATTN_WIDTH, ATTN_WIDTH, SSD_D_INNER, SSD_CONV_DIM, SSD_HEADS, D_MODEL, D_MODEL)
D_IN_PROJ = 6928

kernel_name = "hybrid_dilated_attn_ssd_hmoe_step"


def _rms_norm(x, w):
    xf = x.astype(jnp.float32)
    y = xf * lax.rsqrt(jnp.mean(xf * xf, axis=-1, keepdims=True) + NORM_EPS)
    return (y * w.astype(jnp.float32)).astype(x.dtype)


def _rope(x, pos):
    half = HEAD_DIM // 2
    inv_freq = ROPE_THETA ** (-jnp.arange(half, dtype=jnp.float32) * (2.0 / HEAD_DIM))
    ang = pos.astype(jnp.float32)[:, None] * inv_freq[None, :]
    ang = jnp.concatenate([ang, ang], axis=-1)[None, :, None, :]
    xf = x.astype(jnp.float32)
    rot = jnp.concatenate([-xf[..., half:], xf[..., :half]], axis=-1)
    return (xf * jnp.cos(ang) + rot * jnp.sin(ang)).astype(x.dtype)


def _dilated_attn_prompt(q, k, v, dil, span):
    b, S, H, Dh = q.shape
    n = S // dil
    nb = -(-n // span)
    npad = nb * span

    def to_blocks(t):
        t = t.reshape(b, n, dil, H, Dh).transpose(0, 2, 1, 3, 4)
        t = jnp.pad(t, ((0, 0), (0, 0), (0, npad - n), (0, 0), (0, 0)))
        return t.reshape(b, dil, nb, span, H, Dh).astype(jnp.float32)

    def with_prev(t):
        prev = jnp.pad(t[:, :, :-1], ((0, 0), (0, 0), (1, 0), (0, 0), (0, 0), (0, 0)))
        return jnp.concatenate([prev, t], axis=3)

    qb = to_blocks(q)
    kk = with_prev(to_blocks(k))
    vv = with_prev(to_blocks(v))
    s = jnp.einsum("brcqhd,brckhd->brchqk", qb, kk) * (Dh ** -0.5)
    i = jnp.arange(span)[:, None]
    j = jnp.arange(2 * span)[None, :]
    blk = jnp.arange(nb)[:, None, None]
    valid = (j >= i) & (j <= i + span) & ((blk > 0) | (j >= span))
    s = jnp.where(valid[None, None, :, None], s, -jnp.inf)
    m = jnp.max(s, axis=-1, keepdims=True)
    p = jnp.exp(s - m)
    den = jnp.sum(p, axis=-1)
    o = jnp.einsum("brchqk,brckhd->brcqhd", p, vv) / jnp.swapaxes(den, -1, -2)[..., None]
    lse = jnp.swapaxes(m[..., 0] + jnp.log(den), -1, -2)
    o = o.reshape(b, dil, npad, H, Dh)[:, :, :n].transpose(0, 2, 1, 3, 4).reshape(b, S, H, Dh)
    lse = lse.reshape(b, dil, npad, H)[:, :, :n].transpose(0, 2, 1, 3).reshape(b, S, H)
    return o, lse


def _dilated_attn_step(q, k_new, v_new, buf, dil, span, window):
    b, T, H, Dh = q.shape
    Lb = buf.shape[1]
    kk = jnp.concatenate([buf[:, :, 0].astype(k_new.dtype), k_new], axis=1)
    vv = jnp.concatenate([buf[:, :, 1].astype(v_new.dtype), v_new], axis=1)
    idx = Lb + jnp.arange(T)[:, None] - dil * jnp.arange(span + 1)[None, :]
    valid = idx >= 0
    idx = jnp.maximum(idx, 0)
    kg = kk[:, idx].astype(jnp.float32)
    vg = vv[:, idx].astype(jnp.float32)
    s = jnp.einsum("bthd,btkhd->bhtk", q.astype(jnp.float32), kg) * (Dh ** -0.5)
    s = jnp.where(valid[None, None], s, -jnp.inf)
    m = jnp.max(s, axis=-1, keepdims=True)
    p = jnp.exp(s - m)
    den = jnp.sum(p, axis=-1)
    o = jnp.einsum("bhtk,btkhd->bthd", p, vg) / jnp.swapaxes(den, 1, 2)[..., None]
    lse = jnp.swapaxes(m[..., 0] + jnp.log(den), 1, 2)
    new_len = min(window, Lb + T)
    new_buf = jnp.stack([kk, vv], axis=2)[:, Lb + T - new_len:]
    return o, lse, new_buf


def _attn_branch(q, k, v, kv_bufs):
    b, L = q.shape[:2]
    outs, lses, new_kv = [], [], []
    for gi, (window, dil) in enumerate(ATTN_GROUPS):
        hs = slice(gi * HEADS_PER_GROUP, (gi + 1) * HEADS_PER_GROUP)
        span = window // dil
        qg, kg, vg = q[:, :, hs], k[:, :, hs], v[:, :, hs]
        if kv_bufs is None:
            o, lse = _dilated_attn_prompt(qg, kg, vg, dil, span)
            keep = min(window, L)
            nbuf = jnp.stack([kg, vg], axis=2)[:, L - keep:]
        else:
            o, lse, nbuf = _dilated_attn_step(qg, kg, vg, kv_bufs[gi], dil, span, window)
        outs.append(o)
        lses.append(lse)
        new_kv.append(nbuf)
    wts = jax.nn.softmax(jnp.stack(lses), axis=0)
    o = jnp.sum(wts[..., None] * jnp.stack(outs), axis=0)
    return o.reshape(b, L, ATTN_OUT_WIDTH).astype(q.dtype), new_kv


def _ssd_scan(x, dt, A, Bm, Cm, h0, chunk):
    b, L, G, E, P = x.shape
    N = Bm.shape[-1]
    nc = L // chunk
    x = x.reshape(b, nc, chunk, G, E, P)
    dt = dt.reshape(b, nc, chunk, G, E)
    Bm = Bm.reshape(b, nc, chunk, G, N)
    Cm = Cm.reshape(b, nc, chunk, G, N)
    a_cs = jnp.cumsum(dt * A, axis=2)
    xdt = x * dt[..., None]
    seg = a_cs[:, :, :, None] - a_cs[:, :, None, :]
    causal = (jnp.arange(chunk)[:, None] >= jnp.arange(chunk)[None, :])[None, None, :, :, None, None]
    decay_ls = jnp.exp(jnp.where(causal, seg, -jnp.inf))
    cb = jnp.einsum("bclgn,bcsgn->bclsg", Cm, Bm)
    y_diag = jnp.einsum("bclsg,bclsge,bcsgep->bclgep", cb, decay_ls, xdt)
    decay_end = jnp.exp(a_cs[:, :, -1:] - a_cs)
    states = jnp.einsum("bclgn,bclge,bclgep->bcgepn", Bm, decay_end, xdt)
    chunk_decay = jnp.exp(a_cs[:, :, -1])

    def step(h, inp):
        dec, st = inp
        return h * dec[..., None, None] + st, h

    h_final, h_prev = lax.scan(step, h0, (jnp.moveaxis(chunk_decay, 1, 0), jnp.moveaxis(states, 1, 0)))
    h_prev = jnp.moveaxis(h_prev, 0, 1)
    y_off = jnp.einsum("bclgn,bcgepn,bclge->bclgep", Cm, h_prev, jnp.exp(a_cs))
    return (y_diag + y_off).reshape(b, L, G, E, P), h_final


def _ssd_branch(z, xbc, dt_raw, conv_state, ssm_state, conv_w, conv_b, dt_bias, A_log, D_skip, ssd_norm_w):
    b, L, _ = xbc.shape
    if conv_state is None:
        conv_state = jnp.zeros((b, SSD_CONV - 1, SSD_CONV_DIM), xbc.dtype)
    xpad = jnp.concatenate([conv_state.astype(xbc.dtype), xbc], axis=1)
    xc = conv_b + sum(xpad[:, t:t + L] * conv_w[t] for t in range(SSD_CONV))
    xc = jax.nn.silu(xc)
    new_conv = xpad[:, L:]
    xs = xc[..., :SSD_D_INNER].reshape(b, L, SSD_GROUPS, SSD_HEADS_PER_GROUP, SSD_HEAD_DIM).astype(jnp.float32)
    Bm = xc[..., SSD_D_INNER:SSD_D_INNER + SSD_GROUPS * SSD_STATE].reshape(b, L, SSD_GROUPS, SSD_STATE).astype(jnp.float32)
    Cm = xc[..., SSD_D_INNER + SSD_GROUPS * SSD_STATE:].reshape(b, L, SSD_GROUPS, SSD_STATE).astype(jnp.float32)
    dt = jax.nn.softplus(dt_raw.astype(jnp.float32) + dt_bias.astype(jnp.float32))
    dt = dt.reshape(b, L, SSD_GROUPS, SSD_HEADS_PER_GROUP)
    A = -jnp.exp(A_log.astype(jnp.float32)).reshape(SSD_GROUPS, SSD_HEADS_PER_GROUP)
    if ssm_state is None:
        h0 = jnp.zeros((b, SSD_GROUPS, SSD_HEADS_PER_GROUP, SSD_HEAD_DIM, SSD_STATE), jnp.float32)
    else:
        h0 = ssm_state.astype(jnp.float32).reshape(b, SSD_GROUPS, SSD_HEADS_PER_GROUP, SSD_HEAD_DIM, SSD_STATE)
    chunk = SSD_CHUNK if L % SSD_CHUNK == 0 else L
    y, h_final = _ssd_scan(xs, dt, A, Bm, Cm, h0, chunk)
    y = y + xs * D_skip.astype(jnp.float32).reshape(SSD_GROUPS, SSD_HEADS_PER_GROUP)[..., None]
    g = (y.reshape(b, L, SSD_D_INNER) * jax.nn.silu(z.astype(jnp.float32))).reshape(b, L, SSD_GROUPS, -1)
    g = g * lax.rsqrt(jnp.mean(g * g, axis=-1, keepdims=True) + NORM_EPS)
    y_out = (g.reshape(b, L, SSD_D_INNER) * ssd_norm_w.astype(jnp.float32)).astype(z.dtype)
    return y_out, new_conv, h_final.reshape(b, SSD_HEADS, SSD_HEAD_DIM, SSD_STATE)


def _hier_moe(x, w_rc, b_rc, w_rf, b_rf, w_eg, w_eu, w_ed):
    n, d = x.shape
    xf = x.astype(jnp.float32)
    lc = xf @ w_rc.astype(jnp.float32) + b_rc.astype(jnp.float32)
    pc = jax.nn.softmax(lc, axis=-1)
    grp = jnp.argmax(lc, axis=-1)
    p_grp = jnp.take_along_axis(pc, grp[:, None], axis=-1)[:, 0]
    lf = (xf @ w_rf.astype(jnp.float32) + b_rf.astype(jnp.float32)).reshape(n, N_EXPERT_GROUPS, EXPERTS_PER_GROUP)
    lf = jnp.take_along_axis(lf, grp[:, None, None], axis=1)[:, 0]
    top_v, top_i = lax.top_k(lf, TOP_K)
    gate = p_grp[:, None] * jax.nn.softmax(top_v, axis=-1)
    expert = grp[:, None] * EXPERTS_PER_GROUP + top_i
    n_assign = n * TOP_K
    e_flat = expert.reshape(-1)
    w_flat = gate.reshape(-1)
    t_flat = jnp.repeat(jnp.arange(n), TOP_K)
    order = jnp.argsort(e_flat)
    e_s, t_s, w_s = e_flat[order], t_flat[order], w_flat[order]
    counts = jnp.zeros((N_EXPERTS,), jnp.int32).at[e_flat].add(1)
    starts = jnp.cumsum(counts) - counts
    pcounts = ((counts + MOE_BLOCK - 1) // MOE_BLOCK) * MOE_BLOCK
    pends = jnp.cumsum(pcounts)
    pstarts = pends - pcounts
    dest = pstarts[e_s] + (jnp.arange(n_assign) - starts[e_s])
    n_blocks = -(-n_assign // MOE_BLOCK) + N_EXPERTS
    xbuf = jnp.zeros((n_blocks * MOE_BLOCK, d), x.dtype).at[dest].set(x[t_s])
    blk_e = jnp.minimum(jnp.searchsorted(pends, jnp.arange(n_blocks) * MOE_BLOCK, side="right"), N_EXPERTS - 1)

    def run_block(args):
        xb, e = args
        hb = jax.nn.silu(xb @ w_eg[e]) * (xb @ w_eu[e])
        return hb @ w_ed[e]

    out = lax.map(run_block, (xbuf.reshape(n_blocks, MOE_BLOCK, d), blk_e)).reshape(n_blocks * MOE_BLOCK, d)
    y = jnp.zeros((n, d), jnp.float32).at[t_s].add(out[dest].astype(jnp.float32) * w_s[:, None])
    return y.astype(x.dtype)


def _decoder_layer(x, pos, kv_bufs, conv_state, ssm_state, norm1_w, w_in, w_attn_br, w_ssd_br, b_gate, w_out,
                   conv_w, conv_b, dt_bias, A_log, D_skip, ssd_norm_w, norm2_w, w_rc, b_rc, w_rf, b_rf,
                   w_eg, w_eu, w_ed):
    b, L, _ = x.shape
    xn = _rms_norm(x, norm1_w)
    u = xn @ w_in
    offs = [int(o) for o in np.cumsum(IN_SPLITS)[:-1]]
    q, k, v, z, xbc, dt_raw, g_a, g_s = jnp.split(u, offs, axis=-1)
    q = _rope(q.reshape(b, L, N_ATTN_HEADS, HEAD_DIM), pos)
    k = _rope(k.reshape(b, L, N_ATTN_HEADS, HEAD_DIM), pos)
    v = v.reshape(b, L, N_ATTN_HEADS, HEAD_DIM)
    y_attn, new_kv = _attn_branch(q, k, v, kv_bufs)
    y_ssd, new_conv, new_ssm = _ssd_branch(z, xbc, dt_raw, conv_state, ssm_state, conv_w, conv_b, dt_bias,
                                           A_log, D_skip, ssd_norm_w)
    merged = (jax.nn.sigmoid(g_a + b_gate[0]) * (y_attn @ w_attn_br)
              + jax.nn.sigmoid(g_s + b_gate[1]) * (y_ssd @ w_ssd_br))
    h = x + merged @ w_out
    hn = _rms_norm(h, norm2_w).reshape(b * L, D_MODEL)
    h = h + _hier_moe(hn, w_rc, b_rc, w_rf, b_rf, w_eg, w_eu, w_ed).reshape(b, L, D_MODEL)
    return h, new_kv, new_conv, new_ssm


def setup_inputs(seed: int = 0) -> dict:
    key = jax.random.key(seed)
    ks = jax.random.split(key, 32)

    def nrm(k, shape, scale):
        return jax.random.normal(k, shape, jnp.float32) * scale

    kv_shape = lambda w: (DEPTH, DEC_BATCH, min(w, PAST_LEN), 2, HEADS_PER_GROUP, HEAD_DIM)
    dt0 = jnp.exp(jax.random.uniform(ks[12], (DEPTH, SSD_HEADS), jnp.float32, math.log(1e-3), math.log(1e-1)))
    return {
        "x_prompt": nrm(ks[0], (BATCH, SEQ, D_MODEL), 1.0),
        "x_sample": nrm(ks[1], (DEC_BATCH, DEC_SEQ, D_MODEL), 1.0),
        "cache_kv_w128": nrm(ks[2], kv_shape(ATTN_GROUPS[0][0]), 1.0),
        "cache_kv_w512": nrm(ks[3], kv_shape(ATTN_GROUPS[1][0]), 1.0),
        "cache_kv_w2048": nrm(ks[4], kv_shape(ATTN_GROUPS[2][0]), 1.0),
        "state_conv": nrm(ks[5], (DEPTH, DEC_BATCH, SSD_CONV - 1, SSD_CONV_DIM), 1.0),
        "state_ssm": nrm(ks[6], (DEPTH, DEC_BATCH, SSD_HEADS, SSD_HEAD_DIM, SSD_STATE), 0.5),
        "norm1_w": 1.0 + nrm(ks[7], (DEPTH, D_MODEL), 0.02),
        "w_in": nrm(ks[8], (DEPTH, D_MODEL, D_IN_PROJ), D_MODEL ** -0.5),
        "w_attn_br": nrm(ks[9], (DEPTH, ATTN_OUT_WIDTH, D_MODEL), ATTN_OUT_WIDTH ** -0.5),
        "w_ssd_br": nrm(ks[10], (DEPTH, SSD_D_INNER, D_MODEL), SSD_D_INNER ** -0.5),
        "b_gate": nrm(ks[11], (DEPTH, 2, D_MODEL), 0.02),
        "w_out": nrm(ks[13], (DEPTH, D_MODEL, D_MODEL), D_MODEL ** -0.5),
        "conv_w": nrm(ks[14], (DEPTH, SSD_CONV, SSD_CONV_DIM), SSD_CONV ** -0.5),
        "conv_b": nrm(ks[15], (DEPTH, SSD_CONV_DIM), 0.02),
        "dt_bias": dt0 + jnp.log(-jnp.expm1(-dt0)),
        "A_log": jnp.log(jax.random.uniform(ks[16], (DEPTH, SSD_HEADS), jnp.float32, 1.0, 16.0)),
        "D_skip": 1.0 + nrm(ks[17], (DEPTH, SSD_HEADS), 0.1),
        "ssd_norm_w": 1.0 + nrm(ks[18], (DEPTH, SSD_D_INNER), 0.02),
        "norm2_w": 1.0 + nrm(ks[19], (DEPTH, D_MODEL), 0.02),
        "w_router_coarse": nrm(ks[20], (DEPTH, D_MODEL, N_EXPERT_GROUPS), D_MODEL ** -0.5),
        "b_router_coarse": nrm(ks[21], (DEPTH, N_EXPERT_GROUPS), 0.01),
        "w_router_fine": nrm(ks[22], (DEPTH, D_MODEL, N_EXPERTS), D_MODEL ** -0.5),
        "b_router_fine": nrm(ks[23], (DEPTH, N_EXPERTS), 0.01),
        "w_expert_gate": nrm(ks[24], (DEPTH, N_EXPERTS, D_MODEL, D_FF_EXPERT), D_MODEL ** -0.5),
        "w_expert_up": nrm(ks[25], (DEPTH, N_EXPERTS, D_MODEL, D_FF_EXPERT), D_MODEL ** -0.5),
        "w_expert_down": nrm(ks[26], (DEPTH, N_EXPERTS, D_FF_EXPERT, D_MODEL), D_FF_EXPERT ** -0.5),
        "final_norm_w": 1.0 + nrm(ks[27], (D_MODEL,), 0.02),
    }


def reference(x_prompt, x_sample, cache_kv_w128, cache_kv_w512, cache_kv_w2048, state_conv, state_ssm,
              norm1_w, w_in, w_attn_br, w_ssd_br, b_gate, w_out, conv_w, conv_b, dt_bias, A_log, D_skip,
              ssd_norm_w, norm2_w, w_router_coarse, b_router_coarse, w_router_fine, b_router_fine,
              w_expert_gate, w_expert_up, w_expert_down, final_norm_w):
    pos_p = jnp.arange(x_prompt.shape[1], dtype=jnp.int32)
    pos_s = PAST_LEN + jnp.arange(x_sample.shape[1], dtype=jnp.int32)
    h_p, h_s = x_prompt, x_sample
    p_kv = ([], [], [])
    s_kv = ([], [], [])
    p_conv, p_ssm, s_conv, s_ssm = [], [], [], []
    for l in range(DEPTH):
        wl = (norm1_w[l], w_in[l], w_attn_br[l], w_ssd_br[l], b_gate[l], w_out[l], conv_w[l], conv_b[l],
              dt_bias[l], A_log[l], D_skip[l], ssd_norm_w[l], norm2_w[l], w_router_coarse[l], b_router_coarse[l],
              w_router_fine[l], b_router_fine[l], w_expert_gate[l], w_expert_up[l], w_expert_down[l])
        h_p, kv_p, c_p, st_p = _decoder_layer(h_p, pos_p, None, None, None, *wl)
        bufs = (cache_kv_w128[l], cache_kv_w512[l], cache_kv_w2048[l])
        h_s, kv_s, c_s, st_s = _decoder_layer(h_s, pos_s, bufs, state_conv[l], state_ssm[l], *wl)
        for gi in range(len(ATTN_GROUPS)):
            p_kv[gi].append(kv_p[gi])
            s_kv[gi].append(kv_s[gi])
        p_conv.append(c_p)
        p_ssm.append(st_p)
        s_conv.append(c_s)
        s_ssm.append(st_s)
    y_prompt = _rms_norm(h_p, final_norm_w)
    y_sample = _rms_norm(h_s, final_norm_w)
    return (y_prompt, y_sample,
            jnp.stack(p_kv[0]), jnp.stack(p_kv[1]), jnp.stack(p_kv[2]), jnp.stack(p_conv), jnp.stack(p_ssm),
            jnp.stack(s_kv[0]), jnp.stack(s_kv[1]), jnp.stack(s_kv[2]), jnp.stack(s_conv), jnp.stack(s_ssm))
```

```python
import functools

import jax
import jax.numpy as jnp
from jax import lax
from jax.experimental import pallas as pl
from jax.experimental.pallas import tpu as pltpu

F32 = jnp.float32
BF16 = jnp.bfloat16
I32 = jnp.int32

D_MODEL = 1024
HEAD_DIM = 64
ATTN_GROUPS = ((128, 1), (512, 4), (2048, 16))
SPAN = 128
HEADS_PER_GROUP = 4
GROUP_W = HEADS_PER_GROUP * HEAD_DIM
ATTN_W = GROUP_W * len(ATTN_GROUPS)
ROPE_THETA = 10000.0
PAST_LEN = 8192
SSD_INNER = 1024
SSD_HEADS = 16
SSD_STATE = 128
SSD_CONV_DIM = 1536
SSD_CHUNK = 128
N_GROUPS_E = 4
EXPERTS_PER_GROUP = 8
N_EXPERTS = 32
D_FF = 512
EPS = 1e-6
LANES = 128
SUBLANES = 8
NEG = -1e30

_OFF_Q, _OFF_K, _OFF_V, _OFF_Z, _OFF_XBC, _OFF_GA, _OFF_GS, _OFF_DT = 0, 768, 1536, 2304, 3328, 4864, 5888, 6912
_W_IN_COLS = 7040

ROW_TILE = 256
EXPERT_TILE = 256
VMEM_LIMIT = 56 * 1024 * 1024


def _cparams(sem):
    return pltpu.CompilerParams(dimension_semantics=sem, vmem_limit_bytes=VMEM_LIMIT)


def _sigmoid(x):
    return 1.0 / (1.0 + jnp.exp(-x))


def _nt_dot(a, b):
    return lax.dot_general(a, b, (((1,), (1,)), ((), ())), preferred_element_type=F32)


def _inproj_kernel(x_ref, nw_ref, w_ref, cos_ref, sin_ref,
                   q_ref, k_ref, v_ref, z_ref, xbc_ref, ga_ref, gs_ref, dt_ref):
    x = x_ref[...]
    xn = x * lax.rsqrt(jnp.mean(x * x, axis=-1, keepdims=True) + EPS) * nw_ref[...]
    xb = xn.astype(BF16)
    cos = cos_ref[...]
    sin = sin_ref[...]
    lane = lax.broadcasted_iota(I32, cos.shape, 1)
    first_half = (lane % HEAD_DIM) < (HEAD_DIM // 2)

    def mm(lo, hi):
        return jnp.dot(xb, w_ref[:, lo:hi], preferred_element_type=F32)

    def rope_store(o_ref, base):
        u = mm(base, base + ATTN_W)
        for c in range(ATTN_W // LANES):
            uc = u[:, c * LANES:(c + 1) * LANES]
            ur = jnp.where(first_half, pltpu.roll(uc, LANES - HEAD_DIM // 2, 1),
                           pltpu.roll(uc, HEAD_DIM // 2, 1))
            o_ref[:, c * LANES:(c + 1) * LANES] = uc * cos + ur * sin

    rope_store(q_ref, _OFF_Q)
    rope_store(k_ref, _OFF_K)
    v_ref[...] = mm(_OFF_V, _OFF_Z)
    z_ref[...] = mm(_OFF_Z, _OFF_XBC)
    xbc_ref[...] = mm(_OFF_XBC, _OFF_GA)
    ga_ref[...] = mm(_OFF_GA, _OFF_GS)
    gs_ref[...] = mm(_OFF_GS, _OFF_DT)
    dt_ref[...] = mm(_OFF_DT, _W_IN_COLS)


def _rope_tables(pos):
    half = HEAD_DIM // 2
    inv_freq = ROPE_THETA ** (-jnp.arange(half, dtype=F32) * (2.0 / HEAD_DIM))
    ang = pos.astype(F32)[:, None] * inv_freq[None, :]
    ang = jnp.tile(ang, (1, LANES // half))
    lane = jnp.arange(LANES)
    sign = jnp.where((lane % HEAD_DIM) < half, -1.0, 1.0).astype(F32)
    return jnp.cos(ang), jnp.sin(ang) * sign[None, :]


def _inproj(x2d, seq_len, pos0, norm_w, w_packed):
    n = x2d.shape[0]
    tm = min(ROW_TILE, n)
    cos, sin = _rope_tables(pos0 + jnp.arange(seq_len, dtype=I32))
    if seq_len < tm:
        cos = jnp.tile(cos, (tm // seq_len, 1))
        sin = jnp.tile(sin, (tm // seq_len, 1))
    tab_blocks = cos.shape[0] // tm
    row = lambda i: (i, 0)
    fixed = lambda i: (0, 0)
    tab = lambda i: (i % tab_blocks, 0)
    widths = (ATTN_W, ATTN_W, ATTN_W, SSD_INNER, SSD_CONV_DIM, D_MODEL, D_MODEL, LANES)
    return pl.pallas_call(
        _inproj_kernel,
        grid=(n // tm,),
        in_specs=[pl.BlockSpec((tm, D_MODEL), row),
                  pl.BlockSpec((1, D_MODEL), fixed),
                  pl.BlockSpec((D_MODEL, _W_IN_COLS), fixed, pipeline_mode=pl.Buffered(1)),
                  pl.BlockSpec((tm, LANES), tab),
                  pl.BlockSpec((tm, LANES), tab)],
        out_specs=[pl.BlockSpec((tm, w), row) for w in widths],
        out_shape=[jax.ShapeDtypeStruct((n, w), F32) for w in widths],
        compiler_params=_cparams(("parallel",)),
        name="inproj",
    )(x2d, norm_w.reshape(1, D_MODEL), w_packed, cos, sin)


def _attn_prompt_kernel(q_ref, kp_ref, kc_ref, vp_ref, vc_ref, o_ref, l_ref):
    c = pl.program_id(2)
    q = q_ref[0] * (HEAD_DIM ** -0.5)
    kp, kc, vp, vc = kp_ref[0], kc_ref[0], vp_ref[0], vc_ref[0]
    i = lax.broadcasted_iota(I32, (SPAN, SPAN), 0)
    j = lax.broadcasted_iota(I32, (SPAN, SPAN), 1)
    valid_prev = (j >= i) & (c > 0)
    valid_cur = j <= i
    for h in range(HEADS_PER_GROUP):
        hs = slice(h * HEAD_DIM, (h + 1) * HEAD_DIM)
        qh = q[:, hs].astype(BF16)
        sp = jnp.where(valid_prev, _nt_dot(qh, kp[:, hs].astype(BF16)), NEG)
        sc = jnp.where(valid_cur, _nt_dot(qh, kc[:, hs].astype(BF16)), NEG)
        m = jnp.maximum(jnp.max(sp, axis=-1, keepdims=True), jnp.max(sc, axis=-1, keepdims=True))
        pp = jnp.exp(sp - m)
        pc = jnp.exp(sc - m)
        den = jnp.sum(pp, axis=-1, keepdims=True) + jnp.sum(pc, axis=-1, keepdims=True)
        acc = (jnp.dot(pp.astype(BF16), vp[:, hs].astype(BF16), preferred_element_type=F32)
               + jnp.dot(pc.astype(BF16), vc[:, hs].astype(BF16), preferred_element_type=F32))
        o_ref[0, :, hs] = acc / den
        l_ref[0, :, hs] = jnp.broadcast_to(m + jnp.log(den), (SPAN, HEAD_DIM))


def _attn_prompt(q, k, v, batch, seq, gi, dil):
    n_cls = seq // dil
    nblk = n_cls // SPAN
    ngrp = len(ATTN_GROUPS)
    view = lambda t: t.reshape(batch, n_cls, dil * ATTN_W)
    cur = lambda b, r, c: (b, c, r * ngrp + gi)
    prev = lambda b, r, c: (b, jnp.maximum(c - 1, 0), r * ngrp + gi)
    out = lambda b, r, c: (b, c, r)
    blk = (1, SPAN, GROUP_W)
    o, l = pl.pallas_call(
        _attn_prompt_kernel,
        grid=(batch, dil, nblk),
        in_specs=[pl.BlockSpec(blk, cur), pl.BlockSpec(blk, prev), pl.BlockSpec(blk, cur),
                  pl.BlockSpec(blk, prev), pl.BlockSpec(blk, cur)],
        out_specs=[pl.BlockSpec(blk, out), pl.BlockSpec(blk, out)],
        out_shape=[jax.ShapeDtypeStruct((batch, n_cls, dil * GROUP_W), F32)] * 2,
        compiler_params=_cparams(("parallel", "parallel", "arbitrary")),
        name=f"attn_prompt_d{dil}",
    )(view(q), view(k), view(k), view(v), view(v))
    return o.reshape(batch * seq, GROUP_W), l.reshape(batch * seq, GROUP_W)


def _attn_step_kernel(q_ref, kn_ref, vn_ref, buf_ref, o_ref, l_ref, nbuf_ref, *, buf_len, dil, n_new):
    q = q_ref[0] * (HEAD_DIM ** -0.5)
    kn, vn = kn_ref[0], vn_ref[0]
    buf = buf_ref[0]
    t = lax.broadcasted_iota(I32, (n_new, buf_len), 0)
    i = lax.broadcasted_iota(I32, (n_new, buf_len), 1)
    delta = buf_len + t - i
    valid_b = delta <= dil * SPAN
    tn = lax.broadcasted_iota(I32, (n_new, LANES), 0)
    un = lax.broadcasted_iota(I32, (n_new, LANES), 1)
    dn = tn - un
    valid_n = (dn >= 0) & (un < n_new)
    if dil > 1:
        valid_b = valid_b & ((delta & (dil - 1)) == 0)
        valid_n = valid_n & ((dn & (dil - 1)) == 0)
    zpad = jnp.zeros((LANES - n_new, HEAD_DIM), F32)
    for h in range(HEADS_PER_GROUP):
        hs = slice(h * HEAD_DIM, (h + 1) * HEAD_DIM)
        vs = slice(GROUP_W + h * HEAD_DIM, GROUP_W + (h + 1) * HEAD_DIM)
        qh = q[:, hs].astype(BF16)
        knp = jnp.concatenate([kn[:, hs], zpad], axis=0).astype(BF16)
        vnp = jnp.concatenate([vn[:, hs], zpad], axis=0).astype(BF16)
        sb = jnp.where(valid_b, _nt_dot(qh, buf[:, hs].astype(BF16)), NEG)
        sn = jnp.where(valid_n, _nt_dot(qh, knp), NEG)
        m = jnp.maximum(jnp.max(sb, axis=-1, keepdims=True), jnp.max(sn, axis=-1, keepdims=True))
        pb = jnp.exp(sb - m)
        pn = jnp.exp(sn - m)
        den = jnp.sum(pb, axis=-1, keepdims=True) + jnp.sum(pn, axis=-1, keepdims=True)
        acc = (jnp.dot(pb.astype(BF16), buf[:, vs].astype(BF16), preferred_element_type=F32)
               + jnp.dot(pn.astype(BF16), vnp, preferred_element_type=F32))
        o_ref[0, :, hs] = acc / den
        l_ref[0, :, hs] = jnp.broadcast_to(m + jnp.log(den), (n_new, HEAD_DIM))
    nbuf_ref[0, 0:buf_len - n_new, :] = buf[n_new:, :]
    nbuf_ref[0, buf_len - n_new:buf_len, 0:GROUP_W] = kn
    nbuf_ref[0, buf_len - n_new:buf_len, GROUP_W:2 * GROUP_W] = vn


def _attn_step(q, k, v, buf, batch, n_new, gi, dil):
    buf_len = buf.shape[1]
    assert dil & (dil - 1) == 0 and buf_len >= dil * SPAN and n_new % SUBLANES == 0
    view = lambda t: t.reshape(batch, n_new, ATTN_W)
    tok = pl.BlockSpec((1, n_new, GROUP_W), lambda b: (b, 0, gi))
    full = pl.BlockSpec((1, buf_len, 2 * GROUP_W), lambda b: (b, 0, 0))
    osp = pl.BlockSpec((1, n_new, GROUP_W), lambda b: (b, 0, 0))
    o, l, nbuf = pl.pallas_call(
        functools.partial(_attn_step_kernel, buf_len=buf_len, dil=dil, n_new=n_new),
        grid=(batch,),
        in_specs=[tok, tok, tok, full],
        out_specs=[osp, osp, full],
        out_shape=[jax.ShapeDtypeStruct((batch, n_new, GROUP_W), F32)] * 2
        + [jax.ShapeDtypeStruct((batch, buf_len, 2 * GROUP_W), F32)],
        compiler_params=_cparams(("parallel",)),
        name=f"attn_step_d{dil}",
    )(view(q), view(k), view(v), buf.reshape(batch, buf_len, 2 * GROUP_W))
    return (o.reshape(batch * n_new, GROUP_W), l.reshape(batch * n_new, GROUP_W),
            nbuf.reshape(batch, buf_len, 2, HEADS_PER_GROUP, HEAD_DIM))


def _split3(a):
    a1 = a.astype(BF16)
    r1 = a - a1.astype(F32)
    a2 = r1.astype(BF16)
    a3 = (r1 - a2.astype(F32)).astype(BF16)
    return a1, a2, a3


def _ssd_kernel(xbc_ref, z_ref, dt_ref, cst_ref, h0_ref, cw_ref, cb_ref, dtb_ref, alog_ref, dfull_ref, nw_ref,
                y_ref, cout_ref, hout_ref, xpad_ref, h_ref, y_scr, *, n_valid, n_chunks):
    c = pl.program_id(1)
    lc = SSD_CHUNK
    pad = SUBLANES

    @pl.when(c == 0)
    def _():
        xpad_ref[0:pad, :] = cst_ref[0]
        h_ref[...] = h0_ref[0]

    @pl.when(c > 0)
    def _():
        xpad_ref[0:pad, :] = xpad_ref[lc:lc + pad, :]

    if n_valid == lc:
        xpad_ref[pad:pad + lc, :] = xbc_ref[...]
        z = z_ref[...]
        dtr = dt_ref[...]
    else:
        fill = lambda w: jnp.zeros((lc - n_valid, w), F32)
        xpad_ref[pad:pad + n_valid, :] = xbc_ref[...]
        xpad_ref[pad + n_valid:pad + lc, :] = fill(SSD_CONV_DIM)
        z = jnp.concatenate([z_ref[...], fill(SSD_INNER)], axis=0)
        dtr = jnp.concatenate([dt_ref[...], fill(LANES)], axis=0)

    xc = cb_ref[...]
    for tap in range(4):
        xc = xc + xpad_ref[pl.ds(pad - 3 + tap, lc), :] * cw_ref[tap:tap + 1, :]
    xc = xc * _sigmoid(xc)
    cout_ref[0] = xpad_ref[pl.ds(pad + n_valid - 3, 3), :]

    xs = xc[:, :SSD_INNER]
    bm = xc[:, SSD_INNER:SSD_INNER + 2 * SSD_STATE]
    cm = xc[:, SSD_INNER + 2 * SSD_STATE:]

    row = lax.broadcasted_iota(I32, (lc, lc), 0)
    col = lax.broadcasted_iota(I32, (lc, lc), 1)
    causal = row >= col
    dtv = dtr + dtb_ref[...]
    dt = jnp.maximum(dtv, 0.0) + jnp.log1p(jnp.exp(-jnp.abs(dtv)))
    if n_valid < lc:
        dt = jnp.where(row < n_valid, dt, 0.0)
    a = dt * (-jnp.exp(alog_ref[...]))
    tri = causal.astype(BF16)
    a_cs = sum(jnp.dot(tri, p, preferred_element_type=F32) for p in _split3(a))
    a_cs_t = a_cs.T
    lane = lax.broadcasted_iota(I32, (lc, LANES), 1)
    low_half = lane < HEAD_DIM

    for g in range(2):
        bg = bm[:, g * SSD_STATE:(g + 1) * SSD_STATE]
        cg = cm[:, g * SSD_STATE:(g + 1) * SSD_STATE].astype(BF16)
        cb = _nt_dot(cg, bg.astype(BF16))
        for pair in range(4):
            e0 = g * 8 + pair * 2
            ps = slice(e0 * HEAD_DIM, (e0 + 2) * HEAD_DIM)
            dt_pair = jnp.where(low_half, dt[:, e0:e0 + 1], dt[:, e0 + 1:e0 + 2])
            xdt = xs[:, ps] * dt_pair
            xdt_t = xdt.T.astype(BF16)
            xdt_b = xdt.astype(BF16)
            for k in range(2):
                e = e0 + k
                acol = a_cs[:, e:e + 1]
                arow = a_cs_t[e:e + 1, :]
                decay = jnp.exp(jnp.where(causal, acol - arow, NEG))
                m = (cb * decay).astype(BF16)
                y_diag = jnp.dot(m, xdt_b[:, k * HEAD_DIM:(k + 1) * HEAD_DIM], preferred_element_type=F32)
                h_prev = h_ref[e]
                y_off = _nt_dot(cg, h_prev.astype(BF16)) * jnp.exp(acol)
                y_scr[:, e * HEAD_DIM:(e + 1) * HEAD_DIM] = y_diag + y_off
                a_last = a_cs[lc - 1:lc, e:e + 1]
                bd = (bg * jnp.exp(a_last - acol)).astype(BF16)
                st = jnp.dot(xdt_t[k * HEAD_DIM:(k + 1) * HEAD_DIM, :], bd, preferred_element_type=F32)
                h_ref[e] = h_prev * jnp.exp(a_last) + st

    y = y_scr[...] + xs * dfull_ref[...]
    gate = y * (z * _sigmoid(z))
    half = SSD_INNER // 2
    for g in range(2):
        gg = gate[:, g * half:(g + 1) * half]
        gg = gg * lax.rsqrt(jnp.mean(gg * gg, axis=-1, keepdims=True) + EPS)
        y_ref[:, g * half:(g + 1) * half] = (gg * nw_ref[:, g * half:(g + 1) * half])[0:n_valid]

    @pl.when(c == n_chunks - 1)
    def _():
        hout_ref[0] = h_ref[...]


def _ssd(xbc, z, dt_raw, conv_state, ssm_state, batch, seq, conv_w, conv_b, dt_bias, a_log, d_skip, norm_w):
    n_valid = min(seq, SSD_CHUNK)
    n_chunks = seq // n_valid
    assert n_valid % SUBLANES == 0 and seq % n_valid == 0
    padl = lambda t: jnp.pad(t.reshape(1, SSD_HEADS), ((0, 0), (0, LANES - SSD_HEADS)))
    cst = jnp.pad(conv_state, ((0, 0), (SUBLANES - 3, 0), (0, 0)))
    tokrow = lambda b, c: (b * n_chunks + c, 0)
    fixed = lambda b, c: (0, 0)
    per_b3 = lambda b, c: (b, 0, 0)
    kern = functools.partial(_ssd_kernel, n_valid=n_valid, n_chunks=n_chunks)
    y, cout, hout = pl.pallas_call(
        kern,
        grid=(batch, n_chunks),
        in_specs=[pl.BlockSpec((n_valid, SSD_CONV_DIM), tokrow),
                  pl.BlockSpec((n_valid, SSD_INNER), tokrow),
                  pl.BlockSpec((n_valid, LANES), tokrow),
                  pl.BlockSpec((1, SUBLANES, SSD_CONV_DIM), per_b3),
                  pl.BlockSpec((1, SSD_HEADS, HEAD_DIM, SSD_STATE), lambda b, c: (b, 0, 0, 0)),
                  pl.BlockSpec((4, SSD_CONV_DIM), fixed),
                  pl.BlockSpec((1, SSD_CONV_DIM), fixed),
                  pl.BlockSpec((1, LANES), fixed),
                  pl.BlockSpec((1, LANES), fixed),
                  pl.BlockSpec((1, SSD_INNER), fixed),
                  pl.BlockSpec((1, SSD_INNER), fixed)],
        out_specs=[pl.BlockSpec((n_valid, SSD_INNER), tokrow),
                   pl.BlockSpec((1, 3, SSD_CONV_DIM), per_b3),
                   pl.BlockSpec((1, SSD_HEADS, HEAD_DIM, SSD_STATE), lambda b, c: (b, 0, 0, 0))],
        out_shape=[jax.ShapeDtypeStruct((batch * seq, SSD_INNER), F32),
                   jax.ShapeDtypeStruct((batch, 3, SSD_CONV_DIM), F32),
                   jax.ShapeDtypeStruct((batch, SSD_HEADS, HEAD_DIM, SSD_STATE), F32)],
        scratch_shapes=[pltpu.VMEM((SSD_CHUNK + SUBLANES, SSD_CONV_DIM), F32),
                        pltpu.VMEM((SSD_HEADS, HEAD_DIM, SSD_STATE), F32),
                        pltpu.VMEM((SSD_CHUNK, SSD_INNER), F32)],
        compiler_params=_cparams(("parallel", "arbitrary")),
        name="ssd",
    )(xbc, z, dt_raw, cst, ssm_state, conv_w, conv_b.reshape(1, SSD_CONV_DIM), padl(dt_bias), padl(a_log),
      jnp.repeat(d_skip, HEAD_DIM).reshape(1, SSD_INNER), norm_w.reshape(1, SSD_INNER))
    return y, cout, hout


def _merge_kernel(x_ref, o0_ref, o1_ref, o2_ref, l0_ref, l1_ref, l2_ref, ys_ref, ga_ref, gs_ref,
                  wab_ref, wsb_ref, bg_ref, wo_ref, n2_ref, wr_ref, br_ref,
                  h_ref, hn_ref, ri_ref, rf_ref, cnt_ref, carry_ref):
    step = pl.program_id(0)

    @pl.when(step == 0)
    def _():
        carry_ref[...] = jnp.zeros_like(carry_ref)

    l0, l1, l2 = l0_ref[...], l1_ref[...], l2_ref[...]
    m = jnp.maximum(jnp.maximum(l0, l1), l2)
    w0, w1, w2 = jnp.exp(l0 - m), jnp.exp(l1 - m), jnp.exp(l2 - m)
    y_attn = (w0 * o0_ref[...] + w1 * o1_ref[...] + w2 * o2_ref[...]) / (w0 + w1 + w2)
    pa = jnp.dot(y_attn.astype(BF16), wab_ref[...], preferred_element_type=F32)
    ps = jnp.dot(ys_ref[...].astype(BF16), wsb_ref[...], preferred_element_type=F32)
    merged = _sigmoid(ga_ref[...] + bg_ref[0:1, :]) * pa + _sigmoid(gs_ref[...] + bg_ref[1:2, :]) * ps
    h = x_ref[...] + jnp.dot(merged.astype(BF16), wo_ref[...], preferred_element_type=F32)
    h_ref[...] = h
    hn = h * lax.rsqrt(jnp.mean(h * h, axis=-1, keepdims=True) + EPS) * n2_ref[...]
    hnb = hn.astype(BF16)
    hn_ref[...] = hnb

    logits = jnp.dot(hnb, wr_ref[...], preferred_element_type=F32) + br_ref[...]
    tm = logits.shape[0]
    lane = lax.broadcasted_iota(I32, (tm, LANES), 1)
    big = jnp.int32(LANES)

    def top(vals):
        v = jnp.max(vals, axis=-1, keepdims=True)
        idx = jnp.min(jnp.where(vals == v, lane, big), axis=-1, keepdims=True)
        return v, idx

    is_coarse = (lane >= N_EXPERTS) & (lane < N_EXPERTS + N_GROUPS_E)
    lc = jnp.where(is_coarse, logits, NEG)
    mc, ic = top(lc)
    p_grp = 1.0 / jnp.sum(jnp.exp(lc - mc), axis=-1, keepdims=True)
    lo = (ic - N_EXPERTS) * EXPERTS_PER_GROUP
    lf = jnp.where((lane >= lo) & (lane < lo + EXPERTS_PER_GROUP), logits, NEG)
    v1, i1 = top(lf)
    v2, i2 = top(jnp.where(lane == i1, NEG, lf))
    e2 = jnp.exp(v2 - v1)
    g1 = p_grp / (1.0 + e2)
    g2 = p_grp * e2 / (1.0 + e2)

    oh1 = lane == i1
    oh2 = lane == i2
    cnt = oh1.astype(F32) + oh2.astype(F32)
    r = lax.broadcasted_iota(I32, (tm, tm), 0)
    s = lax.broadcasted_iota(I32, (tm, tm), 1)
    before = jnp.dot((r > s).astype(BF16), cnt.astype(BF16), preferred_element_type=F32) + carry_ref[0:1, :]
    r1 = jnp.sum(jnp.where(oh1, before, 0.0), axis=-1, keepdims=True)
    r2 = jnp.sum(jnp.where(oh2, before, 0.0), axis=-1, keepdims=True)
    new_carry = carry_ref[0:1, :] + jnp.sum(cnt, axis=0, keepdims=True)
    carry_ref[...] = jnp.broadcast_to(new_carry, carry_ref.shape)
    cnt_ref[...] = jnp.broadcast_to(new_carry, cnt_ref.shape).astype(I32)

    ri = jnp.where(lane == 0, i1, jnp.where(lane == 1, i2, 0))
    ri = jnp.where(lane == 2, r1.astype(I32), jnp.where(lane == 3, r2.astype(I32), ri))
    ri_ref[...] = ri
    rf_ref[...] = jnp.where(lane == 0, g1, jnp.where(lane == 1, g2, 0.0))


def _merge(x2d, outs, lses, y_ssd, g_a, g_s, wab, wsb, b_gate, wo, norm2_w, w_router, b_router):
    n = x2d.shape[0]
    tm = min(ROW_TILE, n)
    row = lambda i: (i, 0)
    fixed = lambda i: (0, 0)
    wide = pl.BlockSpec((tm, D_MODEL), row)
    grp = pl.BlockSpec((tm, GROUP_W), row)
    info = pl.BlockSpec((tm, LANES), row)
    return pl.pallas_call(
        _merge_kernel,
        grid=(n // tm,),
        in_specs=[wide, grp, grp, grp, grp, grp, grp, wide, wide, wide,
                  pl.BlockSpec((GROUP_W, D_MODEL), fixed),
                  pl.BlockSpec((SSD_INNER, D_MODEL), fixed),
                  pl.BlockSpec((2, D_MODEL), fixed),
                  pl.BlockSpec((D_MODEL, D_MODEL), fixed),
                  pl.BlockSpec((1, D_MODEL), fixed),
                  pl.BlockSpec((D_MODEL, LANES), fixed),
                  pl.BlockSpec((1, LANES), fixed)],
        out_specs=[wide, wide, info, info, pl.BlockSpec((SUBLANES, LANES), fixed)],
        out_shape=[jax.ShapeDtypeStruct((n, D_MODEL), F32),
                   jax.ShapeDtypeStruct((n, D_MODEL), BF16),
                   jax.ShapeDtypeStruct((n, LANES), I32),
                   jax.ShapeDtypeStruct((n, LANES), F32),
                   jax.ShapeDtypeStruct((SUBLANES, LANES), I32)],
        scratch_shapes=[pltpu.VMEM((SUBLANES, LANES), F32)],
        compiler_params=_cparams(("arbitrary",)),
        name="merge_out",
    )(x2d, *outs, *lses, y_ssd, g_a, g_s, wab, wsb, b_gate, wo, norm2_w.reshape(1, D_MODEL), w_router, b_router)


def _expert_kernel(te_ref, nu_ref, x_ref, wg_ref, wu_ref, wd_ref, o_ref, wgb, wub, wdb):
    i = pl.program_id(0)
    changed = (i == 0) | (te_ref[i] != te_ref[jnp.maximum(i - 1, 0)])

    @pl.when(changed)
    def _():
        wgb[...] = wg_ref[0].astype(BF16)
        wub[...] = wu_ref[0].astype(BF16)
        wdb[...] = wd_ref[0].astype(BF16)

    @pl.when(i < nu_ref[0])
    def _():
        x = x_ref[...]
        hg = jnp.dot(x, wgb[...], preferred_element_type=F32)
        hu = jnp.dot(x, wub[...], preferred_element_type=F32)
        hb = (hg * _sigmoid(hg)) * hu
        o_ref[...] = jnp.dot(hb.astype(BF16), wdb[...], preferred_element_type=F32)

    @pl.when(i >= nu_ref[0])
    def _():
        o_ref[...] = jnp.zeros_like(o_ref)


def _experts(xs, tile_expert, n_used, w_eg, w_eu, w_ed):
    n_slots = xs.shape[0]
    n_tiles = n_slots // EXPERT_TILE
    row = lambda i, te, nu: (i, 0)
    wsel = lambda i, te, nu: (te[i], 0, 0)
    return pl.pallas_call(
        _expert_kernel,
        grid_spec=pltpu.PrefetchScalarGridSpec(
            num_scalar_prefetch=2,
            grid=(n_tiles,),
            in_specs=[pl.BlockSpec((EXPERT_TILE, D_MODEL), row),
                      pl.BlockSpec((1, D_MODEL, D_FF), wsel),
                      pl.BlockSpec((1, D_MODEL, D_FF), wsel),
                      pl.BlockSpec((1, D_FF, D_MODEL), wsel)],
            out_specs=pl.BlockSpec((EXPERT_TILE, D_MODEL), row),
            scratch_shapes=[pltpu.VMEM((D_MODEL, D_FF), BF16),
                            pltpu.VMEM((D_MODEL, D_FF), BF16),
                            pltpu.VMEM((D_FF, D_MODEL), BF16)]),
        out_shape=jax.ShapeDtypeStruct((n_slots, D_MODEL), F32),
        compiler_params=_cparams(("arbitrary",)),
        name="experts",
    )(tile_expert, n_used, xs, w_eg, w_eu, w_ed)


def _final_kernel(h_ref, y0_ref, y1_ref, rf_ref, fw_ref, o_ref):
    g = rf_ref[...]
    moe = y0_ref[...] * g[:, 0:1] + y1_ref[...] * g[:, 1:2]
    h = h_ref[...] + moe
    o_ref[...] = h * lax.rsqrt(jnp.mean(h * h, axis=-1, keepdims=True) + EPS) * fw_ref[...]


def _final(h, y0, y1, rf, final_w):
    n = h.shape[0]
    tm = min(ROW_TILE, n)
    row = lambda i: (i, 0)
    wide = pl.BlockSpec((tm, D_MODEL), row)
    return pl.pallas_call(
        _final_kernel,
        grid=(n // tm,),
        in_specs=[wide, wide, wide, pl.BlockSpec((tm, LANES), row), pl.BlockSpec((1, D_MODEL), lambda i: (0, 0))],
        out_specs=wide,
        out_shape=jax.ShapeDtypeStruct((n, D_MODEL), F32),
        compiler_params=_cparams(("parallel",)),
        name="final",
    )(h, y0, y1, rf, final_w.reshape(1, D_MODEL))


def _moe_and_final(h, hn, ri, rf, counts, w_eg, w_eu, w_ed, final_w):
    n = h.shape[0]
    counts = counts[0, :N_EXPERTS]
    padded = ((counts + EXPERT_TILE - 1) // EXPERT_TILE) * EXPERT_TILE
    ends = jnp.cumsum(padded)
    starts = ends - padded
    n_tiles = (2 * n) // EXPERT_TILE + N_EXPERTS
    n_slots = n_tiles * EXPERT_TILE
    dest = starts[ri[:, 0:2]] + ri[:, 2:4]
    src = jnp.zeros((n_slots,), I32).at[dest.reshape(-1)].set(jnp.repeat(jnp.arange(n, dtype=I32), 2))
    tile_expert = jnp.minimum(
        jnp.searchsorted(ends, jnp.arange(n_tiles, dtype=I32) * EXPERT_TILE, side="right"), N_EXPERTS - 1
    ).astype(I32)
    n_used = (ends[-1] // EXPERT_TILE).astype(I32).reshape(1)
    out = _experts(hn[src], tile_expert, n_used, w_eg, w_eu, w_ed)
    return _final(h, out[dest[:, 0]], out[dest[:, 1]], rf, final_w)


def _layer(x, pos0, kv_bufs, conv_state, ssm_state, p):
    batch, seq, _ = x.shape
    n = batch * seq
    x2d = x.reshape(n, D_MODEL)
    q, k, v, z, xbc, g_a, g_s, dt_raw = _inproj(x2d, seq, pos0, p["norm1_w"], p["w_in"])

    outs, lses, new_kv = [], [], []
    for gi, (window, dil) in enumerate(ATTN_GROUPS):
        if kv_bufs is None:
            o, l = _attn_prompt(q, k, v, batch, seq, gi, dil)
            keep = min(window, seq)
            hs = slice(gi * GROUP_W, (gi + 1) * GROUP_W)
            kk = k.reshape(batch, seq, ATTN_W)[:, seq - keep:, hs]
            vv = v.reshape(batch, seq, ATTN_W)[:, seq - keep:, hs]
            nbuf = jnp.stack([kk, vv], axis=2).reshape(batch, keep, 2, HEADS_PER_GROUP, HEAD_DIM)
        else:
            o, l, nbuf = _attn_step(q, k, v, kv_bufs[gi], batch, seq, gi, dil)
        outs.append(o)
        lses.append(l)
        new_kv.append(nbuf)

    if conv_state is None:
        conv_state = jnp.zeros((batch, 3, SSD_CONV_DIM), F32)
        ssm_state = jnp.zeros((batch, SSD_HEADS, HEAD_DIM, SSD_STATE), F32)
    y_ssd, new_conv, new_ssm = _ssd(xbc, z, dt_raw, conv_state, ssm_state, batch, seq, p["conv_w"], p["conv_b"],
                                    p["dt_bias"], p["A_log"], p["D_skip"], p["ssd_norm_w"])

    h, hn, ri, rf, counts = _merge(x2d, outs, lses, y_ssd, g_a, g_s, p["w_attn_br"], p["w_ssd_br"], p["b_gate"],
                                   p["w_out"], p["norm2_w"], p["w_router"], p["b_router"])
    y = _moe_and_final(h, hn, ri, rf, counts, p["w_eg"], p["w_eu"], p["w_ed"], p["final_norm_w"])
    return y.reshape(batch, seq, D_MODEL), new_kv, new_conv, new_ssm


def _pack_w_in(w_in):
    offs = (0, 768, 1536, 2304, 3328, 4864, 4880, 5904, 6928)
    q, k, v, z, xbc, dt, g_a, g_s = (w_in[:, offs[i]:offs[i + 1]] for i in range(8))
    dt = jnp.pad(dt, ((0, 0), (0, LANES - SSD_HEADS)))
    return jnp.concatenate([q, k, v, z, xbc, g_a, g_s, dt], axis=1).astype(BF16)


def kernel(x_prompt, x_sample, cache_kv_w128, cache_kv_w512, cache_kv_w2048, state_conv, state_ssm, norm1_w, w_in, w_attn_br, w_ssd_br, b_gate, w_out, conv_w, conv_b, dt_bias, A_log, D_skip, ssd_norm_w, norm2_w, w_router_coarse, b_router_coarse, w_router_fine, b_router_fine, w_expert_gate, w_expert_up, w_expert_down, final_norm_w):
    depth = norm1_w.shape[0]
    assert depth == 1, "the final norm is fused into the layer's last kernel"
    l = 0
    rpad = LANES - N_EXPERTS - N_GROUPS_E
    p = dict(
        norm1_w=norm1_w[l], w_in=_pack_w_in(w_in[l]),
        w_attn_br=w_attn_br[l].astype(BF16), w_ssd_br=w_ssd_br[l].astype(BF16), b_gate=b_gate[l],
        w_out=w_out[l].astype(BF16), conv_w=conv_w[l], conv_b=conv_b[l], dt_bias=dt_bias[l], A_log=A_log[l],
        D_skip=D_skip[l], ssd_norm_w=ssd_norm_w[l], norm2_w=norm2_w[l],
        w_router=jnp.pad(jnp.concatenate([w_router_fine[l], w_router_coarse[l]], axis=1),
                         ((0, 0), (0, rpad))).astype(BF16),
        b_router=jnp.pad(jnp.concatenate([b_router_fine[l], b_router_coarse[l]]), (0, rpad)).reshape(1, LANES),
        w_eg=w_expert_gate[l], w_eu=w_expert_up[l], w_ed=w_expert_down[l], final_norm_w=final_norm_w,
    )
    y_p, kv_p, c_p, st_p = _layer(x_prompt, 0, None, None, None, p)
    bufs = (cache_kv_w128[l], cache_kv_w512[l], cache_kv_w2048[l])
    y_s, kv_s, c_s, st_s = _layer(x_sample, PAST_LEN, bufs, state_conv[l], state_ssm[l], p)
    lead = lambda t: t[None]
    return (y_p, y_s, lead(kv_p[0]), lead(kv_p[1]), lead(kv_p[2]), lead(c_p), lead(st_p),
            lead(kv_s[0]), lead(kv_s[1]), lead(kv_s[2]), lead(c_s), lead(st_s))
```

```python
import functools

import jax
import jax.numpy as jnp
from jax import lax
from jax.experimental import pallas as pl
from jax.experimental.pallas import tpu as pltpu

F32 = jnp.float32
BF16 = jnp.bfloat16
I32 = jnp.int32

D_MODEL = 1024
HEAD_DIM = 64
ATTN_GROUPS = ((128, 1), (512, 4), (2048, 16))
SPAN = 128
HEADS_PER_GROUP = 4
GROUP_W = HEADS_PER_GROUP * HEAD_DIM
ATTN_W = GROUP_W * len(ATTN_GROUPS)
ROPE_THETA = 10000.0
PAST_LEN = 8192
SSD_INNER = 1024
SSD_HEADS = 16
SSD_STATE = 128
SSD_CONV_DIM = 1536
SSD_CHUNK = 128
N_GROUPS_E = 4
EXPERTS_PER_GROUP = 8
N_EXPERTS = 32
D_FF = 512
EPS = 1e-6
LANES = 128
SUBLANES = 8
NEG = -1e30

_OFF_Q, _OFF_K, _OFF_V, _OFF_Z, _OFF_XBC, _OFF_GA, _OFF_GS, _OFF_DT = 0, 768, 1536, 2304, 3328, 4864, 5888, 6912
_W_IN_COLS = 7040

ROW_TILE = 256
EXPERT_TILE = 256
VMEM_LIMIT = 56 * 1024 * 1024


def _cparams(sem, **kw):
    return pltpu.CompilerParams(dimension_semantics=sem, vmem_limit_bytes=VMEM_LIMIT, **kw)


def _sigmoid(x):
    return 1.0 / (1.0 + jnp.exp(-x))


ROW_CHUNKS = D_MODEL // LANES


def _store_row_tiles(ref, val):
    rows = val.shape[0]
    for c in range(ROW_CHUNKS):
        ref[pl.ds(c, rows, stride=ROW_CHUNKS), :] = val[:, c * LANES:(c + 1) * LANES]


def _load_row_tiles(ref, first, rows):
    return jnp.concatenate(
        [ref[pl.ds(first * ROW_CHUNKS + c, rows, stride=ROW_CHUNKS), :] for c in range(ROW_CHUNKS)], axis=1)


def _nt_dot(a, b):
    return lax.dot_general(a, b, (((1,), (1,)), ((), ())), preferred_element_type=F32)


def _store_by_class(o_ref, scr_ref, val, dil):
    if dil == 1:
        o_ref[0, 0] = val
        return
    rows = val.shape[0]
    for half in range(GROUP_W // LANES):
        lanes = slice(half * LANES, (half + 1) * LANES)
        scr_ref[half] = val[:, lanes]
        for r in range(dil):
            o_ref[0, r, :, lanes] = scr_ref[half, pl.ds(r, rows // dil, stride=dil), :]


def _inproj_kernel(x_ref, nw_ref, w_ref, cos_ref, sin_ref, *refs, class_major):
    if class_major:
        qkv_refs, (z_ref, xbc_ref, ga_ref, gs_ref, dt_ref, scr_ref) = refs[:9], refs[9:]
    else:
        qkv_refs, (z_ref, xbc_ref, ga_ref, gs_ref, dt_ref) = refs[:3], refs[3:]
    x = x_ref[...]
    xn = x * lax.rsqrt(jnp.mean(x * x, axis=-1, keepdims=True) + EPS) * nw_ref[...]
    xb = xn.astype(BF16)
    cos = cos_ref[...]
    sin = sin_ref[...]
    lane = lax.broadcasted_iota(I32, cos.shape, 1)
    first_half = (lane % HEAD_DIM) < (HEAD_DIM // 2)

    def mm(lo, hi):
        return jnp.dot(xb, w_ref[:, lo:hi], preferred_element_type=F32)

    def rope(uc):
        ur = jnp.where(first_half, pltpu.roll(uc, LANES - HEAD_DIM // 2, 1), pltpu.roll(uc, HEAD_DIM // 2, 1))
        return uc * cos + ur * sin

    def emit(which, base, roped):
        u = mm(base, base + ATTN_W)
        chunks = [u[:, c * LANES:(c + 1) * LANES] for c in range(ATTN_W // LANES)]
        if roped:
            chunks = [rope(uc) for uc in chunks]
        if class_major:
            for gi, (_, dil) in enumerate(ATTN_GROUPS):
                val = jnp.concatenate(chunks[2 * gi:2 * gi + 2], axis=1)
                _store_by_class(qkv_refs[3 * which + gi], scr_ref, val, dil)
        else:
            for c, uc in enumerate(chunks):
                qkv_refs[which][:, c * LANES:(c + 1) * LANES] = uc

    emit(0, _OFF_Q, True)
    emit(1, _OFF_K, True)
    emit(2, _OFF_V, False)
    z_ref[...] = mm(_OFF_Z, _OFF_XBC)
    xbc_ref[...] = mm(_OFF_XBC, _OFF_GA)
    ga_ref[...] = mm(_OFF_GA, _OFF_GS)
    gs_ref[...] = mm(_OFF_GS, _OFF_DT)
    dt_ref[...] = mm(_OFF_DT, _W_IN_COLS)


def _rope_tables(pos):
    half = HEAD_DIM // 2
    inv_freq = ROPE_THETA ** (-jnp.arange(half, dtype=F32) * (2.0 / HEAD_DIM))
    ang = pos.astype(F32)[:, None] * inv_freq[None, :]
    ang = jnp.tile(ang, (1, LANES // half))
    lane = jnp.arange(LANES)
    sign = jnp.where((lane % HEAD_DIM) < half, -1.0, 1.0).astype(F32)
    return jnp.cos(ang), jnp.sin(ang) * sign[None, :]


def _inproj(x2d, batch, seq_len, pos0, norm_w, w_packed, class_major):
    n = x2d.shape[0]
    tm = min(ROW_TILE, n)
    cos, sin = _rope_tables(pos0 + jnp.arange(seq_len, dtype=I32))
    if seq_len < tm:
        cos = jnp.tile(cos, (tm // seq_len, 1))
        sin = jnp.tile(sin, (tm // seq_len, 1))
    tab_blocks = cos.shape[0] // tm
    row = lambda i: (i, 0)
    fixed = lambda i: (0, 0)
    tab = lambda i: (i % tab_blocks, 0)
    widths = (SSD_INNER, SSD_CONV_DIM, D_MODEL, D_MODEL, LANES)
    out_specs = [pl.BlockSpec((tm, w), row) for w in widths]
    out_shape = [jax.ShapeDtypeStruct((n, w), F32) for w in widths]
    scratch = []
    if class_major:
        assert seq_len % tm == 0
        per_seq = seq_len // tm
        cls = lambda i: (i // per_seq, 0, i % per_seq, 0)
        dils = [d for _, d in ATTN_GROUPS] * 3
        out_specs = [pl.BlockSpec((1, d, tm // d, GROUP_W), cls) for d in dils] + out_specs
        out_shape = [jax.ShapeDtypeStruct((batch, d, seq_len // d, GROUP_W), F32) for d in dils] + out_shape
        scratch = [pltpu.VMEM((GROUP_W // LANES, tm, LANES), F32)]
    else:
        out_specs = [pl.BlockSpec((tm, ATTN_W), row)] * 3 + out_specs
        out_shape = [jax.ShapeDtypeStruct((n, ATTN_W), F32)] * 3 + out_shape
    res = pl.pallas_call(
        functools.partial(_inproj_kernel, class_major=class_major),
        grid=(n // tm,),
        in_specs=[pl.BlockSpec((tm, D_MODEL), row),
                  pl.BlockSpec((1, D_MODEL), fixed),
                  pl.BlockSpec((D_MODEL, _W_IN_COLS), fixed, pipeline_mode=pl.Buffered(1)),
                  pl.BlockSpec((tm, LANES), tab),
                  pl.BlockSpec((tm, LANES), tab)],
        out_specs=out_specs,
        out_shape=out_shape,
        scratch_shapes=scratch,
        compiler_params=_cparams(("parallel",)),
        name="inproj",
    )(x2d, norm_w.reshape(1, D_MODEL), w_packed, cos, sin)
    if class_major:
        return (tuple(res[0:3]), tuple(res[3:6]), tuple(res[6:9])) + tuple(res[9:])
    return tuple(res)


ATTN_QBLOCKS = 4


def _attn_prompt_kernel(q_ref, kp_ref, kc_ref, vp_ref, vc_ref, o_ref, l_ref, *, qblocks):
    c = pl.program_id(2)
    i = lax.broadcasted_iota(I32, (SPAN, SPAN), 0)
    j = lax.broadcasted_iota(I32, (SPAN, SPAN), 1)
    upper = j >= i
    lower = j <= i
    kcur = kc_ref[0, 0].astype(BF16)
    vcur = vc_ref[0, 0].astype(BF16)
    kprev = kp_ref[0, 0].astype(BF16)
    vprev = vp_ref[0, 0].astype(BF16)
    for b in range(qblocks):
        rows = slice(b * SPAN, (b + 1) * SPAN)
        q = q_ref[0, 0, rows, :] * (HEAD_DIM ** -0.5)
        if b == 0:
            kp, vp, valid_prev = kprev, vprev, upper & (c > 0)
        else:
            prows = slice((b - 1) * SPAN, b * SPAN)
            kp, vp, valid_prev = kcur[prows], vcur[prows], upper
        kc, vc = kcur[rows], vcur[rows]
        for h in range(HEADS_PER_GROUP):
            hs = slice(h * HEAD_DIM, (h + 1) * HEAD_DIM)
            qh = q[:, hs].astype(BF16)
            sp = jnp.where(valid_prev, _nt_dot(qh, kp[:, hs]), NEG)
            sc = jnp.where(lower, _nt_dot(qh, kc[:, hs]), NEG)
            m = jnp.maximum(jnp.max(sp, axis=-1, keepdims=True), jnp.max(sc, axis=-1, keepdims=True))
            pp = jnp.exp(sp - m)
            pc = jnp.exp(sc - m)
            den = jnp.sum(pp, axis=-1, keepdims=True) + jnp.sum(pc, axis=-1, keepdims=True)
            acc = (jnp.dot(pp.astype(BF16), vp[:, hs], preferred_element_type=F32)
                   + jnp.dot(pc.astype(BF16), vc[:, hs], preferred_element_type=F32))
            o_ref[0, 0, rows, hs] = acc / den
            l_ref[0, 0, rows, hs] = jnp.broadcast_to(m + jnp.log(den), (SPAN, HEAD_DIM))


def _attn_prompt(q, k, v, dil):
    batch, _, n_cls, _ = q.shape
    qblocks = min(ATTN_QBLOCKS, n_cls // SPAN)
    tq = qblocks * SPAN
    assert n_cls % tq == 0
    cur = lambda b, r, c: (b, r, c, 0)
    prev = lambda b, r, c: (b, r, jnp.maximum(c * qblocks - 1, 0), 0)
    big = pl.BlockSpec((1, 1, tq, GROUP_W), cur)
    small = pl.BlockSpec((1, 1, SPAN, GROUP_W), prev)
    return pl.pallas_call(
        functools.partial(_attn_prompt_kernel, qblocks=qblocks),
        grid=(batch, dil, n_cls // tq),
        in_specs=[big, small, big, small, big],
        out_specs=[big, big],
        out_shape=[jax.ShapeDtypeStruct(q.shape, F32)] * 2,
        compiler_params=_cparams(("parallel", "parallel", "arbitrary")),
        name=f"attn_prompt_d{dil}",
    )(q, k, k, v, v)


def _attn_step_kernel(q_ref, kn_ref, vn_ref, buf_ref, o_ref, l_ref, nbuf_ref, *, buf_len, dil, n_new):
    q = q_ref[0] * (HEAD_DIM ** -0.5)
    kn, vn = kn_ref[0], vn_ref[0]
    buf = buf_ref[0]
    t = lax.broadcasted_iota(I32, (n_new, buf_len), 0)
    i = lax.broadcasted_iota(I32, (n_new, buf_len), 1)
    delta = buf_len + t - i
    valid_b = delta <= dil * SPAN
    tn = lax.broadcasted_iota(I32, (n_new, LANES), 0)
    un = lax.broadcasted_iota(I32, (n_new, LANES), 1)
    dn = tn - un
    valid_n = (dn >= 0) & (un < n_new)
    if dil > 1:
        valid_b = valid_b & ((delta & (dil - 1)) == 0)
        valid_n = valid_n & ((dn & (dil - 1)) == 0)
    zpad = jnp.zeros((LANES - n_new, HEAD_DIM), F32)
    for h in range(HEADS_PER_GROUP):
        hs = slice(h * HEAD_DIM, (h + 1) * HEAD_DIM)
        vs = slice(GROUP_W + h * HEAD_DIM, GROUP_W + (h + 1) * HEAD_DIM)
        qh = q[:, hs].astype(BF16)
        knp = jnp.concatenate([kn[:, hs], zpad], axis=0).astype(BF16)
        vnp = jnp.concatenate([vn[:, hs], zpad], axis=0).astype(BF16)
        sb = jnp.where(valid_b, _nt_dot(qh, buf[:, hs].astype(BF16)), NEG)
        sn = jnp.where(valid_n, _nt_dot(qh, knp), NEG)
        m = jnp.maximum(jnp.max(sb, axis=-1, keepdims=True), jnp.max(sn, axis=-1, keepdims=True))
        pb = jnp.exp(sb - m)
        pn = jnp.exp(sn - m)
        den = jnp.sum(pb, axis=-1, keepdims=True) + jnp.sum(pn, axis=-1, keepdims=True)
        acc = (jnp.dot(pb.astype(BF16), buf[:, vs].astype(BF16), preferred_element_type=F32)
               + jnp.dot(pn.astype(BF16), vnp, preferred_element_type=F32))
        o_ref[0, :, hs] = acc / den
        l_ref[0, :, hs] = jnp.broadcast_to(m + jnp.log(den), (n_new, HEAD_DIM))
    nbuf_ref[0, 0:buf_len - n_new, :] = buf[n_new:, :]
    nbuf_ref[0, buf_len - n_new:buf_len, 0:GROUP_W] = kn
    nbuf_ref[0, buf_len - n_new:buf_len, GROUP_W:2 * GROUP_W] = vn


def _attn_step(q, k, v, buf, batch, n_new, gi, dil):
    buf_len = buf.shape[1]
    assert dil & (dil - 1) == 0 and buf_len >= dil * SPAN and n_new % SUBLANES == 0
    view = lambda t: t.reshape(batch, n_new, ATTN_W)
    tok = pl.BlockSpec((1, n_new, GROUP_W), lambda b: (b, 0, gi))
    full = pl.BlockSpec((1, buf_len, 2 * GROUP_W), lambda b: (b, 0, 0))
    osp = pl.BlockSpec((1, n_new, GROUP_W), lambda b: (b, 0, 0))
    o, l, nbuf = pl.pallas_call(
        functools.partial(_attn_step_kernel, buf_len=buf_len, dil=dil, n_new=n_new),
        grid=(batch,),
        in_specs=[tok, tok, tok, full],
        out_specs=[osp, osp, full],
        out_shape=[jax.ShapeDtypeStruct((batch, n_new, GROUP_W), F32)] * 2
        + [jax.ShapeDtypeStruct((batch, buf_len, 2 * GROUP_W), F32)],
        compiler_params=_cparams(("parallel",)),
        name=f"attn_step_d{dil}",
    )(view(q), view(k), view(v), buf.reshape(batch, buf_len, 2 * GROUP_W))
    return (o.reshape(batch * n_new, GROUP_W), l.reshape(batch * n_new, GROUP_W),
            nbuf.reshape(batch, buf_len, 2, HEADS_PER_GROUP, HEAD_DIM))


def _split3(a):
    a1 = a.astype(BF16)
    r1 = a - a1.astype(F32)
    a2 = r1.astype(BF16)
    a3 = (r1 - a2.astype(F32)).astype(BF16)
    return a1, a2, a3


def _ssd_kernel(xbc_ref, z_ref, dt_ref, cst_ref, h0_ref, cw_ref, cb_ref, dtb_ref, alog_ref, dfull_ref, nw_ref,
                y_ref, cout_ref, hout_ref, xpad_ref, h_ref, y_scr, *, n_valid, n_chunks):
    c = pl.program_id(1)
    lc = SSD_CHUNK
    pad = SUBLANES

    @pl.when(c == 0)
    def _():
        xpad_ref[0:pad, :] = cst_ref[0]
        h_ref[...] = h0_ref[0]

    @pl.when(c > 0)
    def _():
        xpad_ref[0:pad, :] = xpad_ref[lc:lc + pad, :]

    if n_valid == lc:
        xpad_ref[pad:pad + lc, :] = xbc_ref[...]
        z = z_ref[...]
        dtr = dt_ref[...]
    else:
        fill = lambda w: jnp.zeros((lc - n_valid, w), F32)
        xpad_ref[pad:pad + n_valid, :] = xbc_ref[...]
        xpad_ref[pad + n_valid:pad + lc, :] = fill(SSD_CONV_DIM)
        z = jnp.concatenate([z_ref[...], fill(SSD_INNER)], axis=0)
        dtr = jnp.concatenate([dt_ref[...], fill(LANES)], axis=0)

    xc = cb_ref[...]
    for tap in range(4):
        xc = xc + xpad_ref[pl.ds(pad - 3 + tap, lc), :] * cw_ref[tap:tap + 1, :]
    xc = xc * _sigmoid(xc)
    cout_ref[0] = xpad_ref[pl.ds(pad + n_valid - 3, 3), :]

    xs = xc[:, :SSD_INNER]
    bm = xc[:, SSD_INNER:SSD_INNER + 2 * SSD_STATE]
    cm = xc[:, SSD_INNER + 2 * SSD_STATE:]

    row = lax.broadcasted_iota(I32, (lc, lc), 0)
    col = lax.broadcasted_iota(I32, (lc, lc), 1)
    causal = row >= col
    dtv = dtr + dtb_ref[...]
    dt = jnp.maximum(dtv, 0.0) + jnp.log1p(jnp.exp(-jnp.abs(dtv)))
    if n_valid < lc:
        dt = jnp.where(row < n_valid, dt, 0.0)
    a = dt * (-jnp.exp(alog_ref[...]))
    tri = causal.astype(BF16)
    a_cs = sum(jnp.dot(tri, p, preferred_element_type=F32) for p in _split3(a))
    a_cs_t = a_cs.T
    lane = lax.broadcasted_iota(I32, (lc, LANES), 1)
    low_half = lane < HEAD_DIM

    for g in range(2):
        bg = bm[:, g * SSD_STATE:(g + 1) * SSD_STATE]
        cg = cm[:, g * SSD_STATE:(g + 1) * SSD_STATE].astype(BF16)
        cb = _nt_dot(cg, bg.astype(BF16))
        for pair in range(4):
            e0 = g * 8 + pair * 2
            ps = slice(e0 * HEAD_DIM, (e0 + 2) * HEAD_DIM)
            dt_pair = jnp.where(low_half, dt[:, e0:e0 + 1], dt[:, e0 + 1:e0 + 2])
            xdt = xs[:, ps] * dt_pair
            xdt_t = xdt.T.astype(BF16)
            xdt_b = xdt.astype(BF16)
            for k in range(2):
                e = e0 + k
                acol = a_cs[:, e:e + 1]
                arow = a_cs_t[e:e + 1, :]
                decay = jnp.exp(jnp.where(causal, acol - arow, NEG))
                m = (cb * decay).astype(BF16)
                y_diag = jnp.dot(m, xdt_b[:, k * HEAD_DIM:(k + 1) * HEAD_DIM], preferred_element_type=F32)
                h_prev = h_ref[e]
                y_off = _nt_dot(cg, h_prev.astype(BF16)) * jnp.exp(acol)
                y_scr[:, e * HEAD_DIM:(e + 1) * HEAD_DIM] = y_diag + y_off
                a_last = a_cs[lc - 1:lc, e:e + 1]
                bd = (bg * jnp.exp(a_last - acol)).astype(BF16)
                st = jnp.dot(xdt_t[k * HEAD_DIM:(k + 1) * HEAD_DIM, :], bd, preferred_element_type=F32)
                h_ref[e] = h_prev * jnp.exp(a_last) + st

    y = y_scr[...] + xs * dfull_ref[...]
    gate = y * (z * _sigmoid(z))
    half = SSD_INNER // 2
    for g in range(2):
        gg = gate[:, g * half:(g + 1) * half]
        gg = gg * lax.rsqrt(jnp.mean(gg * gg, axis=-1, keepdims=True) + EPS)
        y_ref[:, g * half:(g + 1) * half] = (gg * nw_ref[:, g * half:(g + 1) * half])[0:n_valid]

    @pl.when(c == n_chunks - 1)
    def _():
        hout_ref[0] = h_ref[...]


def _ssd(xbc, z, dt_raw, conv_state, ssm_state, batch, seq, conv_w, conv_b, dt_bias, a_log, d_skip, norm_w):
    n_valid = min(seq, SSD_CHUNK)
    n_chunks = seq // n_valid
    assert n_valid % SUBLANES == 0 and seq % n_valid == 0
    padl = lambda t: jnp.pad(t.reshape(1, SSD_HEADS), ((0, 0), (0, LANES - SSD_HEADS)))
    cst = jnp.pad(conv_state, ((0, 0), (SUBLANES - 3, 0), (0, 0)))
    tokrow = lambda b, c: (b * n_chunks + c, 0)
    fixed = lambda b, c: (0, 0)
    per_b3 = lambda b, c: (b, 0, 0)
    kern = functools.partial(_ssd_kernel, n_valid=n_valid, n_chunks=n_chunks)
    y, cout, hout = pl.pallas_call(
        kern,
        grid=(batch, n_chunks),
        in_specs=[pl.BlockSpec((n_valid, SSD_CONV_DIM), tokrow),
                  pl.BlockSpec((n_valid, SSD_INNER), tokrow),
                  pl.BlockSpec((n_valid, LANES), tokrow),
                  pl.BlockSpec((1, SUBLANES, SSD_CONV_DIM), per_b3),
                  pl.BlockSpec((1, SSD_HEADS, HEAD_DIM, SSD_STATE), lambda b, c: (b, 0, 0, 0)),
                  pl.BlockSpec((4, SSD_CONV_DIM), fixed),
                  pl.BlockSpec((1, SSD_CONV_DIM), fixed),
                  pl.BlockSpec((1, LANES), fixed),
                  pl.BlockSpec((1, LANES), fixed),
                  pl.BlockSpec((1, SSD_INNER), fixed),
                  pl.BlockSpec((1, SSD_INNER), fixed)],
        out_specs=[pl.BlockSpec((n_valid, SSD_INNER), tokrow),
                   pl.BlockSpec((1, 3, SSD_CONV_DIM), per_b3),
                   pl.BlockSpec((1, SSD_HEADS, HEAD_DIM, SSD_STATE), lambda b, c: (b, 0, 0, 0))],
        out_shape=[jax.ShapeDtypeStruct((batch * seq, SSD_INNER), F32),
                   jax.ShapeDtypeStruct((batch, 3, SSD_CONV_DIM), F32),
                   jax.ShapeDtypeStruct((batch, SSD_HEADS, HEAD_DIM, SSD_STATE), F32)],
        scratch_shapes=[pltpu.VMEM((SSD_CHUNK + SUBLANES, SSD_CONV_DIM), F32),
                        pltpu.VMEM((SSD_HEADS, HEAD_DIM, SSD_STATE), F32),
                        pltpu.VMEM((SSD_CHUNK, SSD_INNER), F32)],
        compiler_params=_cparams(("parallel", "arbitrary")),
        name="ssd",
    )(xbc, z, dt_raw, cst, ssm_state, conv_w, conv_b.reshape(1, SSD_CONV_DIM), padl(dt_bias), padl(a_log),
      jnp.repeat(d_skip, HEAD_DIM).reshape(1, SSD_INNER), norm_w.reshape(1, SSD_INNER))
    return y, cout, hout


def _merge_kernel(x_ref, o0_ref, o1_ref, o2_ref, l0_ref, l1_ref, l2_ref, ys_ref, ga_ref, gs_ref,
                  wab_ref, wsb_ref, bg_ref, wo_ref, n2_ref, wr_ref, br_ref,
                  h_ref, hn_ref, ri_ref, rf_ref, cnt_ref, carry_ref, *scr, class_major):
    step = pl.program_id(0)

    @pl.when(step == 0)
    def _():
        carry_ref[...] = jnp.zeros_like(carry_ref)

    def load(ref, gi):
        if not class_major:
            return ref[...]
        dil = ATTN_GROUPS[gi][1]
        if dil == 1:
            return ref[0, 0]
        rows = ref.shape[1] * ref.shape[2]
        halves = []
        for half in range(GROUP_W // LANES):
            for r in range(dil):
                scr[0][half, pl.ds(r, rows // dil, stride=dil), :] = ref[0, r, :, half * LANES:(half + 1) * LANES]
            halves.append(scr[0][half])
        return jnp.concatenate(halves, axis=1)

    l0, l1, l2 = load(l0_ref, 0), load(l1_ref, 1), load(l2_ref, 2)
    m = jnp.maximum(jnp.maximum(l0, l1), l2)
    w0, w1, w2 = jnp.exp(l0 - m), jnp.exp(l1 - m), jnp.exp(l2 - m)
    y_attn = (w0 * load(o0_ref, 0) + w1 * load(o1_ref, 1) + w2 * load(o2_ref, 2)) / (w0 + w1 + w2)
    pa = jnp.dot(y_attn.astype(BF16), wab_ref[...], preferred_element_type=F32)
    ps = jnp.dot(ys_ref[...].astype(BF16), wsb_ref[...], preferred_element_type=F32)
    merged = _sigmoid(ga_ref[...] + bg_ref[0:1, :]) * pa + _sigmoid(gs_ref[...] + bg_ref[1:2, :]) * ps
    h = x_ref[...] + jnp.dot(merged.astype(BF16), wo_ref[...], preferred_element_type=F32)
    h_ref[...] = h
    hn = h * lax.rsqrt(jnp.mean(h * h, axis=-1, keepdims=True) + EPS) * n2_ref[...]
    hnb = hn.astype(BF16)
    _store_row_tiles(hn_ref, hn)

    logits = jnp.dot(hnb, wr_ref[...], preferred_element_type=F32) + br_ref[...]
    tm = logits.shape[0]
    lane = lax.broadcasted_iota(I32, (tm, LANES), 1)
    big = jnp.int32(LANES)

    def top(vals):
        v = jnp.max(vals, axis=-1, keepdims=True)
        idx = jnp.min(jnp.where(vals == v, lane, big), axis=-1, keepdims=True)
        return v, idx

    is_coarse = (lane >= N_EXPERTS) & (lane < N_EXPERTS + N_GROUPS_E)
    lc = jnp.where(is_coarse, logits, NEG)
    mc, ic = top(lc)
    p_grp = 1.0 / jnp.sum(jnp.exp(lc - mc), axis=-1, keepdims=True)
    lo = (ic - N_EXPERTS) * EXPERTS_PER_GROUP
    lf = jnp.where((lane >= lo) & (lane < lo + EXPERTS_PER_GROUP), logits, NEG)
    v1, i1 = top(lf)
    v2, i2 = top(jnp.where(lane == i1, NEG, lf))
    e2 = jnp.exp(v2 - v1)
    g1 = p_grp / (1.0 + e2)
    g2 = p_grp * e2 / (1.0 + e2)

    oh1 = lane == i1
    oh2 = lane == i2
    cnt = oh1.astype(F32) + oh2.astype(F32)
    r = lax.broadcasted_iota(I32, (tm, tm), 0)
    s = lax.broadcasted_iota(I32, (tm, tm), 1)
    before = jnp.dot((r > s).astype(BF16), cnt.astype(BF16), preferred_element_type=F32) + carry_ref[0:1, :]
    r1 = jnp.sum(jnp.where(oh1, before, 0.0), axis=-1, keepdims=True)
    r2 = jnp.sum(jnp.where(oh2, before, 0.0), axis=-1, keepdims=True)
    new_carry = carry_ref[0:1, :] + jnp.sum(cnt, axis=0, keepdims=True)
    carry_ref[...] = jnp.broadcast_to(new_carry, carry_ref.shape)
    cnt_ref[...] = jnp.broadcast_to(new_carry, cnt_ref.shape).astype(I32)

    ri = jnp.where(lane == 0, i1, jnp.where(lane == 1, i2, 0))
    ri = jnp.where(lane == 2, r1.astype(I32), jnp.where(lane == 3, r2.astype(I32), ri))
    ri_ref[...] = ri
    rf_ref[...] = jnp.where(lane == 0, g1, jnp.where(lane == 1, g2, 0.0))


def _merge(x2d, outs, lses, y_ssd, g_a, g_s, wab, wsb, b_gate, wo, norm2_w, w_router, b_router, class_major):
    n = x2d.shape[0]
    tm = min(ROW_TILE, n)
    row = lambda i: (i, 0)
    fixed = lambda i: (0, 0)
    wide = pl.BlockSpec((tm, D_MODEL), row)
    info = pl.BlockSpec((tm, LANES), row)
    if class_major:
        per_seq = outs[0].shape[2] // tm
        cls = lambda i: (i // per_seq, 0, i % per_seq, 0)
        grps = [pl.BlockSpec((1, d, tm // d, GROUP_W), cls) for _, d in ATTN_GROUPS]
        scratch = [pltpu.VMEM((GROUP_W // LANES, tm, LANES), F32)]
    else:
        grps = [pl.BlockSpec((tm, GROUP_W), row)] * 3
        scratch = []
    return pl.pallas_call(
        functools.partial(_merge_kernel, class_major=class_major),
        grid=(n // tm,),
        in_specs=[wide, *grps, *grps, wide, wide, wide,
                  pl.BlockSpec((GROUP_W, D_MODEL), fixed),
                  pl.BlockSpec((SSD_INNER, D_MODEL), fixed),
                  pl.BlockSpec((2, D_MODEL), fixed),
                  pl.BlockSpec((D_MODEL, D_MODEL), fixed),
                  pl.BlockSpec((1, D_MODEL), fixed),
                  pl.BlockSpec((D_MODEL, LANES), fixed),
                  pl.BlockSpec((1, LANES), fixed)],
        out_specs=[wide, pl.BlockSpec((tm * ROW_CHUNKS, LANES), row), info, info,
                   pl.BlockSpec((SUBLANES, LANES), fixed)],
        out_shape=[jax.ShapeDtypeStruct((n, D_MODEL), F32),
                   jax.ShapeDtypeStruct((n * ROW_CHUNKS, LANES), F32),
                   jax.ShapeDtypeStruct((n, LANES), I32),
                   jax.ShapeDtypeStruct((n, LANES), F32),
                   jax.ShapeDtypeStruct((SUBLANES, LANES), I32)],
        scratch_shapes=[pltpu.VMEM((SUBLANES, LANES), F32)] + scratch,
        compiler_params=_cparams(("arbitrary",)),
        name="merge_out",
    )(x2d, *outs, *lses, y_ssd, g_a, g_s, wab, wsb, b_gate, wo, norm2_w.reshape(1, D_MODEL), w_router, b_router)


GATHER_TILE = 256
INDEX_BATCH = 16


def _wait_rows(like_hbm, dst, sem, rows):
    pltpu.make_async_copy(like_hbm.at[pl.ds(0, rows)], dst, sem).wait()


def _dispatch_kernel(dest_ref, hn_hbm, xs_init_hbm, xs_hbm, sem, *, tm):
    del xs_init_hbm
    i = pl.program_id(0)
    slot = i % 2
    for j0 in range(0, 2 * tm, INDEX_BATCH):
        slots = [dest_ref[i * 2 * tm + j0 + u] for u in range(INDEX_BATCH)]
        for u, dst in enumerate(slots):
            pltpu.make_async_copy(hn_hbm.at[i * tm + (j0 + u) % tm], xs_hbm.at[dst], sem.at[slot]).start()

    @pl.when(i > 0)
    def _():
        _wait_rows(hn_hbm, xs_hbm.at[pl.ds(0, 2 * tm)], sem.at[1 - slot], 2 * tm)

    @pl.when(i == pl.num_programs(0) - 1)
    def _():
        _wait_rows(hn_hbm, xs_hbm.at[pl.ds(0, 2 * tm)], sem.at[slot], 2 * tm)


def _dispatch(hn, dest_tiles, n_slots):
    n = hn.shape[0]
    tm = min(GATHER_TILE, n)
    anyspec = pl.BlockSpec(memory_space=pl.ANY)
    return pl.pallas_call(
        functools.partial(_dispatch_kernel, tm=tm),
        grid_spec=pltpu.PrefetchScalarGridSpec(
            num_scalar_prefetch=1,
            grid=(n // tm,),
            in_specs=[anyspec, anyspec],
            out_specs=anyspec,
            scratch_shapes=[pltpu.SemaphoreType.DMA((2,))]),
        out_shape=jax.ShapeDtypeStruct((n_slots, ROW_CHUNKS, LANES), F32),
        input_output_aliases={2: 0},
        compiler_params=_cparams(("arbitrary",), disable_bounds_checks=True, has_side_effects=True),
        name="dispatch",
    )(dest_tiles, hn, jnp.zeros((n_slots, ROW_CHUNKS, LANES), F32))


def _expert_kernel(te_ref, nu_ref, x_ref, wg_ref, wu_ref, wd_ref, o_ref, wgb, wub, wdb):
    i = pl.program_id(0)
    changed = (i == 0) | (te_ref[i] != te_ref[jnp.maximum(i - 1, 0)])

    @pl.when(changed)
    def _():
        wgb[...] = wg_ref[0].astype(BF16)
        wub[...] = wu_ref[0].astype(BF16)
        wdb[...] = wd_ref[0].astype(BF16)

    @pl.when(i < nu_ref[0])
    def _():
        x = _load_row_tiles(x_ref, 0, EXPERT_TILE).astype(BF16)
        hg = jnp.dot(x, wgb[...], preferred_element_type=F32)
        hu = jnp.dot(x, wub[...], preferred_element_type=F32)
        hb = (hg * _sigmoid(hg)) * hu
        _store_row_tiles(o_ref, jnp.dot(hb.astype(BF16), wdb[...], preferred_element_type=F32))

    @pl.when(i >= nu_ref[0])
    def _():
        o_ref[...] = jnp.zeros_like(o_ref)


def _experts(xs, tile_expert, n_used, w_eg, w_eu, w_ed):
    n_slots = xs.shape[0] // ROW_CHUNKS
    n_tiles = n_slots // EXPERT_TILE
    row = lambda i, te, nu: (i, 0)
    wsel = lambda i, te, nu: (te[i], 0, 0)
    return pl.pallas_call(
        _expert_kernel,
        grid_spec=pltpu.PrefetchScalarGridSpec(
            num_scalar_prefetch=2,
            grid=(n_tiles,),
            in_specs=[pl.BlockSpec((EXPERT_TILE * ROW_CHUNKS, LANES), row),
                      pl.BlockSpec((1, D_MODEL, D_FF), wsel),
                      pl.BlockSpec((1, D_MODEL, D_FF), wsel),
                      pl.BlockSpec((1, D_FF, D_MODEL), wsel)],
            out_specs=pl.BlockSpec((EXPERT_TILE * ROW_CHUNKS, LANES), row),
            scratch_shapes=[pltpu.VMEM((D_MODEL, D_FF), BF16),
                            pltpu.VMEM((D_MODEL, D_FF), BF16),
                            pltpu.VMEM((D_FF, D_MODEL), BF16)]),
        out_shape=jax.ShapeDtypeStruct((n_slots * ROW_CHUNKS, LANES), F32),
        compiler_params=_cparams(("arbitrary",)),
        name="experts",
    )(tile_expert, n_used, xs, w_eg, w_eu, w_ed)


def _final_kernel(dest_ref, h_ref, rf_ref, fw_ref, out_hbm, o_ref, ybuf, sem, *, tm):
    i = pl.program_id(0)
    slot = i % 2

    def gather(tile, into):
        for j0 in range(0, 2 * tm, INDEX_BATCH):
            slots = [dest_ref[tile * 2 * tm + j0 + u] for u in range(INDEX_BATCH)]
            for u, src in enumerate(slots):
                first = pl.multiple_of(src * ROW_CHUNKS, ROW_CHUNKS)
                pltpu.make_async_copy(out_hbm.at[pl.ds(first, ROW_CHUNKS), :],
                                      ybuf.at[into, pl.ds((j0 + u) * ROW_CHUNKS, ROW_CHUNKS), :],
                                      sem.at[into]).start()

    @pl.when(i == 0)
    def _():
        gather(0, 0)

    @pl.when(i + 1 < pl.num_programs(0))
    def _():
        gather(i + 1, 1 - slot)

    pltpu.make_async_copy(out_hbm.at[pl.ds(0, 2 * tm * ROW_CHUNKS), :], ybuf.at[slot], sem.at[slot]).wait()
    g = rf_ref[...]
    moe = (_load_row_tiles(ybuf.at[slot], 0, tm) * g[:, 0:1] + _load_row_tiles(ybuf.at[slot], tm, tm) * g[:, 1:2])
    h = h_ref[...] + moe
    o_ref[...] = h * lax.rsqrt(jnp.mean(h * h, axis=-1, keepdims=True) + EPS) * fw_ref[...]


def _final(h, out, dest_tiles, rf, final_w):
    n = h.shape[0]
    tm = min(GATHER_TILE, n)
    row = lambda i, d: (i, 0)
    wide = pl.BlockSpec((tm, D_MODEL), row)
    return pl.pallas_call(
        functools.partial(_final_kernel, tm=tm),
        grid_spec=pltpu.PrefetchScalarGridSpec(
            num_scalar_prefetch=1,
            grid=(n // tm,),
            in_specs=[wide, pl.BlockSpec((tm, LANES), row), pl.BlockSpec((1, D_MODEL), lambda i, d: (0, 0)),
                      pl.BlockSpec(memory_space=pl.ANY)],
            out_specs=wide,
            scratch_shapes=[pltpu.VMEM((2, 2 * tm * ROW_CHUNKS, LANES), F32), pltpu.SemaphoreType.DMA((2,))]),
        out_shape=jax.ShapeDtypeStruct((n, D_MODEL), F32),
        compiler_params=_cparams(("arbitrary",), disable_bounds_checks=True),
        name="final",
    )(dest_tiles, h, rf, final_w.reshape(1, D_MODEL), out)


def _moe_and_final(h, hn, ri, rf, counts, w_eg, w_eu, w_ed, final_w):
    n = h.shape[0]
    tm = min(GATHER_TILE, n)
    counts = counts[0, :N_EXPERTS]
    padded = ((counts + EXPERT_TILE - 1) // EXPERT_TILE) * EXPERT_TILE
    ends = jnp.cumsum(padded)
    starts = ends - padded
    n_tiles = (2 * n) // EXPERT_TILE + N_EXPERTS
    n_slots = n_tiles * EXPERT_TILE
    experts = jnp.arange(N_EXPERTS, dtype=I32)
    start_of = jnp.sum(jnp.where(ri[:, 0:2, None] == experts, starts, 0), axis=-1)
    dest = start_of + ri[:, 2:4]
    dest_tiles = dest.reshape(n // tm, tm, 2).transpose(0, 2, 1).reshape(-1)
    tile_start = jnp.arange(n_tiles, dtype=I32) * EXPERT_TILE
    tile_expert = jnp.minimum(jnp.sum((ends[None, :] <= tile_start[:, None]).astype(I32), axis=1), N_EXPERTS - 1)
    n_used = (ends[-1] // EXPERT_TILE).astype(I32).reshape(1)
    xs = _dispatch(hn.reshape(n, ROW_CHUNKS, LANES), dest_tiles, n_slots)
    out = _experts(xs.reshape(n_slots * ROW_CHUNKS, LANES), tile_expert, n_used, w_eg, w_eu, w_ed)
    return _final(h, out, dest_tiles, rf, final_w)


def _layer(x, pos0, kv_bufs, conv_state, ssm_state, p):
    batch, seq, _ = x.shape
    n = batch * seq
    x2d = x.reshape(n, D_MODEL)
    prompt = kv_bufs is None
    q, k, v, z, xbc, g_a, g_s, dt_raw = _inproj(x2d, batch, seq, pos0, p["norm1_w"], p["w_in"], prompt)

    outs, lses, new_kv = [], [], []
    for gi, (window, dil) in enumerate(ATTN_GROUPS):
        if prompt:
            o, l = _attn_prompt(q[gi], k[gi], v[gi], dil)
            keep = min(window, seq) // dil
            tail = lambda t: t[:, :, seq // dil - keep:, :].transpose(0, 2, 1, 3).reshape(batch, keep * dil, GROUP_W)
            nbuf = jnp.stack([tail(k[gi]), tail(v[gi])], axis=2).reshape(
                batch, keep * dil, 2, HEADS_PER_GROUP, HEAD_DIM)
        else:
            o, l, nbuf = _attn_step(q, k, v, kv_bufs[gi], batch, seq, gi, dil)
        outs.append(o)
        lses.append(l)
        new_kv.append(nbuf)

    if conv_state is None:
        conv_state = jnp.zeros((batch, 3, SSD_CONV_DIM), F32)
        ssm_state = jnp.zeros((batch, SSD_HEADS, HEAD_DIM, SSD_STATE), F32)
    y_ssd, new_conv, new_ssm = _ssd(xbc, z, dt_raw, conv_state, ssm_state, batch, seq, p["conv_w"], p["conv_b"],
                                    p["dt_bias"], p["A_log"], p["D_skip"], p["ssd_norm_w"])

    h, hn, ri, rf, counts = _merge(x2d, outs, lses, y_ssd, g_a, g_s, p["w_attn_br"], p["w_ssd_br"], p["b_gate"],
                                   p["w_out"], p["norm2_w"], p["w_router"], p["b_router"], prompt)
    y = _moe_and_final(h, hn, ri, rf, counts, p["w_eg"], p["w_eu"], p["w_ed"], p["final_norm_w"])
    return y.reshape(batch, seq, D_MODEL), new_kv, new_conv, new_ssm


def _pack_w_in(w_in):
    offs = (0, 768, 1536, 2304, 3328, 4864, 4880, 5904, 6928)
    q, k, v, z, xbc, dt, g_a, g_s = (w_in[:, offs[i]:offs[i + 1]] for i in range(8))
    dt = jnp.pad(dt, ((0, 0), (0, LANES - SSD_HEADS)))
    return jnp.concatenate([q, k, v, z, xbc, g_a, g_s, dt], axis=1).astype(BF16)


def kernel(x_prompt, x_sample, cache_kv_w128, cache_kv_w512, cache_kv_w2048, state_conv, state_ssm, norm1_w, w_in, w_attn_br, w_ssd_br, b_gate, w_out, conv_w, conv_b, dt_bias, A_log, D_skip, ssd_norm_w, norm2_w, w_router_coarse, b_router_coarse, w_router_fine, b_router_fine, w_expert_gate, w_expert_up, w_expert_down, final_norm_w):
    depth = norm1_w.shape[0]
    assert depth == 1, "the final norm is fused into the layer's last kernel"
    l = 0
    rpad = LANES - N_EXPERTS - N_GROUPS_E
    p = dict(
        norm1_w=norm1_w[l], w_in=_pack_w_in(w_in[l]),
        w_attn_br=w_attn_br[l].astype(BF16), w_ssd_br=w_ssd_br[l].astype(BF16), b_gate=b_gate[l],
        w_out=w_out[l].astype(BF16), conv_w=conv_w[l], conv_b=conv_b[l], dt_bias=dt_bias[l], A_log=A_log[l],
        D_skip=D_skip[l], ssd_norm_w=ssd_norm_w[l], norm2_w=norm2_w[l],
        w_router=jnp.pad(jnp.concatenate([w_router_fine[l], w_router_coarse[l]], axis=1),
                         ((0, 0), (0, rpad))).astype(BF16),
        b_router=jnp.pad(jnp.concatenate([b_router_fine[l], b_router_coarse[l]]), (0, rpad)).reshape(1, LANES),
        w_eg=w_expert_gate[l], w_eu=w_expert_up[l], w_ed=w_expert_down[l], final_norm_w=final_norm_w,
    )
    y_p, kv_p, c_p, st_p = _layer(x_prompt, 0, None, None, None, p)
    bufs = (cache_kv_w128[l], cache_kv_w512[l], cache_kv_w2048[l])
    y_s, kv_s, c_s, st_s = _layer(x_sample, PAST_LEN, bufs, state_conv[l], state_ssm[l], p)
    lead = lambda t: t[None]
    return (y_p, y_s, lead(kv_p[0]), lead(kv_p[1]), lead(kv_p[2]), lead(c_p), lead(st_p),
            lead(kv_s[0]), lead(kv_s[1]), lead(kv_s[2]), lead(c_s), lead(st_s))
```

```python
import functools

import jax
import jax.numpy as jnp
from jax import lax
from jax.experimental import pallas as pl
from jax.experimental.pallas import tpu as pltpu

F32 = jnp.float32
BF16 = jnp.bfloat16
I32 = jnp.int32

D_MODEL = 1024
HEAD_DIM = 64
ATTN_GROUPS = ((128, 1), (512, 4), (2048, 16))
SPAN = 128
HEADS_PER_GROUP = 4
GROUP_W = HEADS_PER_GROUP * HEAD_DIM
ATTN_W = GROUP_W * len(ATTN_GROUPS)
ROPE_THETA = 10000.0
PAST_LEN = 8192
SSD_INNER = 1024
SSD_HEADS = 16
SSD_STATE = 128
SSD_CONV_DIM = 1536
SSD_CHUNK = 128
N_GROUPS_E = 4
EXPERTS_PER_GROUP = 8
N_EXPERTS = 32
D_FF = 512
EPS = 1e-6
LANES = 128
SUBLANES = 8
NEG = -1e30

_OFF_Q, _OFF_K, _OFF_V, _OFF_Z, _OFF_XBC, _OFF_GA, _OFF_GS, _OFF_DT = 0, 768, 1536, 2304, 3328, 4864, 5888, 6912
_W_IN_COLS = 7040

ROW_TILE = 256
EXPERT_TILE = 256
VMEM_LIMIT = 56 * 1024 * 1024


def _cparams(sem, **kw):
    return pltpu.CompilerParams(dimension_semantics=sem, vmem_limit_bytes=VMEM_LIMIT, **kw)


def _sigmoid(x):
    return 1.0 / (1.0 + jnp.exp(-x))


ROW_CHUNKS = D_MODEL // LANES


def _store_row_tiles(ref, val):
    rows = val.shape[0]
    for c in range(ROW_CHUNKS):
        ref[pl.ds(c, rows, stride=ROW_CHUNKS), :] = val[:, c * LANES:(c + 1) * LANES]


def _load_row_tiles(ref, first, rows):
    return jnp.concatenate(
        [ref[pl.ds(first * ROW_CHUNKS + c, rows, stride=ROW_CHUNKS), :] for c in range(ROW_CHUNKS)], axis=1)


def _nt_dot(a, b):
    return lax.dot_general(a, b, (((1,), (1,)), ((), ())), preferred_element_type=F32)


def _store_by_class(o_ref, scr_ref, val, dil):
    if dil == 1:
        o_ref[0, 0] = val
        return
    rows = val.shape[0]
    for half in range(GROUP_W // LANES):
        lanes = slice(half * LANES, (half + 1) * LANES)
        scr_ref[half] = val[:, lanes]
        for r in range(dil):
            o_ref[0, r, :, lanes] = scr_ref[half, pl.ds(r, rows // dil, stride=dil), :]


def _inproj_kernel(x_ref, nw_ref, w_ref, cos_ref, sin_ref, *refs, class_major):
    if class_major:
        qkv_refs, (z_ref, xbc_ref, ga_ref, gs_ref, dt_ref, scr_ref) = refs[:9], refs[9:]
    else:
        qkv_refs, (z_ref, xbc_ref, ga_ref, gs_ref, dt_ref) = refs[:3], refs[3:]
    x = x_ref[...]
    xn = x * lax.rsqrt(jnp.mean(x * x, axis=-1, keepdims=True) + EPS) * nw_ref[...]
    xb = xn.astype(BF16)
    cos = cos_ref[...]
    sin = sin_ref[...]
    lane = lax.broadcasted_iota(I32, cos.shape, 1)
    first_half = (lane % HEAD_DIM) < (HEAD_DIM // 2)

    def mm(lo, hi):
        return jnp.dot(xb, w_ref[:, lo:hi], preferred_element_type=F32)

    def rope(uc):
        ur = jnp.where(first_half, pltpu.roll(uc, LANES - HEAD_DIM // 2, 1), pltpu.roll(uc, HEAD_DIM // 2, 1))
        return uc * cos + ur * sin

    def emit(which, base, roped):
        u = mm(base, base + ATTN_W)
        chunks = [u[:, c * LANES:(c + 1) * LANES] for c in range(ATTN_W // LANES)]
        if roped:
            chunks = [rope(uc) for uc in chunks]
        if class_major:
            for gi, (_, dil) in enumerate(ATTN_GROUPS):
                val = jnp.concatenate(chunks[2 * gi:2 * gi + 2], axis=1)
                _store_by_class(qkv_refs[3 * which + gi], scr_ref, val, dil)
        else:
            for c, uc in enumerate(chunks):
                qkv_refs[which][:, c * LANES:(c + 1) * LANES] = uc

    emit(0, _OFF_Q, True)
    emit(1, _OFF_K, True)
    emit(2, _OFF_V, False)
    z_ref[...] = mm(_OFF_Z, _OFF_XBC)
    xbc_ref[...] = mm(_OFF_XBC, _OFF_GA)
    ga_ref[...] = mm(_OFF_GA, _OFF_GS)
    gs_ref[...] = mm(_OFF_GS, _OFF_DT)
    dt_ref[...] = mm(_OFF_DT, _W_IN_COLS)


def _rope_tables(pos):
    half = HEAD_DIM // 2
    inv_freq = ROPE_THETA ** (-jnp.arange(half, dtype=F32) * (2.0 / HEAD_DIM))
    ang = pos.astype(F32)[:, None] * inv_freq[None, :]
    ang = jnp.tile(ang, (1, LANES // half))
    lane = jnp.arange(LANES)
    sign = jnp.where((lane % HEAD_DIM) < half, -1.0, 1.0).astype(F32)
    return jnp.cos(ang), jnp.sin(ang) * sign[None, :]


def _inproj(x2d, batch, seq_len, pos0, norm_w, w_packed, class_major):
    n = x2d.shape[0]
    tm = min(ROW_TILE, n)
    cos, sin = _rope_tables(pos0 + jnp.arange(seq_len, dtype=I32))
    if seq_len < tm:
        cos = jnp.tile(cos, (tm // seq_len, 1))
        sin = jnp.tile(sin, (tm // seq_len, 1))
    tab_blocks = cos.shape[0] // tm
    row = lambda i: (i, 0)
    fixed = lambda i: (0, 0)
    tab = lambda i: (i % tab_blocks, 0)
    widths = (SSD_INNER, SSD_CONV_DIM, D_MODEL, D_MODEL, LANES)
    out_specs = [pl.BlockSpec((tm, w), row) for w in widths]
    out_shape = [jax.ShapeDtypeStruct((n, w), F32) for w in widths]
    scratch = []
    if class_major:
        assert seq_len % tm == 0
        per_seq = seq_len // tm
        cls = lambda i: (i // per_seq, 0, i % per_seq, 0)
        dils = [d for _, d in ATTN_GROUPS] * 3
        out_specs = [pl.BlockSpec((1, d, tm // d, GROUP_W), cls) for d in dils] + out_specs
        out_shape = [jax.ShapeDtypeStruct((batch, d, seq_len // d, GROUP_W), F32) for d in dils] + out_shape
        scratch = [pltpu.VMEM((GROUP_W // LANES, tm, LANES), F32)]
    else:
        out_specs = [pl.BlockSpec((tm, ATTN_W), row)] * 3 + out_specs
        out_shape = [jax.ShapeDtypeStruct((n, ATTN_W), F32)] * 3 + out_shape
    res = pl.pallas_call(
        functools.partial(_inproj_kernel, class_major=class_major),
        grid=(n // tm,),
        in_specs=[pl.BlockSpec((tm, D_MODEL), row),
                  pl.BlockSpec((1, D_MODEL), fixed),
                  pl.BlockSpec((D_MODEL, _W_IN_COLS), fixed, pipeline_mode=pl.Buffered(1)),
                  pl.BlockSpec((tm, LANES), tab),
                  pl.BlockSpec((tm, LANES), tab)],
        out_specs=out_specs,
        out_shape=out_shape,
        scratch_shapes=scratch,
        compiler_params=_cparams(("parallel",)),
        name="inproj",
    )(x2d, norm_w.reshape(1, D_MODEL), w_packed, cos, sin)
    if class_major:
        return (tuple(res[0:3]), tuple(res[3:6]), tuple(res[6:9])) + tuple(res[9:])
    return tuple(res)


ATTN_QBLOCKS = 4


def _attn_prompt_kernel(q_ref, kp_ref, kc_ref, vp_ref, vc_ref, o_ref, l_ref, *, qblocks):
    c = pl.program_id(2)
    i = lax.broadcasted_iota(I32, (SPAN, SPAN), 0)
    j = lax.broadcasted_iota(I32, (SPAN, SPAN), 1)
    upper = j >= i
    lower = j <= i
    kcur = kc_ref[0, 0].astype(BF16)
    vcur = vc_ref[0, 0].astype(BF16)
    kprev = kp_ref[0, 0].astype(BF16)
    vprev = vp_ref[0, 0].astype(BF16)
    for b in range(qblocks):
        rows = slice(b * SPAN, (b + 1) * SPAN)
        q = q_ref[0, 0, rows, :] * (HEAD_DIM ** -0.5)
        if b == 0:
            kp, vp, valid_prev = kprev, vprev, upper & (c > 0)
        else:
            prows = slice((b - 1) * SPAN, b * SPAN)
            kp, vp, valid_prev = kcur[prows], vcur[prows], upper
        kc, vc = kcur[rows], vcur[rows]
        for h in range(HEADS_PER_GROUP):
            hs = slice(h * HEAD_DIM, (h + 1) * HEAD_DIM)
            qh = q[:, hs].astype(BF16)
            sp = jnp.where(valid_prev, _nt_dot(qh, kp[:, hs]), NEG)
            sc = jnp.where(lower, _nt_dot(qh, kc[:, hs]), NEG)
            m = jnp.maximum(jnp.max(sp, axis=-1, keepdims=True), jnp.max(sc, axis=-1, keepdims=True))
            pp = jnp.exp(sp - m)
            pc = jnp.exp(sc - m)
            den = jnp.sum(pp, axis=-1, keepdims=True) + jnp.sum(pc, axis=-1, keepdims=True)
            acc = (jnp.dot(pp.astype(BF16), vp[:, hs], preferred_element_type=F32)
                   + jnp.dot(pc.astype(BF16), vc[:, hs], preferred_element_type=F32))
            o_ref[0, 0, rows, hs] = acc / den
            l_ref[0, 0, rows, hs] = jnp.broadcast_to(m + jnp.log(den), (SPAN, HEAD_DIM))


def _attn_prompt(q, k, v, dil):
    batch, _, n_cls, _ = q.shape
    qblocks = min(ATTN_QBLOCKS, n_cls // SPAN)
    tq = qblocks * SPAN
    assert n_cls % tq == 0
    cur = lambda b, r, c: (b, r, c, 0)
    prev = lambda b, r, c: (b, r, jnp.maximum(c * qblocks - 1, 0), 0)
    big = pl.BlockSpec((1, 1, tq, GROUP_W), cur)
    small = pl.BlockSpec((1, 1, SPAN, GROUP_W), prev)
    return pl.pallas_call(
        functools.partial(_attn_prompt_kernel, qblocks=qblocks),
        grid=(batch, dil, n_cls // tq),
        in_specs=[big, small, big, small, big],
        out_specs=[big, big],
        out_shape=[jax.ShapeDtypeStruct(q.shape, F32)] * 2,
        compiler_params=_cparams(("parallel", "parallel", "arbitrary")),
        name=f"attn_prompt_d{dil}",
    )(q, k, k, v, v)


def _attn_step_kernel(q_ref, kn_ref, vn_ref, buf_ref, o_ref, l_ref, nbuf_ref, *, buf_len, dil, n_new):
    q = q_ref[0] * (HEAD_DIM ** -0.5)
    kn, vn = kn_ref[0], vn_ref[0]
    buf = buf_ref[0]
    t = lax.broadcasted_iota(I32, (n_new, buf_len), 0)
    i = lax.broadcasted_iota(I32, (n_new, buf_len), 1)
    delta = buf_len + t - i
    valid_b = delta <= dil * SPAN
    tn = lax.broadcasted_iota(I32, (n_new, LANES), 0)
    un = lax.broadcasted_iota(I32, (n_new, LANES), 1)
    dn = tn - un
    valid_n = (dn >= 0) & (un < n_new)
    if dil > 1:
        valid_b = valid_b & ((delta & (dil - 1)) == 0)
        valid_n = valid_n & ((dn & (dil - 1)) == 0)
    zpad = jnp.zeros((LANES - n_new, HEAD_DIM), F32)
    for h in range(HEADS_PER_GROUP):
        hs = slice(h * HEAD_DIM, (h + 1) * HEAD_DIM)
        vs = slice(GROUP_W + h * HEAD_DIM, GROUP_W + (h + 1) * HEAD_DIM)
        qh = q[:, hs].astype(BF16)
        knp = jnp.concatenate([kn[:, hs], zpad], axis=0).astype(BF16)
        vnp = jnp.concatenate([vn[:, hs], zpad], axis=0).astype(BF16)
        sb = jnp.where(valid_b, _nt_dot(qh, buf[:, hs].astype(BF16)), NEG)
        sn = jnp.where(valid_n, _nt_dot(qh, knp), NEG)
        m = jnp.maximum(jnp.max(sb, axis=-1, keepdims=True), jnp.max(sn, axis=-1, keepdims=True))
        pb = jnp.exp(sb - m)
        pn = jnp.exp(sn - m)
        den = jnp.sum(pb, axis=-1, keepdims=True) + jnp.sum(pn, axis=-1, keepdims=True)
        acc = (jnp.dot(pb.astype(BF16), buf[:, vs].astype(BF16), preferred_element_type=F32)
               + jnp.dot(pn.astype(BF16), vnp, preferred_element_type=F32))
        o_ref[0, :, hs] = acc / den
        l_ref[0, :, hs] = jnp.broadcast_to(m + jnp.log(den), (n_new, HEAD_DIM))
    nbuf_ref[0, 0:buf_len - n_new, :] = buf[n_new:, :]
    nbuf_ref[0, buf_len - n_new:buf_len, 0:GROUP_W] = kn
    nbuf_ref[0, buf_len - n_new:buf_len, GROUP_W:2 * GROUP_W] = vn


def _attn_step(q, k, v, buf, batch, n_new, gi, dil):
    buf_len = buf.shape[1]
    assert dil & (dil - 1) == 0 and buf_len >= dil * SPAN and n_new % SUBLANES == 0
    view = lambda t: t.reshape(batch, n_new, ATTN_W)
    tok = pl.BlockSpec((1, n_new, GROUP_W), lambda b: (b, 0, gi))
    full = pl.BlockSpec((1, buf_len, 2 * GROUP_W), lambda b: (b, 0, 0))
    osp = pl.BlockSpec((1, n_new, GROUP_W), lambda b: (b, 0, 0))
    o, l, nbuf = pl.pallas_call(
        functools.partial(_attn_step_kernel, buf_len=buf_len, dil=dil, n_new=n_new),
        grid=(batch,),
        in_specs=[tok, tok, tok, full],
        out_specs=[osp, osp, full],
        out_shape=[jax.ShapeDtypeStruct((batch, n_new, GROUP_W), F32)] * 2
        + [jax.ShapeDtypeStruct((batch, buf_len, 2 * GROUP_W), F32)],
        compiler_params=_cparams(("parallel",)),
        name=f"attn_step_d{dil}",
    )(view(q), view(k), view(v), buf.reshape(batch, buf_len, 2 * GROUP_W))
    return (o.reshape(batch * n_new, GROUP_W), l.reshape(batch * n_new, GROUP_W),
            nbuf.reshape(batch, buf_len, 2, HEADS_PER_GROUP, HEAD_DIM))


def _split3(a):
    a1 = a.astype(BF16)
    r1 = a - a1.astype(F32)
    a2 = r1.astype(BF16)
    a3 = (r1 - a2.astype(F32)).astype(BF16)
    return a1, a2, a3


def _ssd_kernel(xbc_ref, z_ref, dt_ref, cst_ref, h0_ref, cw_ref, cb_ref, dtb_ref, alog_ref, dfull_ref, nw_ref,
                y_ref, cout_ref, hout_ref, xpad_ref, h_ref, y_scr, *, n_valid, n_chunks):
    c = pl.program_id(1)
    lc = SSD_CHUNK
    pad = SUBLANES

    @pl.when(c == 0)
    def _():
        xpad_ref[0:pad, :] = cst_ref[0]
        h_ref[...] = h0_ref[0]

    @pl.when(c > 0)
    def _():
        xpad_ref[0:pad, :] = xpad_ref[lc:lc + pad, :]

    if n_valid == lc:
        xpad_ref[pad:pad + lc, :] = xbc_ref[...]
        z = z_ref[...]
        dtr = dt_ref[...]
    else:
        fill = lambda w: jnp.zeros((lc - n_valid, w), F32)
        xpad_ref[pad:pad + n_valid, :] = xbc_ref[...]
        xpad_ref[pad + n_valid:pad + lc, :] = fill(SSD_CONV_DIM)
        z = jnp.concatenate([z_ref[...], fill(SSD_INNER)], axis=0)
        dtr = jnp.concatenate([dt_ref[...], fill(LANES)], axis=0)

    xc = cb_ref[...]
    for tap in range(4):
        xc = xc + xpad_ref[pl.ds(pad - 3 + tap, lc), :] * cw_ref[tap:tap + 1, :]
    xc = xc * _sigmoid(xc)
    cout_ref[0] = xpad_ref[pl.ds(pad + n_valid - 3, 3), :]

    xs = xc[:, :SSD_INNER]
    bm = xc[:, SSD_INNER:SSD_INNER + 2 * SSD_STATE]
    cm = xc[:, SSD_INNER + 2 * SSD_STATE:]

    row = lax.broadcasted_iota(I32, (lc, lc), 0)
    col = lax.broadcasted_iota(I32, (lc, lc), 1)
    causal = row >= col
    dtv = dtr + dtb_ref[...]
    dt = jnp.maximum(dtv, 0.0) + jnp.log1p(jnp.exp(-jnp.abs(dtv)))
    if n_valid < lc:
        dt = jnp.where(row < n_valid, dt, 0.0)
    a = dt * (-jnp.exp(alog_ref[...]))
    tri = causal.astype(BF16)
    a_cs = sum(jnp.dot(tri, p, preferred_element_type=F32) for p in _split3(a))
    a_cs_t = a_cs.T
    lane = lax.broadcasted_iota(I32, (lc, LANES), 1)
    low_half = lane < HEAD_DIM

    for g in range(2):
        bg = bm[:, g * SSD_STATE:(g + 1) * SSD_STATE]
        cg = cm[:, g * SSD_STATE:(g + 1) * SSD_STATE].astype(BF16)
        cb = _nt_dot(cg, bg.astype(BF16))
        for pair in range(4):
            e0 = g * 8 + pair * 2
            ps = slice(e0 * HEAD_DIM, (e0 + 2) * HEAD_DIM)
            dt_pair = jnp.where(low_half, dt[:, e0:e0 + 1], dt[:, e0 + 1:e0 + 2])
            xdt = xs[:, ps] * dt_pair
            xdt_t = xdt.T.astype(BF16)
            xdt_b = xdt.astype(BF16)
            for k in range(2):
                e = e0 + k
                acol = a_cs[:, e:e + 1]
                arow = a_cs_t[e:e + 1, :]
                decay = jnp.exp(jnp.where(causal, acol - arow, NEG))
                m = (cb * decay).astype(BF16)
                y_diag = jnp.dot(m, xdt_b[:, k * HEAD_DIM:(k + 1) * HEAD_DIM], preferred_element_type=F32)
                h_prev = h_ref[e]
                y_off = _nt_dot(cg, h_prev.astype(BF16)) * jnp.exp(acol)
                y_scr[:, e * HEAD_DIM:(e + 1) * HEAD_DIM] = y_diag + y_off
                a_last = a_cs[lc - 1:lc, e:e + 1]
                bd = (bg * jnp.exp(a_last - acol)).astype(BF16)
                st = jnp.dot(xdt_t[k * HEAD_DIM:(k + 1) * HEAD_DIM, :], bd, preferred_element_type=F32)
                h_ref[e] = h_prev * jnp.exp(a_last) + st

    y = y_scr[...] + xs * dfull_ref[...]
    gate = y * (z * _sigmoid(z))
    half = SSD_INNER // 2
    for g in range(2):
        gg = gate[:, g * half:(g + 1) * half]
        gg = gg * lax.rsqrt(jnp.mean(gg * gg, axis=-1, keepdims=True) + EPS)
        y_ref[:, g * half:(g + 1) * half] = (gg * nw_ref[:, g * half:(g + 1) * half])[0:n_valid]

    @pl.when(c == n_chunks - 1)
    def _():
        hout_ref[0] = h_ref[...]


def _ssd(xbc, z, dt_raw, conv_state, ssm_state, batch, seq, conv_w, conv_b, dt_bias, a_log, d_skip, norm_w):
    n_valid = min(seq, SSD_CHUNK)
    n_chunks = seq // n_valid
    assert n_valid % SUBLANES == 0 and seq % n_valid == 0
    padl = lambda t: jnp.pad(t.reshape(1, SSD_HEADS), ((0, 0), (0, LANES - SSD_HEADS)))
    cst = jnp.pad(conv_state, ((0, 0), (SUBLANES - 3, 0), (0, 0)))
    tokrow = lambda b, c: (b * n_chunks + c, 0)
    fixed = lambda b, c: (0, 0)
    per_b3 = lambda b, c: (b, 0, 0)
    kern = functools.partial(_ssd_kernel, n_valid=n_valid, n_chunks=n_chunks)
    y, cout, hout = pl.pallas_call(
        kern,
        grid=(batch, n_chunks),
        in_specs=[pl.BlockSpec((n_valid, SSD_CONV_DIM), tokrow),
                  pl.BlockSpec((n_valid, SSD_INNER), tokrow),
                  pl.BlockSpec((n_valid, LANES), tokrow),
                  pl.BlockSpec((1, SUBLANES, SSD_CONV_DIM), per_b3),
                  pl.BlockSpec((1, SSD_HEADS, HEAD_DIM, SSD_STATE), lambda b, c: (b, 0, 0, 0)),
                  pl.BlockSpec((4, SSD_CONV_DIM), fixed),
                  pl.BlockSpec((1, SSD_CONV_DIM), fixed),
                  pl.BlockSpec((1, LANES), fixed),
                  pl.BlockSpec((1, LANES), fixed),
                  pl.BlockSpec((1, SSD_INNER), fixed),
                  pl.BlockSpec((1, SSD_INNER), fixed)],
        out_specs=[pl.BlockSpec((n_valid, SSD_INNER), tokrow),
                   pl.BlockSpec((1, 3, SSD_CONV_DIM), per_b3),
                   pl.BlockSpec((1, SSD_HEADS, HEAD_DIM, SSD_STATE), lambda b, c: (b, 0, 0, 0))],
        out_shape=[jax.ShapeDtypeStruct((batch * seq, SSD_INNER), F32),
                   jax.ShapeDtypeStruct((batch, 3, SSD_CONV_DIM), F32),
                   jax.ShapeDtypeStruct((batch, SSD_HEADS, HEAD_DIM, SSD_STATE), F32)],
        scratch_shapes=[pltpu.VMEM((SSD_CHUNK + SUBLANES, SSD_CONV_DIM), F32),
                        pltpu.VMEM((SSD_HEADS, HEAD_DIM, SSD_STATE), F32),
                        pltpu.VMEM((SSD_CHUNK, SSD_INNER), F32)],
        compiler_params=_cparams(("parallel", "arbitrary")),
        name="ssd",
    )(xbc, z, dt_raw, cst, ssm_state, conv_w, conv_b.reshape(1, SSD_CONV_DIM), padl(dt_bias), padl(a_log),
      jnp.repeat(d_skip, HEAD_DIM).reshape(1, SSD_INNER), norm_w.reshape(1, SSD_INNER))
    return y, cout, hout


def _merge_kernel(x_ref, o0_ref, o1_ref, o2_ref, l0_ref, l1_ref, l2_ref, ys_ref, ga_ref, gs_ref,
                  wab_ref, wsb_ref, bg_ref, wo_ref, n2_ref, wr_ref, br_ref,
                  h_ref, hn_ref, ri_ref, rf_ref, cnt_ref, carry_ref, *scr, class_major):
    step = pl.program_id(0)

    @pl.when(step == 0)
    def _():
        carry_ref[...] = jnp.zeros_like(carry_ref)

    def load(ref, gi):
        if not class_major:
            return ref[...]
        dil = ATTN_GROUPS[gi][1]
        if dil == 1:
            return ref[0, 0]
        rows = ref.shape[1] * ref.shape[2]
        halves = []
        for half in range(GROUP_W // LANES):
            for r in range(dil):
                scr[0][half, pl.ds(r, rows // dil, stride=dil), :] = ref[0, r, :, half * LANES:(half + 1) * LANES]
            halves.append(scr[0][half])
        return jnp.concatenate(halves, axis=1)

    l0, l1, l2 = load(l0_ref, 0), load(l1_ref, 1), load(l2_ref, 2)
    m = jnp.maximum(jnp.maximum(l0, l1), l2)
    w0, w1, w2 = jnp.exp(l0 - m), jnp.exp(l1 - m), jnp.exp(l2 - m)
    y_attn = (w0 * load(o0_ref, 0) + w1 * load(o1_ref, 1) + w2 * load(o2_ref, 2)) / (w0 + w1 + w2)
    pa = jnp.dot(y_attn.astype(BF16), wab_ref[...], preferred_element_type=F32)
    ps = jnp.dot(ys_ref[...].astype(BF16), wsb_ref[...], preferred_element_type=F32)
    merged = _sigmoid(ga_ref[...] + bg_ref[0:1, :]) * pa + _sigmoid(gs_ref[...] + bg_ref[1:2, :]) * ps
    h = x_ref[...] + jnp.dot(merged.astype(BF16), wo_ref[...], preferred_element_type=F32)
    h_ref[...] = h
    hn = h * lax.rsqrt(jnp.mean(h * h, axis=-1, keepdims=True) + EPS) * n2_ref[...]
    hnb = hn.astype(BF16)
    _store_row_tiles(hn_ref, hn)

    logits = jnp.dot(hnb, wr_ref[...], preferred_element_type=F32) + br_ref[...]
    tm = logits.shape[0]
    lane = lax.broadcasted_iota(I32, (tm, LANES), 1)
    big = jnp.int32(LANES)

    def top(vals):
        v = jnp.max(vals, axis=-1, keepdims=True)
        idx = jnp.min(jnp.where(vals == v, lane, big), axis=-1, keepdims=True)
        return v, idx

    is_coarse = (lane >= N_EXPERTS) & (lane < N_EXPERTS + N_GROUPS_E)
    lc = jnp.where(is_coarse, logits, NEG)
    mc, ic = top(lc)
    p_grp = 1.0 / jnp.sum(jnp.exp(lc - mc), axis=-1, keepdims=True)
    lo = (ic - N_EXPERTS) * EXPERTS_PER_GROUP
    lf = jnp.where((lane >= lo) & (lane < lo + EXPERTS_PER_GROUP), logits, NEG)
    v1, i1 = top(lf)
    v2, i2 = top(jnp.where(lane == i1, NEG, lf))
    e2 = jnp.exp(v2 - v1)
    g1 = p_grp / (1.0 + e2)
    g2 = p_grp * e2 / (1.0 + e2)

    oh1 = lane == i1
    oh2 = lane == i2
    cnt = oh1.astype(F32) + oh2.astype(F32)
    r = lax.broadcasted_iota(I32, (tm, tm), 0)
    s = lax.broadcasted_iota(I32, (tm, tm), 1)
    before = jnp.dot((r > s).astype(BF16), cnt.astype(BF16), preferred_element_type=F32) + carry_ref[0:1, :]
    r1 = jnp.sum(jnp.where(oh1, before, 0.0), axis=-1, keepdims=True)
    r2 = jnp.sum(jnp.where(oh2, before, 0.0), axis=-1, keepdims=True)
    new_carry = carry_ref[0:1, :] + jnp.sum(cnt, axis=0, keepdims=True)
    carry_ref[...] = jnp.broadcast_to(new_carry, carry_ref.shape)
    cnt_ref[...] = jnp.broadcast_to(new_carry, cnt_ref.shape).astype(I32)

    ri = jnp.where(lane == 0, i1, jnp.where(lane == 1, i2, 0))
    ri = jnp.where(lane == 2, r1.astype(I32), jnp.where(lane == 3, r2.astype(I32), ri))
    ri_ref[...] = ri
    rf_ref[...] = jnp.where(lane == 0, g1, jnp.where(lane == 1, g2, 0.0))


def _merge(x2d, outs, lses, y_ssd, g_a, g_s, wab, wsb, b_gate, wo, norm2_w, w_router, b_router, class_major):
    n = x2d.shape[0]
    tm = min(ROW_TILE, n)
    row = lambda i: (i, 0)
    fixed = lambda i: (0, 0)
    wide = pl.BlockSpec((tm, D_MODEL), row)
    info = pl.BlockSpec((tm, LANES), row)
    if class_major:
        per_seq = outs[0].shape[2] // tm
        cls = lambda i: (i // per_seq, 0, i % per_seq, 0)
        grps = [pl.BlockSpec((1, d, tm // d, GROUP_W), cls) for _, d in ATTN_GROUPS]
        scratch = [pltpu.VMEM((GROUP_W // LANES, tm, LANES), F32)]
    else:
        grps = [pl.BlockSpec((tm, GROUP_W), row)] * 3
        scratch = []
    return pl.pallas_call(
        functools.partial(_merge_kernel, class_major=class_major),
        grid=(n // tm,),
        in_specs=[wide, *grps, *grps, wide, wide, wide,
                  pl.BlockSpec((GROUP_W, D_MODEL), fixed),
                  pl.BlockSpec((SSD_INNER, D_MODEL), fixed),
                  pl.BlockSpec((2, D_MODEL), fixed),
                  pl.BlockSpec((D_MODEL, D_MODEL), fixed),
                  pl.BlockSpec((1, D_MODEL), fixed),
                  pl.BlockSpec((D_MODEL, LANES), fixed),
                  pl.BlockSpec((1, LANES), fixed)],
        out_specs=[wide, pl.BlockSpec((tm * ROW_CHUNKS, LANES), row), info, info,
                   pl.BlockSpec((SUBLANES, LANES), fixed)],
        out_shape=[jax.ShapeDtypeStruct((n, D_MODEL), F32),
                   jax.ShapeDtypeStruct((n * ROW_CHUNKS, LANES), F32),
                   jax.ShapeDtypeStruct((n, LANES), I32),
                   jax.ShapeDtypeStruct((n, LANES), F32),
                   jax.ShapeDtypeStruct((SUBLANES, LANES), I32)],
        scratch_shapes=[pltpu.VMEM((SUBLANES, LANES), F32)] + scratch,
        compiler_params=_cparams(("arbitrary",)),
        name="merge_out",
    )(x2d, *outs, *lses, y_ssd, g_a, g_s, wab, wsb, b_gate, wo, norm2_w.reshape(1, D_MODEL), w_router, b_router)


GATHER_TILE = 256
DISPATCH_TILE = 512
INDEX_BATCH = 16


def _wait_rows(like_hbm, dst, sem, rows):
    pltpu.make_async_copy(like_hbm.at[pl.ds(0, rows)], dst, sem).wait()


def _dispatch_kernel(dest_ref, hn_ref, xs_init_hbm, xs_hbm, sem, *, tm):
    del xs_init_hbm
    i = pl.program_id(0)
    for j0 in range(0, 2 * tm, INDEX_BATCH):
        slots = [dest_ref[i * 2 * tm + j0 + u] for u in range(INDEX_BATCH)]
        for u, dst in enumerate(slots):
            tok = (j0 + u) % tm
            first = pl.multiple_of(dst * ROW_CHUNKS, ROW_CHUNKS)
            pltpu.make_async_copy(hn_ref.at[pl.ds(tok * ROW_CHUNKS, ROW_CHUNKS), :],
                                  xs_hbm.at[pl.ds(first, ROW_CHUNKS), :], sem).start()
    for _ in range(2):
        pltpu.make_async_copy(hn_ref, xs_hbm.at[pl.ds(0, tm * ROW_CHUNKS), :], sem).wait()


def _dispatch(hn, dest_tiles, n_slots, tm):
    n = hn.shape[0] // ROW_CHUNKS
    anyspec = pl.BlockSpec(memory_space=pl.ANY)
    return pl.pallas_call(
        functools.partial(_dispatch_kernel, tm=tm),
        grid_spec=pltpu.PrefetchScalarGridSpec(
            num_scalar_prefetch=1,
            grid=(n // tm,),
            in_specs=[pl.BlockSpec((tm * ROW_CHUNKS, LANES), lambda i, d: (i, 0)), anyspec],
            out_specs=anyspec,
            scratch_shapes=[pltpu.SemaphoreType.DMA(())]),
        out_shape=jax.ShapeDtypeStruct((n_slots * ROW_CHUNKS, LANES), F32),
        input_output_aliases={2: 0},
        compiler_params=_cparams(("arbitrary",), disable_bounds_checks=True, has_side_effects=True),
        name="dispatch",
    )(dest_tiles, hn, jnp.zeros((n_slots * ROW_CHUNKS, LANES), F32))


def _expert_kernel(te_ref, nu_ref, x_ref, wg_ref, wu_ref, wd_ref, o_ref, wgb, wub, wdb):
    i = pl.program_id(0)
    changed = (i == 0) | (te_ref[i] != te_ref[jnp.maximum(i - 1, 0)])

    @pl.when(changed)
    def _():
        wgb[...] = wg_ref[0].astype(BF16)
        wub[...] = wu_ref[0].astype(BF16)
        wdb[...] = wd_ref[0].astype(BF16)

    @pl.when(i < nu_ref[0])
    def _():
        x = _load_row_tiles(x_ref, 0, EXPERT_TILE).astype(BF16)
        hg = jnp.dot(x, wgb[...], preferred_element_type=F32)
        hu = jnp.dot(x, wub[...], preferred_element_type=F32)
        hb = (hg * _sigmoid(hg)) * hu
        _store_row_tiles(o_ref, jnp.dot(hb.astype(BF16), wdb[...], preferred_element_type=F32))

    @pl.when(i >= nu_ref[0])
    def _():
        o_ref[...] = jnp.zeros_like(o_ref)


def _experts(xs, tile_expert, n_used, w_eg, w_eu, w_ed):
    n_slots = xs.shape[0] // ROW_CHUNKS
    n_tiles = n_slots // EXPERT_TILE
    row = lambda i, te, nu: (i, 0)
    wsel = lambda i, te, nu: (te[i], 0, 0)
    return pl.pallas_call(
        _expert_kernel,
        grid_spec=pltpu.PrefetchScalarGridSpec(
            num_scalar_prefetch=2,
            grid=(n_tiles,),
            in_specs=[pl.BlockSpec((EXPERT_TILE * ROW_CHUNKS, LANES), row),
                      pl.BlockSpec((1, D_MODEL, D_FF), wsel),
                      pl.BlockSpec((1, D_MODEL, D_FF), wsel),
                      pl.BlockSpec((1, D_FF, D_MODEL), wsel)],
            out_specs=pl.BlockSpec((EXPERT_TILE * ROW_CHUNKS, LANES), row),
            scratch_shapes=[pltpu.VMEM((D_MODEL, D_FF), BF16),
                            pltpu.VMEM((D_MODEL, D_FF), BF16),
                            pltpu.VMEM((D_FF, D_MODEL), BF16)]),
        out_shape=jax.ShapeDtypeStruct((n_slots * ROW_CHUNKS, LANES), F32),
        compiler_params=_cparams(("arbitrary",)),
        name="experts",
    )(tile_expert, n_used, xs, w_eg, w_eu, w_ed)


def _final_kernel(dest_ref, h_ref, rf_ref, fw_ref, out_hbm, o_ref, ybuf, sem, *, tm):
    i = pl.program_id(0)
    slot = i % 2

    def gather(tile, into):
        for j0 in range(0, 2 * tm, INDEX_BATCH):
            slots = [dest_ref[tile * 2 * tm + j0 + u] for u in range(INDEX_BATCH)]
            for u, src in enumerate(slots):
                first = pl.multiple_of(src * ROW_CHUNKS, ROW_CHUNKS)
                pltpu.make_async_copy(out_hbm.at[pl.ds(first, ROW_CHUNKS), :],
                                      ybuf.at[into, pl.ds((j0 + u) * ROW_CHUNKS, ROW_CHUNKS), :],
                                      sem.at[into]).start()

    @pl.when(i == 0)
    def _():
        gather(0, 0)

    @pl.when(i + 1 < pl.num_programs(0))
    def _():
        gather(i + 1, 1 - slot)

    pltpu.make_async_copy(out_hbm.at[pl.ds(0, 2 * tm * ROW_CHUNKS), :], ybuf.at[slot], sem.at[slot]).wait()
    g = rf_ref[...]
    moe = (_load_row_tiles(ybuf.at[slot], 0, tm) * g[:, 0:1] + _load_row_tiles(ybuf.at[slot], tm, tm) * g[:, 1:2])
    h = h_ref[...] + moe
    o_ref[...] = h * lax.rsqrt(jnp.mean(h * h, axis=-1, keepdims=True) + EPS) * fw_ref[...]


def _final(h, out, dest_tiles, rf, final_w):
    n = h.shape[0]
    tm = min(GATHER_TILE, n)
    row = lambda i, d: (i, 0)
    wide = pl.BlockSpec((tm, D_MODEL), row)
    return pl.pallas_call(
        functools.partial(_final_kernel, tm=tm),
        grid_spec=pltpu.PrefetchScalarGridSpec(
            num_scalar_prefetch=1,
            grid=(n // tm,),
            in_specs=[wide, pl.BlockSpec((tm, LANES), row), pl.BlockSpec((1, D_MODEL), lambda i, d: (0, 0)),
                      pl.BlockSpec(memory_space=pl.ANY)],
            out_specs=wide,
            scratch_shapes=[pltpu.VMEM((2, 2 * tm * ROW_CHUNKS, LANES), F32), pltpu.SemaphoreType.DMA((2,))]),
        out_shape=jax.ShapeDtypeStruct((n, D_MODEL), F32),
        compiler_params=_cparams(("arbitrary",), disable_bounds_checks=True),
        name="final",
    )(dest_tiles, h, rf, final_w.reshape(1, D_MODEL), out)


def _moe_and_final(h, hn, ri, rf, counts, w_eg, w_eu, w_ed, final_w):
    n = h.shape[0]
    tm = min(GATHER_TILE, n)
    counts = counts[0, :N_EXPERTS]
    padded = ((counts + EXPERT_TILE - 1) // EXPERT_TILE) * EXPERT_TILE
    ends = jnp.cumsum(padded)
    starts = ends - padded
    n_tiles = (2 * n) // EXPERT_TILE + N_EXPERTS
    n_slots = n_tiles * EXPERT_TILE
    experts = jnp.arange(N_EXPERTS, dtype=I32)
    start_of = jnp.sum(jnp.where(ri[:, 0:2, None] == experts, starts, 0), axis=-1)
    dest = start_of + ri[:, 2:4]
    by_tile = lambda t: dest.reshape(n // t, t, 2).transpose(0, 2, 1).reshape(-1)
    tile_start = jnp.arange(n_tiles, dtype=I32) * EXPERT_TILE
    tile_expert = jnp.minimum(jnp.sum((ends[None, :] <= tile_start[:, None]).astype(I32), axis=1), N_EXPERTS - 1)
    n_used = (ends[-1] // EXPERT_TILE).astype(I32).reshape(1)
    td = min(DISPATCH_TILE, n)
    xs = _dispatch(hn, by_tile(td), n_slots, td)
    out = _experts(xs, tile_expert, n_used, w_eg, w_eu, w_ed)
    return _final(h, out, by_tile(tm), rf, final_w)


def _layer(x, pos0, kv_bufs, conv_state, ssm_state, p):
    batch, seq, _ = x.shape
    n = batch * seq
    x2d = x.reshape(n, D_MODEL)
    prompt = kv_bufs is None
    q, k, v, z, xbc, g_a, g_s, dt_raw = _inproj(x2d, batch, seq, pos0, p["norm1_w"], p["w_in"], prompt)

    outs, lses, new_kv = [], [], []
    for gi, (window, dil) in enumerate(ATTN_GROUPS):
        if prompt:
            o, l = _attn_prompt(q[gi], k[gi], v[gi], dil)
            keep = min(window, seq) // dil
            tail = lambda t: t[:, :, seq // dil - keep:, :].transpose(0, 2, 1, 3).reshape(batch, keep * dil, GROUP_W)
            nbuf = jnp.stack([tail(k[gi]), tail(v[gi])], axis=2).reshape(
                batch, keep * dil, 2, HEADS_PER_GROUP, HEAD_DIM)
        else:
            o, l, nbuf = _attn_step(q, k, v, kv_bufs[gi], batch, seq, gi, dil)
        outs.append(o)
        lses.append(l)
        new_kv.append(nbuf)

    if conv_state is None:
        conv_state = jnp.zeros((batch, 3, SSD_CONV_DIM), F32)
        ssm_state = jnp.zeros((batch, SSD_HEADS, HEAD_DIM, SSD_STATE), F32)
    y_ssd, new_conv, new_ssm = _ssd(xbc, z, dt_raw, conv_state, ssm_state, batch, seq, p["conv_w"], p["conv_b"],
                                    p["dt_bias"], p["A_log"], p["D_skip"], p["ssd_norm_w"])

    h, hn, ri, rf, counts = _merge(x2d, outs, lses, y_ssd, g_a, g_s, p["w_attn_br"], p["w_ssd_br"], p["b_gate"],
                                   p["w_out"], p["norm2_w"], p["w_router"], p["b_router"], prompt)
    y = _moe_and_final(h, hn, ri, rf, counts, p["w_eg"], p["w_eu"], p["w_ed"], p["final_norm_w"])
    return y.reshape(batch, seq, D_MODEL), new_kv, new_conv, new_ssm


def _pack_w_in(w_in):
    offs = (0, 768, 1536, 2304, 3328, 4864, 4880, 5904, 6928)
    q, k, v, z, xbc, dt, g_a, g_s = (w_in[:, offs[i]:offs[i + 1]] for i in range(8))
    dt = jnp.pad(dt, ((0, 0), (0, LANES - SSD_HEADS)))
    return jnp.concatenate([q, k, v, z, xbc, g_a, g_s, dt], axis=1).astype(BF16)


def kernel(x_prompt, x_sample, cache_kv_w128, cache_kv_w512, cache_kv_w2048, state_conv, state_ssm, norm1_w, w_in, w_attn_br, w_ssd_br, b_gate, w_out, conv_w, conv_b, dt_bias, A_log, D_skip, ssd_norm_w, norm2_w, w_router_coarse, b_router_coarse, w_router_fine, b_router_fine, w_expert_gate, w_expert_up, w_expert_down, final_norm_w):
    depth = norm1_w.shape[0]
    assert depth == 1, "the final norm is fused into the layer's last kernel"
    l = 0
    rpad = LANES - N_EXPERTS - N_GROUPS_E
    p = dict(
        norm1_w=norm1_w[l], w_in=_pack_w_in(w_in[l]),
        w_attn_br=w_attn_br[l].astype(BF16), w_ssd_br=w_ssd_br[l].astype(BF16), b_gate=b_gate[l],
        w_out=w_out[l].astype(BF16), conv_w=conv_w[l], conv_b=conv_b[l], dt_bias=dt_bias[l], A_log=A_log[l],
        D_skip=D_skip[l], ssd_norm_w=ssd_norm_w[l], norm2_w=norm2_w[l],
        w_router=jnp.pad(jnp.concatenate([w_router_fine[l], w_router_coarse[l]], axis=1),
                         ((0, 0), (0, rpad))).astype(BF16),
        b_router=jnp.pad(jnp.concatenate([b_router_fine[l], b_router_coarse[l]]), (0, rpad)).reshape(1, LANES),
        w_eg=w_expert_gate[l], w_eu=w_expert_up[l], w_ed=w_expert_down[l], final_norm_w=final_norm_w,
    )
    y_p, kv_p, c_p, st_p = _layer(x_prompt, 0, None, None, None, p)
    bufs = (cache_kv_w128[l], cache_kv_w512[l], cache_kv_w2048[l])
    y_s, kv_s, c_s, st_s = _layer(x_sample, PAST_LEN, bufs, state_conv[l], state_ssm[l], p)
    lead = lambda t: t[None]
    return (y_p, y_s, lead(kv_p[0]), lead(kv_p[1]), lead(kv_p[2]), lead(c_p), lead(st_p),
            lead(kv_s[0]), lead(kv_s[1]), lead(kv_s[2]), lead(c_s), lead(st_s))
```

```python
import functools

import jax
import jax.numpy as jnp
from jax import lax
from jax.experimental import pallas as pl
from jax.experimental.pallas import tpu as pltpu

F32 = jnp.float32
BF16 = jnp.bfloat16
I32 = jnp.int32

D_MODEL = 1024
HEAD_DIM = 64
ATTN_GROUPS = ((128, 1), (512, 4), (2048, 16))
SPAN = 128
HEADS_PER_GROUP = 4
GROUP_W = HEADS_PER_GROUP * HEAD_DIM
ATTN_W = GROUP_W * len(ATTN_GROUPS)
ROPE_THETA = 10000.0
PAST_LEN = 8192
SSD_INNER = 1024
SSD_HEADS = 16
SSD_STATE = 128
SSD_CONV_DIM = 1536
SSD_CHUNK = 128
N_GROUPS_E = 4
EXPERTS_PER_GROUP = 8
N_EXPERTS = 32
D_FF = 512
EPS = 1e-6
LANES = 128
SUBLANES = 8
NEG = -1e30

_OFF_Q, _OFF_K, _OFF_V, _OFF_Z, _OFF_XBC, _OFF_GA, _OFF_GS, _OFF_DT = 0, 768, 1536, 2304, 3328, 4864, 5888, 6912
_W_IN_COLS = 7040

ROW_TILE = 256
EXPERT_TILE = 256
VMEM_LIMIT = 56 * 1024 * 1024


def _cparams(sem, **kw):
    return pltpu.CompilerParams(dimension_semantics=sem, vmem_limit_bytes=VMEM_LIMIT, **kw)


def _sigmoid(x):
    return 1.0 / (1.0 + jnp.exp(-x))


ROW_CHUNKS = D_MODEL // LANES


def _store_row_tiles(ref, val):
    rows = val.shape[0]
    for c in range(ROW_CHUNKS):
        ref[pl.ds(c, rows, stride=ROW_CHUNKS), :] = val[:, c * LANES:(c + 1) * LANES]


def _load_row_tiles(ref, first, rows):
    return jnp.concatenate(
        [ref[pl.ds(first * ROW_CHUNKS + c, rows, stride=ROW_CHUNKS), :] for c in range(ROW_CHUNKS)], axis=1)


def _nt_dot(a, b):
    return lax.dot_general(a, b, (((1,), (1,)), ((), ())), preferred_element_type=F32)


def _store_by_class(o_ref, scr_ref, val, dil):
    if dil == 1:
        o_ref[0, 0] = val
        return
    rows = val.shape[0]
    for half in range(GROUP_W // LANES):
        lanes = slice(half * LANES, (half + 1) * LANES)
        scr_ref[half] = val[:, lanes]
        for r in range(dil):
            o_ref[0, r, :, lanes] = scr_ref[half, pl.ds(r, rows // dil, stride=dil), :]


def _inproj_kernel(x_ref, nw_ref, w_ref, cos_ref, sin_ref, *refs, class_major):
    if class_major:
        qkv_refs, (z_ref, xbc_ref, ga_ref, gs_ref, dt_ref, scr_ref) = refs[:9], refs[9:]
    else:
        qkv_refs, (z_ref, xbc_ref, ga_ref, gs_ref, dt_ref) = refs[:3], refs[3:]
    x = x_ref[...]
    xn = x * lax.rsqrt(jnp.mean(x * x, axis=-1, keepdims=True) + EPS) * nw_ref[...]
    xb = xn.astype(BF16)
    cos = cos_ref[...]
    sin = sin_ref[...]
    lane = lax.broadcasted_iota(I32, cos.shape, 1)
    first_half = (lane % HEAD_DIM) < (HEAD_DIM // 2)

    def mm(lo, hi):
        return jnp.dot(xb, w_ref[:, lo:hi], preferred_element_type=F32)

    def rope(uc):
        ur = jnp.where(first_half, pltpu.roll(uc, LANES - HEAD_DIM // 2, 1), pltpu.roll(uc, HEAD_DIM // 2, 1))
        return uc * cos + ur * sin

    def emit(which, base, roped):
        u = mm(base, base + ATTN_W)
        chunks = [u[:, c * LANES:(c + 1) * LANES] for c in range(ATTN_W // LANES)]
        if roped:
            chunks = [rope(uc) for uc in chunks]
        if class_major:
            for gi, (_, dil) in enumerate(ATTN_GROUPS):
                val = jnp.concatenate(chunks[2 * gi:2 * gi + 2], axis=1)
                _store_by_class(qkv_refs[3 * which + gi], scr_ref, val, dil)
        else:
            for c, uc in enumerate(chunks):
                qkv_refs[which][:, c * LANES:(c + 1) * LANES] = uc

    emit(0, _OFF_Q, True)
    emit(1, _OFF_K, True)
    emit(2, _OFF_V, False)
    z_ref[...] = mm(_OFF_Z, _OFF_XBC)
    xbc_ref[...] = mm(_OFF_XBC, _OFF_GA)
    ga_ref[...] = mm(_OFF_GA, _OFF_GS)
    gs_ref[...] = mm(_OFF_GS, _OFF_DT)
    dt_ref[...] = mm(_OFF_DT, _W_IN_COLS)


def _rope_tables(pos):
    half = HEAD_DIM // 2
    inv_freq = ROPE_THETA ** (-jnp.arange(half, dtype=F32) * (2.0 / HEAD_DIM))
    ang = pos.astype(F32)[:, None] * inv_freq[None, :]
    ang = jnp.tile(ang, (1, LANES // half))
    lane = jnp.arange(LANES)
    sign = jnp.where((lane % HEAD_DIM) < half, -1.0, 1.0).astype(F32)
    return jnp.cos(ang), jnp.sin(ang) * sign[None, :]


def _inproj(x2d, batch, seq_len, pos0, norm_w, w_packed, class_major):
    n = x2d.shape[0]
    tm = min(ROW_TILE, n)
    cos, sin = _rope_tables(pos0 + jnp.arange(seq_len, dtype=I32))
    if seq_len < tm:
        cos = jnp.tile(cos, (tm // seq_len, 1))
        sin = jnp.tile(sin, (tm // seq_len, 1))
    tab_blocks = cos.shape[0] // tm
    row = lambda i: (i, 0)
    fixed = lambda i: (0, 0)
    tab = lambda i: (i % tab_blocks, 0)
    widths = (SSD_INNER, SSD_CONV_DIM, D_MODEL, D_MODEL, LANES)
    out_specs = [pl.BlockSpec((tm, w), row) for w in widths]
    out_shape = [jax.ShapeDtypeStruct((n, w), F32) for w in widths]
    scratch = []
    if class_major:
        assert seq_len % tm == 0
        per_seq = seq_len // tm
        cls = lambda i: (i // per_seq, 0, i % per_seq, 0)
        dils = [d for _, d in ATTN_GROUPS] * 3
        out_specs = [pl.BlockSpec((1, d, tm // d, GROUP_W), cls) for d in dils] + out_specs
        out_shape = [jax.ShapeDtypeStruct((batch, d, seq_len // d, GROUP_W), F32) for d in dils] + out_shape
        scratch = [pltpu.VMEM((GROUP_W // LANES, tm, LANES), F32)]
    else:
        out_specs = [pl.BlockSpec((tm, ATTN_W), row)] * 3 + out_specs
        out_shape = [jax.ShapeDtypeStruct((n, ATTN_W), F32)] * 3 + out_shape
    res = pl.pallas_call(
        functools.partial(_inproj_kernel, class_major=class_major),
        grid=(n // tm,),
        in_specs=[pl.BlockSpec((tm, D_MODEL), row),
                  pl.BlockSpec((1, D_MODEL), fixed),
                  pl.BlockSpec((D_MODEL, _W_IN_COLS), fixed, pipeline_mode=pl.Buffered(1)),
                  pl.BlockSpec((tm, LANES), tab),
                  pl.BlockSpec((tm, LANES), tab)],
        out_specs=out_specs,
        out_shape=out_shape,
        scratch_shapes=scratch,
        compiler_params=_cparams(("parallel",)),
        name="inproj",
    )(x2d, norm_w.reshape(1, D_MODEL), w_packed, cos, sin)
    if class_major:
        return (tuple(res[0:3]), tuple(res[3:6]), tuple(res[6:9])) + tuple(res[9:])
    return tuple(res)


ATTN_QBLOCKS = 4


def _attn_prompt_kernel(q_ref, kp_ref, kc_ref, vp_ref, vc_ref, o_ref, l_ref, *, qblocks):
    c = pl.program_id(2)
    i = lax.broadcasted_iota(I32, (SPAN, SPAN), 0)
    j = lax.broadcasted_iota(I32, (SPAN, SPAN), 1)
    upper = j >= i
    lower = j <= i
    kcur = kc_ref[0, 0].astype(BF16)
    vcur = vc_ref[0, 0].astype(BF16)
    kprev = kp_ref[0, 0].astype(BF16)
    vprev = vp_ref[0, 0].astype(BF16)
    for b in range(qblocks):
        rows = slice(b * SPAN, (b + 1) * SPAN)
        q = q_ref[0, 0, rows, :] * (HEAD_DIM ** -0.5)
        if b == 0:
            kp, vp, valid_prev = kprev, vprev, upper & (c > 0)
        else:
            prows = slice((b - 1) * SPAN, b * SPAN)
            kp, vp, valid_prev = kcur[prows], vcur[prows], upper
        kc, vc = kcur[rows], vcur[rows]
        for h in range(HEADS_PER_GROUP):
            hs = slice(h * HEAD_DIM, (h + 1) * HEAD_DIM)
            qh = q[:, hs].astype(BF16)
            sp = jnp.where(valid_prev, _nt_dot(qh, kp[:, hs]), NEG)
            sc = jnp.where(lower, _nt_dot(qh, kc[:, hs]), NEG)
            m = jnp.max(jnp.maximum(sp, sc), axis=-1, keepdims=True)
            pp = jnp.exp(sp - m)
            pc = jnp.exp(sc - m)
            den = jnp.sum(pp + pc, axis=-1, keepdims=True)
            acc = (jnp.dot(pp.astype(BF16), vp[:, hs], preferred_element_type=F32)
                   + jnp.dot(pc.astype(BF16), vc[:, hs], preferred_element_type=F32))
            o_ref[0, 0, rows, hs] = acc / den
            l_ref[0, 0, rows, hs] = jnp.broadcast_to(m + jnp.log(den), (SPAN, HEAD_DIM))


def _attn_prompt(q, k, v, dil):
    batch, _, n_cls, _ = q.shape
    qblocks = min(ATTN_QBLOCKS, n_cls // SPAN)
    tq = qblocks * SPAN
    assert n_cls % tq == 0
    cur = lambda b, r, c: (b, r, c, 0)
    prev = lambda b, r, c: (b, r, jnp.maximum(c * qblocks - 1, 0), 0)
    big = pl.BlockSpec((1, 1, tq, GROUP_W), cur)
    small = pl.BlockSpec((1, 1, SPAN, GROUP_W), prev)
    return pl.pallas_call(
        functools.partial(_attn_prompt_kernel, qblocks=qblocks),
        grid=(batch, dil, n_cls // tq),
        in_specs=[big, small, big, small, big],
        out_specs=[big, big],
        out_shape=[jax.ShapeDtypeStruct(q.shape, F32)] * 2,
        compiler_params=_cparams(("parallel", "parallel", "arbitrary")),
        name=f"attn_prompt_d{dil}",
    )(q, k, k, v, v)


def _attn_step_kernel(q_ref, kn_ref, vn_ref, buf_ref, o_ref, l_ref, nbuf_ref, *, buf_len, dil, n_new):
    q = q_ref[0] * (HEAD_DIM ** -0.5)
    t = lax.broadcasted_iota(I32, (n_new, buf_len), 0)
    i = lax.broadcasted_iota(I32, (n_new, buf_len), 1)
    delta = buf_len + t - i
    valid_b = delta <= dil * SPAN
    first_new = LANES - n_new
    tn = lax.broadcasted_iota(I32, (n_new, LANES), 0)
    un = lax.broadcasted_iota(I32, (n_new, LANES), 1) - first_new
    dn = tn - un
    valid_n = (dn >= 0) & (un >= 0)
    if dil > 1:
        valid_b = valid_b & ((delta & (dil - 1)) == 0)
        valid_n = valid_n & ((dn & (dil - 1)) == 0)

    def to_columns(x):
        xp = jnp.concatenate([x, jnp.zeros((LANES - n_new, GROUP_W), F32)], axis=0)
        xt = jnp.concatenate([xp[:, c * LANES:(c + 1) * LANES].T for c in range(GROUP_W // LANES)], axis=0)
        return pltpu.roll(xt, first_new, 1)

    new_cols = (to_columns(kn_ref[0]), to_columns(vn_ref[0]))
    is_new = lax.broadcasted_iota(I32, (HEAD_DIM, LANES), 1) >= first_new
    tail = slice(buf_len - LANES, buf_len)
    for h in range(HEADS_PER_GROUP):
        hs = slice(h * HEAD_DIM, (h + 1) * HEAD_DIM)
        qh = q[:, hs].astype(BF16)
        kt = buf_ref[0, 0, h]
        vt = buf_ref[0, 1, h]
        knt = new_cols[0][hs]
        vnt = new_cols[1][hs]
        sb = jnp.where(valid_b, jnp.dot(qh, kt.astype(BF16), preferred_element_type=F32), NEG)
        sn = jnp.where(valid_n, jnp.dot(qh, knt.astype(BF16), preferred_element_type=F32), NEG)
        m = jnp.maximum(jnp.max(sb, axis=-1, keepdims=True), jnp.max(sn, axis=-1, keepdims=True))
        pb = jnp.exp(sb - m)
        pn = jnp.exp(sn - m)
        den = jnp.sum(pb, axis=-1, keepdims=True) + jnp.sum(pn, axis=-1, keepdims=True)
        acc = _nt_dot(pb.astype(BF16), vt.astype(BF16)) + _nt_dot(pn.astype(BF16), vnt.astype(BF16))
        o_ref[0, :, hs] = acc / den
        l_ref[0, :, hs] = jnp.broadcast_to(m + jnp.log(den), (n_new, HEAD_DIM))
        for kv, (old, new) in enumerate(((kt, knt), (vt, vnt))):
            shifted = pltpu.roll(old, buf_len - n_new, 1)
            nbuf_ref[0, kv, h] = shifted
            nbuf_ref[0, kv, h, :, tail] = jnp.where(is_new, new, shifted[:, tail])


def _attn_step(q, k, v, buf, batch, n_new, gi, dil):
    buf_len = buf.shape[1]
    assert dil & (dil - 1) == 0 and buf_len >= dil * SPAN and n_new % SUBLANES == 0 and buf_len % LANES == 0
    view = lambda t: t.reshape(batch, n_new, ATTN_W)
    tok = pl.BlockSpec((1, n_new, GROUP_W), lambda b: (b, 0, gi))
    full = pl.BlockSpec((1, 2, HEADS_PER_GROUP, HEAD_DIM, buf_len), lambda b: (b, 0, 0, 0, 0))
    osp = pl.BlockSpec((1, n_new, GROUP_W), lambda b: (b, 0, 0))
    o, l, nbuf = pl.pallas_call(
        functools.partial(_attn_step_kernel, buf_len=buf_len, dil=dil, n_new=n_new),
        grid=(batch,),
        in_specs=[tok, tok, tok, full],
        out_specs=[osp, osp, full],
        out_shape=[jax.ShapeDtypeStruct((batch, n_new, GROUP_W), F32)] * 2
        + [jax.ShapeDtypeStruct((batch, 2, HEADS_PER_GROUP, HEAD_DIM, buf_len), F32)],
        compiler_params=_cparams(("parallel",)),
        name=f"attn_step_d{dil}",
    )(view(q), view(k), view(v), jnp.transpose(buf, (0, 2, 3, 4, 1)))
    return (o.reshape(batch * n_new, GROUP_W), l.reshape(batch * n_new, GROUP_W),
            jnp.transpose(nbuf, (0, 4, 1, 2, 3)))


def _split3(a):
    a1 = a.astype(BF16)
    r1 = a - a1.astype(F32)
    a2 = r1.astype(BF16)
    a3 = (r1 - a2.astype(F32)).astype(BF16)
    return a1, a2, a3


def _ssd_kernel(xbc_ref, z_ref, dt_ref, cst_ref, h0_ref, cw_ref, cb_ref, dtb_ref, alog_ref, dfull_ref, nw_ref,
                y_ref, cout_ref, hout_ref, xpad_ref, h_ref, y_scr, *, n_valid, n_chunks):
    c = pl.program_id(1)
    lc = SSD_CHUNK
    pad = SUBLANES

    @pl.when(c == 0)
    def _():
        xpad_ref[0:pad, :] = cst_ref[0]
        h_ref[...] = h0_ref[0]

    @pl.when(c > 0)
    def _():
        xpad_ref[0:pad, :] = xpad_ref[lc:lc + pad, :]

    if n_valid == lc:
        xpad_ref[pad:pad + lc, :] = xbc_ref[...]
        z = z_ref[...]
        dtr = dt_ref[...]
    else:
        fill = lambda w: jnp.zeros((lc - n_valid, w), F32)
        xpad_ref[pad:pad + n_valid, :] = xbc_ref[...]
        xpad_ref[pad + n_valid:pad + lc, :] = fill(SSD_CONV_DIM)
        z = jnp.concatenate([z_ref[...], fill(SSD_INNER)], axis=0)
        dtr = jnp.concatenate([dt_ref[...], fill(LANES)], axis=0)

    xc = cb_ref[...]
    for tap in range(4):
        xc = xc + xpad_ref[pl.ds(pad - 3 + tap, lc), :] * cw_ref[tap:tap + 1, :]
    xc = xc * _sigmoid(xc)
    cout_ref[0] = xpad_ref[pl.ds(pad + n_valid - 3, 3), :]

    xs = xc[:, :SSD_INNER]
    bm = xc[:, SSD_INNER:SSD_INNER + 2 * SSD_STATE]
    cm = xc[:, SSD_INNER + 2 * SSD_STATE:]

    row = lax.broadcasted_iota(I32, (lc, lc), 0)
    col = lax.broadcasted_iota(I32, (lc, lc), 1)
    causal = row >= col
    dtv = dtr + dtb_ref[...]
    dt = jnp.maximum(dtv, 0.0) + jnp.log1p(jnp.exp(-jnp.abs(dtv)))
    if n_valid < lc:
        dt = jnp.where(row < n_valid, dt, 0.0)
    a = dt * (-jnp.exp(alog_ref[...]))
    tri = causal.astype(BF16)
    a_cs = sum(jnp.dot(tri, p, preferred_element_type=F32) for p in _split3(a))
    a_cs_t = a_cs.T
    lane = lax.broadcasted_iota(I32, (lc, LANES), 1)
    low_half = lane < HEAD_DIM

    for g in range(2):
        bg = bm[:, g * SSD_STATE:(g + 1) * SSD_STATE]
        cg = cm[:, g * SSD_STATE:(g + 1) * SSD_STATE].astype(BF16)
        cb = _nt_dot(cg, bg.astype(BF16))
        for pair in range(4):
            e0 = g * 8 + pair * 2
            ps = slice(e0 * HEAD_DIM, (e0 + 2) * HEAD_DIM)
            dt_pair = jnp.where(low_half, dt[:, e0:e0 + 1], dt[:, e0 + 1:e0 + 2])
            xdt = xs[:, ps] * dt_pair
            xdt_t = xdt.T.astype(BF16)
            xdt_b = xdt.astype(BF16)
            for k in range(2):
                e = e0 + k
                acol = a_cs[:, e:e + 1]
                arow = a_cs_t[e:e + 1, :]
                decay = jnp.exp(jnp.where(causal, acol - arow, NEG))
                m = (cb * decay).astype(BF16)
                y_diag = jnp.dot(m, xdt_b[:, k * HEAD_DIM:(k + 1) * HEAD_DIM], preferred_element_type=F32)
                h_prev = h_ref[e]
                y_off = _nt_dot(cg, h_prev.astype(BF16)) * jnp.exp(acol)
                y_scr[:, e * HEAD_DIM:(e + 1) * HEAD_DIM] = y_diag + y_off
                a_last = a_cs[lc - 1:lc, e:e + 1]
                bd = (bg * jnp.exp(a_last - acol)).astype(BF16)
                st = jnp.dot(xdt_t[k * HEAD_DIM:(k + 1) * HEAD_DIM, :], bd, preferred_element_type=F32)
                h_ref[e] = h_prev * jnp.exp(a_last) + st

    y = y_scr[...] + xs * dfull_ref[...]
    gate = y * (z * _sigmoid(z))
    half = SSD_INNER // 2
    for g in range(2):
        gg = gate[:, g * half:(g + 1) * half]
        gg = gg * lax.rsqrt(jnp.mean(gg * gg, axis=-1, keepdims=True) + EPS)
        y_ref[:, g * half:(g + 1) * half] = (gg * nw_ref[:, g * half:(g + 1) * half])[0:n_valid]

    @pl.when(c == n_chunks - 1)
    def _():
        hout_ref[0] = h_ref[...]


def _ssd(xbc, z, dt_raw, conv_state, ssm_state, batch, seq, conv_w, conv_b, dt_bias, a_log, d_skip, norm_w):
    n_valid = min(seq, SSD_CHUNK)
    n_chunks = seq // n_valid
    assert n_valid % SUBLANES == 0 and seq % n_valid == 0
    padl = lambda t: jnp.pad(t.reshape(1, SSD_HEADS), ((0, 0), (0, LANES - SSD_HEADS)))
    cst = jnp.pad(conv_state, ((0, 0), (SUBLANES - 3, 0), (0, 0)))
    tokrow = lambda b, c: (b * n_chunks + c, 0)
    fixed = lambda b, c: (0, 0)
    per_b3 = lambda b, c: (b, 0, 0)
    kern = functools.partial(_ssd_kernel, n_valid=n_valid, n_chunks=n_chunks)
    y, cout, hout = pl.pallas_call(
        kern,
        grid=(batch, n_chunks),
        in_specs=[pl.BlockSpec((n_valid, SSD_CONV_DIM), tokrow),
                  pl.BlockSpec((n_valid, SSD_INNER), tokrow),
                  pl.BlockSpec((n_valid, LANES), tokrow),
                  pl.BlockSpec((1, SUBLANES, SSD_CONV_DIM), per_b3),
                  pl.BlockSpec((1, SSD_HEADS, HEAD_DIM, SSD_STATE), lambda b, c: (b, 0, 0, 0)),
                  pl.BlockSpec((4, SSD_CONV_DIM), fixed),
                  pl.BlockSpec((1, SSD_CONV_DIM), fixed),
                  pl.BlockSpec((1, LANES), fixed),
                  pl.BlockSpec((1, LANES), fixed),
                  pl.BlockSpec((1, SSD_INNER), fixed),
                  pl.BlockSpec((1, SSD_INNER), fixed)],
        out_specs=[pl.BlockSpec((n_valid, SSD_INNER), tokrow),
                   pl.BlockSpec((1, 3, SSD_CONV_DIM), per_b3),
                   pl.BlockSpec((1, SSD_HEADS, HEAD_DIM, SSD_STATE), lambda b, c: (b, 0, 0, 0))],
        out_shape=[jax.ShapeDtypeStruct((batch * seq, SSD_INNER), F32),
                   jax.ShapeDtypeStruct((batch, 3, SSD_CONV_DIM), F32),
                   jax.ShapeDtypeStruct((batch, SSD_HEADS, HEAD_DIM, SSD_STATE), F32)],
        scratch_shapes=[pltpu.VMEM((SSD_CHUNK + SUBLANES, SSD_CONV_DIM), F32),
                        pltpu.VMEM((SSD_HEADS, HEAD_DIM, SSD_STATE), F32),
                        pltpu.VMEM((SSD_CHUNK, SSD_INNER), F32)],
        compiler_params=_cparams(("parallel", "arbitrary")),
        name="ssd",
    )(xbc, z, dt_raw, cst, ssm_state, conv_w, conv_b.reshape(1, SSD_CONV_DIM), padl(dt_bias), padl(a_log),
      jnp.repeat(d_skip, HEAD_DIM).reshape(1, SSD_INNER), norm_w.reshape(1, SSD_INNER))
    return y, cout, hout


def _merge_kernel(x_ref, o0_ref, o1_ref, o2_ref, l0_ref, l1_ref, l2_ref, ys_ref, ga_ref, gs_ref,
                  wab_ref, wsb_ref, bg_ref, wo_ref, n2_ref, wr_ref, br_ref,
                  h_ref, hn_ref, ri_ref, rf_ref, cnt_ref, carry_ref, *scr, class_major):
    step = pl.program_id(0)

    @pl.when(step == 0)
    def _():
        carry_ref[...] = jnp.zeros_like(carry_ref)

    def load(ref, gi):
        if not class_major:
            return ref[...]
        dil = ATTN_GROUPS[gi][1]
        if dil == 1:
            return ref[0, 0]
        rows = ref.shape[1] * ref.shape[2]
        halves = []
        for half in range(GROUP_W // LANES):
            for r in range(dil):
                scr[0][half, pl.ds(r, rows // dil, stride=dil), :] = ref[0, r, :, half * LANES:(half + 1) * LANES]
            halves.append(scr[0][half])
        return jnp.concatenate(halves, axis=1)

    l0, l1, l2 = load(l0_ref, 0), load(l1_ref, 1), load(l2_ref, 2)
    m = jnp.maximum(jnp.maximum(l0, l1), l2)
    w0, w1, w2 = jnp.exp(l0 - m), jnp.exp(l1 - m), jnp.exp(l2 - m)
    y_attn = (w0 * load(o0_ref, 0) + w1 * load(o1_ref, 1) + w2 * load(o2_ref, 2)) / (w0 + w1 + w2)
    pa = jnp.dot(y_attn.astype(BF16), wab_ref[...], preferred_element_type=F32)
    ps = jnp.dot(ys_ref[...].astype(BF16), wsb_ref[...], preferred_element_type=F32)
    merged = _sigmoid(ga_ref[...] + bg_ref[0:1, :]) * pa + _sigmoid(gs_ref[...] + bg_ref[1:2, :]) * ps
    h = x_ref[...] + jnp.dot(merged.astype(BF16), wo_ref[...], preferred_element_type=F32)
    h_ref[...] = h
    hn = h * lax.rsqrt(jnp.mean(h * h, axis=-1, keepdims=True) + EPS) * n2_ref[...]
    hnb = hn.astype(BF16)
    _store_row_tiles(hn_ref, hn)

    logits = jnp.dot(hnb, wr_ref[...], preferred_element_type=F32) + br_ref[...]
    tm = logits.shape[0]
    lane = lax.broadcasted_iota(I32, (tm, LANES), 1)
    big = jnp.int32(LANES)

    def top(vals):
        v = jnp.max(vals, axis=-1, keepdims=True)
        idx = jnp.min(jnp.where(vals == v, lane, big), axis=-1, keepdims=True)
        return v, idx

    is_coarse = (lane >= N_EXPERTS) & (lane < N_EXPERTS + N_GROUPS_E)
    lc = jnp.where(is_coarse, logits, NEG)
    mc, ic = top(lc)
    p_grp = 1.0 / jnp.sum(jnp.exp(lc - mc), axis=-1, keepdims=True)
    lo = (ic - N_EXPERTS) * EXPERTS_PER_GROUP
    lf = jnp.where((lane >= lo) & (lane < lo + EXPERTS_PER_GROUP), logits, NEG)
    v1, i1 = top(lf)
    v2, i2 = top(jnp.where(lane == i1, NEG, lf))
    e2 = jnp.exp(v2 - v1)
    g1 = p_grp / (1.0 + e2)
    g2 = p_grp * e2 / (1.0 + e2)

    oh1 = lane == i1
    oh2 = lane == i2
    cnt = oh1.astype(F32) + oh2.astype(F32)
    r = lax.broadcasted_iota(I32, (tm, tm), 0)
    s = lax.broadcasted_iota(I32, (tm, tm), 1)
    before = jnp.dot((r > s).astype(BF16), cnt.astype(BF16), preferred_element_type=F32) + carry_ref[0:1, :]
    r1 = jnp.sum(jnp.where(oh1, before, 0.0), axis=-1, keepdims=True)
    r2 = jnp.sum(jnp.where(oh2, before, 0.0), axis=-1, keepdims=True)
    new_carry = carry_ref[0:1, :] + jnp.sum(cnt, axis=0, keepdims=True)
    carry_ref[...] = jnp.broadcast_to(new_carry, carry_ref.shape)
    cnt_ref[...] = jnp.broadcast_to(new_carry, cnt_ref.shape).astype(I32)

    ri = jnp.where(lane == 0, i1, jnp.where(lane == 1, i2, 0))
    ri = jnp.where(lane == 2, r1.astype(I32), jnp.where(lane == 3, r2.astype(I32), ri))
    ri_ref[...] = ri
    rf_ref[...] = jnp.where(lane == 0, g1, jnp.where(lane == 1, g2, 0.0))


def _merge(x2d, outs, lses, y_ssd, g_a, g_s, wab, wsb, b_gate, wo, norm2_w, w_router, b_router, class_major):
    n = x2d.shape[0]
    tm = min(ROW_TILE, n)
    row = lambda i: (i, 0)
    fixed = lambda i: (0, 0)
    wide = pl.BlockSpec((tm, D_MODEL), row)
    info = pl.BlockSpec((tm, LANES), row)
    if class_major:
        per_seq = outs[0].shape[2] // tm
        cls = lambda i: (i // per_seq, 0, i % per_seq, 0)
        grps = [pl.BlockSpec((1, d, tm // d, GROUP_W), cls) for _, d in ATTN_GROUPS]
        scratch = [pltpu.VMEM((GROUP_W // LANES, tm, LANES), F32)]
    else:
        grps = [pl.BlockSpec((tm, GROUP_W), row)] * 3
        scratch = []
    return pl.pallas_call(
        functools.partial(_merge_kernel, class_major=class_major),
        grid=(n // tm,),
        in_specs=[wide, *grps, *grps, wide, wide, wide,
                  pl.BlockSpec((GROUP_W, D_MODEL), fixed),
                  pl.BlockSpec((SSD_INNER, D_MODEL), fixed),
                  pl.BlockSpec((2, D_MODEL), fixed),
                  pl.BlockSpec((D_MODEL, D_MODEL), fixed),
                  pl.BlockSpec((1, D_MODEL), fixed),
                  pl.BlockSpec((D_MODEL, LANES), fixed),
                  pl.BlockSpec((1, LANES), fixed)],
        out_specs=[wide, pl.BlockSpec((tm * ROW_CHUNKS, LANES), row), info, info,
                   pl.BlockSpec((SUBLANES, LANES), fixed)],
        out_shape=[jax.ShapeDtypeStruct((n, D_MODEL), F32),
                   jax.ShapeDtypeStruct((n * ROW_CHUNKS, LANES), F32),
                   jax.ShapeDtypeStruct((n, LANES), I32),
                   jax.ShapeDtypeStruct((n, LANES), F32),
                   jax.ShapeDtypeStruct((SUBLANES, LANES), I32)],
        scratch_shapes=[pltpu.VMEM((SUBLANES, LANES), F32)] + scratch,
        compiler_params=_cparams(("arbitrary",)),
        name="merge_out",
    )(x2d, *outs, *lses, y_ssd, g_a, g_s, wab, wsb, b_gate, wo, norm2_w.reshape(1, D_MODEL), w_router, b_router)


GATHER_TILE = 256
DISPATCH_TILE = 512
INDEX_BATCH = 16


def _dispatch_kernel(dest_ref, hn_ref, xs_init_hbm, xs_hbm, sem, *, tm):
    del xs_init_hbm
    i = pl.program_id(0)
    for j0 in range(0, 2 * tm, INDEX_BATCH):
        slots = [dest_ref[i * 2 * tm + j0 + u] for u in range(INDEX_BATCH)]
        for u, dst in enumerate(slots):
            tok = (j0 + u) % tm
            first = pl.multiple_of(dst * ROW_CHUNKS, ROW_CHUNKS)
            pltpu.make_async_copy(hn_ref.at[pl.ds(tok * ROW_CHUNKS, ROW_CHUNKS), :],
                                  xs_hbm.at[pl.ds(first, ROW_CHUNKS), :], sem).start()
    for _ in range(2):
        pltpu.make_async_copy(hn_ref, xs_hbm.at[pl.ds(0, tm * ROW_CHUNKS), :], sem).wait()


def _dispatch(hn, dest_tiles, xs, tm):
    n = hn.shape[0] // ROW_CHUNKS
    anyspec = pl.BlockSpec(memory_space=pl.ANY)
    return pl.pallas_call(
        functools.partial(_dispatch_kernel, tm=tm),
        grid_spec=pltpu.PrefetchScalarGridSpec(
            num_scalar_prefetch=1,
            grid=(n // tm,),
            in_specs=[pl.BlockSpec((tm * ROW_CHUNKS, LANES), lambda i, d: (i, 0)), anyspec],
            out_specs=anyspec,
            scratch_shapes=[pltpu.SemaphoreType.DMA(())]),
        out_shape=jax.ShapeDtypeStruct(xs.shape, F32),
        input_output_aliases={2: 0},
        compiler_params=_cparams(("arbitrary",), disable_bounds_checks=True, has_side_effects=True),
        name="dispatch",
    )(dest_tiles, hn, xs)


def _expert_kernel(te_ref, nu_ref, x_ref, wg_ref, wu_ref, wd_ref, o_ref, wgb, wub, wdb):
    i = pl.program_id(0)
    changed = (i == 0) | (te_ref[i] != te_ref[jnp.maximum(i - 1, 0)])

    @pl.when(changed)
    def _():
        wgb[...] = wg_ref[0].astype(BF16)
        wub[...] = wu_ref[0].astype(BF16)
        wdb[...] = wd_ref[0].astype(BF16)

    @pl.when(i < nu_ref[0])
    def _():
        x = _load_row_tiles(x_ref, 0, EXPERT_TILE).astype(BF16)
        hg = jnp.dot(x, wgb[...], preferred_element_type=F32)
        hu = jnp.dot(x, wub[...], preferred_element_type=F32)
        hb = (hg * _sigmoid(hg)) * hu
        _store_row_tiles(o_ref, jnp.dot(hb.astype(BF16), wdb[...], preferred_element_type=F32))

    @pl.when(i >= nu_ref[0])
    def _():
        o_ref[...] = jnp.zeros_like(o_ref)


def _experts(xs, tile_expert, n_used, w_eg, w_eu, w_ed):
    n_slots = xs.shape[0] // ROW_CHUNKS
    n_tiles = n_slots // EXPERT_TILE
    row = lambda i, te, nu: (i, 0)
    wsel = lambda i, te, nu: (te[i], 0, 0)
    return pl.pallas_call(
        _expert_kernel,
        grid_spec=pltpu.PrefetchScalarGridSpec(
            num_scalar_prefetch=2,
            grid=(n_tiles,),
            in_specs=[pl.BlockSpec((EXPERT_TILE * ROW_CHUNKS, LANES), row),
                      pl.BlockSpec((1, D_MODEL, D_FF), wsel),
                      pl.BlockSpec((1, D_MODEL, D_FF), wsel),
                      pl.BlockSpec((1, D_FF, D_MODEL), wsel)],
            out_specs=pl.BlockSpec((EXPERT_TILE * ROW_CHUNKS, LANES), row),
            scratch_shapes=[pltpu.VMEM((D_MODEL, D_FF), BF16),
                            pltpu.VMEM((D_MODEL, D_FF), BF16),
                            pltpu.VMEM((D_FF, D_MODEL), BF16)]),
        out_shape=jax.ShapeDtypeStruct((n_slots * ROW_CHUNKS, LANES), F32),
        compiler_params=_cparams(("arbitrary",)),
        name="experts",
    )(tile_expert, n_used, xs, w_eg, w_eu, w_ed)


def _final_kernel(dest_ref, h_ref, rf_ref, fw_ref, out_hbm, o_ref, ybuf, sem, *, tm):
    i = pl.program_id(0)
    slot = i % 2

    def gather(tile, into):
        for j0 in range(0, 2 * tm, INDEX_BATCH):
            slots = [dest_ref[tile * 2 * tm + j0 + u] for u in range(INDEX_BATCH)]
            for u, src in enumerate(slots):
                first = pl.multiple_of(src * ROW_CHUNKS, ROW_CHUNKS)
                pltpu.make_async_copy(out_hbm.at[pl.ds(first, ROW_CHUNKS), :],
                                      ybuf.at[into, pl.ds((j0 + u) * ROW_CHUNKS, ROW_CHUNKS), :],
                                      sem.at[into]).start()

    @pl.when(i == 0)
    def _():
        gather(0, 0)

    @pl.when(i + 1 < pl.num_programs(0))
    def _():
        gather(i + 1, 1 - slot)

    pltpu.make_async_copy(out_hbm.at[pl.ds(0, 2 * tm * ROW_CHUNKS), :], ybuf.at[slot], sem.at[slot]).wait()
    g = rf_ref[...]
    moe = (_load_row_tiles(ybuf.at[slot], 0, tm) * g[:, 0:1] + _load_row_tiles(ybuf.at[slot], tm, tm) * g[:, 1:2])
    h = h_ref[...] + moe
    o_ref[...] = h * lax.rsqrt(jnp.mean(h * h, axis=-1, keepdims=True) + EPS) * fw_ref[...]


def _final(h, out, dest_tiles, rf, final_w):
    n = h.shape[0]
    tm = min(GATHER_TILE, n)
    row = lambda i, d: (i, 0)
    wide = pl.BlockSpec((tm, D_MODEL), row)
    return pl.pallas_call(
        functools.partial(_final_kernel, tm=tm),
        grid_spec=pltpu.PrefetchScalarGridSpec(
            num_scalar_prefetch=1,
            grid=(n // tm,),
            in_specs=[wide, pl.BlockSpec((tm, LANES), row), pl.BlockSpec((1, D_MODEL), lambda i, d: (0, 0)),
                      pl.BlockSpec(memory_space=pl.ANY)],
            out_specs=wide,
            scratch_shapes=[pltpu.VMEM((2, 2 * tm * ROW_CHUNKS, LANES), F32), pltpu.SemaphoreType.DMA((2,))]),
        out_shape=jax.ShapeDtypeStruct((n, D_MODEL), F32),
        compiler_params=_cparams(("arbitrary",), disable_bounds_checks=True),
        name="final",
    )(dest_tiles, h, rf, final_w.reshape(1, D_MODEL), out)


def _moe_and_final(groups, w_eg, w_eu, w_ed, final_w):
    group_counts = [g[4][0, :N_EXPERTS] for g in groups]
    counts = sum(group_counts)
    padded = ((counts + EXPERT_TILE - 1) // EXPERT_TILE) * EXPERT_TILE
    ends = jnp.cumsum(padded)
    starts = ends - padded
    n_assign = sum(2 * g[0].shape[0] for g in groups)
    n_tiles = n_assign // EXPERT_TILE + N_EXPERTS
    n_slots = n_tiles * EXPERT_TILE
    experts = jnp.arange(N_EXPERTS, dtype=I32)
    tile_start = jnp.arange(n_tiles, dtype=I32) * EXPERT_TILE
    tile_expert = jnp.minimum(jnp.sum((ends[None, :] <= tile_start[:, None]).astype(I32), axis=1), N_EXPERTS - 1)
    n_used = (ends[-1] // EXPERT_TILE).astype(I32).reshape(1)

    xs = jnp.zeros((n_slots * ROW_CHUNKS, LANES), F32)
    dests = []
    base = starts
    for (h, hn, ri, rf, _), cnt in zip(groups, group_counts):
        n = h.shape[0]
        dest = jnp.sum(jnp.where(ri[:, 0:2, None] == experts, base, 0), axis=-1) + ri[:, 2:4]
        by_tile = lambda t, dest=dest, n=n: dest.reshape(n // t, t, 2).transpose(0, 2, 1).reshape(-1)
        dests.append(by_tile(min(GATHER_TILE, n)))
        td = min(DISPATCH_TILE, n)
        xs = _dispatch(hn, by_tile(td), xs, td)
        base = base + cnt
    out = _experts(xs, tile_expert, n_used, w_eg, w_eu, w_ed)
    return [_final(h, out, dest_tiles, rf, final_w) for (h, _, _, rf, _), dest_tiles in zip(groups, dests)]


def _layer(x, pos0, kv_bufs, conv_state, ssm_state, p):
    batch, seq, _ = x.shape
    n = batch * seq
    x2d = x.reshape(n, D_MODEL)
    prompt = kv_bufs is None
    q, k, v, z, xbc, g_a, g_s, dt_raw = _inproj(x2d, batch, seq, pos0, p["norm1_w"], p["w_in"], prompt)

    outs, lses, new_kv = [], [], []
    for gi, (window, dil) in enumerate(ATTN_GROUPS):
        if prompt:
            o, l = _attn_prompt(q[gi], k[gi], v[gi], dil)
            keep = min(window, seq) // dil
            tail = lambda t: t[:, :, seq // dil - keep:, :].transpose(0, 2, 1, 3).reshape(batch, keep * dil, GROUP_W)
            nbuf = jnp.stack([tail(k[gi]), tail(v[gi])], axis=2).reshape(
                batch, keep * dil, 2, HEADS_PER_GROUP, HEAD_DIM)
        else:
            o, l, nbuf = _attn_step(q, k, v, kv_bufs[gi], batch, seq, gi, dil)
        outs.append(o)
        lses.append(l)
        new_kv.append(nbuf)

    if conv_state is None:
        conv_state = jnp.zeros((batch, 3, SSD_CONV_DIM), F32)
        ssm_state = jnp.zeros((batch, SSD_HEADS, HEAD_DIM, SSD_STATE), F32)
    y_ssd, new_conv, new_ssm = _ssd(xbc, z, dt_raw, conv_state, ssm_state, batch, seq, p["conv_w"], p["conv_b"],
                                    p["dt_bias"], p["A_log"], p["D_skip"], p["ssd_norm_w"])

    h, hn, ri, rf, counts = _merge(x2d, outs, lses, y_ssd, g_a, g_s, p["w_attn_br"], p["w_ssd_br"], p["b_gate"],
                                   p["w_out"], p["norm2_w"], p["w_router"], p["b_router"], prompt)
    return (h, hn, ri, rf, counts), new_kv, new_conv, new_ssm


def _pack_w_in(w_in):
    offs = (0, 768, 1536, 2304, 3328, 4864, 4880, 5904, 6928)
    q, k, v, z, xbc, dt, g_a, g_s = (w_in[:, offs[i]:offs[i + 1]] for i in range(8))
    dt = jnp.pad(dt, ((0, 0), (0, LANES - SSD_HEADS)))
    return jnp.concatenate([q, k, v, z, xbc, g_a, g_s, dt], axis=1).astype(BF16)


def kernel(x_prompt, x_sample, cache_kv_w128, cache_kv_w512, cache_kv_w2048, state_conv, state_ssm, norm1_w, w_in, w_attn_br, w_ssd_br, b_gate, w_out, conv_w, conv_b, dt_bias, A_log, D_skip, ssd_norm_w, norm2_w, w_router_coarse, b_router_coarse, w_router_fine, b_router_fine, w_expert_gate, w_expert_up, w_expert_down, final_norm_w):
    depth = norm1_w.shape[0]
    assert depth == 1, "the final norm is fused into the layer's last kernel"
    l = 0
    rpad = LANES - N_EXPERTS - N_GROUPS_E
    p = dict(
        norm1_w=norm1_w[l], w_in=_pack_w_in(w_in[l]),
        w_attn_br=w_attn_br[l].astype(BF16), w_ssd_br=w_ssd_br[l].astype(BF16), b_gate=b_gate[l],
        w_out=w_out[l].astype(BF16), conv_w=conv_w[l], conv_b=conv_b[l], dt_bias=dt_bias[l], A_log=A_log[l],
        D_skip=D_skip[l], ssd_norm_w=ssd_norm_w[l], norm2_w=norm2_w[l],
        w_router=jnp.pad(jnp.concatenate([w_router_fine[l], w_router_coarse[l]], axis=1),
                         ((0, 0), (0, rpad))).astype(BF16),
        b_router=jnp.pad(jnp.concatenate([b_router_fine[l], b_router_coarse[l]]), (0, rpad)).reshape(1, LANES),
        w_eg=w_expert_gate[l], w_eu=w_expert_up[l], w_ed=w_expert_down[l], final_norm_w=final_norm_w,
    )
    g_p, kv_p, c_p, st_p = _layer(x_prompt, 0, None, None, None, p)
    bufs = (cache_kv_w128[l], cache_kv_w512[l], cache_kv_w2048[l])
    g_s, kv_s, c_s, st_s = _layer(x_sample, PAST_LEN, bufs, state_conv[l], state_ssm[l], p)
    y_p, y_s = _moe_and_final([g_p, g_s], p["w_eg"], p["w_eu"], p["w_ed"], p["final_norm_w"])
    y_p = y_p.reshape(x_prompt.shape)
    y_s = y_s.reshape(x_sample.shape)
    lead = lambda t: t[None]
    return (y_p, y_s, lead(kv_p[0]), lead(kv_p[1]), lead(kv_p[2]), lead(c_p), lead(st_p),
            lead(kv_s[0]), lead(kv_s[1]), lead(kv_s[2]), lead(c_s), lead(st_s))
```

```python
import functools

import jax
import jax.numpy as jnp
from jax import lax
from jax.experimental import pallas as pl
from jax.experimental.pallas import tpu as pltpu

F32 = jnp.float32
BF16 = jnp.bfloat16
I32 = jnp.int32

D_MODEL = 1024
HEAD_DIM = 64
ATTN_GROUPS = ((128, 1), (512, 4), (2048, 16))
SPAN = 128
HEADS_PER_GROUP = 4
GROUP_W = HEADS_PER_GROUP * HEAD_DIM
ATTN_W = GROUP_W * len(ATTN_GROUPS)
ROPE_THETA = 10000.0
PAST_LEN = 8192
SSD_INNER = 1024
SSD_HEADS = 16
SSD_STATE = 128
SSD_CONV_DIM = 1536
SSD_CHUNK = 128
N_GROUPS_E = 4
EXPERTS_PER_GROUP = 8
N_EXPERTS = 32
D_FF = 512
EPS = 1e-6
LANES = 128
SUBLANES = 8
NEG = -1e30

_OFF_Q, _OFF_K, _OFF_V, _OFF_Z, _OFF_XBC, _OFF_GA, _OFF_GS, _OFF_DT = 0, 768, 1536, 2304, 3328, 4864, 5888, 6912
_W_IN_COLS = 7040

ROW_TILE = 256
EXPERT_TILE = 256
VMEM_LIMIT = 56 * 1024 * 1024


def _cparams(sem, **kw):
    return pltpu.CompilerParams(dimension_semantics=sem, vmem_limit_bytes=VMEM_LIMIT, **kw)


def _sigmoid(x):
    return 1.0 / (1.0 + jnp.exp(-x))


ROW_CHUNKS = D_MODEL // LANES


def _store_row_tiles(ref, val):
    rows = val.shape[0]
    for c in range(ROW_CHUNKS):
        ref[pl.ds(c, rows, stride=ROW_CHUNKS), :] = val[:, c * LANES:(c + 1) * LANES]


def _load_row_tiles(ref, first, rows):
    return jnp.concatenate(
        [ref[pl.ds(first * ROW_CHUNKS + c, rows, stride=ROW_CHUNKS), :] for c in range(ROW_CHUNKS)], axis=1)


def _nt_dot(a, b):
    return lax.dot_general(a, b, (((1,), (1,)), ((), ())), preferred_element_type=F32)


def _store_by_class(o_ref, scr_ref, val, dil):
    if dil == 1:
        o_ref[0, 0] = val
        return
    rows = val.shape[0]
    for half in range(GROUP_W // LANES):
        lanes = slice(half * LANES, (half + 1) * LANES)
        scr_ref[half] = val[:, lanes]
        for r in range(dil):
            o_ref[0, r, :, lanes] = scr_ref[half, pl.ds(r, rows // dil, stride=dil), :]


def _inproj_kernel(x_ref, nw_ref, w_ref, cos_ref, sin_ref, *refs, class_major):
    if class_major:
        qkv_refs, (z_ref, xbc_ref, ga_ref, gs_ref, dt_ref, scr_ref) = refs[:9], refs[9:]
    else:
        qkv_refs, (z_ref, xbc_ref, ga_ref, gs_ref, dt_ref) = refs[:3], refs[3:]
    x = x_ref[...]
    xn = x * lax.rsqrt(jnp.mean(x * x, axis=-1, keepdims=True) + EPS) * nw_ref[...]
    xb = xn.astype(BF16)
    cos = cos_ref[...]
    sin = sin_ref[...]
    lane = lax.broadcasted_iota(I32, cos.shape, 1)
    first_half = (lane % HEAD_DIM) < (HEAD_DIM // 2)

    def mm(lo, hi):
        return jnp.dot(xb, w_ref[:, lo:hi], preferred_element_type=F32)

    def rope(uc):
        ur = jnp.where(first_half, pltpu.roll(uc, LANES - HEAD_DIM // 2, 1), pltpu.roll(uc, HEAD_DIM // 2, 1))
        return uc * cos + ur * sin

    def emit(which, base, roped):
        u = mm(base, base + ATTN_W)
        chunks = [u[:, c * LANES:(c + 1) * LANES] for c in range(ATTN_W // LANES)]
        if roped:
            chunks = [rope(uc) for uc in chunks]
        if class_major:
            for gi, (_, dil) in enumerate(ATTN_GROUPS):
                val = jnp.concatenate(chunks[2 * gi:2 * gi + 2], axis=1)
                _store_by_class(qkv_refs[3 * which + gi], scr_ref, val, dil)
        else:
            for c, uc in enumerate(chunks):
                qkv_refs[which][:, c * LANES:(c + 1) * LANES] = uc

    emit(0, _OFF_Q, True)
    emit(1, _OFF_K, True)
    emit(2, _OFF_V, False)
    z_ref[...] = mm(_OFF_Z, _OFF_XBC)
    xbc_ref[...] = mm(_OFF_XBC, _OFF_GA)
    ga_ref[...] = mm(_OFF_GA, _OFF_GS)
    gs_ref[...] = mm(_OFF_GS, _OFF_DT)
    dt_ref[...] = mm(_OFF_DT, _W_IN_COLS)


def _rope_tables(pos):
    half = HEAD_DIM // 2
    inv_freq = ROPE_THETA ** (-jnp.arange(half, dtype=F32) * (2.0 / HEAD_DIM))
    ang = pos.astype(F32)[:, None] * inv_freq[None, :]
    ang = jnp.tile(ang, (1, LANES // half))
    lane = jnp.arange(LANES)
    sign = jnp.where((lane % HEAD_DIM) < half, -1.0, 1.0).astype(F32)
    return jnp.cos(ang), jnp.sin(ang) * sign[None, :]


def _inproj(x2d, batch, seq_len, pos0, norm_w, w_packed, class_major):
    n = x2d.shape[0]
    tm = min(ROW_TILE, n)
    cos, sin = _rope_tables(pos0 + jnp.arange(seq_len, dtype=I32))
    if seq_len < tm:
        cos = jnp.tile(cos, (tm // seq_len, 1))
        sin = jnp.tile(sin, (tm // seq_len, 1))
    tab_blocks = cos.shape[0] // tm
    row = lambda i: (i, 0)
    fixed = lambda i: (0, 0)
    tab = lambda i: (i % tab_blocks, 0)
    widths = (SSD_INNER, SSD_CONV_DIM, D_MODEL, D_MODEL, LANES)
    out_specs = [pl.BlockSpec((tm, w), row) for w in widths]
    out_shape = [jax.ShapeDtypeStruct((n, w), F32) for w in widths]
    scratch = []
    if class_major:
        assert seq_len % tm == 0
        per_seq = seq_len // tm
        cls = lambda i: (i // per_seq, 0, i % per_seq, 0)
        dils = [d for _, d in ATTN_GROUPS] * 3
        out_specs = [pl.BlockSpec((1, d, tm // d, GROUP_W), cls) for d in dils] + out_specs
        out_shape = [jax.ShapeDtypeStruct((batch, d, seq_len // d, GROUP_W), F32) for d in dils] + out_shape
        scratch = [pltpu.VMEM((GROUP_W // LANES, tm, LANES), F32)]
    else:
        out_specs = [pl.BlockSpec((tm, ATTN_W), row)] * 3 + out_specs
        out_shape = [jax.ShapeDtypeStruct((n, ATTN_W), F32)] * 3 + out_shape
    res = pl.pallas_call(
        functools.partial(_inproj_kernel, class_major=class_major),
        grid=(n // tm,),
        in_specs=[pl.BlockSpec((tm, D_MODEL), row),
                  pl.BlockSpec((1, D_MODEL), fixed),
                  pl.BlockSpec((D_MODEL, _W_IN_COLS), fixed, pipeline_mode=pl.Buffered(1)),
                  pl.BlockSpec((tm, LANES), tab),
                  pl.BlockSpec((tm, LANES), tab)],
        out_specs=out_specs,
        out_shape=out_shape,
        scratch_shapes=scratch,
        compiler_params=_cparams(("parallel",)),
        name="inproj",
    )(x2d, norm_w.reshape(1, D_MODEL), w_packed, cos, sin)
    if class_major:
        return (tuple(res[0:3]), tuple(res[3:6]), tuple(res[6:9])) + tuple(res[9:])
    return tuple(res)


ATTN_QBLOCKS = 4


def _attn_prompt_kernel(q_ref, kp_ref, kc_ref, vp_ref, vc_ref, o_ref, l_ref, k_scr, v_scr, s_scr, p_scr, r_scr,
                        *, qblocks):
    c = pl.program_id(2)
    i = lax.broadcasted_iota(I32, (SPAN, 2 * SPAN), 0)
    j = lax.broadcasted_iota(I32, (SPAN, 2 * SPAN), 1)
    band = (j >= i) & (j <= i + SPAN)
    first_band = band & ((j >= SPAN) | (c > 0))
    k_scr[0:SPAN, :] = kp_ref[0, 0].astype(BF16)
    k_scr[SPAN:, :] = kc_ref[0, 0].astype(BF16)
    v_scr[0:SPAN, :] = vp_ref[0, 0].astype(BF16)
    v_scr[SPAN:, :] = vc_ref[0, 0].astype(BF16)
    units = [(b, h) for b in range(qblocks) for h in range(HEADS_PER_GROUP)]

    for u, (b, h) in enumerate(units):
        hs = slice(h * HEAD_DIM, (h + 1) * HEAD_DIM)
        qh = (q_ref[0, 0, b * SPAN:(b + 1) * SPAN, hs] * (HEAD_DIM ** -0.5)).astype(BF16)
        s_scr[u] = _nt_dot(qh, k_scr[b * SPAN:(b + 2) * SPAN, hs])

    for u, (b, h) in enumerate(units):
        hs = slice(h * HEAD_DIM, (h + 1) * HEAD_DIM)
        s = jnp.where(first_band if b == 0 else band, s_scr[u], NEG)
        m = jnp.max(jnp.maximum(s[:, :SPAN], s[:, SPAN:]), axis=-1, keepdims=True)
        p = jnp.exp(s - m)
        den = jnp.sum(p[:, :SPAN] + p[:, SPAN:], axis=-1, keepdims=True)
        p_scr[u] = p.astype(BF16)
        r_scr[u] = jnp.broadcast_to(1.0 / den, (SPAN, HEAD_DIM))
        l_ref[0, 0, b * SPAN:(b + 1) * SPAN, hs] = jnp.broadcast_to(m + jnp.log(den), (SPAN, HEAD_DIM))

    for u, (b, h) in enumerate(units):
        hs = slice(h * HEAD_DIM, (h + 1) * HEAD_DIM)
        acc = jnp.dot(p_scr[u], v_scr[b * SPAN:(b + 2) * SPAN, hs], preferred_element_type=F32)
        o_ref[0, 0, b * SPAN:(b + 1) * SPAN, hs] = acc * r_scr[u]


def _attn_prompt(q, k, v, dil):
    batch, _, n_cls, _ = q.shape
    qblocks = min(ATTN_QBLOCKS, n_cls // SPAN)
    tq = qblocks * SPAN
    assert n_cls % tq == 0
    cur = lambda b, r, c: (b, r, c, 0)
    prev = lambda b, r, c: (b, r, jnp.maximum(c * qblocks - 1, 0), 0)
    big = pl.BlockSpec((1, 1, tq, GROUP_W), cur)
    small = pl.BlockSpec((1, 1, SPAN, GROUP_W), prev)
    return pl.pallas_call(
        functools.partial(_attn_prompt_kernel, qblocks=qblocks),
        grid=(batch, dil, n_cls // tq),
        in_specs=[big, small, big, small, big],
        out_specs=[big, big],
        out_shape=[jax.ShapeDtypeStruct(q.shape, F32)] * 2,
        scratch_shapes=[pltpu.VMEM((SPAN + tq, GROUP_W), BF16),
                        pltpu.VMEM((SPAN + tq, GROUP_W), BF16),
                        pltpu.VMEM((qblocks * HEADS_PER_GROUP, SPAN, 2 * SPAN), F32),
                        pltpu.VMEM((qblocks * HEADS_PER_GROUP, SPAN, 2 * SPAN), BF16),
                        pltpu.VMEM((qblocks * HEADS_PER_GROUP, SPAN, HEAD_DIM), F32)],
        compiler_params=_cparams(("parallel", "parallel", "arbitrary")),
        name=f"attn_prompt_d{dil}",
    )(q, k, k, v, v)


def _attn_step_kernel(q_ref, kn_ref, vn_ref, buf_ref, o_ref, l_ref, nbuf_ref, *, buf_len, dil, n_new):
    q = q_ref[0] * (HEAD_DIM ** -0.5)
    t = lax.broadcasted_iota(I32, (n_new, buf_len), 0)
    i = lax.broadcasted_iota(I32, (n_new, buf_len), 1)
    delta = buf_len + t - i
    valid_b = delta <= dil * SPAN
    first_new = LANES - n_new
    tn = lax.broadcasted_iota(I32, (n_new, LANES), 0)
    un = lax.broadcasted_iota(I32, (n_new, LANES), 1) - first_new
    dn = tn - un
    valid_n = (dn >= 0) & (un >= 0)
    if dil > 1:
        valid_b = valid_b & ((delta & (dil - 1)) == 0)
        valid_n = valid_n & ((dn & (dil - 1)) == 0)

    def to_columns(x):
        xp = jnp.concatenate([x, jnp.zeros((LANES - n_new, GROUP_W), F32)], axis=0)
        xt = jnp.concatenate([xp[:, c * LANES:(c + 1) * LANES].T for c in range(GROUP_W // LANES)], axis=0)
        return pltpu.roll(xt, first_new, 1)

    new_cols = (to_columns(kn_ref[0]), to_columns(vn_ref[0]))
    is_new = lax.broadcasted_iota(I32, (HEAD_DIM, LANES), 1) >= first_new
    tail = slice(buf_len - LANES, buf_len)
    for h in range(HEADS_PER_GROUP):
        hs = slice(h * HEAD_DIM, (h + 1) * HEAD_DIM)
        qh = q[:, hs].astype(BF16)
        kt = buf_ref[0, 0, h]
        vt = buf_ref[0, 1, h]
        knt = new_cols[0][hs]
        vnt = new_cols[1][hs]
        sb = jnp.where(valid_b, jnp.dot(qh, kt.astype(BF16), preferred_element_type=F32), NEG)
        sn = jnp.where(valid_n, jnp.dot(qh, knt.astype(BF16), preferred_element_type=F32), NEG)
        m = jnp.maximum(jnp.max(sb, axis=-1, keepdims=True), jnp.max(sn, axis=-1, keepdims=True))
        pb = jnp.exp(sb - m)
        pn = jnp.exp(sn - m)
        den = jnp.sum(pb, axis=-1, keepdims=True) + jnp.sum(pn, axis=-1, keepdims=True)
        acc = _nt_dot(pb.astype(BF16), vt.astype(BF16)) + _nt_dot(pn.astype(BF16), vnt.astype(BF16))
        o_ref[0, :, hs] = acc / den
        l_ref[0, :, hs] = jnp.broadcast_to(m + jnp.log(den), (n_new, HEAD_DIM))
        for kv, (old, new) in enumerate(((kt, knt), (vt, vnt))):
            shifted = pltpu.roll(old, buf_len - n_new, 1)
            nbuf_ref[0, kv, h] = shifted
            nbuf_ref[0, kv, h, :, tail] = jnp.where(is_new, new, shifted[:, tail])


def _attn_step(q, k, v, buf, batch, n_new, gi, dil):
    buf_len = buf.shape[1]
    assert dil & (dil - 1) == 0 and buf_len >= dil * SPAN and n_new % SUBLANES == 0 and buf_len % LANES == 0
    view = lambda t: t.reshape(batch, n_new, ATTN_W)
    tok = pl.BlockSpec((1, n_new, GROUP_W), lambda b: (b, 0, gi))
    full = pl.BlockSpec((1, 2, HEADS_PER_GROUP, HEAD_DIM, buf_len), lambda b: (b, 0, 0, 0, 0))
    osp = pl.BlockSpec((1, n_new, GROUP_W), lambda b: (b, 0, 0))
    o, l, nbuf = pl.pallas_call(
        functools.partial(_attn_step_kernel, buf_len=buf_len, dil=dil, n_new=n_new),
        grid=(batch,),
        in_specs=[tok, tok, tok, full],
        out_specs=[osp, osp, full],
        out_shape=[jax.ShapeDtypeStruct((batch, n_new, GROUP_W), F32)] * 2
        + [jax.ShapeDtypeStruct((batch, 2, HEADS_PER_GROUP, HEAD_DIM, buf_len), F32)],
        compiler_params=_cparams(("parallel",)),
        name=f"attn_step_d{dil}",
    )(view(q), view(k), view(v), jnp.transpose(buf, (0, 2, 3, 4, 1)))
    return (o.reshape(batch * n_new, GROUP_W), l.reshape(batch * n_new, GROUP_W),
            jnp.transpose(nbuf, (0, 4, 1, 2, 3)))


def _split3(a):
    a1 = a.astype(BF16)
    r1 = a - a1.astype(F32)
    a2 = r1.astype(BF16)
    a3 = (r1 - a2.astype(F32)).astype(BF16)
    return a1, a2, a3


def _ssd_kernel(xbc_ref, z_ref, dt_ref, cst_ref, h0_ref, cw_ref, cb_ref, dtb_ref, alog_ref, dfull_ref, nw_ref,
                tri_ref, expand_ref, y_ref, cout_ref, hout_ref, xpad_ref, h_ref, *, n_valid, n_chunks):
    c = pl.program_id(1)
    lc = SSD_CHUNK
    pad = SUBLANES
    n_slabs = SSD_CONV_DIM // LANES

    @pl.when(c == 0)
    def _():
        for j in range(n_slabs):
            xpad_ref[j, 0:pad, :] = cst_ref[0, :, j * LANES:(j + 1) * LANES]
        h_ref[...] = h0_ref[0]

    @pl.when(c > 0)
    def _():
        for j in range(n_slabs):
            xpad_ref[j, 0:pad, :] = xpad_ref[j, lc:lc + pad, :]

    if n_valid == lc:
        z = z_ref[...]
        dtr = dt_ref[...]
    else:
        fill = lambda w: jnp.zeros((lc - n_valid, w), F32)
        z = jnp.concatenate([z_ref[...], fill(SSD_INNER)], axis=0)
        dtr = jnp.concatenate([dt_ref[...], fill(LANES)], axis=0)

    slabs = []
    for j in range(n_slabs):
        cols = slice(j * LANES, (j + 1) * LANES)
        xpad_ref[j, pad:pad + n_valid, :] = xbc_ref[:, cols]
        if n_valid < lc:
            xpad_ref[j, pad + n_valid:pad + lc, :] = jnp.zeros((lc - n_valid, LANES), F32)
        xc = cb_ref[:, cols]
        for tap in range(4):
            xc = xc + xpad_ref[j, pl.ds(pad - 3 + tap, lc), :] * cw_ref[tap:tap + 1, cols]
        slabs.append(xc * _sigmoid(xc))
        cout_ref[0, :, cols] = xpad_ref[j, pl.ds(pad + n_valid - 3, 3), :]
    xs = jnp.concatenate(slabs[:SSD_INNER // LANES], axis=1)
    bm = slabs[SSD_INNER // LANES:SSD_INNER // LANES + 2]
    cm = slabs[SSD_INNER // LANES + 2:]

    row = lax.broadcasted_iota(I32, (lc, lc), 0)
    col = lax.broadcasted_iota(I32, (lc, lc), 1)
    causal = row >= col
    dtv = dtr + dtb_ref[...]
    dt = jnp.maximum(dtv, 0.0) + jnp.log1p(jnp.exp(-jnp.abs(dtv)))
    if n_valid < lc:
        dt = jnp.where(row < n_valid, dt, 0.0)
    a = dt * (-jnp.exp(alog_ref[...]))
    tri = tri_ref[...]
    a_cs = sum(jnp.dot(tri, p, preferred_element_type=F32) for p in _split3(a))
    a_cs_t = a_cs.T
    expand = expand_ref[...]
    a_full = sum(jnp.dot(p, expand, preferred_element_type=F32) for p in _split3(a_cs))
    dt_full = sum(jnp.dot(p, expand, preferred_element_type=F32) for p in _split3(dt))
    xdt = xs * dt_full
    xd = xdt * jnp.exp(a_full[lc - 1:lc, :] - a_full)
    grow = jnp.exp(a_full)
    xdt_b = xdt.astype(BF16)
    xd_t = jnp.concatenate([xd[:, k * LANES:(k + 1) * LANES].T for k in range(SSD_INNER // LANES)],
                           axis=0).astype(BF16)
    lane = lax.broadcasted_iota(I32, (lc, LANES), 1)
    low_half = lane < HEAD_DIM
    zero_b = jnp.zeros((lc, LANES), BF16)
    half = SSD_INNER // 2

    y_parts = []
    for g in range(2):
        bg = bm[g].astype(BF16)
        cg = cm[g].astype(BF16)
        cb = jnp.where(causal, _nt_dot(cg, bg), 0.0)
        h_grp = h_ref[g * half:(g + 1) * half, :]
        y_off = _nt_dot(cg, h_grp.astype(BF16)) * grow[:, g * half:(g + 1) * half]
        for pair in range(4):
            e0 = g * 8 + pair * 2
            rows = slice(e0 * HEAD_DIM, (e0 + 2) * HEAD_DIM)
            ms, keep = [], []
            for k in range(2):
                e = e0 + k
                seg = a_cs[:, e:e + 1] - a_cs_t[e:e + 1, :]
                ms.append((cb * jnp.exp(jnp.minimum(seg, 0.0))).astype(BF16))
                keep.append(jnp.broadcast_to(jnp.exp(a_cs[lc - 1:lc, e:e + 1]), (HEAD_DIM, SSD_STATE)))
            pair_b = xdt_b[:, rows]
            rhs = jnp.concatenate([jnp.where(low_half, pair_b, zero_b), jnp.where(low_half, zero_b, pair_b)], axis=0)
            y_parts.append(jnp.dot(jnp.concatenate(ms, axis=1), rhs, preferred_element_type=F32)
                           + y_off[:, pair * LANES:(pair + 1) * LANES])
            st = jnp.dot(xd_t[rows, :], bg, preferred_element_type=F32)
            h_ref[rows, :] = h_grp[pair * LANES:(pair + 1) * LANES, :] * jnp.concatenate(keep, axis=0) + st

    y = jnp.concatenate(y_parts, axis=1) + xs * dfull_ref[...]
    gate = y * (z * _sigmoid(z))
    for g in range(2):
        gg = gate[:, g * half:(g + 1) * half]
        gg = gg * lax.rsqrt(jnp.mean(gg * gg, axis=-1, keepdims=True) + EPS)
        y_ref[:, g * half:(g + 1) * half] = (gg * nw_ref[:, g * half:(g + 1) * half])[0:n_valid]

    @pl.when(c == n_chunks - 1)
    def _():
        hout_ref[0] = h_ref[...]


def _ssd(xbc, z, dt_raw, conv_state, ssm_state, batch, seq, conv_w, conv_b, dt_bias, a_log, d_skip, norm_w):
    n_valid = min(seq, SSD_CHUNK)
    n_chunks = seq // n_valid
    assert n_valid % SUBLANES == 0 and seq % n_valid == 0
    padl = lambda t: jnp.pad(t.reshape(1, SSD_HEADS), ((0, 0), (0, LANES - SSD_HEADS)))
    cst = jnp.pad(conv_state, ((0, 0), (SUBLANES - 3, 0), (0, 0)))
    tri = (jnp.arange(SSD_CHUNK)[:, None] >= jnp.arange(SSD_CHUNK)[None, :]).astype(BF16)
    expand = (jnp.arange(LANES)[:, None] == jnp.arange(SSD_INNER)[None, :] // HEAD_DIM).astype(BF16)
    tokrow = lambda b, c: (b * n_chunks + c, 0)
    fixed = lambda b, c: (0, 0)
    per_b3 = lambda b, c: (b, 0, 0)
    state = pl.BlockSpec((1, SSD_INNER, SSD_STATE), per_b3)
    kern = functools.partial(_ssd_kernel, n_valid=n_valid, n_chunks=n_chunks)
    y, cout, hout = pl.pallas_call(
        kern,
        grid=(batch, n_chunks),
        in_specs=[pl.BlockSpec((n_valid, SSD_CONV_DIM), tokrow),
                  pl.BlockSpec((n_valid, SSD_INNER), tokrow),
                  pl.BlockSpec((n_valid, LANES), tokrow),
                  pl.BlockSpec((1, SUBLANES, SSD_CONV_DIM), per_b3),
                  state,
                  pl.BlockSpec((4, SSD_CONV_DIM), fixed),
                  pl.BlockSpec((1, SSD_CONV_DIM), fixed),
                  pl.BlockSpec((1, LANES), fixed),
                  pl.BlockSpec((1, LANES), fixed),
                  pl.BlockSpec((1, SSD_INNER), fixed),
                  pl.BlockSpec((1, SSD_INNER), fixed),
                  pl.BlockSpec((SSD_CHUNK, SSD_CHUNK), fixed),
                  pl.BlockSpec((LANES, SSD_INNER), fixed)],
        out_specs=[pl.BlockSpec((n_valid, SSD_INNER), tokrow),
                   pl.BlockSpec((1, 3, SSD_CONV_DIM), per_b3),
                   state],
        out_shape=[jax.ShapeDtypeStruct((batch * seq, SSD_INNER), F32),
                   jax.ShapeDtypeStruct((batch, 3, SSD_CONV_DIM), F32),
                   jax.ShapeDtypeStruct((batch, SSD_INNER, SSD_STATE), F32)],
        scratch_shapes=[pltpu.VMEM((SSD_CONV_DIM // LANES, SSD_CHUNK + SUBLANES, LANES), F32),
                        pltpu.VMEM((SSD_INNER, SSD_STATE), F32)],
        compiler_params=_cparams(("parallel", "arbitrary")),
        name="ssd",
    )(xbc, z, dt_raw, cst, ssm_state.reshape(batch, SSD_INNER, SSD_STATE), conv_w, conv_b.reshape(1, SSD_CONV_DIM),
      padl(dt_bias), padl(a_log), jnp.repeat(d_skip, HEAD_DIM).reshape(1, SSD_INNER), norm_w.reshape(1, SSD_INNER),
      tri, expand)
    return y, cout, hout.reshape(batch, SSD_HEADS, HEAD_DIM, SSD_STATE)


def _merge_kernel(x_ref, o0_ref, o1_ref, o2_ref, l0_ref, l1_ref, l2_ref, ys_ref, ga_ref, gs_ref,
                  wab_ref, wsb_ref, bg_ref, wo_ref, n2_ref, wr_ref, br_ref,
                  h_ref, hn_ref, ri_ref, rf_ref, cnt_ref, carry_ref, *scr, class_major):
    step = pl.program_id(0)

    @pl.when(step == 0)
    def _():
        carry_ref[...] = jnp.zeros_like(carry_ref)

    def load(ref, gi):
        if not class_major:
            return ref[...]
        dil = ATTN_GROUPS[gi][1]
        if dil == 1:
            return ref[0, 0]
        rows = ref.shape[1] * ref.shape[2]
        halves = []
        for half in range(GROUP_W // LANES):
            for r in range(dil):
                scr[0][half, pl.ds(r, rows // dil, stride=dil), :] = ref[0, r, :, half * LANES:(half + 1) * LANES]
            halves.append(scr[0][half])
        return jnp.concatenate(halves, axis=1)

    l0, l1, l2 = load(l0_ref, 0), load(l1_ref, 1), load(l2_ref, 2)
    m = jnp.maximum(jnp.maximum(l0, l1), l2)
    w0, w1, w2 = jnp.exp(l0 - m), jnp.exp(l1 - m), jnp.exp(l2 - m)
    y_attn = (w0 * load(o0_ref, 0) + w1 * load(o1_ref, 1) + w2 * load(o2_ref, 2)) / (w0 + w1 + w2)
    pa = jnp.dot(y_attn.astype(BF16), wab_ref[...], preferred_element_type=F32)
    ps = jnp.dot(ys_ref[...].astype(BF16), wsb_ref[...], preferred_element_type=F32)
    merged = _sigmoid(ga_ref[...] + bg_ref[0:1, :]) * pa + _sigmoid(gs_ref[...] + bg_ref[1:2, :]) * ps
    h = x_ref[...] + jnp.dot(merged.astype(BF16), wo_ref[...], preferred_element_type=F32)
    h_ref[...] = h
    hn = h * lax.rsqrt(jnp.mean(h * h, axis=-1, keepdims=True) + EPS) * n2_ref[...]
    hnb = hn.astype(BF16)
    _store_row_tiles(hn_ref, hn)

    logits = jnp.dot(hnb, wr_ref[...], preferred_element_type=F32) + br_ref[...]
    tm = logits.shape[0]
    lane = lax.broadcasted_iota(I32, (tm, LANES), 1)
    big = jnp.int32(LANES)

    def top(vals):
        v = jnp.max(vals, axis=-1, keepdims=True)
        idx = jnp.min(jnp.where(vals == v, lane, big), axis=-1, keepdims=True)
        return v, idx

    is_coarse = (lane >= N_EXPERTS) & (lane < N_EXPERTS + N_GROUPS_E)
    lc = jnp.where(is_coarse, logits, NEG)
    mc, ic = top(lc)
    p_grp = 1.0 / jnp.sum(jnp.exp(lc - mc), axis=-1, keepdims=True)
    lo = (ic - N_EXPERTS) * EXPERTS_PER_GROUP
    lf = jnp.where((lane >= lo) & (lane < lo + EXPERTS_PER_GROUP), logits, NEG)
    v1, i1 = top(lf)
    v2, i2 = top(jnp.where(lane == i1, NEG, lf))
    e2 = jnp.exp(v2 - v1)
    g1 = p_grp / (1.0 + e2)
    g2 = p_grp * e2 / (1.0 + e2)

    oh1 = lane == i1
    oh2 = lane == i2
    cnt = oh1.astype(F32) + oh2.astype(F32)
    r = lax.broadcasted_iota(I32, (tm, tm), 0)
    s = lax.broadcasted_iota(I32, (tm, tm), 1)
    before = jnp.dot((r > s).astype(BF16), cnt.astype(BF16), preferred_element_type=F32) + carry_ref[0:1, :]
    r1 = jnp.sum(jnp.where(oh1, before, 0.0), axis=-1, keepdims=True)
    r2 = jnp.sum(jnp.where(oh2, before, 0.0), axis=-1, keepdims=True)
    new_carry = carry_ref[0:1, :] + jnp.sum(cnt, axis=0, keepdims=True)
    carry_ref[...] = jnp.broadcast_to(new_carry, carry_ref.shape)
    cnt_ref[...] = jnp.broadcast_to(new_carry, cnt_ref.shape).astype(I32)

    ri = jnp.where(lane == 0, i1, jnp.where(lane == 1, i2, 0))
    ri = jnp.where(lane == 2, r1.astype(I32), jnp.where(lane == 3, r2.astype(I32), ri))
    ri_ref[...] = ri
    rf_ref[...] = jnp.where(lane == 0, g1, jnp.where(lane == 1, g2, 0.0))


def _merge(x2d, outs, lses, y_ssd, g_a, g_s, wab, wsb, b_gate, wo, norm2_w, w_router, b_router, class_major):
    n = x2d.shape[0]
    tm = min(ROW_TILE, n)
    row = lambda i: (i, 0)
    fixed = lambda i: (0, 0)
    wide = pl.BlockSpec((tm, D_MODEL), row)
    info = pl.BlockSpec((tm, LANES), row)
    if class_major:
        per_seq = outs[0].shape[2] // tm
        cls = lambda i: (i // per_seq, 0, i % per_seq, 0)
        grps = [pl.BlockSpec((1, d, tm // d, GROUP_W), cls) for _, d in ATTN_GROUPS]
        scratch = [pltpu.VMEM((GROUP_W // LANES, tm, LANES), F32)]
    else:
        grps = [pl.BlockSpec((tm, GROUP_W), row)] * 3
        scratch = []
    return pl.pallas_call(
        functools.partial(_merge_kernel, class_major=class_major),
        grid=(n // tm,),
        in_specs=[wide, *grps, *grps, wide, wide, wide,
                  pl.BlockSpec((GROUP_W, D_MODEL), fixed),
                  pl.BlockSpec((SSD_INNER, D_MODEL), fixed),
                  pl.BlockSpec((2, D_MODEL), fixed),
                  pl.BlockSpec((D_MODEL, D_MODEL), fixed),
                  pl.BlockSpec((1, D_MODEL), fixed),
                  pl.BlockSpec((D_MODEL, LANES), fixed),
                  pl.BlockSpec((1, LANES), fixed)],
        out_specs=[wide, pl.BlockSpec((tm * ROW_CHUNKS, LANES), row), info, info,
                   pl.BlockSpec((SUBLANES, LANES), fixed)],
        out_shape=[jax.ShapeDtypeStruct((n, D_MODEL), F32),
                   jax.ShapeDtypeStruct((n * ROW_CHUNKS, LANES), F32),
                   jax.ShapeDtypeStruct((n, LANES), I32),
                   jax.ShapeDtypeStruct((n, LANES), F32),
                   jax.ShapeDtypeStruct((SUBLANES, LANES), I32)],
        scratch_shapes=[pltpu.VMEM((SUBLANES, LANES), F32)] + scratch,
        compiler_params=_cparams(("arbitrary",)),
        name="merge_out",
    )(x2d, *outs, *lses, y_ssd, g_a, g_s, wab, wsb, b_gate, wo, norm2_w.reshape(1, D_MODEL), w_router, b_router)


GATHER_TILE = 256
DISPATCH_TILE = 512
INDEX_BATCH = 16


def _dispatch_kernel(dest_ref, hn_ref, xs_init_hbm, xs_hbm, sem, *, tm):
    del xs_init_hbm
    i = pl.program_id(0)
    for j0 in range(0, 2 * tm, INDEX_BATCH):
        slots = [dest_ref[i * 2 * tm + j0 + u] for u in range(INDEX_BATCH)]
        for u, dst in enumerate(slots):
            tok = (j0 + u) % tm
            first = pl.multiple_of(dst * ROW_CHUNKS, ROW_CHUNKS)
            pltpu.make_async_copy(hn_ref.at[pl.ds(tok * ROW_CHUNKS, ROW_CHUNKS), :],
                                  xs_hbm.at[pl.ds(first, ROW_CHUNKS), :], sem).start()
    for _ in range(2):
        pltpu.make_async_copy(hn_ref, xs_hbm.at[pl.ds(0, tm * ROW_CHUNKS), :], sem).wait()


def _dispatch(hn, dest_tiles, xs, tm):
    n = hn.shape[0] // ROW_CHUNKS
    anyspec = pl.BlockSpec(memory_space=pl.ANY)
    return pl.pallas_call(
        functools.partial(_dispatch_kernel, tm=tm),
        grid_spec=pltpu.PrefetchScalarGridSpec(
            num_scalar_prefetch=1,
            grid=(n // tm,),
            in_specs=[pl.BlockSpec((tm * ROW_CHUNKS, LANES), lambda i, d: (i, 0)), anyspec],
            out_specs=anyspec,
            scratch_shapes=[pltpu.SemaphoreType.DMA(())]),
        out_shape=jax.ShapeDtypeStruct(xs.shape, F32),
        input_output_aliases={2: 0},
        compiler_params=_cparams(("arbitrary",), disable_bounds_checks=True, has_side_effects=True),
        name="dispatch",
    )(dest_tiles, hn, xs)


def _expert_kernel(te_ref, nu_ref, x_ref, wg_ref, wu_ref, wd_ref, o_ref, wgb, wub, wdb):
    i = pl.program_id(0)
    changed = (i == 0) | (te_ref[i] != te_ref[jnp.maximum(i - 1, 0)])

    @pl.when(changed)
    def _():
        wgb[...] = wg_ref[0].astype(BF16)
        wub[...] = wu_ref[0].astype(BF16)
        wdb[...] = wd_ref[0].astype(BF16)

    @pl.when(i < nu_ref[0])
    def _():
        x = _load_row_tiles(x_ref, 0, EXPERT_TILE).astype(BF16)
        hg = jnp.dot(x, wgb[...], preferred_element_type=F32)
        hu = jnp.dot(x, wub[...], preferred_element_type=F32)
        hb = (hg * _sigmoid(hg)) * hu
        _store_row_tiles(o_ref, jnp.dot(hb.astype(BF16), wdb[...], preferred_element_type=F32))

    @pl.when(i >= nu_ref[0])
    def _():
        o_ref[...] = jnp.zeros_like(o_ref)


def _experts(xs, tile_expert, n_used, w_eg, w_eu, w_ed):
    n_slots = xs.shape[0] // ROW_CHUNKS
    n_tiles = n_slots // EXPERT_TILE
    row = lambda i, te, nu: (i, 0)
    wsel = lambda i, te, nu: (te[i], 0, 0)
    return pl.pallas_call(
        _expert_kernel,
        grid_spec=pltpu.PrefetchScalarGridSpec(
            num_scalar_prefetch=2,
            grid=(n_tiles,),
            in_specs=[pl.BlockSpec((EXPERT_TILE * ROW_CHUNKS, LANES), row),
                      pl.BlockSpec((1, D_MODEL, D_FF), wsel),
                      pl.BlockSpec((1, D_MODEL, D_FF), wsel),
                      pl.BlockSpec((1, D_FF, D_MODEL), wsel)],
            out_specs=pl.BlockSpec((EXPERT_TILE * ROW_CHUNKS, LANES), row),
            scratch_shapes=[pltpu.VMEM((D_MODEL, D_FF), BF16),
                            pltpu.VMEM((D_MODEL, D_FF), BF16),
                            pltpu.VMEM((D_FF, D_MODEL), BF16)]),
        out_shape=jax.ShapeDtypeStruct((n_slots * ROW_CHUNKS, LANES), F32),
        compiler_params=_cparams(("arbitrary",)),
        name="experts",
    )(tile_expert, n_used, xs, w_eg, w_eu, w_ed)


def _final_kernel(dest_ref, h_ref, rf_ref, fw_ref, out_hbm, o_ref, ybuf, sem, *, tm):
    i = pl.program_id(0)
    slot = i % 2

    def gather(tile, into):
        for j0 in range(0, 2 * tm, INDEX_BATCH):
            slots = [dest_ref[tile * 2 * tm + j0 + u] for u in range(INDEX_BATCH)]
            for u, src in enumerate(slots):
                first = pl.multiple_of(src * ROW_CHUNKS, ROW_CHUNKS)
                pltpu.make_async_copy(out_hbm.at[pl.ds(first, ROW_CHUNKS), :],
                                      ybuf.at[into, pl.ds((j0 + u) * ROW_CHUNKS, ROW_CHUNKS), :],
                                      sem.at[into]).start()

    @pl.when(i == 0)
    def _():
        gather(0, 0)

    @pl.when(i + 1 < pl.num_programs(0))
    def _():
        gather(i + 1, 1 - slot)

    pltpu.make_async_copy(out_hbm.at[pl.ds(0, 2 * tm * ROW_CHUNKS), :], ybuf.at[slot], sem.at[slot]).wait()
    g = rf_ref[...]
    moe = (_load_row_tiles(ybuf.at[slot], 0, tm) * g[:, 0:1] + _load_row_tiles(ybuf.at[slot], tm, tm) * g[:, 1:2])
    h = h_ref[...] + moe
    o_ref[...] = h * lax.rsqrt(jnp.mean(h * h, axis=-1, keepdims=True) + EPS) * fw_ref[...]


def _final(h, out, dest_tiles, rf, final_w):
    n = h.shape[0]
    tm = min(GATHER_TILE, n)
    row = lambda i, d: (i, 0)
    wide = pl.BlockSpec((tm, D_MODEL), row)
    return pl.pallas_call(
        functools.partial(_final_kernel, tm=tm),
        grid_spec=pltpu.PrefetchScalarGridSpec(
            num_scalar_prefetch=1,
            grid=(n // tm,),
            in_specs=[wide, pl.BlockSpec((tm, LANES), row), pl.BlockSpec((1, D_MODEL), lambda i, d: (0, 0)),
                      pl.BlockSpec(memory_space=pl.ANY)],
            out_specs=wide,
            scratch_shapes=[pltpu.VMEM((2, 2 * tm * ROW_CHUNKS, LANES), F32), pltpu.SemaphoreType.DMA((2,))]),
        out_shape=jax.ShapeDtypeStruct((n, D_MODEL), F32),
        compiler_params=_cparams(("arbitrary",), disable_bounds_checks=True),
        name="final",
    )(dest_tiles, h, rf, final_w.reshape(1, D_MODEL), out)


def _moe_and_final(groups, w_eg, w_eu, w_ed, final_w):
    group_counts = [g[4][0, :N_EXPERTS] for g in groups]
    counts = sum(group_counts)
    padded = ((counts + EXPERT_TILE - 1) // EXPERT_TILE) * EXPERT_TILE
    ends = jnp.cumsum(padded)
    starts = ends - padded
    n_assign = sum(2 * g[0].shape[0] for g in groups)
    n_tiles = n_assign // EXPERT_TILE + N_EXPERTS
    n_slots = n_tiles * EXPERT_TILE
    experts = jnp.arange(N_EXPERTS, dtype=I32)
    tile_start = jnp.arange(n_tiles, dtype=I32) * EXPERT_TILE
    tile_expert = jnp.minimum(jnp.sum((ends[None, :] <= tile_start[:, None]).astype(I32), axis=1), N_EXPERTS - 1)
    n_used = (ends[-1] // EXPERT_TILE).astype(I32).reshape(1)

    xs = jnp.zeros((n_slots * ROW_CHUNKS, LANES), F32)
    dests = []
    base = starts
    for (h, hn, ri, rf, _), cnt in zip(groups, group_counts):
        n = h.shape[0]
        dest = jnp.sum(jnp.where(ri[:, 0:2, None] == experts, base, 0), axis=-1) + ri[:, 2:4]
        by_tile = lambda t, dest=dest, n=n: dest.reshape(n // t, t, 2).transpose(0, 2, 1).reshape(-1)
        dests.append(by_tile(min(GATHER_TILE, n)))
        td = min(DISPATCH_TILE, n)
        xs = _dispatch(hn, by_tile(td), xs, td)
        base = base + cnt
    out = _experts(xs, tile_expert, n_used, w_eg, w_eu, w_ed)
    return [_final(h, out, dest_tiles, rf, final_w) for (h, _, _, rf, _), dest_tiles in zip(groups, dests)]


def _layer(x, pos0, kv_bufs, conv_state, ssm_state, p):
    batch, seq, _ = x.shape
    n = batch * seq
    x2d = x.reshape(n, D_MODEL)
    prompt = kv_bufs is None
    q, k, v, z, xbc, g_a, g_s, dt_raw = _inproj(x2d, batch, seq, pos0, p["norm1_w"], p["w_in"], prompt)

    outs, lses, new_kv = [], [], []
    for gi, (window, dil) in enumerate(ATTN_GROUPS):
        if prompt:
            o, l = _attn_prompt(q[gi], k[gi], v[gi], dil)
            keep = min(window, seq) // dil
            tail = lambda t: t[:, :, seq // dil - keep:, :].transpose(0, 2, 1, 3).reshape(batch, keep * dil, GROUP_W)
            nbuf = jnp.stack([tail(k[gi]), tail(v[gi])], axis=2).reshape(
                batch, keep * dil, 2, HEADS_PER_GROUP, HEAD_DIM)
        else:
            o, l, nbuf = _attn_step(q, k, v, kv_bufs[gi], batch, seq, gi, dil)
        outs.append(o)
        lses.append(l)
        new_kv.append(nbuf)

    if conv_state is None:
        conv_state = jnp.zeros((batch, 3, SSD_CONV_DIM), F32)
        ssm_state = jnp.zeros((batch, SSD_HEADS, HEAD_DIM, SSD_STATE), F32)
    y_ssd, new_conv, new_ssm = _ssd(xbc, z, dt_raw, conv_state, ssm_state, batch, seq, p["conv_w"], p["conv_b"],
                                    p["dt_bias"], p["A_log"], p["D_skip"], p["ssd_norm_w"])

    h, hn, ri, rf, counts = _merge(x2d, outs, lses, y_ssd, g_a, g_s, p["w_attn_br"], p["w_ssd_br"], p["b_gate"],
                                   p["w_out"], p["norm2_w"], p["w_router"], p["b_router"], prompt)
    return (h, hn, ri, rf, counts), new_kv, new_conv, new_ssm


def _pack_w_in(w_in):
    offs = (0, 768, 1536, 2304, 3328, 4864, 4880, 5904, 6928)
    q, k, v, z, xbc, dt, g_a, g_s = (w_in[:, offs[i]:offs[i + 1]] for i in range(8))
    dt = jnp.pad(dt, ((0, 0), (0, LANES - SSD_HEADS)))
    return jnp.concatenate([q, k, v, z, xbc, g_a, g_s, dt], axis=1).astype(BF16)


def kernel(x_prompt, x_sample, cache_kv_w128, cache_kv_w512, cache_kv_w2048, state_conv, state_ssm, norm1_w, w_in, w_attn_br, w_ssd_br, b_gate, w_out, conv_w, conv_b, dt_bias, A_log, D_skip, ssd_norm_w, norm2_w, w_router_coarse, b_router_coarse, w_router_fine, b_router_fine, w_expert_gate, w_expert_up, w_expert_down, final_norm_w):
    depth = norm1_w.shape[0]
    assert depth == 1, "the final norm is fused into the layer's last kernel"
    l = 0
    rpad = LANES - N_EXPERTS - N_GROUPS_E
    p = dict(
        norm1_w=norm1_w[l], w_in=_pack_w_in(w_in[l]),
        w_attn_br=w_attn_br[l].astype(BF16), w_ssd_br=w_ssd_br[l].astype(BF16), b_gate=b_gate[l],
        w_out=w_out[l].astype(BF16), conv_w=conv_w[l], conv_b=conv_b[l], dt_bias=dt_bias[l], A_log=A_log[l],
        D_skip=D_skip[l], ssd_norm_w=ssd_norm_w[l], norm2_w=norm2_w[l],
        w_router=jnp.pad(jnp.concatenate([w_router_fine[l], w_router_coarse[l]], axis=1),
                         ((0, 0), (0, rpad))).astype(BF16),
        b_router=jnp.pad(jnp.concatenate([b_router_fine[l], b_router_coarse[l]]), (0, rpad)).reshape(1, LANES),
        w_eg=w_expert_gate[l], w_eu=w_expert_up[l], w_ed=w_expert_down[l], final_norm_w=final_norm_w,
    )
    g_p, kv_p, c_p, st_p = _layer(x_prompt, 0, None, None, None, p)
    bufs = (cache_kv_w128[l], cache_kv_w512[l], cache_kv_w2048[l])
    g_s, kv_s, c_s, st_s = _layer(x_sample, PAST_LEN, bufs, state_conv[l], state_ssm[l], p)
    y_p, y_s = _moe_and_final([g_p, g_s], p["w_eg"], p["w_eu"], p["w_ed"], p["final_norm_w"])
    y_p = y_p.reshape(x_prompt.shape)
    y_s = y_s.reshape(x_sample.shape)
    lead = lambda t: t[None]
    return (y_p, y_s, lead(kv_p[0]), lead(kv_p[1]), lead(kv_p[2]), lead(c_p), lead(st_p),
            lead(kv_s[0]), lead(kv_s[1]), lead(kv_s[2]), lead(c_s), lead(st_s))
```

```python
import functools

import jax
import jax.numpy as jnp
from jax import lax
from jax.experimental import pallas as pl
from jax.experimental.pallas import tpu as pltpu

F32 = jnp.float32
BF16 = jnp.bfloat16
I32 = jnp.int32

D_MODEL = 1024
HEAD_DIM = 64
ATTN_GROUPS = ((128, 1), (512, 4), (2048, 16))
SPAN = 128
HEADS_PER_GROUP = 4
GROUP_W = HEADS_PER_GROUP * HEAD_DIM
ATTN_W = GROUP_W * len(ATTN_GROUPS)
ROPE_THETA = 10000.0
PAST_LEN = 8192
SSD_INNER = 1024
SSD_HEADS = 16
SSD_STATE = 128
SSD_CONV_DIM = 1536
SSD_CHUNK = 128
N_GROUPS_E = 4
EXPERTS_PER_GROUP = 8
N_EXPERTS = 32
D_FF = 512
EPS = 1e-6
LANES = 128
SUBLANES = 8
NEG = -1e30

_OFF_Q, _OFF_K, _OFF_V, _OFF_Z, _OFF_XBC, _OFF_GA, _OFF_GS, _OFF_DT = 0, 768, 1536, 2304, 3328, 4864, 5888, 6912
_W_IN_COLS = 7040

ROW_TILE = 256
EXPERT_TILE = 512
VMEM_LIMIT = 56 * 1024 * 1024


def _cparams(sem, **kw):
    return pltpu.CompilerParams(dimension_semantics=sem, vmem_limit_bytes=VMEM_LIMIT, **kw)


def _sigmoid(x):
    return 1.0 / (1.0 + jnp.exp(-x))


ROW_CHUNKS = D_MODEL // LANES


def _store_row_tiles(ref, val):
    rows = val.shape[0]
    for c in range(ROW_CHUNKS):
        ref[pl.ds(c, rows, stride=ROW_CHUNKS), :] = val[:, c * LANES:(c + 1) * LANES]


def _load_row_tiles(ref, first, rows):
    return jnp.concatenate(
        [ref[pl.ds(first * ROW_CHUNKS + c, rows, stride=ROW_CHUNKS), :] for c in range(ROW_CHUNKS)], axis=1)


def _nt_dot(a, b):
    return lax.dot_general(a, b, (((1,), (1,)), ((), ())), preferred_element_type=F32)


def _store_by_class(o_ref, scr_ref, val, dil):
    if dil == 1:
        o_ref[0, 0] = val
        return
    rows = val.shape[0]
    for half in range(GROUP_W // LANES):
        lanes = slice(half * LANES, (half + 1) * LANES)
        scr_ref[half] = val[:, lanes]
        for r in range(dil):
            o_ref[0, r, :, lanes] = scr_ref[half, pl.ds(r, rows // dil, stride=dil), :]


def _inproj_kernel(x_ref, nw_ref, w_ref, cos_ref, sin_ref, *refs, class_major):
    if class_major:
        qkv_refs, (z_ref, xbc_ref, ga_ref, gs_ref, dt_ref, scr_ref) = refs[:9], refs[9:]
    else:
        qkv_refs, (z_ref, xbc_ref, ga_ref, gs_ref, dt_ref) = refs[:3], refs[3:]
    x = x_ref[...]
    xn = x * lax.rsqrt(jnp.mean(x * x, axis=-1, keepdims=True) + EPS) * nw_ref[...]
    xb = xn.astype(BF16)
    cos = cos_ref[...]
    sin = sin_ref[...]
    lane = lax.broadcasted_iota(I32, cos.shape, 1)
    first_half = (lane % HEAD_DIM) < (HEAD_DIM // 2)

    def mm(lo, hi):
        return jnp.dot(xb, w_ref[:, lo:hi], preferred_element_type=F32)

    def rope(uc):
        ur = jnp.where(first_half, pltpu.roll(uc, LANES - HEAD_DIM // 2, 1), pltpu.roll(uc, HEAD_DIM // 2, 1))
        return uc * cos + ur * sin

    def emit(which, base, roped):
        u = mm(base, base + ATTN_W)
        chunks = [u[:, c * LANES:(c + 1) * LANES] for c in range(ATTN_W // LANES)]
        if roped:
            chunks = [rope(uc) for uc in chunks]
        if class_major:
            for gi, (_, dil) in enumerate(ATTN_GROUPS):
                val = jnp.concatenate(chunks[2 * gi:2 * gi + 2], axis=1)
                _store_by_class(qkv_refs[3 * which + gi], scr_ref, val, dil)
        else:
            for c, uc in enumerate(chunks):
                qkv_refs[which][:, c * LANES:(c + 1) * LANES] = uc

    emit(0, _OFF_Q, True)
    emit(1, _OFF_K, True)
    emit(2, _OFF_V, False)
    z_ref[...] = mm(_OFF_Z, _OFF_XBC)
    xbc_ref[...] = mm(_OFF_XBC, _OFF_GA)
    ga_ref[...] = mm(_OFF_GA, _OFF_GS)
    gs_ref[...] = mm(_OFF_GS, _OFF_DT)
    dt_ref[...] = mm(_OFF_DT, _W_IN_COLS)


def _rope_tables(pos):
    half = HEAD_DIM // 2
    inv_freq = ROPE_THETA ** (-jnp.arange(half, dtype=F32) * (2.0 / HEAD_DIM))
    ang = pos.astype(F32)[:, None] * inv_freq[None, :]
    ang = jnp.tile(ang, (1, LANES // half))
    lane = jnp.arange(LANES)
    sign = jnp.where((lane % HEAD_DIM) < half, -1.0, 1.0).astype(F32)
    return jnp.cos(ang), jnp.sin(ang) * sign[None, :]


def _inproj(x2d, batch, seq_len, pos0, norm_w, w_packed, class_major):
    n = x2d.shape[0]
    tm = min(ROW_TILE, n)
    cos, sin = _rope_tables(pos0 + jnp.arange(seq_len, dtype=I32))
    if seq_len < tm:
        cos = jnp.tile(cos, (tm // seq_len, 1))
        sin = jnp.tile(sin, (tm // seq_len, 1))
    tab_blocks = cos.shape[0] // tm
    row = lambda i: (i, 0)
    fixed = lambda i: (0, 0)
    tab = lambda i: (i % tab_blocks, 0)
    widths = (SSD_INNER, SSD_CONV_DIM, D_MODEL, D_MODEL, LANES)
    out_specs = [pl.BlockSpec((tm, w), row) for w in widths]
    out_shape = [jax.ShapeDtypeStruct((n, w), F32) for w in widths]
    scratch = []
    if class_major:
        assert seq_len % tm == 0
        per_seq = seq_len // tm
        cls = lambda i: (i // per_seq, 0, i % per_seq, 0)
        dils = [d for _, d in ATTN_GROUPS] * 3
        out_specs = [pl.BlockSpec((1, d, tm // d, GROUP_W), cls) for d in dils] + out_specs
        out_shape = [jax.ShapeDtypeStruct((batch, d, seq_len // d, GROUP_W), F32) for d in dils] + out_shape
        scratch = [pltpu.VMEM((GROUP_W // LANES, tm, LANES), F32)]
    else:
        out_specs = [pl.BlockSpec((tm, ATTN_W), row)] * 3 + out_specs
        out_shape = [jax.ShapeDtypeStruct((n, ATTN_W), F32)] * 3 + out_shape
    res = pl.pallas_call(
        functools.partial(_inproj_kernel, class_major=class_major),
        grid=(n // tm,),
        in_specs=[pl.BlockSpec((tm, D_MODEL), row),
                  pl.BlockSpec((1, D_MODEL), fixed),
                  pl.BlockSpec((D_MODEL, _W_IN_COLS), fixed, pipeline_mode=pl.Buffered(1)),
                  pl.BlockSpec((tm, LANES), tab),
                  pl.BlockSpec((tm, LANES), tab)],
        out_specs=out_specs,
        out_shape=out_shape,
        scratch_shapes=scratch,
        compiler_params=_cparams(("parallel",)),
        name="inproj",
    )(x2d, norm_w.reshape(1, D_MODEL), w_packed, cos, sin)
    if class_major:
        return (tuple(res[0:3]), tuple(res[3:6]), tuple(res[6:9])) + tuple(res[9:])
    return tuple(res)


ATTN_QBLOCKS = 4


def _attn_prompt_kernel(q_ref, kp_ref, kc_ref, vp_ref, vc_ref, o_ref, l_ref, k_scr, v_scr, s_scr, p_scr, r_scr,
                        *, qblocks):
    c = pl.program_id(2)
    i = lax.broadcasted_iota(I32, (SPAN, 2 * SPAN), 0)
    j = lax.broadcasted_iota(I32, (SPAN, 2 * SPAN), 1)
    band = (j >= i) & (j <= i + SPAN)
    first_band = band & ((j >= SPAN) | (c > 0))
    k_scr[0:SPAN, :] = kp_ref[0, 0].astype(BF16)
    k_scr[SPAN:, :] = kc_ref[0, 0].astype(BF16)
    v_scr[0:SPAN, :] = vp_ref[0, 0].astype(BF16)
    v_scr[SPAN:, :] = vc_ref[0, 0].astype(BF16)
    units = [(b, h) for b in range(qblocks) for h in range(HEADS_PER_GROUP)]

    for u, (b, h) in enumerate(units):
        hs = slice(h * HEAD_DIM, (h + 1) * HEAD_DIM)
        qh = (q_ref[0, 0, b * SPAN:(b + 1) * SPAN, hs] * (HEAD_DIM ** -0.5)).astype(BF16)
        s_scr[u] = _nt_dot(qh, k_scr[b * SPAN:(b + 2) * SPAN, hs])

    for u, (b, h) in enumerate(units):
        hs = slice(h * HEAD_DIM, (h + 1) * HEAD_DIM)
        s = jnp.where(first_band if b == 0 else band, s_scr[u], NEG)
        m = jnp.max(jnp.maximum(s[:, :SPAN], s[:, SPAN:]), axis=-1, keepdims=True)
        p = jnp.exp(s - m)
        den = jnp.sum(p[:, :SPAN] + p[:, SPAN:], axis=-1, keepdims=True)
        p_scr[u] = p.astype(BF16)
        r_scr[u] = jnp.broadcast_to(1.0 / den, (SPAN, HEAD_DIM))
        l_ref[0, 0, b * SPAN:(b + 1) * SPAN, hs] = jnp.broadcast_to(m + jnp.log(den), (SPAN, HEAD_DIM))

    for u, (b, h) in enumerate(units):
        hs = slice(h * HEAD_DIM, (h + 1) * HEAD_DIM)
        acc = jnp.dot(p_scr[u], v_scr[b * SPAN:(b + 2) * SPAN, hs], preferred_element_type=F32)
        o_ref[0, 0, b * SPAN:(b + 1) * SPAN, hs] = acc * r_scr[u]


def _attn_prompt(q, k, v, dil):
    batch, _, n_cls, _ = q.shape
    qblocks = min(ATTN_QBLOCKS, n_cls // SPAN)
    tq = qblocks * SPAN
    assert n_cls % tq == 0
    cur = lambda b, r, c: (b, r, c, 0)
    prev = lambda b, r, c: (b, r, jnp.maximum(c * qblocks - 1, 0), 0)
    big = pl.BlockSpec((1, 1, tq, GROUP_W), cur)
    small = pl.BlockSpec((1, 1, SPAN, GROUP_W), prev)
    return pl.pallas_call(
        functools.partial(_attn_prompt_kernel, qblocks=qblocks),
        grid=(batch, dil, n_cls // tq),
        in_specs=[big, small, big, small, big],
        out_specs=[big, big],
        out_shape=[jax.ShapeDtypeStruct(q.shape, F32)] * 2,
        scratch_shapes=[pltpu.VMEM((SPAN + tq, GROUP_W), BF16),
                        pltpu.VMEM((SPAN + tq, GROUP_W), BF16),
                        pltpu.VMEM((qblocks * HEADS_PER_GROUP, SPAN, 2 * SPAN), F32),
                        pltpu.VMEM((qblocks * HEADS_PER_GROUP, SPAN, 2 * SPAN), BF16),
                        pltpu.VMEM((qblocks * HEADS_PER_GROUP, SPAN, HEAD_DIM), F32)],
        compiler_params=_cparams(("parallel", "parallel", "arbitrary")),
        name=f"attn_prompt_d{dil}",
    )(q, k, k, v, v)


def _attn_step_kernel(q_ref, kn_ref, vn_ref, buf_ref, o_ref, l_ref, nbuf_ref, *, buf_len, dil, n_new):
    q = q_ref[0] * (HEAD_DIM ** -0.5)
    t = lax.broadcasted_iota(I32, (n_new, buf_len), 0)
    i = lax.broadcasted_iota(I32, (n_new, buf_len), 1)
    delta = buf_len + t - i
    valid_b = delta <= dil * SPAN
    first_new = LANES - n_new
    tn = lax.broadcasted_iota(I32, (n_new, LANES), 0)
    un = lax.broadcasted_iota(I32, (n_new, LANES), 1) - first_new
    dn = tn - un
    valid_n = (dn >= 0) & (un >= 0)
    if dil > 1:
        valid_b = valid_b & ((delta & (dil - 1)) == 0)
        valid_n = valid_n & ((dn & (dil - 1)) == 0)

    def to_columns(x):
        xp = jnp.concatenate([x, jnp.zeros((LANES - n_new, GROUP_W), F32)], axis=0)
        xt = jnp.concatenate([xp[:, c * LANES:(c + 1) * LANES].T for c in range(GROUP_W // LANES)], axis=0)
        return pltpu.roll(xt, first_new, 1)

    new_cols = (to_columns(kn_ref[0]), to_columns(vn_ref[0]))
    is_new = lax.broadcasted_iota(I32, (HEAD_DIM, LANES), 1) >= first_new
    tail = slice(buf_len - LANES, buf_len)
    for h in range(HEADS_PER_GROUP):
        hs = slice(h * HEAD_DIM, (h + 1) * HEAD_DIM)
        qh = q[:, hs].astype(BF16)
        kt = buf_ref[0, 0, h]
        vt = buf_ref[0, 1, h]
        knt = new_cols[0][hs]
        vnt = new_cols[1][hs]
        sb = jnp.where(valid_b, jnp.dot(qh, kt.astype(BF16), preferred_element_type=F32), NEG)
        sn = jnp.where(valid_n, jnp.dot(qh, knt.astype(BF16), preferred_element_type=F32), NEG)
        m = jnp.maximum(jnp.max(sb, axis=-1, keepdims=True), jnp.max(sn, axis=-1, keepdims=True))
        pb = jnp.exp(sb - m)
        pn = jnp.exp(sn - m)
        den = jnp.sum(pb, axis=-1, keepdims=True) + jnp.sum(pn, axis=-1, keepdims=True)
        acc = _nt_dot(pb.astype(BF16), vt.astype(BF16)) + _nt_dot(pn.astype(BF16), vnt.astype(BF16))
        o_ref[0, :, hs] = acc / den
        l_ref[0, :, hs] = jnp.broadcast_to(m + jnp.log(den), (n_new, HEAD_DIM))
        for kv, (old, new) in enumerate(((kt, knt), (vt, vnt))):
            shifted = pltpu.roll(old, buf_len - n_new, 1)
            nbuf_ref[0, kv, h] = shifted
            nbuf_ref[0, kv, h, :, tail] = jnp.where(is_new, new, shifted[:, tail])


def _attn_step(q, k, v, buf, batch, n_new, gi, dil):
    buf_len = buf.shape[1]
    assert dil & (dil - 1) == 0 and buf_len >= dil * SPAN and n_new % SUBLANES == 0 and buf_len % LANES == 0
    view = lambda t: t.reshape(batch, n_new, ATTN_W)
    tok = pl.BlockSpec((1, n_new, GROUP_W), lambda b: (b, 0, gi))
    full = pl.BlockSpec((1, 2, HEADS_PER_GROUP, HEAD_DIM, buf_len), lambda b: (b, 0, 0, 0, 0))
    osp = pl.BlockSpec((1, n_new, GROUP_W), lambda b: (b, 0, 0))
    o, l, nbuf = pl.pallas_call(
        functools.partial(_attn_step_kernel, buf_len=buf_len, dil=dil, n_new=n_new),
        grid=(batch,),
        in_specs=[tok, tok, tok, full],
        out_specs=[osp, osp, full],
        out_shape=[jax.ShapeDtypeStruct((batch, n_new, GROUP_W), F32)] * 2
        + [jax.ShapeDtypeStruct((batch, 2, HEADS_PER_GROUP, HEAD_DIM, buf_len), F32)],
        compiler_params=_cparams(("parallel",)),
        name=f"attn_step_d{dil}",
    )(view(q), view(k), view(v), jnp.transpose(buf, (0, 2, 3, 4, 1)))
    return (o.reshape(batch * n_new, GROUP_W), l.reshape(batch * n_new, GROUP_W),
            jnp.transpose(nbuf, (0, 4, 1, 2, 3)))


def _split3(a):
    a1 = a.astype(BF16)
    r1 = a - a1.astype(F32)
    a2 = r1.astype(BF16)
    a3 = (r1 - a2.astype(F32)).astype(BF16)
    return a1, a2, a3


def _ssd_kernel(xbc_ref, z_ref, dt_ref, cst_ref, h0_ref, cw_ref, cb_ref, dtb_ref, alog_ref, dfull_ref, nw_ref,
                tri_ref, expand_ref, y_ref, cout_ref, hout_ref, xpad_ref, h_ref, *, n_valid, n_chunks):
    c = pl.program_id(1)
    lc = SSD_CHUNK
    pad = SUBLANES
    n_slabs = SSD_CONV_DIM // LANES

    @pl.when(c == 0)
    def _():
        for j in range(n_slabs):
            xpad_ref[j, 0:pad, :] = cst_ref[0, :, j * LANES:(j + 1) * LANES]
        h_ref[...] = h0_ref[0]

    @pl.when(c > 0)
    def _():
        for j in range(n_slabs):
            xpad_ref[j, 0:pad, :] = xpad_ref[j, lc:lc + pad, :]

    if n_valid == lc:
        z = z_ref[...]
        dtr = dt_ref[...]
    else:
        fill = lambda w: jnp.zeros((lc - n_valid, w), F32)
        z = jnp.concatenate([z_ref[...], fill(SSD_INNER)], axis=0)
        dtr = jnp.concatenate([dt_ref[...], fill(LANES)], axis=0)

    slabs = []
    for j in range(n_slabs):
        cols = slice(j * LANES, (j + 1) * LANES)
        xpad_ref[j, pad:pad + n_valid, :] = xbc_ref[:, cols]
        if n_valid < lc:
            xpad_ref[j, pad + n_valid:pad + lc, :] = jnp.zeros((lc - n_valid, LANES), F32)
        xc = cb_ref[:, cols]
        for tap in range(4):
            xc = xc + xpad_ref[j, pl.ds(pad - 3 + tap, lc), :] * cw_ref[tap:tap + 1, cols]
        slabs.append(xc * _sigmoid(xc))
        cout_ref[0, :, cols] = xpad_ref[j, pl.ds(pad + n_valid - 3, 3), :]
    xs = jnp.concatenate(slabs[:SSD_INNER // LANES], axis=1)
    bm = slabs[SSD_INNER // LANES:SSD_INNER // LANES + 2]
    cm = slabs[SSD_INNER // LANES + 2:]

    row = lax.broadcasted_iota(I32, (lc, lc), 0)
    col = lax.broadcasted_iota(I32, (lc, lc), 1)
    causal = row >= col
    dtv = dtr + dtb_ref[...]
    dt = jnp.maximum(dtv, 0.0) + jnp.log1p(jnp.exp(-jnp.abs(dtv)))
    if n_valid < lc:
        dt = jnp.where(row < n_valid, dt, 0.0)
    a = dt * (-jnp.exp(alog_ref[...]))
    tri = tri_ref[...]
    a_cs = sum(jnp.dot(tri, p, preferred_element_type=F32) for p in _split3(a))
    a_cs_t = a_cs.T
    expand = expand_ref[...]
    a_full = sum(jnp.dot(p, expand, preferred_element_type=F32) for p in _split3(a_cs))
    dt_full = sum(jnp.dot(p, expand, preferred_element_type=F32) for p in _split3(dt))
    xdt = xs * dt_full
    xd = xdt * jnp.exp(a_full[lc - 1:lc, :] - a_full)
    grow = jnp.exp(a_full)
    xdt_b = xdt.astype(BF16)
    xd_t = jnp.concatenate([xd[:, k * LANES:(k + 1) * LANES].T for k in range(SSD_INNER // LANES)],
                           axis=0).astype(BF16)
    lane = lax.broadcasted_iota(I32, (lc, LANES), 1)
    low_half = lane < HEAD_DIM
    zero_b = jnp.zeros((lc, LANES), BF16)
    half = SSD_INNER // 2

    y_parts = []
    for g in range(2):
        bg = bm[g].astype(BF16)
        cg = cm[g].astype(BF16)
        cb = jnp.where(causal, _nt_dot(cg, bg), 0.0)
        h_grp = h_ref[g * half:(g + 1) * half, :]
        y_off = _nt_dot(cg, h_grp.astype(BF16)) * grow[:, g * half:(g + 1) * half]
        for pair in range(4):
            e0 = g * 8 + pair * 2
            rows = slice(e0 * HEAD_DIM, (e0 + 2) * HEAD_DIM)
            ms, keep = [], []
            for k in range(2):
                e = e0 + k
                seg = a_cs[:, e:e + 1] - a_cs_t[e:e + 1, :]
                ms.append((cb * jnp.exp(jnp.minimum(seg, 0.0))).astype(BF16))
                keep.append(jnp.broadcast_to(jnp.exp(a_cs[lc - 1:lc, e:e + 1]), (HEAD_DIM, SSD_STATE)))
            pair_b = xdt_b[:, rows]
            rhs = jnp.concatenate([jnp.where(low_half, pair_b, zero_b), jnp.where(low_half, zero_b, pair_b)], axis=0)
            y_parts.append(jnp.dot(jnp.concatenate(ms, axis=1), rhs, preferred_element_type=F32)
                           + y_off[:, pair * LANES:(pair + 1) * LANES])
            st = jnp.dot(xd_t[rows, :], bg, preferred_element_type=F32)
            h_ref[rows, :] = h_grp[pair * LANES:(pair + 1) * LANES, :] * jnp.concatenate(keep, axis=0) + st

    y = jnp.concatenate(y_parts, axis=1) + xs * dfull_ref[...]
    gate = y * (z * _sigmoid(z))
    for g in range(2):
        gg = gate[:, g * half:(g + 1) * half]
        gg = gg * lax.rsqrt(jnp.mean(gg * gg, axis=-1, keepdims=True) + EPS)
        y_ref[:, g * half:(g + 1) * half] = (gg * nw_ref[:, g * half:(g + 1) * half])[0:n_valid]

    @pl.when(c == n_chunks - 1)
    def _():
        hout_ref[0] = h_ref[...]


def _ssd(xbc, z, dt_raw, conv_state, ssm_state, batch, seq, conv_w, conv_b, dt_bias, a_log, d_skip, norm_w):
    n_valid = min(seq, SSD_CHUNK)
    n_chunks = seq // n_valid
    assert n_valid % SUBLANES == 0 and seq % n_valid == 0
    padl = lambda t: jnp.pad(t.reshape(1, SSD_HEADS), ((0, 0), (0, LANES - SSD_HEADS)))
    cst = jnp.pad(conv_state, ((0, 0), (SUBLANES - 3, 0), (0, 0)))
    tri = (jnp.arange(SSD_CHUNK)[:, None] >= jnp.arange(SSD_CHUNK)[None, :]).astype(BF16)
    expand = (jnp.arange(LANES)[:, None] == jnp.arange(SSD_INNER)[None, :] // HEAD_DIM).astype(BF16)
    tokrow = lambda b, c: (b * n_chunks + c, 0)
    fixed = lambda b, c: (0, 0)
    per_b3 = lambda b, c: (b, 0, 0)
    state = pl.BlockSpec((1, SSD_INNER, SSD_STATE), per_b3)
    kern = functools.partial(_ssd_kernel, n_valid=n_valid, n_chunks=n_chunks)
    y, cout, hout = pl.pallas_call(
        kern,
        grid=(batch, n_chunks),
        in_specs=[pl.BlockSpec((n_valid, SSD_CONV_DIM), tokrow),
                  pl.BlockSpec((n_valid, SSD_INNER), tokrow),
                  pl.BlockSpec((n_valid, LANES), tokrow),
                  pl.BlockSpec((1, SUBLANES, SSD_CONV_DIM), per_b3),
                  state,
                  pl.BlockSpec((4, SSD_CONV_DIM), fixed),
                  pl.BlockSpec((1, SSD_CONV_DIM), fixed),
                  pl.BlockSpec((1, LANES), fixed),
                  pl.BlockSpec((1, LANES), fixed),
                  pl.BlockSpec((1, SSD_INNER), fixed),
                  pl.BlockSpec((1, SSD_INNER), fixed),
                  pl.BlockSpec((SSD_CHUNK, SSD_CHUNK), fixed),
                  pl.BlockSpec((LANES, SSD_INNER), fixed)],
        out_specs=[pl.BlockSpec((n_valid, SSD_INNER), tokrow),
                   pl.BlockSpec((1, 3, SSD_CONV_DIM), per_b3),
                   state],
        out_shape=[jax.ShapeDtypeStruct((batch * seq, SSD_INNER), F32),
                   jax.ShapeDtypeStruct((batch, 3, SSD_CONV_DIM), F32),
                   jax.ShapeDtypeStruct((batch, SSD_INNER, SSD_STATE), F32)],
        scratch_shapes=[pltpu.VMEM((SSD_CONV_DIM // LANES, SSD_CHUNK + SUBLANES, LANES), F32),
                        pltpu.VMEM((SSD_INNER, SSD_STATE), F32)],
        compiler_params=_cparams(("parallel", "arbitrary")),
        name="ssd",
    )(xbc, z, dt_raw, cst, ssm_state.reshape(batch, SSD_INNER, SSD_STATE), conv_w, conv_b.reshape(1, SSD_CONV_DIM),
      padl(dt_bias), padl(a_log), jnp.repeat(d_skip, HEAD_DIM).reshape(1, SSD_INNER), norm_w.reshape(1, SSD_INNER),
      tri, expand)
    return y, cout, hout.reshape(batch, SSD_HEADS, HEAD_DIM, SSD_STATE)


def _merge_kernel(x_ref, o0_ref, o1_ref, o2_ref, l0_ref, l1_ref, l2_ref, ys_ref, ga_ref, gs_ref,
                  wab_ref, wsb_ref, bg_ref, wo_ref, n2_ref, wr_ref, br_ref,
                  h_ref, hn_ref, ri_ref, rf_ref, cnt_ref, carry_ref, *scr, class_major):
    step = pl.program_id(0)

    @pl.when(step == 0)
    def _():
        carry_ref[...] = jnp.zeros_like(carry_ref)

    def load(ref, gi):
        if not class_major:
            return ref[...]
        dil = ATTN_GROUPS[gi][1]
        if dil == 1:
            return ref[0, 0]
        rows = ref.shape[1] * ref.shape[2]
        halves = []
        for half in range(GROUP_W // LANES):
            for r in range(dil):
                scr[0][half, pl.ds(r, rows // dil, stride=dil), :] = ref[0, r, :, half * LANES:(half + 1) * LANES]
            halves.append(scr[0][half])
        return jnp.concatenate(halves, axis=1)

    l0, l1, l2 = load(l0_ref, 0), load(l1_ref, 1), load(l2_ref, 2)
    m = jnp.maximum(jnp.maximum(l0, l1), l2)
    w0, w1, w2 = jnp.exp(l0 - m), jnp.exp(l1 - m), jnp.exp(l2 - m)
    y_attn = (w0 * load(o0_ref, 0) + w1 * load(o1_ref, 1) + w2 * load(o2_ref, 2)) / (w0 + w1 + w2)
    pa = jnp.dot(y_attn.astype(BF16), wab_ref[...], preferred_element_type=F32)
    ps = jnp.dot(ys_ref[...].astype(BF16), wsb_ref[...], preferred_element_type=F32)
    merged = _sigmoid(ga_ref[...] + bg_ref[0:1, :]) * pa + _sigmoid(gs_ref[...] + bg_ref[1:2, :]) * ps
    h = x_ref[...] + jnp.dot(merged.astype(BF16), wo_ref[...], preferred_element_type=F32)
    h_ref[...] = h
    hn = h * lax.rsqrt(jnp.mean(h * h, axis=-1, keepdims=True) + EPS) * n2_ref[...]
    hnb = hn.astype(BF16)
    _store_row_tiles(hn_ref, hn)

    logits = jnp.dot(hnb, wr_ref[...], preferred_element_type=F32) + br_ref[...]
    tm = logits.shape[0]
    lane = lax.broadcasted_iota(I32, (tm, LANES), 1)
    big = jnp.int32(LANES)

    def top(vals):
        v = jnp.max(vals, axis=-1, keepdims=True)
        idx = jnp.min(jnp.where(vals == v, lane, big), axis=-1, keepdims=True)
        return v, idx

    is_coarse = (lane >= N_EXPERTS) & (lane < N_EXPERTS + N_GROUPS_E)
    lc = jnp.where(is_coarse, logits, NEG)
    mc, ic = top(lc)
    p_grp = 1.0 / jnp.sum(jnp.exp(lc - mc), axis=-1, keepdims=True)
    lo = (ic - N_EXPERTS) * EXPERTS_PER_GROUP
    lf = jnp.where((lane >= lo) & (lane < lo + EXPERTS_PER_GROUP), logits, NEG)
    v1, i1 = top(lf)
    v2, i2 = top(jnp.where(lane == i1, NEG, lf))
    e2 = jnp.exp(v2 - v1)
    g1 = p_grp / (1.0 + e2)
    g2 = p_grp * e2 / (1.0 + e2)

    oh1 = lane == i1
    oh2 = lane == i2
    cnt = oh1.astype(F32) + oh2.astype(F32)
    r = lax.broadcasted_iota(I32, (tm, tm), 0)
    s = lax.broadcasted_iota(I32, (tm, tm), 1)
    before = jnp.dot((r > s).astype(BF16), cnt.astype(BF16), preferred_element_type=F32) + carry_ref[0:1, :]
    r1 = jnp.sum(jnp.where(oh1, before, 0.0), axis=-1, keepdims=True)
    r2 = jnp.sum(jnp.where(oh2, before, 0.0), axis=-1, keepdims=True)
    new_carry = carry_ref[0:1, :] + jnp.sum(cnt, axis=0, keepdims=True)
    carry_ref[...] = jnp.broadcast_to(new_carry, carry_ref.shape)
    cnt_ref[...] = jnp.broadcast_to(new_carry, cnt_ref.shape).astype(I32)

    ri = jnp.where(lane == 0, i1, jnp.where(lane == 1, i2, 0))
    ri = jnp.where(lane == 2, r1.astype(I32), jnp.where(lane == 3, r2.astype(I32), ri))
    ri_ref[...] = ri
    rf_ref[...] = jnp.where(lane == 0, g1, jnp.where(lane == 1, g2, 0.0))


def _merge(x2d, outs, lses, y_ssd, g_a, g_s, wab, wsb, b_gate, wo, norm2_w, w_router, b_router, class_major):
    n = x2d.shape[0]
    tm = min(ROW_TILE, n)
    row = lambda i: (i, 0)
    fixed = lambda i: (0, 0)
    wide = pl.BlockSpec((tm, D_MODEL), row)
    info = pl.BlockSpec((tm, LANES), row)
    if class_major:
        per_seq = outs[0].shape[2] // tm
        cls = lambda i: (i // per_seq, 0, i % per_seq, 0)
        grps = [pl.BlockSpec((1, d, tm // d, GROUP_W), cls) for _, d in ATTN_GROUPS]
        scratch = [pltpu.VMEM((GROUP_W // LANES, tm, LANES), F32)]
    else:
        grps = [pl.BlockSpec((tm, GROUP_W), row)] * 3
        scratch = []
    return pl.pallas_call(
        functools.partial(_merge_kernel, class_major=class_major),
        grid=(n // tm,),
        in_specs=[wide, *grps, *grps, wide, wide, wide,
                  pl.BlockSpec((GROUP_W, D_MODEL), fixed),
                  pl.BlockSpec((SSD_INNER, D_MODEL), fixed),
                  pl.BlockSpec((2, D_MODEL), fixed),
                  pl.BlockSpec((D_MODEL, D_MODEL), fixed),
                  pl.BlockSpec((1, D_MODEL), fixed),
                  pl.BlockSpec((D_MODEL, LANES), fixed),
                  pl.BlockSpec((1, LANES), fixed)],
        out_specs=[wide, pl.BlockSpec((tm * ROW_CHUNKS, LANES), row), info, info,
                   pl.BlockSpec((SUBLANES, LANES), fixed)],
        out_shape=[jax.ShapeDtypeStruct((n, D_MODEL), F32),
                   jax.ShapeDtypeStruct((n * ROW_CHUNKS, LANES), F32),
                   jax.ShapeDtypeStruct((n, LANES), I32),
                   jax.ShapeDtypeStruct((n, LANES), F32),
                   jax.ShapeDtypeStruct((SUBLANES, LANES), I32)],
        scratch_shapes=[pltpu.VMEM((SUBLANES, LANES), F32)] + scratch,
        compiler_params=_cparams(("arbitrary",)),
        name="merge_out",
    )(x2d, *outs, *lses, y_ssd, g_a, g_s, wab, wsb, b_gate, wo, norm2_w.reshape(1, D_MODEL), w_router, b_router)


GATHER_TILE = 256
DISPATCH_TILE = 512
INDEX_BATCH = 16


def _dispatch_kernel(dest_ref, pad_ref, hn_ref, *rest, tm, first_group):
    if first_group:
        xs_hbm, zeros_ref, sem, pad_sem = rest
    else:
        _, xs_hbm, sem = rest
    i = pl.program_id(0)

    if first_group:
        @pl.when(i == 0)
        def _():
            zeros_ref[...] = jnp.zeros_like(zeros_ref)

            def fill(first_slot, size):
                first = pl.multiple_of(first_slot * ROW_CHUNKS, ROW_CHUNKS)
                return pltpu.make_async_copy(zeros_ref.at[pl.ds(0, size * ROW_CHUNKS), :],
                                             xs_hbm.at[pl.ds(first, size * ROW_CHUNKS), :], pad_sem)

            fills = []
            for e in range(N_EXPERTS):
                start, count = pad_ref[e], pad_ref[N_EXPERTS + e]
                size = EXPERT_TILE
                while size >= 1:
                    fills.append(((count & size) != 0, start + (count & ~(2 * size - 1)), size))
                    size //= 2
            tail_first, tail_tiles = pad_ref[2 * N_EXPERTS], pad_ref[2 * N_EXPERTS + 1]
            for t in range(N_EXPERTS):
                fills.append((t < tail_tiles, (tail_first + t) * EXPERT_TILE, EXPERT_TILE))
            for pred, first_slot, size in fills:
                pl.when(pred)(lambda first_slot=first_slot, size=size: fill(first_slot, size).start())
            for pred, first_slot, size in fills:
                pl.when(pred)(lambda first_slot=first_slot, size=size: fill(first_slot, size).wait())

    for j0 in range(0, 2 * tm, INDEX_BATCH):
        slots = [dest_ref[i * 2 * tm + j0 + u] for u in range(INDEX_BATCH)]
        for u, dst in enumerate(slots):
            tok = (j0 + u) % tm
            first = pl.multiple_of(dst * ROW_CHUNKS, ROW_CHUNKS)
            pltpu.make_async_copy(hn_ref.at[pl.ds(tok * ROW_CHUNKS, ROW_CHUNKS), :],
                                  xs_hbm.at[pl.ds(first, ROW_CHUNKS), :], sem).start(priority=u % 2)
    for _ in range(2):
        pltpu.make_async_copy(hn_ref, xs_hbm.at[pl.ds(0, tm * ROW_CHUNKS), :], sem).wait()


def _dispatch(hn, dest_tiles, pad_info, xs, n_slots, tm):
    n = hn.shape[0] // ROW_CHUNKS
    first_group = xs is None
    anyspec = pl.BlockSpec(memory_space=pl.ANY)
    scratch = [pltpu.SemaphoreType.DMA(())]
    if first_group:
        scratch = [pltpu.VMEM((EXPERT_TILE * ROW_CHUNKS, LANES), F32)] + scratch + [pltpu.SemaphoreType.DMA(())]
    return pl.pallas_call(
        functools.partial(_dispatch_kernel, tm=tm, first_group=first_group),
        grid_spec=pltpu.PrefetchScalarGridSpec(
            num_scalar_prefetch=2,
            grid=(n // tm,),
            in_specs=[pl.BlockSpec((tm * ROW_CHUNKS, LANES), lambda i, d, p: (i, 0))] + ([] if first_group else [anyspec]),
            out_specs=anyspec,
            scratch_shapes=scratch),
        out_shape=jax.ShapeDtypeStruct((n_slots * ROW_CHUNKS, LANES), F32),
        input_output_aliases={} if first_group else {3: 0},
        compiler_params=_cparams(("arbitrary",), disable_bounds_checks=True, has_side_effects=True),
        name="dispatch",
    )(dest_tiles, pad_info, hn, *([] if first_group else [xs]))


def _expert_kernel(te_ref, nu_ref, x_ref, wg_ref, wu_ref, wd_ref, o_ref, wgb, wub, wdb):
    i = pl.program_id(0)
    changed = (i == 0) | (te_ref[i] != te_ref[jnp.maximum(i - 1, 0)])

    @pl.when(changed)
    def _():
        wgb[...] = wg_ref[0].astype(BF16)
        wub[...] = wu_ref[0].astype(BF16)
        wdb[...] = wd_ref[0].astype(BF16)

    @pl.when(i < nu_ref[0])
    def _():
        x = _load_row_tiles(x_ref, 0, EXPERT_TILE).astype(BF16)
        hg = jnp.dot(x, wgb[...], preferred_element_type=F32)
        hu = jnp.dot(x, wub[...], preferred_element_type=F32)
        hb = (hg * _sigmoid(hg)) * hu
        _store_row_tiles(o_ref, jnp.dot(hb.astype(BF16), wdb[...], preferred_element_type=F32))

    @pl.when(i >= nu_ref[0])
    def _():
        o_ref[...] = jnp.zeros_like(o_ref)


def _experts(xs, tile_expert, n_used, w_eg, w_eu, w_ed):
    n_slots = xs.shape[0] // ROW_CHUNKS
    n_tiles = n_slots // EXPERT_TILE
    row = lambda i, te, nu: (i, 0)
    used_row = lambda i, te, nu: (jnp.minimum(i, nu[0] - 1), 0)
    wsel = lambda i, te, nu: (te[i], 0, 0)
    return pl.pallas_call(
        _expert_kernel,
        grid_spec=pltpu.PrefetchScalarGridSpec(
            num_scalar_prefetch=2,
            grid=(n_tiles,),
            in_specs=[pl.BlockSpec((EXPERT_TILE * ROW_CHUNKS, LANES), used_row),
                      pl.BlockSpec((1, D_MODEL, D_FF), wsel),
                      pl.BlockSpec((1, D_MODEL, D_FF), wsel),
                      pl.BlockSpec((1, D_FF, D_MODEL), wsel)],
            out_specs=pl.BlockSpec((EXPERT_TILE * ROW_CHUNKS, LANES), row),
            scratch_shapes=[pltpu.VMEM((D_MODEL, D_FF), BF16),
                            pltpu.VMEM((D_MODEL, D_FF), BF16),
                            pltpu.VMEM((D_FF, D_MODEL), BF16)]),
        out_shape=jax.ShapeDtypeStruct((n_slots * ROW_CHUNKS, LANES), F32),
        compiler_params=_cparams(("arbitrary",)),
        name="experts",
    )(tile_expert, n_used, xs, w_eg, w_eu, w_ed)


def _final_kernel(dest_ref, h_ref, rf_ref, fw_ref, out_hbm, o_ref, ybuf, sem, *, tm):
    i = pl.program_id(0)
    slot = i % 2

    def gather(tile, into):
        for j0 in range(0, 2 * tm, INDEX_BATCH):
            slots = [dest_ref[tile * 2 * tm + j0 + u] for u in range(INDEX_BATCH)]
            for u, src in enumerate(slots):
                first = pl.multiple_of(src * ROW_CHUNKS, ROW_CHUNKS)
                pltpu.make_async_copy(out_hbm.at[pl.ds(first, ROW_CHUNKS), :],
                                      ybuf.at[into, pl.ds((j0 + u) * ROW_CHUNKS, ROW_CHUNKS), :],
                                      sem.at[into]).start(priority=u % 2)

    @pl.when(i == 0)
    def _():
        gather(0, 0)

    @pl.when(i + 1 < pl.num_programs(0))
    def _():
        gather(i + 1, 1 - slot)

    pltpu.make_async_copy(out_hbm.at[pl.ds(0, 2 * tm * ROW_CHUNKS), :], ybuf.at[slot], sem.at[slot]).wait()
    g = rf_ref[...]
    moe = (_load_row_tiles(ybuf.at[slot], 0, tm) * g[:, 0:1] + _load_row_tiles(ybuf.at[slot], tm, tm) * g[:, 1:2])
    h = h_ref[...] + moe
    o_ref[...] = h * lax.rsqrt(jnp.mean(h * h, axis=-1, keepdims=True) + EPS) * fw_ref[...]


def _final(h, out, dest_tiles, rf, final_w):
    n = h.shape[0]
    tm = min(GATHER_TILE, n)
    row = lambda i, d: (i, 0)
    wide = pl.BlockSpec((tm, D_MODEL), row)
    return pl.pallas_call(
        functools.partial(_final_kernel, tm=tm),
        grid_spec=pltpu.PrefetchScalarGridSpec(
            num_scalar_prefetch=1,
            grid=(n // tm,),
            in_specs=[wide, pl.BlockSpec((tm, LANES), row), pl.BlockSpec((1, D_MODEL), lambda i, d: (0, 0)),
                      pl.BlockSpec(memory_space=pl.ANY)],
            out_specs=wide,
            scratch_shapes=[pltpu.VMEM((2, 2 * tm * ROW_CHUNKS, LANES), F32), pltpu.SemaphoreType.DMA((2,))]),
        out_shape=jax.ShapeDtypeStruct((n, D_MODEL), F32),
        compiler_params=_cparams(("arbitrary",), disable_bounds_checks=True),
        name="final",
    )(dest_tiles, h, rf, final_w.reshape(1, D_MODEL), out)


def _moe_and_final(groups, w_eg, w_eu, w_ed, final_w):
    group_counts = [g[4][0, :N_EXPERTS] for g in groups]
    counts = sum(group_counts)
    padded = ((counts + EXPERT_TILE - 1) // EXPERT_TILE) * EXPERT_TILE
    ends = jnp.cumsum(padded)
    starts = ends - padded
    n_assign = sum(2 * g[0].shape[0] for g in groups)
    n_tiles = n_assign // EXPERT_TILE + N_EXPERTS
    n_slots = n_tiles * EXPERT_TILE
    experts = jnp.arange(N_EXPERTS, dtype=I32)
    tile_start = jnp.arange(n_tiles, dtype=I32) * EXPERT_TILE
    tile_expert = jnp.minimum(jnp.sum((ends[None, :] <= tile_start[:, None]).astype(I32), axis=1), N_EXPERTS - 1)
    n_used = (ends[-1] // EXPERT_TILE).astype(I32).reshape(1)

    later = sum(g[0].shape[0] for g in groups[1:])
    assert later <= EXPERT_TILE, "fill counts must stay below 2 * EXPERT_TILE"
    n_used_tiles = ends[-1] // EXPERT_TILE
    pad_info = jnp.concatenate([starts + group_counts[0], padded - group_counts[0],
                                jnp.stack([n_used_tiles, n_tiles - n_used_tiles])]).astype(I32)
    xs = None
    dests = []
    base = starts
    for (h, hn, ri, rf, _), cnt in zip(groups, group_counts):
        n = h.shape[0]
        dest = jnp.sum(jnp.where(ri[:, 0:2, None] == experts, base, 0), axis=-1) + ri[:, 2:4]
        by_tile = lambda t, dest=dest, n=n: dest.reshape(n // t, t, 2).transpose(0, 2, 1).reshape(-1)
        dests.append(by_tile(min(GATHER_TILE, n)))
        td = min(DISPATCH_TILE, n)
        xs = _dispatch(hn, by_tile(td), pad_info, xs, n_slots, td)
        base = base + cnt
    out = _experts(xs, tile_expert, n_used, w_eg, w_eu, w_ed)
    return [_final(h, out, dest_tiles, rf, final_w) for (h, _, _, rf, _), dest_tiles in zip(groups, dests)]


def _layer(x, pos0, kv_bufs, conv_state, ssm_state, p):
    batch, seq, _ = x.shape
    n = batch * seq
    x2d = x.reshape(n, D_MODEL)
    prompt = kv_bufs is None
    q, k, v, z, xbc, g_a, g_s, dt_raw = _inproj(x2d, batch, seq, pos0, p["norm1_w"], p["w_in"], prompt)

    outs, lses, new_kv = [], [], []
    for gi, (window, dil) in enumerate(ATTN_GROUPS):
        if prompt:
            o, l = _attn_prompt(q[gi], k[gi], v[gi], dil)
            keep = min(window, seq) // dil
            tail = lambda t: t[:, :, seq // dil - keep:, :].transpose(0, 2, 1, 3).reshape(batch, keep * dil, GROUP_W)
            nbuf = jnp.stack([tail(k[gi]), tail(v[gi])], axis=2).reshape(
                batch, keep * dil, 2, HEADS_PER_GROUP, HEAD_DIM)
        else:
            o, l, nbuf = _attn_step(q, k, v, kv_bufs[gi], batch, seq, gi, dil)
        outs.append(o)
        lses.append(l)
        new_kv.append(nbuf)

    if conv_state is None:
        conv_state = jnp.zeros((batch, 3, SSD_CONV_DIM), F32)
        ssm_state = jnp.zeros((batch, SSD_HEADS, HEAD_DIM, SSD_STATE), F32)
    y_ssd, new_conv, new_ssm = _ssd(xbc, z, dt_raw, conv_state, ssm_state, batch, seq, p["conv_w"], p["conv_b"],
                                    p["dt_bias"], p["A_log"], p["D_skip"], p["ssd_norm_w"])

    h, hn, ri, rf, counts = _merge(x2d, outs, lses, y_ssd, g_a, g_s, p["w_attn_br"], p["w_ssd_br"], p["b_gate"],
                                   p["w_out"], p["norm2_w"], p["w_router"], p["b_router"], prompt)
    return (h, hn, ri, rf, counts), new_kv, new_conv, new_ssm


def _pack_w_in(w_in):
    offs = (0, 768, 1536, 2304, 3328, 4864, 4880, 5904, 6928)
    q, k, v, z, xbc, dt, g_a, g_s = (w_in[:, offs[i]:offs[i + 1]] for i in range(8))
    dt = jnp.pad(dt, ((0, 0), (0, LANES - SSD_HEADS)))
    return jnp.concatenate([q, k, v, z, xbc, g_a, g_s, dt], axis=1).astype(BF16)


def kernel(x_prompt, x_sample, cache_kv_w128, cache_kv_w512, cache_kv_w2048, state_conv, state_ssm, norm1_w, w_in, w_attn_br, w_ssd_br, b_gate, w_out, conv_w, conv_b, dt_bias, A_log, D_skip, ssd_norm_w, norm2_w, w_router_coarse, b_router_coarse, w_router_fine, b_router_fine, w_expert_gate, w_expert_up, w_expert_down, final_norm_w):
    depth = norm1_w.shape[0]
    assert depth == 1, "the final norm is fused into the layer's last kernel"
    l = 0
    rpad = LANES - N_EXPERTS - N_GROUPS_E
    p = dict(
        norm1_w=norm1_w[l], w_in=_pack_w_in(w_in[l]),
        w_attn_br=w_attn_br[l].astype(BF16), w_ssd_br=w_ssd_br[l].astype(BF16), b_gate=b_gate[l],
        w_out=w_out[l].astype(BF16), conv_w=conv_w[l], conv_b=conv_b[l], dt_bias=dt_bias[l], A_log=A_log[l],
        D_skip=D_skip[l], ssd_norm_w=ssd_norm_w[l], norm2_w=norm2_w[l],
        w_router=jnp.pad(jnp.concatenate([w_router_fine[l], w_router_coarse[l]], axis=1),
                         ((0, 0), (0, rpad))).astype(BF16),
        b_router=jnp.pad(jnp.concatenate([b_router_fine[l], b_router_coarse[l]]), (0, rpad)).reshape(1, LANES),
        w_eg=w_expert_gate[l], w_eu=w_expert_up[l], w_ed=w_expert_down[l], final_norm_w=final_norm_w,
    )
    g_p, kv_p, c_p, st_p = _layer(x_prompt, 0, None, None, None, p)
    bufs = (cache_kv_w128[l], cache_kv_w512[l], cache_kv_w2048[l])
    g_s, kv_s, c_s, st_s = _layer(x_sample, PAST_LEN, bufs, state_conv[l], state_ssm[l], p)
    y_p, y_s = _moe_and_final([g_p, g_s], p["w_eg"], p["w_eu"], p["w_ed"], p["final_norm_w"])
    y_p = y_p.reshape(x_prompt.shape)
    y_s = y_s.reshape(x_sample.shape)
    lead = lambda t: t[None]
    return (y_p, y_s, lead(kv_p[0]), lead(kv_p[1]), lead(kv_p[2]), lead(c_p), lead(st_p),
            lead(kv_s[0]), lead(kv_s[1]), lead(kv_s[2]), lead(c_s), lead(st_s))
```

```python
import functools

import jax
import jax.numpy as jnp
from jax import lax
from jax.experimental import pallas as pl
from jax.experimental.pallas import tpu as pltpu

F32 = jnp.float32
BF16 = jnp.bfloat16
I32 = jnp.int32

D_MODEL = 1024
HEAD_DIM = 64
ATTN_GROUPS = ((128, 1), (512, 4), (2048, 16))
SPAN = 128
HEADS_PER_GROUP = 4
GROUP_W = HEADS_PER_GROUP * HEAD_DIM
ATTN_W = GROUP_W * len(ATTN_GROUPS)
ROPE_THETA = 10000.0
PAST_LEN = 8192
SSD_INNER = 1024
SSD_HEADS = 16
SSD_STATE = 128
SSD_CONV_DIM = 1536
SSD_CHUNK = 128
SSD_CHUNKS_PER_STEP = 4
N_GROUPS_E = 4
EXPERTS_PER_GROUP = 8
N_EXPERTS = 32
D_FF = 512
EPS = 1e-6
LANES = 128
SUBLANES = 8
NEG = -1e30

_OFF_Q, _OFF_K, _OFF_V, _OFF_Z, _OFF_XBC, _OFF_DT, _OFF_GA, _OFF_GS = 0, 768, 1536, 2304, 3328, 4864, 4880, 5904
_W_IN_COLS = 6928

ROW_TILE = 512
MERGE_SPLIT = 1
INPROJ_TILE = 512
EXPERT_TILE = 512
VMEM_LIMIT = 56 * 1024 * 1024


def _cparams(sem, **kw):
    return pltpu.CompilerParams(dimension_semantics=sem, vmem_limit_bytes=VMEM_LIMIT, **kw)


def _sigmoid(x):
    return 1.0 / (1.0 + jnp.exp(-x))


ROW_CHUNKS = D_MODEL // LANES


def _store_row_tiles(ref, val):
    rows = val.shape[0]
    for c in range(ROW_CHUNKS):
        ref[pl.ds(c, rows, stride=ROW_CHUNKS), :] = val[:, c * LANES:(c + 1) * LANES]


def _load_row_tiles(ref, first, rows):
    return jnp.concatenate(
        [ref[pl.ds(first * ROW_CHUNKS + c, rows, stride=ROW_CHUNKS), :] for c in range(ROW_CHUNKS)], axis=1)


def _nt_dot(a, b):
    return lax.dot_general(a, b, (((1,), (1,)), ((), ())), preferred_element_type=F32)


def _store_by_class(o_ref, scr_ref, val, dil):
    if dil == 1:
        o_ref[0, 0] = val
        return
    rows = val.shape[0]
    for half in range(GROUP_W // LANES):
        lanes = slice(half * LANES, (half + 1) * LANES)
        scr_ref[half] = val[:, lanes]
        for r in range(dil):
            o_ref[0, r, :, lanes] = scr_ref[half, pl.ds(r, rows // dil, stride=dil), :]


def _inproj_kernel(x_ref, nw_ref, w_ref, cos_ref, sin_ref, *refs, class_major):
    if class_major:
        qkv_refs, (z_ref, xbc_ref, ga_ref, gs_ref, dt_ref, scr_ref) = refs[:9], refs[9:]
    else:
        qkv_refs, (z_ref, xbc_ref, ga_ref, gs_ref, dt_ref) = refs[:3], refs[3:]
    x = x_ref[...]
    xn = x * lax.rsqrt(jnp.mean(x * x, axis=-1, keepdims=True) + EPS) * nw_ref[...]
    xb = xn.astype(BF16)
    cos = cos_ref[...]
    sin = sin_ref[...]
    lane = lax.broadcasted_iota(I32, cos.shape, 1)
    first_half = (lane % HEAD_DIM) < (HEAD_DIM // 2)

    def mm(lo, hi):
        return _nt_dot(xb, w_ref[lo:hi, :])

    def rope(uc):
        ur = jnp.where(first_half, pltpu.roll(uc, LANES - HEAD_DIM // 2, 1), pltpu.roll(uc, HEAD_DIM // 2, 1))
        return uc * cos + ur * sin

    def emit(which, base, roped):
        u = mm(base, base + ATTN_W)
        chunks = [u[:, c * LANES:(c + 1) * LANES] for c in range(ATTN_W // LANES)]
        if roped:
            chunks = [rope(uc) for uc in chunks]
        if class_major:
            for gi, (_, dil) in enumerate(ATTN_GROUPS):
                val = jnp.concatenate(chunks[2 * gi:2 * gi + 2], axis=1)
                _store_by_class(qkv_refs[3 * which + gi], scr_ref.at[3 * which + gi], val, dil)
        else:
            for c, uc in enumerate(chunks):
                qkv_refs[which][:, c * LANES:(c + 1) * LANES] = uc

    emit(0, _OFF_Q, True)
    emit(1, _OFF_K, True)
    emit(2, _OFF_V, False)
    z_ref[...] = mm(_OFF_Z, _OFF_XBC)
    xbc_ref[...] = mm(_OFF_XBC, _OFF_DT)
    ga_ref[...] = mm(_OFF_GA, _OFF_GS)
    gs_ref[...] = mm(_OFF_GS, _W_IN_COLS)
    dt_ref[...] = jnp.where(lane < SSD_HEADS, mm(_OFF_DT, _OFF_DT + LANES), 0.0)


def _rope_tables(pos):
    half = HEAD_DIM // 2
    inv_freq = ROPE_THETA ** (-jnp.arange(half, dtype=F32) * (2.0 / HEAD_DIM))
    ang = pos.astype(F32)[:, None] * inv_freq[None, :]
    ang = jnp.tile(ang, (1, LANES // half))
    lane = jnp.arange(LANES)
    sign = jnp.where((lane % HEAD_DIM) < half, -1.0, 1.0).astype(F32)
    return jnp.cos(ang), jnp.sin(ang) * sign[None, :]


def _inproj(x2d, batch, seq_len, pos0, norm_w, w_packed, class_major):
    n = x2d.shape[0]
    tm = min(INPROJ_TILE, n)
    cos, sin = _rope_tables(pos0 + jnp.arange(seq_len, dtype=I32))
    if seq_len < tm:
        cos = jnp.tile(cos, (tm // seq_len, 1))
        sin = jnp.tile(sin, (tm // seq_len, 1))
    tab_blocks = cos.shape[0] // tm
    row = lambda i: (i, 0)
    fixed = lambda i: (0, 0)
    tab = lambda i: (i % tab_blocks, 0)
    widths = (SSD_INNER, SSD_CONV_DIM, D_MODEL, D_MODEL, LANES)
    out_specs = [pl.BlockSpec((tm, w), row) for w in widths]
    out_shape = [jax.ShapeDtypeStruct((n, w), F32) for w in widths]
    scratch = []
    if class_major:
        assert seq_len % tm == 0
        per_seq = seq_len // tm
        cls = lambda i: (i // per_seq, 0, i % per_seq, 0)
        dils = [d for _, d in ATTN_GROUPS] * 3
        out_specs = [pl.BlockSpec((1, d, tm // d, GROUP_W), cls) for d in dils] + out_specs
        out_shape = [jax.ShapeDtypeStruct((batch, d, seq_len // d, GROUP_W), F32) for d in dils] + out_shape
        scratch = [pltpu.VMEM((len(dils), GROUP_W // LANES, tm, LANES), F32)]
    else:
        out_specs = [pl.BlockSpec((tm, ATTN_W), row)] * 3 + out_specs
        out_shape = [jax.ShapeDtypeStruct((n, ATTN_W), F32)] * 3 + out_shape
    res = pl.pallas_call(
        functools.partial(_inproj_kernel, class_major=class_major),
        grid=(n // tm,),
        in_specs=[pl.BlockSpec((tm, D_MODEL), row),
                  pl.BlockSpec((1, D_MODEL), fixed),
                  pl.BlockSpec((_W_IN_COLS, D_MODEL), fixed, pipeline_mode=pl.Buffered(1)),
                  pl.BlockSpec((tm, LANES), tab),
                  pl.BlockSpec((tm, LANES), tab)],
        out_specs=out_specs,
        out_shape=out_shape,
        scratch_shapes=scratch,
        compiler_params=_cparams(("parallel",)),
        name="inproj",
    )(x2d, norm_w.reshape(1, D_MODEL), w_packed, cos, sin)
    if class_major:
        return (tuple(res[0:3]), tuple(res[3:6]), tuple(res[6:9])) + tuple(res[9:])
    return tuple(res)


ATTN_QBLOCKS = 4


def _attn_prompt_kernel(q_ref, kp_ref, kc_ref, vp_ref, vc_ref, o_ref, l_ref, k_scr, v_scr, s_scr, p_scr, r_scr,
                        *, qblocks):
    c = pl.program_id(2)
    i = lax.broadcasted_iota(I32, (SPAN, 2 * SPAN), 0)
    j = lax.broadcasted_iota(I32, (SPAN, 2 * SPAN), 1)
    band = (j >= i) & (j <= i + SPAN)
    first_band = band & ((j >= SPAN) | (c > 0))
    k_scr[0:SPAN, :] = kp_ref[0, 0].astype(BF16)
    k_scr[SPAN:, :] = kc_ref[0, 0].astype(BF16)
    v_scr[0:SPAN, :] = vp_ref[0, 0].astype(BF16)
    v_scr[SPAN:, :] = vc_ref[0, 0].astype(BF16)
    units = [(b, h) for b in range(qblocks) for h in range(HEADS_PER_GROUP)]

    for u, (b, h) in enumerate(units):
        hs = slice(h * HEAD_DIM, (h + 1) * HEAD_DIM)
        qh = (q_ref[0, 0, b * SPAN:(b + 1) * SPAN, hs] * (HEAD_DIM ** -0.5)).astype(BF16)
        s_scr[u] = _nt_dot(qh, k_scr[b * SPAN:(b + 2) * SPAN, hs])

    for u, (b, h) in enumerate(units):
        hs = slice(h * HEAD_DIM, (h + 1) * HEAD_DIM)
        s = jnp.where(first_band if b == 0 else band, s_scr[u], NEG)
        m = jnp.max(jnp.maximum(s[:, :SPAN], s[:, SPAN:]), axis=-1, keepdims=True)
        p = jnp.exp(s - m)
        den = jnp.sum(p[:, :SPAN] + p[:, SPAN:], axis=-1, keepdims=True)
        p_scr[u] = p.astype(BF16)
        r_scr[u] = jnp.broadcast_to(1.0 / den, (SPAN, HEAD_DIM))
        l_ref[0, 0, b * SPAN:(b + 1) * SPAN, hs] = jnp.broadcast_to(m + jnp.log(den), (SPAN, HEAD_DIM))

    for u, (b, h) in enumerate(units):
        hs = slice(h * HEAD_DIM, (h + 1) * HEAD_DIM)
        acc = jnp.dot(p_scr[u], v_scr[b * SPAN:(b + 2) * SPAN, hs], preferred_element_type=F32)
        o_ref[0, 0, b * SPAN:(b + 1) * SPAN, hs] = acc * r_scr[u]


def _attn_prompt(q, k, v, dil):
    batch, _, n_cls, _ = q.shape
    qblocks = min(ATTN_QBLOCKS, n_cls // SPAN)
    tq = qblocks * SPAN
    assert n_cls % tq == 0
    cur = lambda b, r, c: (b, r, c, 0)
    prev = lambda b, r, c: (b, r, jnp.maximum(c * qblocks - 1, 0), 0)
    big = pl.BlockSpec((1, 1, tq, GROUP_W), cur)
    small = pl.BlockSpec((1, 1, SPAN, GROUP_W), prev)
    return pl.pallas_call(
        functools.partial(_attn_prompt_kernel, qblocks=qblocks),
        grid=(batch, dil, n_cls // tq),
        in_specs=[big, small, big, small, big],
        out_specs=[big, big],
        out_shape=[jax.ShapeDtypeStruct(q.shape, F32)] * 2,
        scratch_shapes=[pltpu.VMEM((SPAN + tq, GROUP_W), BF16),
                        pltpu.VMEM((SPAN + tq, GROUP_W), BF16),
                        pltpu.VMEM((qblocks * HEADS_PER_GROUP, SPAN, 2 * SPAN), F32),
                        pltpu.VMEM((qblocks * HEADS_PER_GROUP, SPAN, 2 * SPAN), BF16),
                        pltpu.VMEM((qblocks * HEADS_PER_GROUP, SPAN, HEAD_DIM), F32)],
        compiler_params=_cparams(("parallel", "parallel", "arbitrary")),
        name=f"attn_prompt_d{dil}",
    )(q, k, k, v, v)


STEP_ROWS_PER_CALL = 2048


def _attn_step_kernel(q_ref, kn_ref, vn_ref, buf_ref, o_ref, l_ref, nbuf_ref, *, buf_len, dil, n_new, seqs):
    t = lax.broadcasted_iota(I32, (n_new, buf_len), 0)
    i = lax.broadcasted_iota(I32, (n_new, buf_len), 1)
    delta = buf_len + t - i
    valid_b = delta <= dil * SPAN
    first_new = LANES - n_new
    tn = lax.broadcasted_iota(I32, (n_new, LANES), 0)
    un = lax.broadcasted_iota(I32, (n_new, LANES), 1) - first_new
    dn = tn - un
    valid_n = (dn >= 0) & (un >= 0)
    if dil > 1:
        valid_b = valid_b & ((delta & (dil - 1)) == 0)
        valid_n = valid_n & ((dn & (dil - 1)) == 0)

    def to_columns(x):
        xp = jnp.concatenate([x, jnp.zeros((LANES - n_new, GROUP_W), F32)], axis=0)
        xt = jnp.concatenate([xp[:, c * LANES:(c + 1) * LANES].T for c in range(GROUP_W // LANES)], axis=0)
        return pltpu.roll(xt, first_new, 1)

    is_new = lax.broadcasted_iota(I32, (HEAD_DIM, LANES), 1) >= first_new
    tail = slice(buf_len - LANES, buf_len)
    for b, h in [(b, h) for b in range(seqs) for h in range(HEADS_PER_GROUP)]:
        if h == 0:
            q = q_ref[b] * (HEAD_DIM ** -0.5)
            new_cols = (to_columns(kn_ref[b]), to_columns(vn_ref[b]))
        hs = slice(h * HEAD_DIM, (h + 1) * HEAD_DIM)
        qh = q[:, hs].astype(BF16)
        kt = buf_ref[b, 0, h]
        vt = buf_ref[b, 1, h]
        knt = new_cols[0][hs]
        vnt = new_cols[1][hs]
        sb = jnp.where(valid_b, jnp.dot(qh, kt.astype(BF16), preferred_element_type=F32), NEG)
        sn = jnp.where(valid_n, jnp.dot(qh, knt.astype(BF16), preferred_element_type=F32), NEG)
        m = jnp.maximum(jnp.max(sb, axis=-1, keepdims=True), jnp.max(sn, axis=-1, keepdims=True))
        pb = jnp.exp(sb - m)
        pn = jnp.exp(sn - m)
        den = jnp.sum(pb, axis=-1, keepdims=True) + jnp.sum(pn, axis=-1, keepdims=True)
        acc = _nt_dot(pb.astype(BF16), vt.astype(BF16)) + _nt_dot(pn.astype(BF16), vnt.astype(BF16))
        o_ref[b, :, hs] = acc / den
        l_ref[b, :, hs] = jnp.broadcast_to(m + jnp.log(den), (n_new, HEAD_DIM))
        for kv, (old, new) in enumerate(((kt, knt), (vt, vnt))):
            shifted = pltpu.roll(old, buf_len - n_new, 1)
            nbuf_ref[b, kv, h] = shifted
            nbuf_ref[b, kv, h, :, tail] = jnp.where(is_new, new, shifted[:, tail])


def _attn_step(q, k, v, buf, batch, n_new, gi, dil):
    buf_len = buf.shape[1]
    assert dil & (dil - 1) == 0 and buf_len >= dil * SPAN and n_new % SUBLANES == 0 and buf_len % LANES == 0
    view = lambda t: t.reshape(batch, n_new, ATTN_W)
    seqs = max(1, min(batch, STEP_ROWS_PER_CALL // buf_len))
    assert batch % seqs == 0
    tok = pl.BlockSpec((seqs, n_new, GROUP_W), lambda b: (b, 0, gi))
    full = pl.BlockSpec((seqs, 2, HEADS_PER_GROUP, HEAD_DIM, buf_len), lambda b: (b, 0, 0, 0, 0))
    osp = pl.BlockSpec((seqs, n_new, GROUP_W), lambda b: (b, 0, 0))
    o, l, nbuf = pl.pallas_call(
        functools.partial(_attn_step_kernel, buf_len=buf_len, dil=dil, n_new=n_new, seqs=seqs),
        grid=(batch // seqs,),
        in_specs=[tok, tok, tok, full],
        out_specs=[osp, osp, full],
        out_shape=[jax.ShapeDtypeStruct((batch, n_new, GROUP_W), F32)] * 2
        + [jax.ShapeDtypeStruct((batch, 2, HEADS_PER_GROUP, HEAD_DIM, buf_len), F32)],
        compiler_params=_cparams(("parallel",)),
        name=f"attn_step_d{dil}",
    )(view(q), view(k), view(v), jnp.transpose(buf, (0, 2, 3, 4, 1)))
    return (o.reshape(batch * n_new, GROUP_W), l.reshape(batch * n_new, GROUP_W),
            jnp.transpose(nbuf, (0, 4, 1, 2, 3)))


def _split3(a):
    a1 = a.astype(BF16)
    r1 = a - a1.astype(F32)
    a2 = r1.astype(BF16)
    a3 = (r1 - a2.astype(F32)).astype(BF16)
    return a1, a2, a3


def _ssd_kernel(*refs, n_valid, n_steps, per_step):
    for sub in range(per_step):
        _ssd_chunk(*refs, n_valid=n_valid, n_steps=n_steps, per_step=per_step, sub=sub)


def _ssd_chunk(xbc_ref, z_ref, dt_ref, cst_ref, h0_ref, cw_ref, cb_ref, dtb_ref, alog_ref, dfull_ref, nw_ref,
               tri_ref, expand_ref, y_ref, cout_ref, hout_ref, xpad_ref, h_ref, *, n_valid, n_steps, per_step, sub):
    c = pl.program_id(1)
    lc = SSD_CHUNK
    pad = SUBLANES
    n_slabs = SSD_CONV_DIM // LANES
    tok = slice(sub * n_valid, (sub + 1) * n_valid)

    def carry_rows():
        for j in range(n_slabs):
            xpad_ref[j, 0:pad, :] = xpad_ref[j, lc:lc + pad, :]

    if sub == 0:
        @pl.when(c == 0)
        def _():
            for j in range(n_slabs):
                xpad_ref[j, 0:pad, :] = cst_ref[0, :, j * LANES:(j + 1) * LANES]
            h_ref[...] = h0_ref[0]

        pl.when(c > 0)(carry_rows)
    else:
        carry_rows()

    if n_valid == lc:
        z = z_ref[tok, :]
        dtr = dt_ref[tok, :]
    else:
        fill = lambda w: jnp.zeros((lc - n_valid, w), F32)
        z = jnp.concatenate([z_ref[tok, :], fill(SSD_INNER)], axis=0)
        dtr = jnp.concatenate([dt_ref[tok, :], fill(LANES)], axis=0)

    slabs = []
    for j in range(n_slabs):
        cols = slice(j * LANES, (j + 1) * LANES)
        xpad_ref[j, pad:pad + n_valid, :] = xbc_ref[tok, cols]
        if n_valid < lc:
            xpad_ref[j, pad + n_valid:pad + lc, :] = jnp.zeros((lc - n_valid, LANES), F32)
        xc = cb_ref[:, cols]
        for tap in range(4):
            xc = xc + xpad_ref[j, pl.ds(pad - 3 + tap, lc), :] * cw_ref[tap:tap + 1, cols]
        slabs.append(xc * _sigmoid(xc))
        cout_ref[0, :, cols] = xpad_ref[j, pl.ds(pad + n_valid - 3, 3), :]
    xs = jnp.concatenate(slabs[:SSD_INNER // LANES], axis=1)
    bm = slabs[SSD_INNER // LANES:SSD_INNER // LANES + 2]
    cm = slabs[SSD_INNER // LANES + 2:]

    row = lax.broadcasted_iota(I32, (lc, lc), 0)
    col = lax.broadcasted_iota(I32, (lc, lc), 1)
    causal = row >= col
    dtv = dtr + dtb_ref[...]
    dt = jnp.maximum(dtv, 0.0) + jnp.log1p(jnp.exp(-jnp.abs(dtv)))
    if n_valid < lc:
        dt = jnp.where(row < n_valid, dt, 0.0)
    a = dt * (-jnp.exp(alog_ref[...]))
    tri = tri_ref[...]
    a_cs = sum(jnp.dot(tri, p, preferred_element_type=F32) for p in _split3(a))
    a_cs_t = a_cs.T
    expand = expand_ref[...]
    a_full = sum(jnp.dot(p, expand, preferred_element_type=F32) for p in _split3(a_cs))
    dt_full = sum(jnp.dot(p, expand, preferred_element_type=F32) for p in _split3(dt))
    xdt = xs * dt_full
    xd = xdt * jnp.exp(a_full[lc - 1:lc, :] - a_full)
    grow = jnp.exp(a_full)
    xdt_b = xdt.astype(BF16)
    xd_t = jnp.concatenate([xd[:, k * LANES:(k + 1) * LANES].T for k in range(SSD_INNER // LANES)],
                           axis=0).astype(BF16)
    lane = lax.broadcasted_iota(I32, (lc, LANES), 1)
    low_half = lane < HEAD_DIM
    zero_b = jnp.zeros((lc, LANES), BF16)
    half = SSD_INNER // 2

    y_parts = []
    for g in range(2):
        bg = bm[g].astype(BF16)
        cg = cm[g].astype(BF16)
        cb = jnp.where(causal, _nt_dot(cg, bg), 0.0)
        h_grp = h_ref[g * half:(g + 1) * half, :]
        y_off = _nt_dot(cg, h_grp.astype(BF16)) * grow[:, g * half:(g + 1) * half]
        for pair in range(4):
            e0 = g * 8 + pair * 2
            rows = slice(e0 * HEAD_DIM, (e0 + 2) * HEAD_DIM)
            ms, keep = [], []
            for k in range(2):
                e = e0 + k
                seg = a_cs[:, e:e + 1] - a_cs_t[e:e + 1, :]
                ms.append((cb * jnp.exp(jnp.minimum(seg, 0.0))).astype(BF16))
                keep.append(jnp.broadcast_to(jnp.exp(a_cs[lc - 1:lc, e:e + 1]), (HEAD_DIM, SSD_STATE)))
            pair_b = xdt_b[:, rows]
            rhs = jnp.concatenate([jnp.where(low_half, pair_b, zero_b), jnp.where(low_half, zero_b, pair_b)], axis=0)
            y_parts.append(jnp.dot(jnp.concatenate(ms, axis=1), rhs, preferred_element_type=F32)
                           + y_off[:, pair * LANES:(pair + 1) * LANES])
            st = jnp.dot(xd_t[rows, :], bg, preferred_element_type=F32)
            h_ref[rows, :] = h_grp[pair * LANES:(pair + 1) * LANES, :] * jnp.concatenate(keep, axis=0) + st

    y = jnp.concatenate(y_parts, axis=1) + xs * dfull_ref[...]
    gate = y * (z * _sigmoid(z))
    for g in range(2):
        gg = gate[:, g * half:(g + 1) * half]
        gg = gg * lax.rsqrt(jnp.mean(gg * gg, axis=-1, keepdims=True) + EPS)
        y_ref[tok, g * half:(g + 1) * half] = (gg * nw_ref[:, g * half:(g + 1) * half])[0:n_valid]

    if sub == per_step - 1:
        @pl.when(c == n_steps - 1)
        def _():
            hout_ref[0] = h_ref[...]


def _ssd(xbc, z, dt_raw, conv_state, ssm_state, batch, seq, conv_w, conv_b, dt_bias, a_log, d_skip, norm_w):
    n_valid = min(seq, SSD_CHUNK)
    n_chunks = seq // n_valid
    assert n_valid % SUBLANES == 0 and seq % n_valid == 0
    padl = lambda t: jnp.pad(t.reshape(1, SSD_HEADS), ((0, 0), (0, LANES - SSD_HEADS)))
    cst = jnp.pad(conv_state, ((0, 0), (SUBLANES - 3, 0), (0, 0)))
    tri = (jnp.arange(SSD_CHUNK)[:, None] >= jnp.arange(SSD_CHUNK)[None, :]).astype(BF16)
    expand = (jnp.arange(LANES)[:, None] == jnp.arange(SSD_INNER)[None, :] // HEAD_DIM).astype(BF16)
    per_step = SSD_CHUNKS_PER_STEP if (n_valid == SSD_CHUNK and n_chunks % SSD_CHUNKS_PER_STEP == 0) else 1
    n_steps = n_chunks // per_step
    step_rows = per_step * n_valid
    tokrow = lambda b, c: (b * n_steps + c, 0)
    fixed = lambda b, c: (0, 0)
    per_b3 = lambda b, c: (b, 0, 0)
    state = pl.BlockSpec((1, SSD_INNER, SSD_STATE), per_b3)
    kern = functools.partial(_ssd_kernel, n_valid=n_valid, n_steps=n_steps, per_step=per_step)
    y, cout, hout = pl.pallas_call(
        kern,
        grid=(batch, n_steps),
        in_specs=[pl.BlockSpec((step_rows, SSD_CONV_DIM), tokrow),
                  pl.BlockSpec((step_rows, SSD_INNER), tokrow),
                  pl.BlockSpec((step_rows, LANES), tokrow),
                  pl.BlockSpec((1, SUBLANES, SSD_CONV_DIM), per_b3),
                  state,
                  pl.BlockSpec((4, SSD_CONV_DIM), fixed),
                  pl.BlockSpec((1, SSD_CONV_DIM), fixed),
                  pl.BlockSpec((1, LANES), fixed),
                  pl.BlockSpec((1, LANES), fixed),
                  pl.BlockSpec((1, SSD_INNER), fixed),
                  pl.BlockSpec((1, SSD_INNER), fixed),
                  pl.BlockSpec((SSD_CHUNK, SSD_CHUNK), fixed),
                  pl.BlockSpec((LANES, SSD_INNER), fixed)],
        out_specs=[pl.BlockSpec((step_rows, SSD_INNER), tokrow),
                   pl.BlockSpec((1, 3, SSD_CONV_DIM), per_b3),
                   state],
        out_shape=[jax.ShapeDtypeStruct((batch * seq, SSD_INNER), F32),
                   jax.ShapeDtypeStruct((batch, 3, SSD_CONV_DIM), F32),
                   jax.ShapeDtypeStruct((batch, SSD_INNER, SSD_STATE), F32)],
        scratch_shapes=[pltpu.VMEM((SSD_CONV_DIM // LANES, SSD_CHUNK + SUBLANES, LANES), F32),
                        pltpu.VMEM((SSD_INNER, SSD_STATE), F32)],
        compiler_params=_cparams(("parallel", "arbitrary")),
        name="ssd",
    )(xbc, z, dt_raw, cst, ssm_state.reshape(batch, SSD_INNER, SSD_STATE), conv_w, conv_b.reshape(1, SSD_CONV_DIM),
      padl(dt_bias), padl(a_log), jnp.repeat(d_skip, HEAD_DIM).reshape(1, SSD_INNER), norm_w.reshape(1, SSD_INNER),
      tri, expand)
    return y, cout, hout.reshape(batch, SSD_HEADS, HEAD_DIM, SSD_STATE)


def _merge_kernel(x_ref, o0_ref, o1_ref, o2_ref, l0_ref, l1_ref, l2_ref, ys_ref, ga_ref, gs_ref,
                  wab_ref, wsb_ref, bg_ref, wo_ref, n2_ref, wr_ref, br_ref,
                  h_ref, hn_ref, ri_ref, rf_ref, cnt_ref, carry_ref, *scr, class_major):
    step = pl.program_id(0)

    @pl.when(step == 0)
    def _():
        carry_ref[...] = jnp.zeros_like(carry_ref)

    def load(ref, gi, slab):
        if not class_major:
            return ref[...]
        dil = ATTN_GROUPS[gi][1]
        if dil == 1:
            return ref[0, 0]
        rows = ref.shape[1] * ref.shape[2]
        halves = []
        for half in range(GROUP_W // LANES):
            for r in range(dil):
                scr[0][slab, half, pl.ds(r, rows // dil, stride=dil), :] = ref[0, r, :, half * LANES:(half + 1) * LANES]
            halves.append(scr[0][slab, half])
        return jnp.concatenate(halves, axis=1)

    lses = (load(l0_ref, 0, 0), load(l1_ref, 1, 0), load(l2_ref, 2, 1))
    outs = (load(o0_ref, 0, 0), load(o1_ref, 1, 2), load(o2_ref, 2, 3))
    tm = x_ref.shape[0]
    sub = tm // MERGE_SPLIT
    lane = lax.broadcasted_iota(I32, (sub, LANES), 1)
    big = jnp.int32(LANES)
    r = lax.broadcasted_iota(I32, (sub, sub), 0)
    s = lax.broadcasted_iota(I32, (sub, sub), 1)
    earlier = (r > s).astype(BF16)

    def top(vals):
        v = jnp.max(vals, axis=-1, keepdims=True)
        idx = jnp.min(jnp.where(vals == v, lane, big), axis=-1, keepdims=True)
        return v, idx

    carry = carry_ref[0:1, :]
    for part in range(MERGE_SPLIT):
        rows = slice(part * sub, (part + 1) * sub)
        l0, l1, l2 = (t[rows] for t in lses)
        m = jnp.maximum(jnp.maximum(l0, l1), l2)
        w0, w1, w2 = jnp.exp(l0 - m), jnp.exp(l1 - m), jnp.exp(l2 - m)
        y_attn = (w0 * outs[0][rows] + w1 * outs[1][rows] + w2 * outs[2][rows]) / (w0 + w1 + w2)
        pa = jnp.dot(y_attn.astype(BF16), wab_ref[...], preferred_element_type=F32)
        ps = jnp.dot(ys_ref[rows, :].astype(BF16), wsb_ref[...], preferred_element_type=F32)
        merged = (_sigmoid(ga_ref[rows, :] + bg_ref[0:1, :]) * pa + _sigmoid(gs_ref[rows, :] + bg_ref[1:2, :]) * ps)
        h = x_ref[rows, :] + jnp.dot(merged.astype(BF16), wo_ref[...], preferred_element_type=F32)
        h_ref[rows, :] = h
        hn = h * lax.rsqrt(jnp.mean(h * h, axis=-1, keepdims=True) + EPS) * n2_ref[...]
        hnb = hn.astype(BF16)
        _store_row_tiles(hn_ref.at[pl.ds(part * sub * ROW_CHUNKS, sub * ROW_CHUNKS), :], hn)

        logits = jnp.dot(hnb, wr_ref[...], preferred_element_type=F32) + br_ref[...]
        is_coarse = (lane >= N_EXPERTS) & (lane < N_EXPERTS + N_GROUPS_E)
        lc = jnp.where(is_coarse, logits, NEG)
        mc, ic = top(lc)
        p_grp = 1.0 / jnp.sum(jnp.exp(lc - mc), axis=-1, keepdims=True)
        lo = (ic - N_EXPERTS) * EXPERTS_PER_GROUP
        lf = jnp.where((lane >= lo) & (lane < lo + EXPERTS_PER_GROUP), logits, NEG)
        v1, i1 = top(lf)
        v2, i2 = top(jnp.where(lane == i1, NEG, lf))
        e2 = jnp.exp(v2 - v1)
        g1 = p_grp / (1.0 + e2)
        g2 = p_grp * e2 / (1.0 + e2)

        oh1 = lane == i1
        oh2 = lane == i2
        cnt = oh1.astype(F32) + oh2.astype(F32)
        before = jnp.dot(earlier, cnt.astype(BF16), preferred_element_type=F32) + carry
        r1 = jnp.sum(jnp.where(oh1, before, 0.0), axis=-1, keepdims=True)
        r2 = jnp.sum(jnp.where(oh2, before, 0.0), axis=-1, keepdims=True)
        carry = carry + jnp.sum(cnt, axis=0, keepdims=True)

        ri = jnp.where(lane == 0, i1, jnp.where(lane == 1, i2, 0))
        ri = jnp.where(lane == 2, r1.astype(I32), jnp.where(lane == 3, r2.astype(I32), ri))
        ri_ref[rows, :] = ri
        rf_ref[rows, :] = jnp.where(lane == 0, g1, jnp.where(lane == 1, g2, 0.0))

    carry_ref[...] = jnp.broadcast_to(carry, carry_ref.shape)
    cnt_ref[...] = jnp.broadcast_to(carry, cnt_ref.shape).astype(I32)


def _merge(x2d, outs, lses, y_ssd, g_a, g_s, wab, wsb, b_gate, wo, norm2_w, w_router, b_router, class_major):
    n = x2d.shape[0]
    tm = min(ROW_TILE, n)
    row = lambda i: (i, 0)
    fixed = lambda i: (0, 0)
    wide = pl.BlockSpec((tm, D_MODEL), row)
    info = pl.BlockSpec((tm, LANES), row)
    if class_major:
        per_seq = outs[0].shape[2] // tm
        cls = lambda i: (i // per_seq, 0, i % per_seq, 0)
        grps = [pl.BlockSpec((1, d, tm // d, GROUP_W), cls) for _, d in ATTN_GROUPS]
        scratch = [pltpu.VMEM((4, GROUP_W // LANES, tm, LANES), F32)]
    else:
        grps = [pl.BlockSpec((tm, GROUP_W), row)] * 3
        scratch = []
    return pl.pallas_call(
        functools.partial(_merge_kernel, class_major=class_major),
        grid=(n // tm,),
        in_specs=[wide, *grps, *grps, wide, wide, wide,
                  pl.BlockSpec((GROUP_W, D_MODEL), fixed),
                  pl.BlockSpec((SSD_INNER, D_MODEL), fixed),
                  pl.BlockSpec((2, D_MODEL), fixed),
                  pl.BlockSpec((D_MODEL, D_MODEL), fixed),
                  pl.BlockSpec((1, D_MODEL), fixed),
                  pl.BlockSpec((D_MODEL, LANES), fixed),
                  pl.BlockSpec((1, LANES), fixed)],
        out_specs=[wide, pl.BlockSpec((tm * ROW_CHUNKS, LANES), row), info, info,
                   pl.BlockSpec((SUBLANES, LANES), fixed)],
        out_shape=[jax.ShapeDtypeStruct((n, D_MODEL), F32),
                   jax.ShapeDtypeStruct((n * ROW_CHUNKS, LANES), F32),
                   jax.ShapeDtypeStruct((n, LANES), I32),
                   jax.ShapeDtypeStruct((n, LANES), F32),
                   jax.ShapeDtypeStruct((SUBLANES, LANES), I32)],
        scratch_shapes=[pltpu.VMEM((SUBLANES, LANES), F32)] + scratch,
        compiler_params=_cparams(("arbitrary",)),
        name="merge_out",
    )(x2d, *outs, *lses, y_ssd, g_a, g_s, wab, wsb, b_gate, wo, norm2_w.reshape(1, D_MODEL), w_router, b_router)


GATHER_TILE = 512
DISPATCH_TILE = 512
INDEX_BATCH = 16


def _dispatch_kernel(dest_ref, pad_ref, hn_ref, *rest, tm, first_group):
    if first_group:
        xs_hbm, zeros_ref, sem, pad_sem = rest
    else:
        _, xs_hbm, sem = rest
    i = pl.program_id(0)

    if first_group:
        @pl.when(i == 0)
        def _():
            zeros_ref[...] = jnp.zeros_like(zeros_ref)

            def fill(first_slot, size):
                first = pl.multiple_of(first_slot * ROW_CHUNKS, ROW_CHUNKS)
                return pltpu.make_async_copy(zeros_ref.at[pl.ds(0, size * ROW_CHUNKS), :],
                                             xs_hbm.at[pl.ds(first, size * ROW_CHUNKS), :], pad_sem)

            fills = []
            for e in range(N_EXPERTS):
                start, count = pad_ref[e], pad_ref[N_EXPERTS + e]
                size = EXPERT_TILE
                while size >= 1:
                    fills.append(((count & size) != 0, start + (count & ~(2 * size - 1)), size))
                    size //= 2
            tail_first, tail_tiles = pad_ref[2 * N_EXPERTS], pad_ref[2 * N_EXPERTS + 1]
            for t in range(N_EXPERTS):
                fills.append((t < tail_tiles, (tail_first + t) * EXPERT_TILE, EXPERT_TILE))
            for pred, first_slot, size in fills:
                pl.when(pred)(lambda first_slot=first_slot, size=size: fill(first_slot, size).start())
            for pred, first_slot, size in fills:
                pl.when(pred)(lambda first_slot=first_slot, size=size: fill(first_slot, size).wait())

    for j0 in range(0, 2 * tm, INDEX_BATCH):
        slots = [dest_ref[i * 2 * tm + j0 + u] for u in range(INDEX_BATCH)]
        for u, dst in enumerate(slots):
            tok = (j0 + u) // 2
            first = pl.multiple_of(dst * ROW_CHUNKS, ROW_CHUNKS)
            pltpu.make_async_copy(hn_ref.at[pl.ds(tok * ROW_CHUNKS, ROW_CHUNKS), :],
                                  xs_hbm.at[pl.ds(first, ROW_CHUNKS), :], sem).start(priority=u % 2)
    for _ in range(2):
        pltpu.make_async_copy(hn_ref, xs_hbm.at[pl.ds(0, tm * ROW_CHUNKS), :], sem).wait()


def _dispatch(hn, dest_tiles, pad_info, xs, n_slots, tm):
    n = hn.shape[0] // ROW_CHUNKS
    first_group = xs is None
    anyspec = pl.BlockSpec(memory_space=pl.ANY)
    scratch = [pltpu.SemaphoreType.DMA(())]
    if first_group:
        scratch = [pltpu.VMEM((EXPERT_TILE * ROW_CHUNKS, LANES), F32)] + scratch + [pltpu.SemaphoreType.DMA(())]
    return pl.pallas_call(
        functools.partial(_dispatch_kernel, tm=tm, first_group=first_group),
        grid_spec=pltpu.PrefetchScalarGridSpec(
            num_scalar_prefetch=2,
            grid=(n // tm,),
            in_specs=[pl.BlockSpec((tm * ROW_CHUNKS, LANES), lambda i, d, p: (i, 0))] + ([] if first_group else [anyspec]),
            out_specs=anyspec,
            scratch_shapes=scratch),
        out_shape=jax.ShapeDtypeStruct((n_slots * ROW_CHUNKS, LANES), F32),
        input_output_aliases={} if first_group else {3: 0},
        compiler_params=_cparams(("arbitrary",), disable_bounds_checks=True, has_side_effects=True),
        name="dispatch",
    )(dest_tiles, pad_info, hn, *([] if first_group else [xs]))


def _expert_kernel(te_ref, nu_ref, first_ref, slot_ref, next_ref, x_ref, wg_hbm, wu_hbm, wd_hbm, o_ref,
                   wg32, wu32, wd32, wgb, wub, wdb, sem):
    i = pl.program_id(0)

    def fetch(expert, slot):
        return [pltpu.make_async_copy(src.at[expert], dst.at[slot], sem.at[slot])
                for src, dst in ((wg_hbm, wg32), (wu_hbm, wu32), (wd_hbm, wd32))]

    @pl.when((i < nu_ref[0]) & (first_ref[i] == 1))
    def _():
        slot = slot_ref[i]

        @pl.when(i == 0)
        def _():
            for c in fetch(te_ref[0], slot):
                c.start()

        for c in fetch(te_ref[i], slot):
            c.wait()
        wgb[...] = wg32[slot].astype(BF16)
        wub[...] = wu32[slot].astype(BF16)
        wdb[...] = wd32[slot].astype(BF16)

        @pl.when(next_ref[i] >= 0)
        def _():
            for c in fetch(next_ref[i], 1 - slot):
                c.start()

    @pl.when(i < nu_ref[0])
    def _():
        x = _load_row_tiles(x_ref, 0, EXPERT_TILE).astype(BF16)
        hg = jnp.dot(x, wgb[...], preferred_element_type=F32)
        hu = jnp.dot(x, wub[...], preferred_element_type=F32)
        hb = (hg * _sigmoid(hg)) * hu
        _store_row_tiles(o_ref, jnp.dot(hb.astype(BF16), wdb[...], preferred_element_type=F32))

    @pl.when(i >= nu_ref[0])
    def _():
        o_ref[...] = jnp.zeros_like(o_ref)


def _experts(xs, tile_expert, n_used, run_first, run_slot, run_next, w_eg, w_eu, w_ed):
    n_slots = xs.shape[0] // ROW_CHUNKS
    n_tiles = n_slots // EXPERT_TILE
    row = lambda i, *_: (i, 0)
    used_row = lambda i, te, nu, *_: (jnp.minimum(i, nu[0] - 1), 0)
    anyspec = pl.BlockSpec(memory_space=pl.ANY)
    return pl.pallas_call(
        _expert_kernel,
        grid_spec=pltpu.PrefetchScalarGridSpec(
            num_scalar_prefetch=5,
            grid=(n_tiles,),
            in_specs=[pl.BlockSpec((EXPERT_TILE * ROW_CHUNKS, LANES), used_row), anyspec, anyspec, anyspec],
            out_specs=pl.BlockSpec((EXPERT_TILE * ROW_CHUNKS, LANES), row),
            scratch_shapes=[pltpu.VMEM((2, D_MODEL, D_FF), F32),
                            pltpu.VMEM((2, D_MODEL, D_FF), F32),
                            pltpu.VMEM((2, D_FF, D_MODEL), F32),
                            pltpu.VMEM((D_MODEL, D_FF), BF16),
                            pltpu.VMEM((D_MODEL, D_FF), BF16),
                            pltpu.VMEM((D_FF, D_MODEL), BF16),
                            pltpu.SemaphoreType.DMA((2,))]),
        out_shape=jax.ShapeDtypeStruct((n_slots * ROW_CHUNKS, LANES), F32),
        compiler_params=_cparams(("arbitrary",)),
        name="experts",
    )(tile_expert, n_used, run_first, run_slot, run_next, xs, w_eg, w_eu, w_ed)


def _final_kernel(dest_ref, h_ref, rf_ref, fw_ref, out_hbm, o_ref, ybuf, sem, *, tm):
    i = pl.program_id(0)
    slot = i % 2

    def gather(tile, into):
        for j0 in range(0, 2 * tm, INDEX_BATCH):
            slots = [dest_ref[tile * 2 * tm + j0 + u] for u in range(INDEX_BATCH)]
            for u, src in enumerate(slots):
                first = pl.multiple_of(src * ROW_CHUNKS, ROW_CHUNKS)
                row = ((j0 + u) % 2) * tm + (j0 + u) // 2
                pltpu.make_async_copy(out_hbm.at[pl.ds(first, ROW_CHUNKS), :],
                                      ybuf.at[into, pl.ds(row * ROW_CHUNKS, ROW_CHUNKS), :],
                                      sem.at[into]).start(priority=u % 2)

    @pl.when(i == 0)
    def _():
        gather(0, 0)

    @pl.when(i + 1 < pl.num_programs(0))
    def _():
        gather(i + 1, 1 - slot)

    pltpu.make_async_copy(out_hbm.at[pl.ds(0, 2 * tm * ROW_CHUNKS), :], ybuf.at[slot], sem.at[slot]).wait()
    g = rf_ref[...]
    moe = (_load_row_tiles(ybuf.at[slot], 0, tm) * g[:, 0:1] + _load_row_tiles(ybuf.at[slot], tm, tm) * g[:, 1:2])
    h = h_ref[...] + moe
    o_ref[...] = h * lax.rsqrt(jnp.mean(h * h, axis=-1, keepdims=True) + EPS) * fw_ref[...]


def _final(h, out, dest_tiles, rf, final_w):
    n = h.shape[0]
    tm = min(GATHER_TILE, n)
    row = lambda i, d: (i, 0)
    wide = pl.BlockSpec((tm, D_MODEL), row)
    return pl.pallas_call(
        functools.partial(_final_kernel, tm=tm),
        grid_spec=pltpu.PrefetchScalarGridSpec(
            num_scalar_prefetch=1,
            grid=(n // tm,),
            in_specs=[wide, pl.BlockSpec((tm, LANES), row), pl.BlockSpec((1, D_MODEL), lambda i, d: (0, 0)),
                      pl.BlockSpec(memory_space=pl.ANY)],
            out_specs=wide,
            scratch_shapes=[pltpu.VMEM((2, 2 * tm * ROW_CHUNKS, LANES), F32), pltpu.SemaphoreType.DMA((2,))]),
        out_shape=jax.ShapeDtypeStruct((n, D_MODEL), F32),
        compiler_params=_cparams(("arbitrary",), disable_bounds_checks=True),
        name="final",
    )(dest_tiles, h, rf, final_w.reshape(1, D_MODEL), out)


def _moe_and_final(groups, w_eg, w_eu, w_ed, final_w):
    group_counts = [g[4][0, :N_EXPERTS] for g in groups]
    counts = sum(group_counts)
    padded = ((counts + EXPERT_TILE - 1) // EXPERT_TILE) * EXPERT_TILE
    ends = jnp.cumsum(padded)
    starts = ends - padded
    n_assign = sum(2 * g[0].shape[0] for g in groups)
    n_tiles = n_assign // EXPERT_TILE + N_EXPERTS
    n_slots = n_tiles * EXPERT_TILE
    experts = jnp.arange(N_EXPERTS, dtype=I32)
    tile_start = jnp.arange(n_tiles, dtype=I32) * EXPERT_TILE
    tile_expert = jnp.minimum(jnp.sum((ends[None, :] <= tile_start[:, None]).astype(I32), axis=1), N_EXPERTS - 1)
    n_used = (ends[-1] // EXPERT_TILE).astype(I32).reshape(1)

    later = sum(g[0].shape[0] for g in groups[1:])
    assert later <= EXPERT_TILE, "fill counts must stay below 2 * EXPERT_TILE"
    n_used_tiles = ends[-1] // EXPERT_TILE
    pad_info = jnp.concatenate([starts + group_counts[0], padded - group_counts[0],
                                jnp.stack([n_used_tiles, n_tiles - n_used_tiles])]).astype(I32)
    xs = None
    dests = []
    base = starts
    for (h, hn, ri, rf, _), cnt in zip(groups, group_counts):
        n = h.shape[0]
        dest = jnp.sum(jnp.where(ri[:, 0:2, None] == experts, base, 0), axis=-1) + ri[:, 2:4]
        dests.append(dest.reshape(-1))
        xs = _dispatch(hn, dests[-1], pad_info, xs, n_slots, min(DISPATCH_TILE, n))
        base = base + cnt
    used = padded > 0
    run_index = jnp.cumsum(used.astype(I32)) - 1
    later_used = used[None, :] & (experts[None, :] > experts[:, None])
    next_used = jnp.min(jnp.where(later_used, experts[None, :], N_EXPERTS), axis=1)
    next_used = jnp.where(next_used < N_EXPERTS, next_used, -1).astype(I32)
    pick = lambda table: jnp.sum(jnp.where(tile_expert[:, None] == experts, table, 0), axis=-1).astype(I32)
    run_first = (tile_start == pick(starts)).astype(I32)
    out = _experts(xs, tile_expert, n_used, run_first, pick(run_index) % 2, pick(next_used), w_eg, w_eu, w_ed)
    return [_final(h, out, dest_tiles, rf, final_w) for (h, _, _, rf, _), dest_tiles in zip(groups, dests)]


def _layer(x, pos0, kv_bufs, conv_state, ssm_state, p):
    batch, seq, _ = x.shape
    n = batch * seq
    x2d = x.reshape(n, D_MODEL)
    prompt = kv_bufs is None
    q, k, v, z, xbc, g_a, g_s, dt_raw = _inproj(x2d, batch, seq, pos0, p["norm1_w"], p["w_in"], prompt)

    outs, lses, new_kv = [], [], []
    for gi, (window, dil) in enumerate(ATTN_GROUPS):
        if prompt:
            o, l = _attn_prompt(q[gi], k[gi], v[gi], dil)
            keep = min(window, seq) // dil
            tail = lambda t: t[:, :, seq // dil - keep:, :].transpose(0, 2, 1, 3).reshape(batch, keep * dil, GROUP_W)
            nbuf = jnp.stack([tail(k[gi]), tail(v[gi])], axis=2).reshape(
                batch, keep * dil, 2, HEADS_PER_GROUP, HEAD_DIM)
        else:
            o, l, nbuf = _attn_step(q, k, v, kv_bufs[gi], batch, seq, gi, dil)
        outs.append(o)
        lses.append(l)
        new_kv.append(nbuf)

    if conv_state is None:
        conv_state = jnp.zeros((batch, 3, SSD_CONV_DIM), F32)
        ssm_state = jnp.zeros((batch, SSD_HEADS, HEAD_DIM, SSD_STATE), F32)
    y_ssd, new_conv, new_ssm = _ssd(xbc, z, dt_raw, conv_state, ssm_state, batch, seq, p["conv_w"], p["conv_b"],
                                    p["dt_bias"], p["A_log"], p["D_skip"], p["ssd_norm_w"])

    h, hn, ri, rf, counts = _merge(x2d, outs, lses, y_ssd, g_a, g_s, p["w_attn_br"], p["w_ssd_br"], p["b_gate"],
                                   p["w_out"], p["norm2_w"], p["w_router"], p["b_router"], prompt)
    return (h, hn, ri, rf, counts), new_kv, new_conv, new_ssm


def _pack_w_in(w_in):
    return w_in.T.astype(BF16)


def kernel(x_prompt, x_sample, cache_kv_w128, cache_kv_w512, cache_kv_w2048, state_conv, state_ssm, norm1_w, w_in, w_attn_br, w_ssd_br, b_gate, w_out, conv_w, conv_b, dt_bias, A_log, D_skip, ssd_norm_w, norm2_w, w_router_coarse, b_router_coarse, w_router_fine, b_router_fine, w_expert_gate, w_expert_up, w_expert_down, final_norm_w):
    depth = norm1_w.shape[0]
    assert depth == 1, "the final norm is fused into the layer's last kernel"
    l = 0
    rpad = LANES - N_EXPERTS - N_GROUPS_E
    p = dict(
        norm1_w=norm1_w[l], w_in=_pack_w_in(w_in[l]),
        w_attn_br=w_attn_br[l].astype(BF16), w_ssd_br=w_ssd_br[l].astype(BF16), b_gate=b_gate[l],
        w_out=w_out[l].astype(BF16), conv_w=conv_w[l], conv_b=conv_b[l], dt_bias=dt_bias[l], A_log=A_log[l],
        D_skip=D_skip[l], ssd_norm_w=ssd_norm_w[l], norm2_w=norm2_w[l],
        w_router=jnp.pad(jnp.concatenate([w_router_fine[l], w_router_coarse[l]], axis=1),
                         ((0, 0), (0, rpad))).astype(BF16),
        b_router=jnp.pad(jnp.concatenate([b_router_fine[l], b_router_coarse[l]]), (0, rpad)).reshape(1, LANES),
        w_eg=w_expert_gate[l], w_eu=w_expert_up[l], w_ed=w_expert_down[l], final_norm_w=final_norm_w,
    )
    g_p, kv_p, c_p, st_p = _layer(x_prompt, 0, None, None, None, p)
    bufs = (cache_kv_w128[l], cache_kv_w512[l], cache_kv_w2048[l])
    g_s, kv_s, c_s, st_s = _layer(x_sample, PAST_LEN, bufs, state_conv[l], state_ssm[l], p)
    y_p, y_s = _moe_and_final([g_p, g_s], p["w_eg"], p["w_eu"], p["w_ed"], p["final_norm_w"])
    y_p = y_p.reshape(x_prompt.shape)
    y_s = y_s.reshape(x_sample.shape)
    lead = lambda t: t[None]
    return (y_p, y_s, lead(kv_p[0]), lead(kv_p[1]), lead(kv_p[2]), lead(c_p), lead(st_p),
            lead(kv_s[0]), lead(kv_s[1]), lead(kv_s[2]), lead(c_s), lead(st_s))
```

```python
import functools

import jax
import jax.numpy as jnp
from jax import lax
from jax.experimental import pallas as pl
from jax.experimental.pallas import tpu as pltpu

F32 = jnp.float32
BF16 = jnp.bfloat16
I32 = jnp.int32

D_MODEL = 1024
HEAD_DIM = 64
ATTN_GROUPS = ((128, 1), (512, 4), (2048, 16))
SPAN = 128
HEADS_PER_GROUP = 4
GROUP_W = HEADS_PER_GROUP * HEAD_DIM
ATTN_W = GROUP_W * len(ATTN_GROUPS)
ROPE_THETA = 10000.0
PAST_LEN = 8192
SSD_INNER = 1024
SSD_HEADS = 16
SSD_STATE = 128
SSD_CONV_DIM = 1536
SSD_CHUNK = 128
SSD_CHUNKS_PER_STEP = 4
N_GROUPS_E = 4
EXPERTS_PER_GROUP = 8
N_EXPERTS = 32
D_FF = 512
EPS = 1e-6
LANES = 128
SUBLANES = 8
NEG = -1e30

_OFF_Q, _OFF_K, _OFF_V, _OFF_Z, _OFF_XBC, _OFF_DT, _OFF_GA, _OFF_GS = 0, 768, 1536, 2304, 3328, 4864, 4880, 5904
_W_IN_COLS = 6928

ROW_TILE = 512
MERGE_SPLIT = 1
INPROJ_TILE = 512
EXPERT_TILE = 512
VMEM_LIMIT = 56 * 1024 * 1024


def _cparams(sem, **kw):
    return pltpu.CompilerParams(dimension_semantics=sem, vmem_limit_bytes=VMEM_LIMIT, **kw)


def _sigmoid(x):
    return 1.0 / (1.0 + jnp.exp(-x))


ROW_CHUNKS = D_MODEL // LANES


def _store_row_tiles(ref, val):
    rows = val.shape[0]
    for c in range(ROW_CHUNKS):
        ref[pl.ds(c, rows, stride=ROW_CHUNKS), :] = val[:, c * LANES:(c + 1) * LANES]


def _load_row_tiles(ref, first, rows):
    return jnp.concatenate(
        [ref[pl.ds(first * ROW_CHUNKS + c, rows, stride=ROW_CHUNKS), :] for c in range(ROW_CHUNKS)], axis=1)


def _nt_dot(a, b):
    return lax.dot_general(a, b, (((1,), (1,)), ((), ())), preferred_element_type=F32)


def _store_by_class(o_ref, scr_ref, val, dil):
    if dil == 1:
        o_ref[0, 0] = val
        return
    rows = val.shape[0]
    for half in range(GROUP_W // LANES):
        lanes = slice(half * LANES, (half + 1) * LANES)
        scr_ref[half] = val[:, lanes]
        for r in range(dil):
            o_ref[0, r, :, lanes] = scr_ref[half, pl.ds(r, rows // dil, stride=dil), :]


def _inproj_kernel(x_ref, nw_ref, w_ref, cos_ref, sin_ref, *refs, class_major):
    if class_major:
        qkv_refs, (z_ref, xbc_ref, ga_ref, gs_ref, dt_ref, scr_ref) = refs[:9], refs[9:]
    else:
        qkv_refs, (z_ref, xbc_ref, ga_ref, gs_ref, dt_ref) = refs[:3], refs[3:]
    x = x_ref[...]
    xn = x * lax.rsqrt(jnp.mean(x * x, axis=-1, keepdims=True) + EPS) * nw_ref[...]
    xb = xn.astype(BF16)
    cos = cos_ref[...]
    sin = sin_ref[...]
    lane = lax.broadcasted_iota(I32, cos.shape, 1)
    first_half = (lane % HEAD_DIM) < (HEAD_DIM // 2)

    def mm(lo, hi):
        return _nt_dot(xb, w_ref[lo:hi, :])

    def rope(uc):
        ur = jnp.where(first_half, pltpu.roll(uc, LANES - HEAD_DIM // 2, 1), pltpu.roll(uc, HEAD_DIM // 2, 1))
        return uc * cos + ur * sin

    def emit(which, base, roped):
        u = mm(base, base + ATTN_W)
        chunks = [u[:, c * LANES:(c + 1) * LANES] for c in range(ATTN_W // LANES)]
        if roped:
            chunks = [rope(uc) for uc in chunks]
        if class_major:
            for gi, (_, dil) in enumerate(ATTN_GROUPS):
                val = jnp.concatenate(chunks[2 * gi:2 * gi + 2], axis=1)
                _store_by_class(qkv_refs[3 * which + gi], scr_ref.at[3 * which + gi], val, dil)
        else:
            for c, uc in enumerate(chunks):
                qkv_refs[which][:, c * LANES:(c + 1) * LANES] = uc

    emit(0, _OFF_Q, True)
    emit(1, _OFF_K, True)
    emit(2, _OFF_V, False)
    z_ref[...] = mm(_OFF_Z, _OFF_XBC)
    xbc_ref[...] = mm(_OFF_XBC, _OFF_DT)
    ga_ref[...] = mm(_OFF_GA, _OFF_GS)
    gs_ref[...] = mm(_OFF_GS, _W_IN_COLS)
    dt_ref[...] = jnp.where(lane < SSD_HEADS, mm(_OFF_DT, _OFF_DT + LANES), 0.0)


def _rope_tables(pos):
    half = HEAD_DIM // 2
    inv_freq = ROPE_THETA ** (-jnp.arange(half, dtype=F32) * (2.0 / HEAD_DIM))
    ang = pos.astype(F32)[:, None] * inv_freq[None, :]
    ang = jnp.tile(ang, (1, LANES // half))
    lane = jnp.arange(LANES)
    sign = jnp.where((lane % HEAD_DIM) < half, -1.0, 1.0).astype(F32)
    return jnp.cos(ang), jnp.sin(ang) * sign[None, :]


def _inproj(x2d, batch, seq_len, pos0, norm_w, w_packed, class_major):
    n = x2d.shape[0]
    tm = min(INPROJ_TILE, n)
    cos, sin = _rope_tables(pos0 + jnp.arange(seq_len, dtype=I32))
    if seq_len < tm:
        cos = jnp.tile(cos, (tm // seq_len, 1))
        sin = jnp.tile(sin, (tm // seq_len, 1))
    tab_blocks = cos.shape[0] // tm
    row = lambda i: (i, 0)
    fixed = lambda i: (0, 0)
    tab = lambda i: (i % tab_blocks, 0)
    widths = (SSD_INNER, SSD_CONV_DIM, D_MODEL, D_MODEL, LANES)
    out_specs = [pl.BlockSpec((tm, w), row) for w in widths]
    out_shape = [jax.ShapeDtypeStruct((n, w), F32) for w in widths]
    scratch = []
    if class_major:
        assert seq_len % tm == 0
        per_seq = seq_len // tm
        cls = lambda i: (i // per_seq, 0, i % per_seq, 0)
        dils = [d for _, d in ATTN_GROUPS] * 3
        out_specs = [pl.BlockSpec((1, d, tm // d, GROUP_W), cls) for d in dils] + out_specs
        out_shape = [jax.ShapeDtypeStruct((batch, d, seq_len // d, GROUP_W), F32) for d in dils] + out_shape
        scratch = [pltpu.VMEM((len(dils), GROUP_W // LANES, tm, LANES), F32)]
    else:
        out_specs = [pl.BlockSpec((tm, ATTN_W), row)] * 3 + out_specs
        out_shape = [jax.ShapeDtypeStruct((n, ATTN_W), F32)] * 3 + out_shape
    res = pl.pallas_call(
        functools.partial(_inproj_kernel, class_major=class_major),
        grid=(n // tm,),
        in_specs=[pl.BlockSpec((tm, D_MODEL), row),
                  pl.BlockSpec((1, D_MODEL), fixed),
                  pl.BlockSpec((_W_IN_COLS, D_MODEL), fixed, pipeline_mode=pl.Buffered(1)),
                  pl.BlockSpec((tm, LANES), tab),
                  pl.BlockSpec((tm, LANES), tab)],
        out_specs=out_specs,
        out_shape=out_shape,
        scratch_shapes=scratch,
        compiler_params=_cparams(("parallel",)),
        name="inproj",
    )(x2d, norm_w.reshape(1, D_MODEL), w_packed, cos, sin)
    if class_major:
        return (tuple(res[0:3]), tuple(res[3:6]), tuple(res[6:9])) + tuple(res[9:])
    return tuple(res)


ATTN_QBLOCKS = 4


def _attn_prompt_kernel(q_ref, kp_ref, kc_ref, vp_ref, vc_ref, o_ref, l_ref, k_scr, v_scr, s_scr, p_scr, r_scr,
                        *, qblocks):
    c = pl.program_id(2)
    i = lax.broadcasted_iota(I32, (SPAN, 2 * SPAN), 0)
    j = lax.broadcasted_iota(I32, (SPAN, 2 * SPAN), 1)
    band = (j >= i) & (j <= i + SPAN)
    first_band = band & ((j >= SPAN) | (c > 0))
    k_scr[0:SPAN, :] = kp_ref[0, 0].astype(BF16)
    k_scr[SPAN:, :] = kc_ref[0, 0].astype(BF16)
    v_scr[0:SPAN, :] = vp_ref[0, 0].astype(BF16)
    v_scr[SPAN:, :] = vc_ref[0, 0].astype(BF16)
    units = [(b, h) for b in range(qblocks) for h in range(HEADS_PER_GROUP)]

    for u, (b, h) in enumerate(units):
        hs = slice(h * HEAD_DIM, (h + 1) * HEAD_DIM)
        qh = (q_ref[0, 0, b * SPAN:(b + 1) * SPAN, hs] * (HEAD_DIM ** -0.5)).astype(BF16)
        s_scr[u] = _nt_dot(qh, k_scr[b * SPAN:(b + 2) * SPAN, hs])

    for u, (b, h) in enumerate(units):
        hs = slice(h * HEAD_DIM, (h + 1) * HEAD_DIM)
        s = jnp.where(first_band if b == 0 else band, s_scr[u], NEG)
        m = jnp.max(jnp.maximum(s[:, :SPAN], s[:, SPAN:]), axis=-1, keepdims=True)
        p = jnp.exp(s - m)
        den = jnp.sum(p[:, :SPAN] + p[:, SPAN:], axis=-1, keepdims=True)
        p_scr[u] = p.astype(BF16)
        r_scr[u] = jnp.broadcast_to(1.0 / den, (SPAN, HEAD_DIM))
        l_ref[0, 0, b * SPAN:(b + 1) * SPAN, hs] = jnp.broadcast_to(m + jnp.log(den), (SPAN, HEAD_DIM))

    for u, (b, h) in enumerate(units):
        hs = slice(h * HEAD_DIM, (h + 1) * HEAD_DIM)
        acc = jnp.dot(p_scr[u], v_scr[b * SPAN:(b + 2) * SPAN, hs], preferred_element_type=F32)
        o_ref[0, 0, b * SPAN:(b + 1) * SPAN, hs] = acc * r_scr[u]


def _attn_prompt(q, k, v, dil):
    batch, _, n_cls, _ = q.shape
    qblocks = min(ATTN_QBLOCKS, n_cls // SPAN)
    tq = qblocks * SPAN
    assert n_cls % tq == 0
    cur = lambda b, r, c: (b, r, c, 0)
    prev = lambda b, r, c: (b, r, jnp.maximum(c * qblocks - 1, 0), 0)
    big = pl.BlockSpec((1, 1, tq, GROUP_W), cur)
    small = pl.BlockSpec((1, 1, SPAN, GROUP_W), prev)
    return pl.pallas_call(
        functools.partial(_attn_prompt_kernel, qblocks=qblocks),
        grid=(batch, dil, n_cls // tq),
        in_specs=[big, small, big, small, big],
        out_specs=[big, big],
        out_shape=[jax.ShapeDtypeStruct(q.shape, F32)] * 2,
        scratch_shapes=[pltpu.VMEM((SPAN + tq, GROUP_W), BF16),
                        pltpu.VMEM((SPAN + tq, GROUP_W), BF16),
                        pltpu.VMEM((qblocks * HEADS_PER_GROUP, SPAN, 2 * SPAN), F32),
                        pltpu.VMEM((qblocks * HEADS_PER_GROUP, SPAN, 2 * SPAN), BF16),
                        pltpu.VMEM((qblocks * HEADS_PER_GROUP, SPAN, HEAD_DIM), F32)],
        compiler_params=_cparams(("parallel", "parallel", "arbitrary")),
        name=f"attn_prompt_d{dil}",
    )(q, k, k, v, v)


STEP_ROWS_PER_CALL = 2048


def _attn_step_kernel(q_ref, kn_ref, vn_ref, buf_ref, o_ref, l_ref, nbuf_ref, *, buf_len, dil, n_new, seqs):
    t = lax.broadcasted_iota(I32, (n_new, buf_len), 0)
    i = lax.broadcasted_iota(I32, (n_new, buf_len), 1)
    delta = buf_len + t - i
    valid_b = delta <= dil * SPAN
    first_new = LANES - n_new
    tn = lax.broadcasted_iota(I32, (n_new, LANES), 0)
    un = lax.broadcasted_iota(I32, (n_new, LANES), 1) - first_new
    dn = tn - un
    valid_n = (dn >= 0) & (un >= 0)
    if dil > 1:
        valid_b = valid_b & ((delta & (dil - 1)) == 0)
        valid_n = valid_n & ((dn & (dil - 1)) == 0)

    def to_columns(x):
        xp = jnp.concatenate([x, jnp.zeros((LANES - n_new, GROUP_W), F32)], axis=0)
        xt = jnp.concatenate([xp[:, c * LANES:(c + 1) * LANES].T for c in range(GROUP_W // LANES)], axis=0)
        return pltpu.roll(xt, first_new, 1)

    is_new = lax.broadcasted_iota(I32, (HEAD_DIM, LANES), 1) >= first_new
    tail = slice(buf_len - LANES, buf_len)
    for b, h in [(b, h) for b in range(seqs) for h in range(HEADS_PER_GROUP)]:
        if h == 0:
            q = q_ref[b] * (HEAD_DIM ** -0.5)
            new_cols = (to_columns(kn_ref[b]), to_columns(vn_ref[b]))
        hs = slice(h * HEAD_DIM, (h + 1) * HEAD_DIM)
        qh = q[:, hs].astype(BF16)
        kt = buf_ref[b, 0, h]
        vt = buf_ref[b, 1, h]
        knt = new_cols[0][hs]
        vnt = new_cols[1][hs]
        sb = jnp.where(valid_b, jnp.dot(qh, kt.astype(BF16), preferred_element_type=F32), NEG)
        sn = jnp.where(valid_n, jnp.dot(qh, knt.astype(BF16), preferred_element_type=F32), NEG)
        m = jnp.maximum(jnp.max(sb, axis=-1, keepdims=True), jnp.max(sn, axis=-1, keepdims=True))
        pb = jnp.exp(sb - m)
        pn = jnp.exp(sn - m)
        den = jnp.sum(pb, axis=-1, keepdims=True) + jnp.sum(pn, axis=-1, keepdims=True)
        acc = _nt_dot(pb.astype(BF16), vt.astype(BF16)) + _nt_dot(pn.astype(BF16), vnt.astype(BF16))
        o_ref[b, :, hs] = acc / den
        l_ref[b, :, hs] = jnp.broadcast_to(m + jnp.log(den), (n_new, HEAD_DIM))
        for kv, (old, new) in enumerate(((kt, knt), (vt, vnt))):
            shifted = pltpu.roll(old, buf_len - n_new, 1)
            nbuf_ref[b, kv, h] = shifted
            nbuf_ref[b, kv, h, :, tail] = jnp.where(is_new, new, shifted[:, tail])


def _attn_step(q, k, v, buf, batch, n_new, gi, dil):
    buf_len = buf.shape[1]
    assert dil & (dil - 1) == 0 and buf_len >= dil * SPAN and n_new % SUBLANES == 0 and buf_len % LANES == 0
    view = lambda t: t.reshape(batch, n_new, ATTN_W)
    seqs = max(1, min(batch, STEP_ROWS_PER_CALL // buf_len))
    assert batch % seqs == 0
    tok = pl.BlockSpec((seqs, n_new, GROUP_W), lambda b: (b, 0, gi))
    full = pl.BlockSpec((seqs, 2, HEADS_PER_GROUP, HEAD_DIM, buf_len), lambda b: (b, 0, 0, 0, 0))
    osp = pl.BlockSpec((seqs, n_new, GROUP_W), lambda b: (b, 0, 0))
    o, l, nbuf = pl.pallas_call(
        functools.partial(_attn_step_kernel, buf_len=buf_len, dil=dil, n_new=n_new, seqs=seqs),
        grid=(batch // seqs,),
        in_specs=[tok, tok, tok, full],
        out_specs=[osp, osp, full],
        out_shape=[jax.ShapeDtypeStruct((batch, n_new, GROUP_W), F32)] * 2
        + [jax.ShapeDtypeStruct((batch, 2, HEADS_PER_GROUP, HEAD_DIM, buf_len), F32)],
        compiler_params=_cparams(("parallel",)),
        name=f"attn_step_d{dil}",
    )(view(q), view(k), view(v), jnp.transpose(buf, (0, 2, 3, 4, 1)))
    return (o.reshape(batch * n_new, GROUP_W), l.reshape(batch * n_new, GROUP_W),
            jnp.transpose(nbuf, (0, 4, 1, 2, 3)))


def _split3(a):
    a1 = a.astype(BF16)
    r1 = a - a1.astype(F32)
    a2 = r1.astype(BF16)
    a3 = (r1 - a2.astype(F32)).astype(BF16)
    return a1, a2, a3


def _ssd_kernel(*refs, n_valid, n_steps, per_step):
    for sub in range(per_step):
        _ssd_chunk(*refs, n_valid=n_valid, n_steps=n_steps, per_step=per_step, sub=sub)


def _ssd_chunk(xbc_ref, z_ref, dt_ref, cst_ref, h0_ref, cw_ref, cb_ref, dtb_ref, alog_ref, dfull_ref, nw_ref,
               tri_ref, expand_ref, y_ref, cout_ref, hout_ref, xpad_ref, h_ref, *, n_valid, n_steps, per_step, sub):
    c = pl.program_id(1)
    lc = SSD_CHUNK
    pad = SUBLANES
    n_slabs = SSD_CONV_DIM // LANES
    tok = slice(sub * n_valid, (sub + 1) * n_valid)

    def carry_rows():
        for j in range(n_slabs):
            xpad_ref[j, 0:pad, :] = xpad_ref[j, lc:lc + pad, :]

    if sub == 0:
        @pl.when(c == 0)
        def _():
            for j in range(n_slabs):
                xpad_ref[j, 0:pad, :] = cst_ref[0, :, j * LANES:(j + 1) * LANES]
            h_ref[...] = h0_ref[0]

        pl.when(c > 0)(carry_rows)
    else:
        carry_rows()

    if n_valid == lc:
        z = z_ref[tok, :]
        dtr = dt_ref[tok, :]
    else:
        fill = lambda w: jnp.zeros((lc - n_valid, w), F32)
        z = jnp.concatenate([z_ref[tok, :], fill(SSD_INNER)], axis=0)
        dtr = jnp.concatenate([dt_ref[tok, :], fill(LANES)], axis=0)

    slabs = []
    for j in range(n_slabs):
        cols = slice(j * LANES, (j + 1) * LANES)
        xpad_ref[j, pad:pad + n_valid, :] = xbc_ref[tok, cols]
        if n_valid < lc:
            xpad_ref[j, pad + n_valid:pad + lc, :] = jnp.zeros((lc - n_valid, LANES), F32)
        xc = cb_ref[:, cols]
        for tap in range(4):
            xc = xc + xpad_ref[j, pl.ds(pad - 3 + tap, lc), :] * cw_ref[tap:tap + 1, cols]
        slabs.append(xc * _sigmoid(xc))
        cout_ref[0, :, cols] = xpad_ref[j, pl.ds(pad + n_valid - 3, 3), :]
    xs = jnp.concatenate(slabs[:SSD_INNER // LANES], axis=1)
    bm = slabs[SSD_INNER // LANES:SSD_INNER // LANES + 2]
    cm = slabs[SSD_INNER // LANES + 2:]

    row = lax.broadcasted_iota(I32, (lc, lc), 0)
    col = lax.broadcasted_iota(I32, (lc, lc), 1)
    causal = row >= col
    dtv = dtr + dtb_ref[...]
    dt = jnp.maximum(dtv, 0.0) + jnp.log1p(jnp.exp(-jnp.abs(dtv)))
    if n_valid < lc:
        dt = jnp.where(row < n_valid, dt, 0.0)
    a = dt * (-jnp.exp(alog_ref[...]))
    tri = tri_ref[...]
    a_cs = sum(jnp.dot(tri, p, preferred_element_type=F32) for p in _split3(a))
    a_cs_t = a_cs.T
    expand = expand_ref[...]
    a_full = sum(jnp.dot(p, expand, preferred_element_type=F32) for p in _split3(a_cs))
    dt_full = sum(jnp.dot(p, expand, preferred_element_type=F32) for p in _split3(dt))
    xdt = xs * dt_full
    xd = xdt * jnp.exp(a_full[lc - 1:lc, :] - a_full)
    grow = jnp.exp(a_full)
    xdt_b = xdt.astype(BF16)
    xd_t = jnp.concatenate([xd[:, k * LANES:(k + 1) * LANES].T for k in range(SSD_INNER // LANES)],
                           axis=0).astype(BF16)
    lane = lax.broadcasted_iota(I32, (lc, LANES), 1)
    low_half = lane < HEAD_DIM
    zero_b = jnp.zeros((lc, LANES), BF16)
    half = SSD_INNER // 2

    y_parts = []
    for g in range(2):
        bg = bm[g].astype(BF16)
        cg = cm[g].astype(BF16)
        cb = jnp.where(causal, _nt_dot(cg, bg), 0.0)
        h_grp = h_ref[g * half:(g + 1) * half, :]
        y_off = _nt_dot(cg, h_grp.astype(BF16)) * grow[:, g * half:(g + 1) * half]
        for pair in range(4):
            e0 = g * 8 + pair * 2
            rows = slice(e0 * HEAD_DIM, (e0 + 2) * HEAD_DIM)
            ms, keep = [], []
            for k in range(2):
                e = e0 + k
                seg = a_cs[:, e:e + 1] - a_cs_t[e:e + 1, :]
                ms.append((cb * jnp.exp(jnp.minimum(seg, 0.0))).astype(BF16))
                keep.append(jnp.broadcast_to(jnp.exp(a_cs[lc - 1:lc, e:e + 1]), (HEAD_DIM, SSD_STATE)))
            pair_b = xdt_b[:, rows]
            rhs = jnp.concatenate([jnp.where(low_half, pair_b, zero_b), jnp.where(low_half, zero_b, pair_b)], axis=0)
            y_parts.append(jnp.dot(jnp.concatenate(ms, axis=1), rhs, preferred_element_type=F32)
                           + y_off[:, pair * LANES:(pair + 1) * LANES])
            st = jnp.dot(xd_t[rows, :], bg, preferred_element_type=F32)
            h_ref[rows, :] = h_grp[pair * LANES:(pair + 1) * LANES, :] * jnp.concatenate(keep, axis=0) + st

    y = jnp.concatenate(y_parts, axis=1) + xs * dfull_ref[...]
    gate = y * (z * _sigmoid(z))
    for g in range(2):
        gg = gate[:, g * half:(g + 1) * half]
        gg = gg * lax.rsqrt(jnp.mean(gg * gg, axis=-1, keepdims=True) + EPS)
        y_ref[tok, g * half:(g + 1) * half] = (gg * nw_ref[:, g * half:(g + 1) * half])[0:n_valid]

    if sub == per_step - 1:
        @pl.when(c == n_steps - 1)
        def _():
            hout_ref[0] = h_ref[...]


def _ssd(xbc, z, dt_raw, conv_state, ssm_state, batch, seq, conv_w, conv_b, dt_bias, a_log, d_skip, norm_w):
    n_valid = min(seq, SSD_CHUNK)
    n_chunks = seq // n_valid
    assert n_valid % SUBLANES == 0 and seq % n_valid == 0
    padl = lambda t: jnp.pad(t.reshape(1, SSD_HEADS), ((0, 0), (0, LANES - SSD_HEADS)))
    cst = jnp.pad(conv_state, ((0, 0), (SUBLANES - 3, 0), (0, 0)))
    tri = (jnp.arange(SSD_CHUNK)[:, None] >= jnp.arange(SSD_CHUNK)[None, :]).astype(BF16)
    expand = (jnp.arange(LANES)[:, None] == jnp.arange(SSD_INNER)[None, :] // HEAD_DIM).astype(BF16)
    per_step = SSD_CHUNKS_PER_STEP if (n_valid == SSD_CHUNK and n_chunks % SSD_CHUNKS_PER_STEP == 0) else 1
    n_steps = n_chunks // per_step
    step_rows = per_step * n_valid
    tokrow = lambda b, c: (b * n_steps + c, 0)
    fixed = lambda b, c: (0, 0)
    per_b3 = lambda b, c: (b, 0, 0)
    state = pl.BlockSpec((1, SSD_INNER, SSD_STATE), per_b3)
    kern = functools.partial(_ssd_kernel, n_valid=n_valid, n_steps=n_steps, per_step=per_step)
    y, cout, hout = pl.pallas_call(
        kern,
        grid=(batch, n_steps),
        in_specs=[pl.BlockSpec((step_rows, SSD_CONV_DIM), tokrow),
                  pl.BlockSpec((step_rows, SSD_INNER), tokrow),
                  pl.BlockSpec((step_rows, LANES), tokrow),
                  pl.BlockSpec((1, SUBLANES, SSD_CONV_DIM), per_b3),
                  state,
                  pl.BlockSpec((4, SSD_CONV_DIM), fixed),
                  pl.BlockSpec((1, SSD_CONV_DIM), fixed),
                  pl.BlockSpec((1, LANES), fixed),
                  pl.BlockSpec((1, LANES), fixed),
                  pl.BlockSpec((1, SSD_INNER), fixed),
                  pl.BlockSpec((1, SSD_INNER), fixed),
                  pl.BlockSpec((SSD_CHUNK, SSD_CHUNK), fixed),
                  pl.BlockSpec((LANES, SSD_INNER), fixed)],
        out_specs=[pl.BlockSpec((step_rows, SSD_INNER), tokrow),
                   pl.BlockSpec((1, 3, SSD_CONV_DIM), per_b3),
                   state],
        out_shape=[jax.ShapeDtypeStruct((batch * seq, SSD_INNER), F32),
                   jax.ShapeDtypeStruct((batch, 3, SSD_CONV_DIM), F32),
                   jax.ShapeDtypeStruct((batch, SSD_INNER, SSD_STATE), F32)],
        scratch_shapes=[pltpu.VMEM((SSD_CONV_DIM // LANES, SSD_CHUNK + SUBLANES, LANES), F32),
                        pltpu.VMEM((SSD_INNER, SSD_STATE), F32)],
        compiler_params=_cparams(("parallel", "arbitrary")),
        name="ssd",
    )(xbc, z, dt_raw, cst, ssm_state.reshape(batch, SSD_INNER, SSD_STATE), conv_w, conv_b.reshape(1, SSD_CONV_DIM),
      padl(dt_bias), padl(a_log), jnp.repeat(d_skip, HEAD_DIM).reshape(1, SSD_INNER), norm_w.reshape(1, SSD_INNER),
      tri, expand)
    return y, cout, hout.reshape(batch, SSD_HEADS, HEAD_DIM, SSD_STATE)


def _merge_kernel(x_ref, o0_ref, o1_ref, o2_ref, l0_ref, l1_ref, l2_ref, ys_ref, ga_ref, gs_ref,
                  wab_ref, wsb_ref, bg_ref, wo_ref, n2_ref, wr_ref, br_ref,
                  h_ref, hn_ref, ri_ref, rf_ref, cnt_ref, carry_ref, *scr, class_major):
    step = pl.program_id(0)

    @pl.when(step == 0)
    def _():
        carry_ref[...] = jnp.zeros_like(carry_ref)

    def load(ref, gi, slab):
        if not class_major:
            return ref[...]
        dil = ATTN_GROUPS[gi][1]
        if dil == 1:
            return ref[0, 0]
        rows = ref.shape[1] * ref.shape[2]
        halves = []
        for half in range(GROUP_W // LANES):
            for r in range(dil):
                scr[0][slab, half, pl.ds(r, rows // dil, stride=dil), :] = ref[0, r, :, half * LANES:(half + 1) * LANES]
            halves.append(scr[0][slab, half])
        return jnp.concatenate(halves, axis=1)

    lses = (load(l0_ref, 0, 0), load(l1_ref, 1, 0), load(l2_ref, 2, 1))
    outs = (load(o0_ref, 0, 0), load(o1_ref, 1, 2), load(o2_ref, 2, 3))
    tm = x_ref.shape[0]
    sub = tm // MERGE_SPLIT
    lane = lax.broadcasted_iota(I32, (sub, LANES), 1)
    big = jnp.int32(LANES)
    r = lax.broadcasted_iota(I32, (sub, sub), 0)
    s = lax.broadcasted_iota(I32, (sub, sub), 1)
    earlier = (r > s).astype(BF16)

    def top(vals):
        v = jnp.max(vals, axis=-1, keepdims=True)
        idx = jnp.min(jnp.where(vals == v, lane, big), axis=-1, keepdims=True)
        return v, idx

    carry = carry_ref[0:1, :]
    for part in range(MERGE_SPLIT):
        rows = slice(part * sub, (part + 1) * sub)
        l0, l1, l2 = (t[rows] for t in lses)
        m = jnp.maximum(jnp.maximum(l0, l1), l2)
        w0, w1, w2 = jnp.exp(l0 - m), jnp.exp(l1 - m), jnp.exp(l2 - m)
        y_attn = (w0 * outs[0][rows] + w1 * outs[1][rows] + w2 * outs[2][rows]) / (w0 + w1 + w2)
        pa = jnp.dot(y_attn.astype(BF16), wab_ref[...], preferred_element_type=F32)
        ps = jnp.dot(ys_ref[rows, :].astype(BF16), wsb_ref[...], preferred_element_type=F32)
        merged = (_sigmoid(ga_ref[rows, :] + bg_ref[0:1, :]) * pa + _sigmoid(gs_ref[rows, :] + bg_ref[1:2, :]) * ps)
        h = x_ref[rows, :] + jnp.dot(merged.astype(BF16), wo_ref[...], preferred_element_type=F32)
        h_ref[rows, :] = h
        hn = h * lax.rsqrt(jnp.mean(h * h, axis=-1, keepdims=True) + EPS) * n2_ref[...]
        hnb = hn.astype(BF16)
        _store_row_tiles(hn_ref.at[pl.ds(part * sub * ROW_CHUNKS, sub * ROW_CHUNKS), :], hn)

        logits = jnp.dot(hnb, wr_ref[...], preferred_element_type=F32) + br_ref[...]
        is_coarse = (lane >= N_EXPERTS) & (lane < N_EXPERTS + N_GROUPS_E)
        lc = jnp.where(is_coarse, logits, NEG)
        mc, ic = top(lc)
        p_grp = 1.0 / jnp.sum(jnp.exp(lc - mc), axis=-1, keepdims=True)
        lo = (ic - N_EXPERTS) * EXPERTS_PER_GROUP
        lf = jnp.where((lane >= lo) & (lane < lo + EXPERTS_PER_GROUP), logits, NEG)
        v1, i1 = top(lf)
        v2, i2 = top(jnp.where(lane == i1, NEG, lf))
        e2 = jnp.exp(v2 - v1)
        g1 = p_grp / (1.0 + e2)
        g2 = p_grp * e2 / (1.0 + e2)

        oh1 = lane == i1
        oh2 = lane == i2
        cnt = oh1.astype(F32) + oh2.astype(F32)
        before = jnp.dot(earlier, cnt.astype(BF16), preferred_element_type=F32) + carry
        r1 = jnp.sum(jnp.where(oh1, before, 0.0), axis=-1, keepdims=True)
        r2 = jnp.sum(jnp.where(oh2, before, 0.0), axis=-1, keepdims=True)
        carry = carry + jnp.sum(cnt, axis=0, keepdims=True)

        ri = jnp.where(lane == 0, i1, jnp.where(lane == 1, i2, 0))
        ri = jnp.where(lane == 2, r1.astype(I32), jnp.where(lane == 3, r2.astype(I32), ri))
        ri_ref[rows, :] = ri
        rf_ref[rows, :] = jnp.where(lane == 0, g1, jnp.where(lane == 1, g2, 0.0))

    carry_ref[...] = jnp.broadcast_to(carry, carry_ref.shape)
    cnt_ref[...] = jnp.broadcast_to(carry, cnt_ref.shape).astype(I32)


def _merge(x2d, outs, lses, y_ssd, g_a, g_s, wab, wsb, b_gate, wo, norm2_w, w_router, b_router, class_major):
    n = x2d.shape[0]
    tm = min(ROW_TILE, n)
    row = lambda i: (i, 0)
    fixed = lambda i: (0, 0)
    wide = pl.BlockSpec((tm, D_MODEL), row)
    info = pl.BlockSpec((tm, LANES), row)
    if class_major:
        per_seq = outs[0].shape[2] // tm
        cls = lambda i: (i // per_seq, 0, i % per_seq, 0)
        grps = [pl.BlockSpec((1, d, tm // d, GROUP_W), cls) for _, d in ATTN_GROUPS]
        scratch = [pltpu.VMEM((4, GROUP_W // LANES, tm, LANES), F32)]
    else:
        grps = [pl.BlockSpec((tm, GROUP_W), row)] * 3
        scratch = []
    return pl.pallas_call(
        functools.partial(_merge_kernel, class_major=class_major),
        grid=(n // tm,),
        in_specs=[wide, *grps, *grps, wide, wide, wide,
                  pl.BlockSpec((GROUP_W, D_MODEL), fixed),
                  pl.BlockSpec((SSD_INNER, D_MODEL), fixed),
                  pl.BlockSpec((2, D_MODEL), fixed),
                  pl.BlockSpec((D_MODEL, D_MODEL), fixed),
                  pl.BlockSpec((1, D_MODEL), fixed),
                  pl.BlockSpec((D_MODEL, LANES), fixed),
                  pl.BlockSpec((1, LANES), fixed)],
        out_specs=[wide, pl.BlockSpec((tm * ROW_CHUNKS, LANES), row), info, info,
                   pl.BlockSpec((SUBLANES, LANES), fixed)],
        out_shape=[jax.ShapeDtypeStruct((n, D_MODEL), F32),
                   jax.ShapeDtypeStruct((n * ROW_CHUNKS, LANES), F32),
                   jax.ShapeDtypeStruct((n, LANES), I32),
                   jax.ShapeDtypeStruct((n, LANES), F32),
                   jax.ShapeDtypeStruct((SUBLANES, LANES), I32)],
        scratch_shapes=[pltpu.VMEM((SUBLANES, LANES), F32)] + scratch,
        compiler_params=_cparams(("arbitrary",)),
        name="merge_out",
    )(x2d, *outs, *lses, y_ssd, g_a, g_s, wab, wsb, b_gate, wo, norm2_w.reshape(1, D_MODEL), w_router, b_router)


GATHER_TILE = 512
DISPATCH_TILE = 512
INDEX_BATCH = 16


def _dispatch_kernel(dest_ref, pad_ref, hn_ref, *rest, tm, first_group):
    if first_group:
        xs_hbm, zeros_ref, sem, pad_sem = rest
    else:
        _, xs_hbm, sem = rest
    i = pl.program_id(0)

    if first_group:
        @pl.when(i == 0)
        def _():
            zeros_ref[...] = jnp.zeros_like(zeros_ref)

            def fill(first_slot, size):
                first = pl.multiple_of(first_slot * ROW_CHUNKS, ROW_CHUNKS)
                return pltpu.make_async_copy(zeros_ref.at[pl.ds(0, size * ROW_CHUNKS), :],
                                             xs_hbm.at[pl.ds(first, size * ROW_CHUNKS), :], pad_sem)

            fills = []
            for e in range(N_EXPERTS):
                start, count = pad_ref[e], pad_ref[N_EXPERTS + e]
                size = EXPERT_TILE
                while size >= 1:
                    fills.append(((count & size) != 0, start + (count & ~(2 * size - 1)), size))
                    size //= 2
            tail_first, tail_tiles = pad_ref[2 * N_EXPERTS], pad_ref[2 * N_EXPERTS + 1]
            for t in range(N_EXPERTS):
                fills.append((t < tail_tiles, (tail_first + t) * EXPERT_TILE, EXPERT_TILE))
            for pred, first_slot, size in fills:
                pl.when(pred)(lambda first_slot=first_slot, size=size: fill(first_slot, size).start())
            for pred, first_slot, size in fills:
                pl.when(pred)(lambda first_slot=first_slot, size=size: fill(first_slot, size).wait())

    for j0 in range(0, 2 * tm, INDEX_BATCH):
        slots = [dest_ref[i * 2 * tm + j0 + u] for u in range(INDEX_BATCH)]
        for u, dst in enumerate(slots):
            tok = (j0 + u) // 2
            first = pl.multiple_of(dst * ROW_CHUNKS, ROW_CHUNKS)
            pltpu.make_async_copy(hn_ref.at[pl.ds(tok * ROW_CHUNKS, ROW_CHUNKS), :],
                                  xs_hbm.at[pl.ds(first, ROW_CHUNKS), :], sem).start(priority=u % 2)
    for _ in range(2):
        pltpu.make_async_copy(hn_ref, xs_hbm.at[pl.ds(0, tm * ROW_CHUNKS), :], sem).wait()


def _dispatch(hn, dest_tiles, pad_info, xs, n_slots, tm):
    n = hn.shape[0] // ROW_CHUNKS
    first_group = xs is None
    anyspec = pl.BlockSpec(memory_space=pl.ANY)
    scratch = [pltpu.SemaphoreType.DMA(())]
    if first_group:
        scratch = [pltpu.VMEM((EXPERT_TILE * ROW_CHUNKS, LANES), F32)] + scratch + [pltpu.SemaphoreType.DMA(())]
    return pl.pallas_call(
        functools.partial(_dispatch_kernel, tm=tm, first_group=first_group),
        grid_spec=pltpu.PrefetchScalarGridSpec(
            num_scalar_prefetch=2,
            grid=(n // tm,),
            in_specs=[pl.BlockSpec((tm * ROW_CHUNKS, LANES), lambda i, d, p: (i, 0))] + ([] if first_group else [anyspec]),
            out_specs=anyspec,
            scratch_shapes=scratch),
        out_shape=jax.ShapeDtypeStruct((n_slots * ROW_CHUNKS, LANES), F32),
        input_output_aliases={} if first_group else {3: 0},
        compiler_params=_cparams(("arbitrary",), disable_bounds_checks=True, has_side_effects=True),
        name="dispatch",
    )(dest_tiles, pad_info, hn, *([] if first_group else [xs]))


def _expert_kernel(te_ref, nu_ref, first_ref, slot_ref, next_ref, x_ref, wg_hbm, wu_hbm, wd_hbm, o_ref,
                   wg32, wu32, wd32, wgb, wub, wdb, sem):
    i = pl.program_id(0)

    def fetch(expert, slot):
        return [pltpu.make_async_copy(src.at[expert], dst.at[slot], sem.at[slot])
                for src, dst in ((wg_hbm, wg32), (wu_hbm, wu32), (wd_hbm, wd32))]

    @pl.when((i < nu_ref[0]) & (first_ref[i] == 1))
    def _():
        slot = slot_ref[i]

        @pl.when(i == 0)
        def _():
            for c in fetch(te_ref[0], slot):
                c.start()

        for c in fetch(te_ref[i], slot):
            c.wait()
        wgb[...] = wg32[slot].astype(BF16)
        wub[...] = wu32[slot].astype(BF16)
        wdb[...] = wd32[slot].astype(BF16)

        @pl.when(next_ref[i] >= 0)
        def _():
            for c in fetch(next_ref[i], 1 - slot):
                c.start()

    @pl.when(i < nu_ref[0])
    def _():
        x = _load_row_tiles(x_ref, 0, EXPERT_TILE).astype(BF16)
        hg = jnp.dot(x, wgb[...], preferred_element_type=F32)
        hu = jnp.dot(x, wub[...], preferred_element_type=F32)
        hb = (hg * _sigmoid(hg)) * hu
        _store_row_tiles(o_ref, jnp.dot(hb.astype(BF16), wdb[...], preferred_element_type=F32))

    @pl.when(i >= nu_ref[0])
    def _():
        o_ref[...] = jnp.zeros_like(o_ref)


def _experts(xs, tile_expert, n_used, run_first, run_slot, run_next, w_eg, w_eu, w_ed):
    n_slots = xs.shape[0] // ROW_CHUNKS
    n_tiles = n_slots // EXPERT_TILE
    row = lambda i, *_: (i, 0)
    used_row = lambda i, te, nu, *_: (jnp.minimum(i, nu[0] - 1), 0)
    anyspec = pl.BlockSpec(memory_space=pl.ANY)
    return pl.pallas_call(
        _expert_kernel,
        grid_spec=pltpu.PrefetchScalarGridSpec(
            num_scalar_prefetch=5,
            grid=(n_tiles,),
            in_specs=[pl.BlockSpec((EXPERT_TILE * ROW_CHUNKS, LANES), used_row), anyspec, anyspec, anyspec],
            out_specs=pl.BlockSpec((EXPERT_TILE * ROW_CHUNKS, LANES), row),
            scratch_shapes=[pltpu.VMEM((2, D_MODEL, D_FF), F32),
                            pltpu.VMEM((2, D_MODEL, D_FF), F32),
                            pltpu.VMEM((2, D_FF, D_MODEL), F32),
                            pltpu.VMEM((D_MODEL, D_FF), BF16),
                            pltpu.VMEM((D_MODEL, D_FF), BF16),
                            pltpu.VMEM((D_FF, D_MODEL), BF16),
                            pltpu.SemaphoreType.DMA((2,))]),
        out_shape=jax.ShapeDtypeStruct((n_slots * ROW_CHUNKS, LANES), F32),
        compiler_params=_cparams(("arbitrary",)),
        name="experts",
    )(tile_expert, n_used, run_first, run_slot, run_next, xs, w_eg, w_eu, w_ed)


FINAL_BLOCKS = 8


def _final_kernel(dest_ref, h_ref, rf_ref, fw_ref, out_hbm, o_ref, ybuf_a, ybuf_b, sem, *, tm):
    i = pl.program_id(0)
    last = pl.num_programs(0) - 1
    bufs = (ybuf_a, ybuf_b)
    nblk = FINAL_BLOCKS if tm % (FINAL_BLOCKS * SUBLANES) == 0 else 1
    tb = tm // nblk

    def gather(tile, into, lo, hi):
        for j0 in range(lo, hi, INDEX_BATCH):
            slots = [dest_ref[tile * 2 * tm + j0 + u] for u in range(INDEX_BATCH)]
            for u, src in enumerate(slots):
                first = pl.multiple_of(src * ROW_CHUNKS, ROW_CHUNKS)
                row = ((j0 + u) % 2) * tm + (j0 + u) // 2
                pltpu.make_async_copy(out_hbm.at[pl.ds(first, ROW_CHUNKS), :],
                                      bufs[into].at[pl.ds(row * ROW_CHUNKS, ROW_CHUNKS), :],
                                      sem.at[into]).start(priority=u % 2)

    def wait_rows(which):
        pltpu.make_async_copy(out_hbm.at[pl.ds(0, 2 * tm * ROW_CHUNKS), :], bufs[which], sem.at[which]).wait()

    @pl.when(i == 0)
    def _():
        gather(0, 0, 0, 2 * tm)

    def step(cur):
        wait_rows(cur)
        nxt = jnp.minimum(i + 1, last)
        for blk in range(nblk):
            gather(nxt, 1 - cur, 2 * blk * tb, 2 * (blk + 1) * tb)
            rows = slice(blk * tb, (blk + 1) * tb)
            g = rf_ref[rows, :]
            moe = (_load_row_tiles(bufs[cur], blk * tb, tb) * g[:, 0:1]
                   + _load_row_tiles(bufs[cur], tm + blk * tb, tb) * g[:, 1:2])
            h = h_ref[rows, :] + moe
            o_ref[rows, :] = h * lax.rsqrt(jnp.mean(h * h, axis=-1, keepdims=True) + EPS) * fw_ref[...]

        @pl.when(i == last)
        def _():
            wait_rows(1 - cur)

    for cur in range(2):
        pl.when(i % 2 == cur)(functools.partial(step, cur))


def _final(h, out, dest_tiles, rf, final_w):
    n = h.shape[0]
    tm = min(GATHER_TILE, n)
    row = lambda i, d: (i, 0)
    wide = pl.BlockSpec((tm, D_MODEL), row)
    return pl.pallas_call(
        functools.partial(_final_kernel, tm=tm),
        grid_spec=pltpu.PrefetchScalarGridSpec(
            num_scalar_prefetch=1,
            grid=(n // tm,),
            in_specs=[wide, pl.BlockSpec((tm, LANES), row), pl.BlockSpec((1, D_MODEL), lambda i, d: (0, 0)),
                      pl.BlockSpec(memory_space=pl.ANY)],
            out_specs=wide,
            scratch_shapes=[pltpu.VMEM((2 * tm * ROW_CHUNKS, LANES), F32), pltpu.VMEM((2 * tm * ROW_CHUNKS, LANES), F32),
                            pltpu.SemaphoreType.DMA((2,))]),
        out_shape=jax.ShapeDtypeStruct((n, D_MODEL), F32),
        compiler_params=_cparams(("arbitrary",), disable_bounds_checks=True),
        name="final",
    )(dest_tiles, h, rf, final_w.reshape(1, D_MODEL), out)


def _moe_and_final(groups, w_eg, w_eu, w_ed, final_w):
    group_counts = [g[4][0, :N_EXPERTS] for g in groups]
    counts = sum(group_counts)
    padded = ((counts + EXPERT_TILE - 1) // EXPERT_TILE) * EXPERT_TILE
    ends = jnp.cumsum(padded)
    starts = ends - padded
    n_assign = sum(2 * g[0].shape[0] for g in groups)
    n_tiles = n_assign // EXPERT_TILE + N_EXPERTS
    n_slots = n_tiles * EXPERT_TILE
    experts = jnp.arange(N_EXPERTS, dtype=I32)
    tile_start = jnp.arange(n_tiles, dtype=I32) * EXPERT_TILE
    tile_expert = jnp.minimum(jnp.sum((ends[None, :] <= tile_start[:, None]).astype(I32), axis=1), N_EXPERTS - 1)
    n_used = (ends[-1] // EXPERT_TILE).astype(I32).reshape(1)

    later = sum(g[0].shape[0] for g in groups[1:])
    assert later <= EXPERT_TILE, "fill counts must stay below 2 * EXPERT_TILE"
    n_used_tiles = ends[-1] // EXPERT_TILE
    pad_info = jnp.concatenate([starts + group_counts[0], padded - group_counts[0],
                                jnp.stack([n_used_tiles, n_tiles - n_used_tiles])]).astype(I32)
    xs = None
    dests = []
    base = starts
    for (h, hn, ri, rf, _), cnt in zip(groups, group_counts):
        n = h.shape[0]
        dest = jnp.sum(jnp.where(ri[:, 0:2, None] == experts, base, 0), axis=-1) + ri[:, 2:4]
        dests.append(dest.reshape(-1))
        xs = _dispatch(hn, dests[-1], pad_info, xs, n_slots, min(DISPATCH_TILE, n))
        base = base + cnt
    used = padded > 0
    run_index = jnp.cumsum(used.astype(I32)) - 1
    later_used = used[None, :] & (experts[None, :] > experts[:, None])
    next_used = jnp.min(jnp.where(later_used, experts[None, :], N_EXPERTS), axis=1)
    next_used = jnp.where(next_used < N_EXPERTS, next_used, -1).astype(I32)
    pick = lambda table: jnp.sum(jnp.where(tile_expert[:, None] == experts, table, 0), axis=-1).astype(I32)
    run_first = (tile_start == pick(starts)).astype(I32)
    out = _experts(xs, tile_expert, n_used, run_first, pick(run_index) % 2, pick(next_used), w_eg, w_eu, w_ed)
    return [_final(h, out, dest_tiles, rf, final_w) for (h, _, _, rf, _), dest_tiles in zip(groups, dests)]


def _layer(x, pos0, kv_bufs, conv_state, ssm_state, p):
    batch, seq, _ = x.shape
    n = batch * seq
    x2d = x.reshape(n, D_MODEL)
    prompt = kv_bufs is None
    q, k, v, z, xbc, g_a, g_s, dt_raw = _inproj(x2d, batch, seq, pos0, p["norm1_w"], p["w_in"], prompt)

    outs, lses, new_kv = [], [], []
    for gi, (window, dil) in enumerate(ATTN_GROUPS):
        if prompt:
            o, l = _attn_prompt(q[gi], k[gi], v[gi], dil)
            keep = min(window, seq) // dil
            tail = lambda t: t[:, :, seq // dil - keep:, :].transpose(0, 2, 1, 3).reshape(batch, keep * dil, GROUP_W)
            nbuf = jnp.stack([tail(k[gi]), tail(v[gi])], axis=2).reshape(
                batch, keep * dil, 2, HEADS_PER_GROUP, HEAD_DIM)
        else:
            o, l, nbuf = _attn_step(q, k, v, kv_bufs[gi], batch, seq, gi, dil)
        outs.append(o)
        lses.append(l)
        new_kv.append(nbuf)

    if conv_state is None:
        conv_state = jnp.zeros((batch, 3, SSD_CONV_DIM), F32)
        ssm_state = jnp.zeros((batch, SSD_HEADS, HEAD_DIM, SSD_STATE), F32)
    y_ssd, new_conv, new_ssm = _ssd(xbc, z, dt_raw, conv_state, ssm_state, batch, seq, p["conv_w"], p["conv_b"],
                                    p["dt_bias"], p["A_log"], p["D_skip"], p["ssd_norm_w"])

    h, hn, ri, rf, counts = _merge(x2d, outs, lses, y_ssd, g_a, g_s, p["w_attn_br"], p["w_ssd_br"], p["b_gate"],
                                   p["w_out"], p["norm2_w"], p["w_router"], p["b_router"], prompt)
    return (h, hn, ri, rf, counts), new_kv, new_conv, new_ssm


def _pack_w_in(w_in):
    return w_in.T.astype(BF16)


def kernel(x_prompt, x_sample, cache_kv_w128, cache_kv_w512, cache_kv_w2048, state_conv, state_ssm, norm1_w, w_in, w_attn_br, w_ssd_br, b_gate, w_out, conv_w, conv_b, dt_bias, A_log, D_skip, ssd_norm_w, norm2_w, w_router_coarse, b_router_coarse, w_router_fine, b_router_fine, w_expert_gate, w_expert_up, w_expert_down, final_norm_w):
    depth = norm1_w.shape[0]
    assert depth == 1, "the final norm is fused into the layer's last kernel"
    l = 0
    rpad = LANES - N_EXPERTS - N_GROUPS_E
    p = dict(
        norm1_w=norm1_w[l], w_in=_pack_w_in(w_in[l]),
        w_attn_br=w_attn_br[l].astype(BF16), w_ssd_br=w_ssd_br[l].astype(BF16), b_gate=b_gate[l],
        w_out=w_out[l].astype(BF16), conv_w=conv_w[l], conv_b=conv_b[l], dt_bias=dt_bias[l], A_log=A_log[l],
        D_skip=D_skip[l], ssd_norm_w=ssd_norm_w[l], norm2_w=norm2_w[l],
        w_router=jnp.pad(jnp.concatenate([w_router_fine[l], w_router_coarse[l]], axis=1),
                         ((0, 0), (0, rpad))).astype(BF16),
        b_router=jnp.pad(jnp.concatenate([b_router_fine[l], b_router_coarse[l]]), (0, rpad)).reshape(1, LANES),
        w_eg=w_expert_gate[l], w_eu=w_expert_up[l], w_ed=w_expert_down[l], final_norm_w=final_norm_w,
    )
    g_p, kv_p, c_p, st_p = _layer(x_prompt, 0, None, None, None, p)
    bufs = (cache_kv_w128[l], cache_kv_w512[l], cache_kv_w2048[l])
    g_s, kv_s, c_s, st_s = _layer(x_sample, PAST_LEN, bufs, state_conv[l], state_ssm[l], p)
    y_p, y_s = _moe_and_final([g_p, g_s], p["w_eg"], p["w_eu"], p["w_ed"], p["final_norm_w"])
    y_p = y_p.reshape(x_prompt.shape)
    y_s = y_s.reshape(x_sample.shape)
    lead = lambda t: t[None]
    return (y_p, y_s, lead(kv_p[0]), lead(kv_p[1]), lead(kv_p[2]), lead(c_p), lead(st_p),
            lead(kv_s[0]), lead(kv_s[1]), lead(kv_s[2]), lead(c_s), lead(st_s))
```

```python
import functools

import jax
import jax.numpy as jnp
from jax import lax
from jax.experimental import pallas as pl
from jax.experimental.pallas import tpu as pltpu

F32 = jnp.float32
BF16 = jnp.bfloat16
I32 = jnp.int32

D_MODEL = 1024
HEAD_DIM = 64
ATTN_GROUPS = ((128, 1), (512, 4), (2048, 16))
SPAN = 128
HEADS_PER_GROUP = 4
GROUP_W = HEADS_PER_GROUP * HEAD_DIM
ATTN_W = GROUP_W * len(ATTN_GROUPS)
ROPE_THETA = 10000.0
PAST_LEN = 8192
SSD_INNER = 1024
SSD_HEADS = 16
SSD_STATE = 128
SSD_CONV_DIM = 1536
SSD_CHUNK = 128
SSD_CHUNKS_PER_STEP = 4
N_GROUPS_E = 4
EXPERTS_PER_GROUP = 8
N_EXPERTS = 32
D_FF = 512
EPS = 1e-6
LANES = 128
SUBLANES = 8
NEG = -1e30

_OFF_Q, _OFF_K, _OFF_V, _OFF_Z, _OFF_XBC, _OFF_DT, _OFF_GA, _OFF_GS = 0, 768, 1536, 2304, 3328, 4864, 4880, 5904
_W_IN_COLS = 6928

ROW_TILE = 512
MERGE_SPLIT = 1
INPROJ_TILE = 512
EXPERT_TILE = 512
VMEM_LIMIT = 56 * 1024 * 1024


def _cparams(sem, **kw):
    return pltpu.CompilerParams(dimension_semantics=sem, vmem_limit_bytes=VMEM_LIMIT, **kw)


def _sigmoid(x):
    return 1.0 / (1.0 + jnp.exp(-x))


ROW_CHUNKS = D_MODEL // LANES


def _store_row_tiles(ref, val):
    rows = val.shape[0]
    for c in range(ROW_CHUNKS):
        ref[pl.ds(c, rows, stride=ROW_CHUNKS), :] = val[:, c * LANES:(c + 1) * LANES]


def _load_row_tiles(ref, first, rows):
    return jnp.concatenate(
        [ref[pl.ds(first * ROW_CHUNKS + c, rows, stride=ROW_CHUNKS), :] for c in range(ROW_CHUNKS)], axis=1)


def _nt_dot(a, b):
    return lax.dot_general(a, b, (((1,), (1,)), ((), ())), preferred_element_type=F32)


def _store_by_class(o_ref, scr_ref, val, dil):
    if dil == 1:
        o_ref[0, 0] = val
        return
    rows = val.shape[0]
    for half in range(GROUP_W // LANES):
        lanes = slice(half * LANES, (half + 1) * LANES)
        scr_ref[half] = val[:, lanes]
        for r in range(dil):
            o_ref[0, r, :, lanes] = scr_ref[half, pl.ds(r, rows // dil, stride=dil), :]


def _inproj_kernel(x_ref, nw_ref, w_ref, cos_ref, sin_ref, *refs, class_major):
    if class_major:
        qkv_refs, (z_ref, xbc_ref, ga_ref, gs_ref, dt_ref, scr_ref) = refs[:9], refs[9:]
    else:
        qkv_refs, (z_ref, xbc_ref, ga_ref, gs_ref, dt_ref) = refs[:3], refs[3:]
    x = x_ref[...]
    xn = x * lax.rsqrt(jnp.mean(x * x, axis=-1, keepdims=True) + EPS) * nw_ref[...]
    xb = xn.astype(BF16)
    cos = cos_ref[...]
    sin = sin_ref[...]
    lane = lax.broadcasted_iota(I32, cos.shape, 1)
    first_half = (lane % HEAD_DIM) < (HEAD_DIM // 2)

    def mm(lo, hi):
        return _nt_dot(xb, w_ref[lo:hi, :])

    def rope(uc):
        ur = jnp.where(first_half, pltpu.roll(uc, LANES - HEAD_DIM // 2, 1), pltpu.roll(uc, HEAD_DIM // 2, 1))
        return uc * cos + ur * sin

    def emit(which, base, roped):
        u = mm(base, base + ATTN_W)
        chunks = [u[:, c * LANES:(c + 1) * LANES] for c in range(ATTN_W // LANES)]
        if roped:
            chunks = [rope(uc) for uc in chunks]
        if class_major:
            for gi, (_, dil) in enumerate(ATTN_GROUPS):
                val = jnp.concatenate(chunks[2 * gi:2 * gi + 2], axis=1)
                _store_by_class(qkv_refs[3 * which + gi], scr_ref.at[3 * which + gi], val, dil)
        else:
            for c, uc in enumerate(chunks):
                qkv_refs[which][:, c * LANES:(c + 1) * LANES] = uc

    emit(0, _OFF_Q, True)
    emit(1, _OFF_K, True)
    emit(2, _OFF_V, False)
    z_ref[...] = mm(_OFF_Z, _OFF_XBC)
    xbc_ref[...] = mm(_OFF_XBC, _OFF_DT)
    ga_ref[...] = mm(_OFF_GA, _OFF_GS)
    gs_ref[...] = mm(_OFF_GS, _W_IN_COLS)
    dt_ref[...] = jnp.where(lane < SSD_HEADS, mm(_OFF_DT, _OFF_DT + LANES), 0.0)


def _rope_tables(pos):
    half = HEAD_DIM // 2
    inv_freq = ROPE_THETA ** (-jnp.arange(half, dtype=F32) * (2.0 / HEAD_DIM))
    ang = pos.astype(F32)[:, None] * inv_freq[None, :]
    ang = jnp.tile(ang, (1, LANES // half))
    lane = jnp.arange(LANES)
    sign = jnp.where((lane % HEAD_DIM) < half, -1.0, 1.0).astype(F32)
    return jnp.cos(ang), jnp.sin(ang) * sign[None, :]


def _inproj(x2d, batch, seq_len, pos0, norm_w, w_packed, class_major):
    n = x2d.shape[0]
    tm = min(INPROJ_TILE, n)
    cos, sin = _rope_tables(pos0 + jnp.arange(seq_len, dtype=I32))
    if seq_len < tm:
        cos = jnp.tile(cos, (tm // seq_len, 1))
        sin = jnp.tile(sin, (tm // seq_len, 1))
    tab_blocks = cos.shape[0] // tm
    row = lambda i: (i, 0)
    fixed = lambda i: (0, 0)
    tab = lambda i: (i % tab_blocks, 0)
    widths = (SSD_INNER, SSD_CONV_DIM, D_MODEL, D_MODEL, LANES)
    out_specs = [pl.BlockSpec((tm, w), row) for w in widths]
    out_shape = [jax.ShapeDtypeStruct((n, w), F32) for w in widths]
    scratch = []
    if class_major:
        assert seq_len % tm == 0
        per_seq = seq_len // tm
        cls = lambda i: (i // per_seq, 0, i % per_seq, 0)
        dils = [d for _, d in ATTN_GROUPS] * 3
        out_specs = [pl.BlockSpec((1, d, tm // d, GROUP_W), cls) for d in dils] + out_specs
        out_shape = [jax.ShapeDtypeStruct((batch, d, seq_len // d, GROUP_W), F32) for d in dils] + out_shape
        scratch = [pltpu.VMEM((len(dils), GROUP_W // LANES, tm, LANES), F32)]
    else:
        out_specs = [pl.BlockSpec((tm, ATTN_W), row)] * 3 + out_specs
        out_shape = [jax.ShapeDtypeStruct((n, ATTN_W), F32)] * 3 + out_shape
    res = pl.pallas_call(
        functools.partial(_inproj_kernel, class_major=class_major),
        grid=(n // tm,),
        in_specs=[pl.BlockSpec((tm, D_MODEL), row),
                  pl.BlockSpec((1, D_MODEL), fixed),
                  pl.BlockSpec((_W_IN_COLS, D_MODEL), fixed, pipeline_mode=pl.Buffered(1)),
                  pl.BlockSpec((tm, LANES), tab),
                  pl.BlockSpec((tm, LANES), tab)],
        out_specs=out_specs,
        out_shape=out_shape,
        scratch_shapes=scratch,
        compiler_params=_cparams(("parallel",)),
        name="inproj",
    )(x2d, norm_w.reshape(1, D_MODEL), w_packed, cos, sin)
    if class_major:
        return (tuple(res[0:3]), tuple(res[3:6]), tuple(res[6:9])) + tuple(res[9:])
    return tuple(res)


ATTN_QBLOCKS = 4


def _attn_prompt_kernel(q_ref, kp_ref, kc_ref, vp_ref, vc_ref, o_ref, l_ref, k_scr, v_scr, s_scr, p_scr, r_scr,
                        *, qblocks):
    c = pl.program_id(2)
    i = lax.broadcasted_iota(I32, (SPAN, 2 * SPAN), 0)
    j = lax.broadcasted_iota(I32, (SPAN, 2 * SPAN), 1)
    band = (j >= i) & (j <= i + SPAN)
    first_band = band & ((j >= SPAN) | (c > 0))
    k_scr[0:SPAN, :] = kp_ref[0, 0].astype(BF16)
    k_scr[SPAN:, :] = kc_ref[0, 0].astype(BF16)
    v_scr[0:SPAN, :] = vp_ref[0, 0].astype(BF16)
    v_scr[SPAN:, :] = vc_ref[0, 0].astype(BF16)
    units = [(b, h) for b in range(qblocks) for h in range(HEADS_PER_GROUP)]

    for u, (b, h) in enumerate(units):
        hs = slice(h * HEAD_DIM, (h + 1) * HEAD_DIM)
        qh = (q_ref[0, 0, b * SPAN:(b + 1) * SPAN, hs] * (HEAD_DIM ** -0.5)).astype(BF16)
        s_scr[u] = _nt_dot(qh, k_scr[b * SPAN:(b + 2) * SPAN, hs])

    for u, (b, h) in enumerate(units):
        hs = slice(h * HEAD_DIM, (h + 1) * HEAD_DIM)
        s = jnp.where(first_band if b == 0 else band, s_scr[u], NEG)
        m = jnp.max(jnp.maximum(s[:, :SPAN], s[:, SPAN:]), axis=-1, keepdims=True)
        p = jnp.exp(s - m)
        den = jnp.sum(p[:, :SPAN] + p[:, SPAN:], axis=-1, keepdims=True)
        p_scr[u] = p.astype(BF16)
        r_scr[u] = jnp.broadcast_to(1.0 / den, (SPAN, HEAD_DIM))
        l_ref[0, 0, b * SPAN:(b + 1) * SPAN, hs] = jnp.broadcast_to(m + jnp.log(den), (SPAN, HEAD_DIM))

    for u, (b, h) in enumerate(units):
        hs = slice(h * HEAD_DIM, (h + 1) * HEAD_DIM)
        acc = jnp.dot(p_scr[u], v_scr[b * SPAN:(b + 2) * SPAN, hs], preferred_element_type=F32)
        o_ref[0, 0, b * SPAN:(b + 1) * SPAN, hs] = acc * r_scr[u]


def _attn_prompt(q, k, v, dil):
    batch, _, n_cls, _ = q.shape
    qblocks = min(ATTN_QBLOCKS, n_cls // SPAN)
    tq = qblocks * SPAN
    assert n_cls % tq == 0
    cur = lambda b, r, c: (b, r, c, 0)
    prev = lambda b, r, c: (b, r, jnp.maximum(c * qblocks - 1, 0), 0)
    big = pl.BlockSpec((1, 1, tq, GROUP_W), cur)
    small = pl.BlockSpec((1, 1, SPAN, GROUP_W), prev)
    return pl.pallas_call(
        functools.partial(_attn_prompt_kernel, qblocks=qblocks),
        grid=(batch, dil, n_cls // tq),
        in_specs=[big, small, big, small, big],
        out_specs=[big, big],
        out_shape=[jax.ShapeDtypeStruct(q.shape, F32)] * 2,
        scratch_shapes=[pltpu.VMEM((SPAN + tq, GROUP_W), BF16),
                        pltpu.VMEM((SPAN + tq, GROUP_W), BF16),
                        pltpu.VMEM((qblocks * HEADS_PER_GROUP, SPAN, 2 * SPAN), F32),
                        pltpu.VMEM((qblocks * HEADS_PER_GROUP, SPAN, 2 * SPAN), BF16),
                        pltpu.VMEM((qblocks * HEADS_PER_GROUP, SPAN, HEAD_DIM), F32)],
        compiler_params=_cparams(("parallel", "parallel", "arbitrary")),
        name=f"attn_prompt_d{dil}",
    )(q, k, k, v, v)


STEP_ROWS_PER_CALL = 2048


def _attn_step_kernel(q_ref, kn_ref, vn_ref, buf_ref, o_ref, l_ref, nbuf_ref, *, buf_len, dil, n_new, seqs):
    t = lax.broadcasted_iota(I32, (n_new, buf_len), 0)
    i = lax.broadcasted_iota(I32, (n_new, buf_len), 1)
    delta = buf_len + t - i
    valid_b = delta <= dil * SPAN
    first_new = LANES - n_new
    tn = lax.broadcasted_iota(I32, (n_new, LANES), 0)
    un = lax.broadcasted_iota(I32, (n_new, LANES), 1) - first_new
    dn = tn - un
    valid_n = (dn >= 0) & (un >= 0)
    if dil > 1:
        valid_b = valid_b & ((delta & (dil - 1)) == 0)
        valid_n = valid_n & ((dn & (dil - 1)) == 0)

    def to_columns(x):
        xp = jnp.concatenate([x, jnp.zeros((LANES - n_new, GROUP_W), F32)], axis=0)
        xt = jnp.concatenate([xp[:, c * LANES:(c + 1) * LANES].T for c in range(GROUP_W // LANES)], axis=0)
        return pltpu.roll(xt, first_new, 1)

    is_new = lax.broadcasted_iota(I32, (HEAD_DIM, LANES), 1) >= first_new
    tail = slice(buf_len - LANES, buf_len)
    for b, h in [(b, h) for b in range(seqs) for h in range(HEADS_PER_GROUP)]:
        if h == 0:
            q = q_ref[b] * (HEAD_DIM ** -0.5)
            new_cols = (to_columns(kn_ref[b]), to_columns(vn_ref[b]))
        hs = slice(h * HEAD_DIM, (h + 1) * HEAD_DIM)
        qh = q[:, hs].astype(BF16)
        kt = buf_ref[b, 0, h]
        vt = buf_ref[b, 1, h]
        knt = new_cols[0][hs]
        vnt = new_cols[1][hs]
        sb = jnp.where(valid_b, jnp.dot(qh, kt.astype(BF16), preferred_element_type=F32), NEG)
        sn = jnp.where(valid_n, jnp.dot(qh, knt.astype(BF16), preferred_element_type=F32), NEG)
        m = jnp.maximum(jnp.max(sb, axis=-1, keepdims=True), jnp.max(sn, axis=-1, keepdims=True))
        pb = jnp.exp(sb - m)
        pn = jnp.exp(sn - m)
        den = jnp.sum(pb, axis=-1, keepdims=True) + jnp.sum(pn, axis=-1, keepdims=True)
        acc = _nt_dot(pb.astype(BF16), vt.astype(BF16)) + _nt_dot(pn.astype(BF16), vnt.astype(BF16))
        o_ref[b, :, hs] = acc / den
        l_ref[b, :, hs] = jnp.broadcast_to(m + jnp.log(den), (n_new, HEAD_DIM))
        for kv, (old, new) in enumerate(((kt, knt), (vt, vnt))):
            shifted = pltpu.roll(old, buf_len - n_new, 1)
            nbuf_ref[b, kv, h] = shifted
            nbuf_ref[b, kv, h, :, tail] = jnp.where(is_new, new, shifted[:, tail])


def _attn_step(q, k, v, buf, batch, n_new, gi, dil):
    buf_len = buf.shape[1]
    assert dil & (dil - 1) == 0 and buf_len >= dil * SPAN and n_new % SUBLANES == 0 and buf_len % LANES == 0
    view = lambda t: t.reshape(batch, n_new, ATTN_W)
    seqs = max(1, min(batch, STEP_ROWS_PER_CALL // buf_len))
    assert batch % seqs == 0
    tok = pl.BlockSpec((seqs, n_new, GROUP_W), lambda b: (b, 0, gi))
    full = pl.BlockSpec((seqs, 2, HEADS_PER_GROUP, HEAD_DIM, buf_len), lambda b: (b, 0, 0, 0, 0))
    osp = pl.BlockSpec((seqs, n_new, GROUP_W), lambda b: (b, 0, 0))
    o, l, nbuf = pl.pallas_call(
        functools.partial(_attn_step_kernel, buf_len=buf_len, dil=dil, n_new=n_new, seqs=seqs),
        grid=(batch // seqs,),
        in_specs=[tok, tok, tok, full],
        out_specs=[osp, osp, full],
        out_shape=[jax.ShapeDtypeStruct((batch, n_new, GROUP_W), F32)] * 2
        + [jax.ShapeDtypeStruct((batch, 2, HEADS_PER_GROUP, HEAD_DIM, buf_len), F32)],
        compiler_params=_cparams(("parallel",)),
        name=f"attn_step_d{dil}",
    )(view(q), view(k), view(v), jnp.transpose(buf, (0, 2, 3, 4, 1)))
    return (o.reshape(batch * n_new, GROUP_W), l.reshape(batch * n_new, GROUP_W),
            jnp.transpose(nbuf, (0, 4, 1, 2, 3)))


def _split3(a):
    a1 = a.astype(BF16)
    r1 = a - a1.astype(F32)
    a2 = r1.astype(BF16)
    a3 = (r1 - a2.astype(F32)).astype(BF16)
    return a1, a2, a3


def _ssd_kernel(*refs, n_valid, n_steps, per_step):
    for sub in range(per_step):
        _ssd_chunk(*refs, n_valid=n_valid, n_steps=n_steps, per_step=per_step, sub=sub)


def _ssd_chunk(xbc_ref, z_ref, dt_ref, cst_ref, h0_ref, cw_ref, cb_ref, dtb_ref, alog_ref, dfull_ref, nw_ref,
               tri_ref, expand_ref, y_ref, cout_ref, hout_ref, xpad_ref, h_ref, *, n_valid, n_steps, per_step, sub):
    c = pl.program_id(1)
    lc = SSD_CHUNK
    pad = SUBLANES
    n_slabs = SSD_CONV_DIM // LANES
    tok = slice(sub * n_valid, (sub + 1) * n_valid)

    def carry_rows():
        for j in range(n_slabs):
            xpad_ref[j, 0:pad, :] = xpad_ref[j, lc:lc + pad, :]

    if sub == 0:
        @pl.when(c == 0)
        def _():
            for j in range(n_slabs):
                xpad_ref[j, 0:pad, :] = cst_ref[0, :, j * LANES:(j + 1) * LANES]
            h_ref[...] = h0_ref[0]

        pl.when(c > 0)(carry_rows)
    else:
        carry_rows()

    if n_valid == lc:
        z = z_ref[tok, :]
        dtr = dt_ref[tok, :]
    else:
        fill = lambda w: jnp.zeros((lc - n_valid, w), F32)
        z = jnp.concatenate([z_ref[tok, :], fill(SSD_INNER)], axis=0)
        dtr = jnp.concatenate([dt_ref[tok, :], fill(LANES)], axis=0)

    slabs = []
    for j in range(n_slabs):
        cols = slice(j * LANES, (j + 1) * LANES)
        xpad_ref[j, pad:pad + n_valid, :] = xbc_ref[tok, cols]
        if n_valid < lc:
            xpad_ref[j, pad + n_valid:pad + lc, :] = jnp.zeros((lc - n_valid, LANES), F32)
        xc = cb_ref[:, cols]
        for tap in range(4):
            xc = xc + xpad_ref[j, pl.ds(pad - 3 + tap, lc), :] * cw_ref[tap:tap + 1, cols]
        slabs.append(xc * _sigmoid(xc))
        cout_ref[0, :, cols] = xpad_ref[j, pl.ds(pad + n_valid - 3, 3), :]
    xs = jnp.concatenate(slabs[:SSD_INNER // LANES], axis=1)
    bm = slabs[SSD_INNER // LANES:SSD_INNER // LANES + 2]
    cm = slabs[SSD_INNER // LANES + 2:]

    row = lax.broadcasted_iota(I32, (lc, lc), 0)
    col = lax.broadcasted_iota(I32, (lc, lc), 1)
    causal = row >= col
    dtv = dtr + dtb_ref[...]
    dt = jnp.maximum(dtv, 0.0) + jnp.log1p(jnp.exp(-jnp.abs(dtv)))
    if n_valid < lc:
        dt = jnp.where(row < n_valid, dt, 0.0)
    a = dt * (-jnp.exp(alog_ref[...]))
    tri = tri_ref[...]
    a_cs = sum(jnp.dot(tri, p, preferred_element_type=F32) for p in _split3(a))
    a_cs_t = a_cs.T
    expand = expand_ref[...]
    a_full = sum(jnp.dot(p, expand, preferred_element_type=F32) for p in _split3(a_cs))
    dt_full = sum(jnp.dot(p, expand, preferred_element_type=F32) for p in _split3(dt))
    xdt = xs * dt_full
    xd = xdt * jnp.exp(a_full[lc - 1:lc, :] - a_full)
    grow = jnp.exp(a_full)
    xdt_b = xdt.astype(BF16)
    xd_t = jnp.concatenate([xd[:, k * LANES:(k + 1) * LANES].T for k in range(SSD_INNER // LANES)],
                           axis=0).astype(BF16)
    lane = lax.broadcasted_iota(I32, (lc, LANES), 1)
    low_half = lane < HEAD_DIM
    zero_b = jnp.zeros((lc, LANES), BF16)
    half = SSD_INNER // 2

    y_parts = []
    for g in range(2):
        bg = bm[g].astype(BF16)
        cg = cm[g].astype(BF16)
        cb = jnp.where(causal, _nt_dot(cg, bg), 0.0)
        h_grp = h_ref[g * half:(g + 1) * half, :]
        y_off = _nt_dot(cg, h_grp.astype(BF16)) * grow[:, g * half:(g + 1) * half]
        for pair in range(4):
            e0 = g * 8 + pair * 2
            rows = slice(e0 * HEAD_DIM, (e0 + 2) * HEAD_DIM)
            ms, keep = [], []
            for k in range(2):
                e = e0 + k
                seg = a_cs[:, e:e + 1] - a_cs_t[e:e + 1, :]
                ms.append((cb * jnp.exp(jnp.minimum(seg, 0.0))).astype(BF16))
                keep.append(jnp.broadcast_to(jnp.exp(a_cs[lc - 1:lc, e:e + 1]), (HEAD_DIM, SSD_STATE)))
            pair_b = xdt_b[:, rows]
            rhs = jnp.concatenate([jnp.where(low_half, pair_b, zero_b), jnp.where(low_half, zero_b, pair_b)], axis=0)
            y_parts.append(jnp.dot(jnp.concatenate(ms, axis=1), rhs, preferred_element_type=F32)
                           + y_off[:, pair * LANES:(pair + 1) * LANES])
            st = jnp.dot(xd_t[rows, :], bg, preferred_element_type=F32)
            h_ref[rows, :] = h_grp[pair * LANES:(pair + 1) * LANES, :] * jnp.concatenate(keep, axis=0) + st

    y = jnp.concatenate(y_parts, axis=1) + xs * dfull_ref[...]
    gate = y * (z * _sigmoid(z))
    for g in range(2):
        gg = gate[:, g * half:(g + 1) * half]
        gg = gg * lax.rsqrt(jnp.mean(gg * gg, axis=-1, keepdims=True) + EPS)
        y_ref[tok, g * half:(g + 1) * half] = (gg * nw_ref[:, g * half:(g + 1) * half])[0:n_valid]

    if sub == per_step - 1:
        @pl.when(c == n_steps - 1)
        def _():
            hout_ref[0] = h_ref[...]


def _ssd(xbc, z, dt_raw, conv_state, ssm_state, batch, seq, conv_w, conv_b, dt_bias, a_log, d_skip, norm_w):
    n_valid = min(seq, SSD_CHUNK)
    n_chunks = seq // n_valid
    assert n_valid % SUBLANES == 0 and seq % n_valid == 0
    padl = lambda t: jnp.pad(t.reshape(1, SSD_HEADS), ((0, 0), (0, LANES - SSD_HEADS)))
    cst = jnp.pad(conv_state, ((0, 0), (SUBLANES - 3, 0), (0, 0)))
    tri = (jnp.arange(SSD_CHUNK)[:, None] >= jnp.arange(SSD_CHUNK)[None, :]).astype(BF16)
    expand = (jnp.arange(LANES)[:, None] == jnp.arange(SSD_INNER)[None, :] // HEAD_DIM).astype(BF16)
    per_step = SSD_CHUNKS_PER_STEP if (n_valid == SSD_CHUNK and n_chunks % SSD_CHUNKS_PER_STEP == 0) else 1
    n_steps = n_chunks // per_step
    step_rows = per_step * n_valid
    tokrow = lambda b, c: (b * n_steps + c, 0)
    fixed = lambda b, c: (0, 0)
    per_b3 = lambda b, c: (b, 0, 0)
    state = pl.BlockSpec((1, SSD_INNER, SSD_STATE), per_b3)
    kern = functools.partial(_ssd_kernel, n_valid=n_valid, n_steps=n_steps, per_step=per_step)
    y, cout, hout = pl.pallas_call(
        kern,
        grid=(batch, n_steps),
        in_specs=[pl.BlockSpec((step_rows, SSD_CONV_DIM), tokrow),
                  pl.BlockSpec((step_rows, SSD_INNER), tokrow),
                  pl.BlockSpec((step_rows, LANES), tokrow),
                  pl.BlockSpec((1, SUBLANES, SSD_CONV_DIM), per_b3),
                  state,
                  pl.BlockSpec((4, SSD_CONV_DIM), fixed),
                  pl.BlockSpec((1, SSD_CONV_DIM), fixed),
                  pl.BlockSpec((1, LANES), fixed),
                  pl.BlockSpec((1, LANES), fixed),
                  pl.BlockSpec((1, SSD_INNER), fixed),
                  pl.BlockSpec((1, SSD_INNER), fixed),
                  pl.BlockSpec((SSD_CHUNK, SSD_CHUNK), fixed),
                  pl.BlockSpec((LANES, SSD_INNER), fixed)],
        out_specs=[pl.BlockSpec((step_rows, SSD_INNER), tokrow),
                   pl.BlockSpec((1, 3, SSD_CONV_DIM), per_b3),
                   state],
        out_shape=[jax.ShapeDtypeStruct((batch * seq, SSD_INNER), F32),
                   jax.ShapeDtypeStruct((batch, 3, SSD_CONV_DIM), F32),
                   jax.ShapeDtypeStruct((batch, SSD_INNER, SSD_STATE), F32)],
        scratch_shapes=[pltpu.VMEM((SSD_CONV_DIM // LANES, SSD_CHUNK + SUBLANES, LANES), F32),
                        pltpu.VMEM((SSD_INNER, SSD_STATE), F32)],
        compiler_params=_cparams(("parallel", "arbitrary")),
        name="ssd",
    )(xbc, z, dt_raw, cst, ssm_state.reshape(batch, SSD_INNER, SSD_STATE), conv_w, conv_b.reshape(1, SSD_CONV_DIM),
      padl(dt_bias), padl(a_log), jnp.repeat(d_skip, HEAD_DIM).reshape(1, SSD_INNER), norm_w.reshape(1, SSD_INNER),
      tri, expand)
    return y, cout, hout.reshape(batch, SSD_HEADS, HEAD_DIM, SSD_STATE)


def _merge_kernel(x_ref, o0_ref, o1_ref, o2_ref, l0_ref, l1_ref, l2_ref, ys_ref, ga_ref, gs_ref,
                  wab_ref, wsb_ref, bg_ref, wo_ref, n2_ref, wr_ref, br_ref,
                  h_ref, hn_ref, ri_ref, rf_ref, cnt_ref, carry_ref, *scr, class_major):
    step = pl.program_id(0)

    @pl.when(step == 0)
    def _():
        carry_ref[...] = jnp.zeros_like(carry_ref)

    def load(ref, gi, slab):
        if not class_major:
            return ref[...]
        dil = ATTN_GROUPS[gi][1]
        if dil == 1:
            return ref[0, 0]
        rows = ref.shape[1] * ref.shape[2]
        halves = []
        for half in range(GROUP_W // LANES):
            for r in range(dil):
                scr[0][slab, half, pl.ds(r, rows // dil, stride=dil), :] = ref[0, r, :, half * LANES:(half + 1) * LANES]
            halves.append(scr[0][slab, half])
        return jnp.concatenate(halves, axis=1)

    lses = (load(l0_ref, 0, 0), load(l1_ref, 1, 0), load(l2_ref, 2, 1))
    outs = (load(o0_ref, 0, 0), load(o1_ref, 1, 2), load(o2_ref, 2, 3))
    tm = x_ref.shape[0]
    sub = tm // MERGE_SPLIT
    lane = lax.broadcasted_iota(I32, (sub, LANES), 1)
    big = jnp.int32(LANES)
    r = lax.broadcasted_iota(I32, (sub, sub), 0)
    s = lax.broadcasted_iota(I32, (sub, sub), 1)
    earlier = (r > s).astype(BF16)

    def top(vals):
        v = jnp.max(vals, axis=-1, keepdims=True)
        idx = jnp.min(jnp.where(vals == v, lane, big), axis=-1, keepdims=True)
        return v, idx

    carry = carry_ref[0:1, :]
    for part in range(MERGE_SPLIT):
        rows = slice(part * sub, (part + 1) * sub)
        l0, l1, l2 = (t[rows] for t in lses)
        m = jnp.maximum(jnp.maximum(l0, l1), l2)
        w0, w1, w2 = jnp.exp(l0 - m), jnp.exp(l1 - m), jnp.exp(l2 - m)
        y_attn = (w0 * outs[0][rows] + w1 * outs[1][rows] + w2 * outs[2][rows]) / (w0 + w1 + w2)
        pa = jnp.dot(y_attn.astype(BF16), wab_ref[...], preferred_element_type=F32)
        ps = jnp.dot(ys_ref[rows, :].astype(BF16), wsb_ref[...], preferred_element_type=F32)
        merged = (_sigmoid(ga_ref[rows, :] + bg_ref[0:1, :]) * pa + _sigmoid(gs_ref[rows, :] + bg_ref[1:2, :]) * ps)
        h = x_ref[rows, :] + jnp.dot(merged.astype(BF16), wo_ref[...], preferred_element_type=F32)
        h_ref[rows, :] = h
        hn = h * lax.rsqrt(jnp.mean(h * h, axis=-1, keepdims=True) + EPS) * n2_ref[...]
        hnb = hn.astype(BF16)
        _store_row_tiles(hn_ref.at[pl.ds(part * sub * ROW_CHUNKS, sub * ROW_CHUNKS), :], hn)

        logits = jnp.dot(hnb, wr_ref[...], preferred_element_type=F32) + br_ref[...]
        is_coarse = (lane >= N_EXPERTS) & (lane < N_EXPERTS + N_GROUPS_E)
        lc = jnp.where(is_coarse, logits, NEG)
        mc, ic = top(lc)
        p_grp = 1.0 / jnp.sum(jnp.exp(lc - mc), axis=-1, keepdims=True)
        lo = (ic - N_EXPERTS) * EXPERTS_PER_GROUP
        lf = jnp.where((lane >= lo) & (lane < lo + EXPERTS_PER_GROUP), logits, NEG)
        v1, i1 = top(lf)
        v2, i2 = top(jnp.where(lane == i1, NEG, lf))
        e2 = jnp.exp(v2 - v1)
        g1 = p_grp / (1.0 + e2)
        g2 = p_grp * e2 / (1.0 + e2)

        oh1 = lane == i1
        oh2 = lane == i2
        cnt = oh1.astype(F32) + oh2.astype(F32)
        before = jnp.dot(earlier, cnt.astype(BF16), preferred_element_type=F32) + carry
        r1 = jnp.sum(jnp.where(oh1, before, 0.0), axis=-1, keepdims=True)
        r2 = jnp.sum(jnp.where(oh2, before, 0.0), axis=-1, keepdims=True)
        carry = carry + jnp.sum(cnt, axis=0, keepdims=True)

        ri = jnp.where(lane == 0, i1, jnp.where(lane == 1, i2, 0))
        ri = jnp.where(lane == 2, r1.astype(I32), jnp.where(lane == 3, r2.astype(I32), ri))
        ri_ref[rows, :] = ri
        rf_ref[rows, :] = jnp.where(lane == 0, g1, jnp.where(lane == 1, g2, 0.0))

    carry_ref[...] = jnp.broadcast_to(carry, carry_ref.shape)
    cnt_ref[...] = jnp.broadcast_to(carry, cnt_ref.shape).astype(I32)


def _merge(x2d, outs, lses, y_ssd, g_a, g_s, wab, wsb, b_gate, wo, norm2_w, w_router, b_router, class_major):
    n = x2d.shape[0]
    tm = min(ROW_TILE, n)
    row = lambda i: (i, 0)
    fixed = lambda i: (0, 0)
    wide = pl.BlockSpec((tm, D_MODEL), row)
    info = pl.BlockSpec((tm, LANES), row)
    if class_major:
        per_seq = outs[0].shape[2] // tm
        cls = lambda i: (i // per_seq, 0, i % per_seq, 0)
        grps = [pl.BlockSpec((1, d, tm // d, GROUP_W), cls) for _, d in ATTN_GROUPS]
        scratch = [pltpu.VMEM((4, GROUP_W // LANES, tm, LANES), F32)]
    else:
        grps = [pl.BlockSpec((tm, GROUP_W), row)] * 3
        scratch = []
    return pl.pallas_call(
        functools.partial(_merge_kernel, class_major=class_major),
        grid=(n // tm,),
        in_specs=[wide, *grps, *grps, wide, wide, wide,
                  pl.BlockSpec((GROUP_W, D_MODEL), fixed),
                  pl.BlockSpec((SSD_INNER, D_MODEL), fixed),
                  pl.BlockSpec((2, D_MODEL), fixed),
                  pl.BlockSpec((D_MODEL, D_MODEL), fixed),
                  pl.BlockSpec((1, D_MODEL), fixed),
                  pl.BlockSpec((D_MODEL, LANES), fixed),
                  pl.BlockSpec((1, LANES), fixed)],
        out_specs=[wide, pl.BlockSpec((tm * ROW_CHUNKS, LANES), row), info, info,
                   pl.BlockSpec((SUBLANES, LANES), fixed)],
        out_shape=[jax.ShapeDtypeStruct((n, D_MODEL), F32),
                   jax.ShapeDtypeStruct((n * ROW_CHUNKS, LANES), F32),
                   jax.ShapeDtypeStruct((n, LANES), I32),
                   jax.ShapeDtypeStruct((n, LANES), F32),
                   jax.ShapeDtypeStruct((SUBLANES, LANES), I32)],
        scratch_shapes=[pltpu.VMEM((SUBLANES, LANES), F32)] + scratch,
        compiler_params=_cparams(("arbitrary",)),
        name="merge_out",
    )(x2d, *outs, *lses, y_ssd, g_a, g_s, wab, wsb, b_gate, wo, norm2_w.reshape(1, D_MODEL), w_router, b_router)


GATHER_TILE = 512
DISPATCH_TILE = 512
INDEX_BATCH = 16


def _dispatch_kernel(dest_ref, pad_ref, hn_hbm, *rest, tm, first_group):
    if first_group:
        xs_hbm, ring, in_sem, sem, zeros_ref, pad_sem = rest
    else:
        _, xs_hbm, ring, in_sem, sem = rest
    i = pl.program_id(0)
    n_steps = pl.num_programs(0)
    slot = i % 3

    def load(tile):
        into = tile % 3
        return pltpu.make_async_copy(hn_hbm.at[pl.ds(tile * tm * ROW_CHUNKS, tm * ROW_CHUNKS), :], ring.at[into],
                                     in_sem.at[into])

    def wait_scatter(which):
        for _ in range(2):
            pltpu.make_async_copy(ring.at[which], xs_hbm.at[pl.ds(0, tm * ROW_CHUNKS), :], sem.at[which]).wait()

    @pl.when(i == 0)
    def _():
        load(0).start()

        @pl.when(n_steps > 1)
        def _():
            load(1).start()

    if first_group:
        @pl.when(i == 0)
        def _():
            zeros_ref[...] = jnp.zeros_like(zeros_ref)

            def fill(first_slot, size):
                first = pl.multiple_of(first_slot * ROW_CHUNKS, ROW_CHUNKS)
                return pltpu.make_async_copy(zeros_ref.at[pl.ds(0, size * ROW_CHUNKS), :],
                                             xs_hbm.at[pl.ds(first, size * ROW_CHUNKS), :], pad_sem)

            fills = []
            for e in range(N_EXPERTS):
                start, count = pad_ref[e], pad_ref[N_EXPERTS + e]
                size = EXPERT_TILE
                while size >= 1:
                    fills.append(((count & size) != 0, start + (count & ~(2 * size - 1)), size))
                    size //= 2
            tail_first, tail_tiles = pad_ref[2 * N_EXPERTS], pad_ref[2 * N_EXPERTS + 1]
            for t in range(N_EXPERTS):
                fills.append((t < tail_tiles, (tail_first + t) * EXPERT_TILE, EXPERT_TILE))
            for pred, first_slot, size in fills:
                pl.when(pred)(lambda first_slot=first_slot, size=size: fill(first_slot, size).start())
            for pred, first_slot, size in fills:
                pl.when(pred)(lambda first_slot=first_slot, size=size: fill(first_slot, size).wait())

    load(i).wait()
    for j0 in range(0, 2 * tm, INDEX_BATCH):
        slots = [dest_ref[i * 2 * tm + j0 + u] for u in range(INDEX_BATCH)]
        for u, dst in enumerate(slots):
            tok = (j0 + u) // 2
            first = pl.multiple_of(dst * ROW_CHUNKS, ROW_CHUNKS)
            pltpu.make_async_copy(ring.at[slot, pl.ds(tok * ROW_CHUNKS, ROW_CHUNKS), :],
                                  xs_hbm.at[pl.ds(first, ROW_CHUNKS), :], sem.at[slot]).start(priority=u % 2)

    @pl.when(i > 0)
    def _():
        wait_scatter((i + 2) % 3)

    @pl.when(i + 2 < n_steps)
    def _():
        load(i + 2).start()

    @pl.when(i == n_steps - 1)
    def _():
        wait_scatter(slot)


def _dispatch(hn, dest_tiles, pad_info, xs, n_slots, tm):
    n = hn.shape[0] // ROW_CHUNKS
    first_group = xs is None
    anyspec = pl.BlockSpec(memory_space=pl.ANY)
    scratch = [pltpu.VMEM((3, tm * ROW_CHUNKS, LANES), F32), pltpu.SemaphoreType.DMA((3,)),
               pltpu.SemaphoreType.DMA((3,))]
    if first_group:
        scratch += [pltpu.VMEM((EXPERT_TILE * ROW_CHUNKS, LANES), F32), pltpu.SemaphoreType.DMA(())]
    return pl.pallas_call(
        functools.partial(_dispatch_kernel, tm=tm, first_group=first_group),
        grid_spec=pltpu.PrefetchScalarGridSpec(
            num_scalar_prefetch=2,
            grid=(n // tm,),
            in_specs=[anyspec] + ([] if first_group else [anyspec]),
            out_specs=anyspec,
            scratch_shapes=scratch),
        out_shape=jax.ShapeDtypeStruct((n_slots * ROW_CHUNKS, LANES), F32),
        input_output_aliases={} if first_group else {3: 0},
        compiler_params=_cparams(("arbitrary",), disable_bounds_checks=True, has_side_effects=True),
        name="dispatch",
    )(dest_tiles, pad_info, hn, *([] if first_group else [xs]))


def _expert_kernel(te_ref, nu_ref, first_ref, slot_ref, next_ref, x_ref, wg_hbm, wu_hbm, wd_hbm, o_ref,
                   wg32, wu32, wd32, wgb, wub, wdb, sem):
    i = pl.program_id(0)

    def fetch(expert, slot):
        return [pltpu.make_async_copy(src.at[expert], dst.at[slot], sem.at[slot])
                for src, dst in ((wg_hbm, wg32), (wu_hbm, wu32), (wd_hbm, wd32))]

    @pl.when((i < nu_ref[0]) & (first_ref[i] == 1))
    def _():
        slot = slot_ref[i]

        @pl.when(i == 0)
        def _():
            for c in fetch(te_ref[0], slot):
                c.start()

        for c in fetch(te_ref[i], slot):
            c.wait()
        wgb[...] = wg32[slot].astype(BF16)
        wub[...] = wu32[slot].astype(BF16)
        wdb[...] = wd32[slot].astype(BF16)

        @pl.when(next_ref[i] >= 0)
        def _():
            for c in fetch(next_ref[i], 1 - slot):
                c.start()

    @pl.when(i < nu_ref[0])
    def _():
        x = _load_row_tiles(x_ref, 0, EXPERT_TILE).astype(BF16)
        hg = jnp.dot(x, wgb[...], preferred_element_type=F32)
        hu = jnp.dot(x, wub[...], preferred_element_type=F32)
        hb = (hg * _sigmoid(hg)) * hu
        _store_row_tiles(o_ref, jnp.dot(hb.astype(BF16), wdb[...], preferred_element_type=F32))

    @pl.when(i >= nu_ref[0])
    def _():
        o_ref[...] = jnp.zeros_like(o_ref)


def _experts(xs, tile_expert, n_used, run_first, run_slot, run_next, w_eg, w_eu, w_ed):
    n_slots = xs.shape[0] // ROW_CHUNKS
    n_tiles = n_slots // EXPERT_TILE
    row = lambda i, *_: (i, 0)
    used_row = lambda i, te, nu, *_: (jnp.minimum(i, nu[0] - 1), 0)
    anyspec = pl.BlockSpec(memory_space=pl.ANY)
    return pl.pallas_call(
        _expert_kernel,
        grid_spec=pltpu.PrefetchScalarGridSpec(
            num_scalar_prefetch=5,
            grid=(n_tiles,),
            in_specs=[pl.BlockSpec((EXPERT_TILE * ROW_CHUNKS, LANES), used_row), anyspec, anyspec, anyspec],
            out_specs=pl.BlockSpec((EXPERT_TILE * ROW_CHUNKS, LANES), row),
            scratch_shapes=[pltpu.VMEM((2, D_MODEL, D_FF), F32),
                            pltpu.VMEM((2, D_MODEL, D_FF), F32),
                            pltpu.VMEM((2, D_FF, D_MODEL), F32),
                            pltpu.VMEM((D_MODEL, D_FF), BF16),
                            pltpu.VMEM((D_MODEL, D_FF), BF16),
                            pltpu.VMEM((D_FF, D_MODEL), BF16),
                            pltpu.SemaphoreType.DMA((2,))]),
        out_shape=jax.ShapeDtypeStruct((n_slots * ROW_CHUNKS, LANES), F32),
        compiler_params=_cparams(("arbitrary",)),
        name="experts",
    )(tile_expert, n_used, run_first, run_slot, run_next, xs, w_eg, w_eu, w_ed)


FINAL_BLOCKS = 8


def _final_kernel(dest_ref, h_ref, rf_ref, fw_ref, out_hbm, o_ref, ybuf_a, ybuf_b, sem, *, tm):
    i = pl.program_id(0)
    last = pl.num_programs(0) - 1
    bufs = (ybuf_a, ybuf_b)
    nblk = FINAL_BLOCKS if tm % (FINAL_BLOCKS * SUBLANES) == 0 else 1
    tb = tm // nblk

    def gather(tile, into, lo, hi):
        for j0 in range(lo, hi, INDEX_BATCH):
            slots = [dest_ref[tile * 2 * tm + j0 + u] for u in range(INDEX_BATCH)]
            for u, src in enumerate(slots):
                first = pl.multiple_of(src * ROW_CHUNKS, ROW_CHUNKS)
                row = ((j0 + u) % 2) * tm + (j0 + u) // 2
                pltpu.make_async_copy(out_hbm.at[pl.ds(first, ROW_CHUNKS), :],
                                      bufs[into].at[pl.ds(row * ROW_CHUNKS, ROW_CHUNKS), :],
                                      sem.at[into]).start(priority=u % 2)

    def wait_rows(which):
        pltpu.make_async_copy(out_hbm.at[pl.ds(0, 2 * tm * ROW_CHUNKS), :], bufs[which], sem.at[which]).wait()

    @pl.when(i == 0)
    def _():
        gather(0, 0, 0, 2 * tm)

    def step(cur):
        wait_rows(cur)
        nxt = jnp.minimum(i + 1, last)
        for blk in range(nblk):
            gather(nxt, 1 - cur, 2 * blk * tb, 2 * (blk + 1) * tb)
            rows = slice(blk * tb, (blk + 1) * tb)
            g = rf_ref[rows, :]
            moe = (_load_row_tiles(bufs[cur], blk * tb, tb) * g[:, 0:1]
                   + _load_row_tiles(bufs[cur], tm + blk * tb, tb) * g[:, 1:2])
            h = h_ref[rows, :] + moe
            o_ref[rows, :] = h * lax.rsqrt(jnp.mean(h * h, axis=-1, keepdims=True) + EPS) * fw_ref[...]

        @pl.when(i == last)
        def _():
            wait_rows(1 - cur)

    for cur in range(2):
        pl.when(i % 2 == cur)(functools.partial(step, cur))


def _final(h, out, dest_tiles, rf, final_w):
    n = h.shape[0]
    tm = min(GATHER_TILE, n)
    row = lambda i, d: (i, 0)
    wide = pl.BlockSpec((tm, D_MODEL), row)
    return pl.pallas_call(
        functools.partial(_final_kernel, tm=tm),
        grid_spec=pltpu.PrefetchScalarGridSpec(
            num_scalar_prefetch=1,
            grid=(n // tm,),
            in_specs=[wide, pl.BlockSpec((tm, LANES), row), pl.BlockSpec((1, D_MODEL), lambda i, d: (0, 0)),
                      pl.BlockSpec(memory_space=pl.ANY)],
            out_specs=wide,
            scratch_shapes=[pltpu.VMEM((2 * tm * ROW_CHUNKS, LANES), F32), pltpu.VMEM((2 * tm * ROW_CHUNKS, LANES), F32),
                            pltpu.SemaphoreType.DMA((2,))]),
        out_shape=jax.ShapeDtypeStruct((n, D_MODEL), F32),
        compiler_params=_cparams(("arbitrary",), disable_bounds_checks=True),
        name="final",
    )(dest_tiles, h, rf, final_w.reshape(1, D_MODEL), out)


def _moe_and_final(groups, w_eg, w_eu, w_ed, final_w):
    group_counts = [g[4][0, :N_EXPERTS] for g in groups]
    counts = sum(group_counts)
    padded = ((counts + EXPERT_TILE - 1) // EXPERT_TILE) * EXPERT_TILE
    ends = jnp.cumsum(padded)
    starts = ends - padded
    n_assign = sum(2 * g[0].shape[0] for g in groups)
    n_tiles = n_assign // EXPERT_TILE + N_EXPERTS
    n_slots = n_tiles * EXPERT_TILE
    experts = jnp.arange(N_EXPERTS, dtype=I32)
    tile_start = jnp.arange(n_tiles, dtype=I32) * EXPERT_TILE
    tile_expert = jnp.minimum(jnp.sum((ends[None, :] <= tile_start[:, None]).astype(I32), axis=1), N_EXPERTS - 1)
    n_used = (ends[-1] // EXPERT_TILE).astype(I32).reshape(1)

    later = sum(g[0].shape[0] for g in groups[1:])
    assert later <= EXPERT_TILE, "fill counts must stay below 2 * EXPERT_TILE"
    n_used_tiles = ends[-1] // EXPERT_TILE
    pad_info = jnp.concatenate([starts + group_counts[0], padded - group_counts[0],
                                jnp.stack([n_used_tiles, n_tiles - n_used_tiles])]).astype(I32)
    xs = None
    dests = []
    base = starts
    for (h, hn, ri, rf, _), cnt in zip(groups, group_counts):
        n = h.shape[0]
        dest = jnp.sum(jnp.where(ri[:, 0:2, None] == experts, base, 0), axis=-1) + ri[:, 2:4]
        dests.append(dest.reshape(-1))
        xs = _dispatch(hn, dests[-1], pad_info, xs, n_slots, min(DISPATCH_TILE, n))
        base = base + cnt
    used = padded > 0
    run_index = jnp.cumsum(used.astype(I32)) - 1
    later_used = used[None, :] & (experts[None, :] > experts[:, None])
    next_used = jnp.min(jnp.where(later_used, experts[None, :], N_EXPERTS), axis=1)
    next_used = jnp.where(next_used < N_EXPERTS, next_used, -1).astype(I32)
    pick = lambda table: jnp.sum(jnp.where(tile_expert[:, None] == experts, table, 0), axis=-1).astype(I32)
    run_first = (tile_start == pick(starts)).astype(I32)
    out = _experts(xs, tile_expert, n_used, run_first, pick(run_index) % 2, pick(next_used), w_eg, w_eu, w_ed)
    return [_final(h, out, dest_tiles, rf, final_w) for (h, _, _, rf, _), dest_tiles in zip(groups, dests)]


def _layer(x, pos0, kv_bufs, conv_state, ssm_state, p):
    batch, seq, _ = x.shape
    n = batch * seq
    x2d = x.reshape(n, D_MODEL)
    prompt = kv_bufs is None
    q, k, v, z, xbc, g_a, g_s, dt_raw = _inproj(x2d, batch, seq, pos0, p["norm1_w"], p["w_in"], prompt)

    outs, lses, new_kv = [], [], []
    for gi, (window, dil) in enumerate(ATTN_GROUPS):
        if prompt:
            o, l = _attn_prompt(q[gi], k[gi], v[gi], dil)
            keep = min(window, seq) // dil
            tail = lambda t: t[:, :, seq // dil - keep:, :].transpose(0, 2, 1, 3).reshape(batch, keep * dil, GROUP_W)
            nbuf = jnp.stack([tail(k[gi]), tail(v[gi])], axis=2).reshape(
                batch, keep * dil, 2, HEADS_PER_GROUP, HEAD_DIM)
        else:
            o, l, nbuf = _attn_step(q, k, v, kv_bufs[gi], batch, seq, gi, dil)
        outs.append(o)
        lses.append(l)
        new_kv.append(nbuf)

    if conv_state is None:
        conv_state = jnp.zeros((batch, 3, SSD_CONV_DIM), F32)
        ssm_state = jnp.zeros((batch, SSD_HEADS, HEAD_DIM, SSD_STATE), F32)
    y_ssd, new_conv, new_ssm = _ssd(xbc, z, dt_raw, conv_state, ssm_state, batch, seq, p["conv_w"], p["conv_b"],
                                    p["dt_bias"], p["A_log"], p["D_skip"], p["ssd_norm_w"])

    h, hn, ri, rf, counts = _merge(x2d, outs, lses, y_ssd, g_a, g_s, p["w_attn_br"], p["w_ssd_br"], p["b_gate"],
                                   p["w_out"], p["norm2_w"], p["w_router"], p["b_router"], prompt)
    return (h, hn, ri, rf, counts), new_kv, new_conv, new_ssm


def _pack_w_in(w_in):
    return w_in.T.astype(BF16)


def kernel(x_prompt, x_sample, cache_kv_w128, cache_kv_w512, cache_kv_w2048, state_conv, state_ssm, norm1_w, w_in, w_attn_br, w_ssd_br, b_gate, w_out, conv_w, conv_b, dt_bias, A_log, D_skip, ssd_norm_w, norm2_w, w_router_coarse, b_router_coarse, w_router_fine, b_router_fine, w_expert_gate, w_expert_up, w_expert_down, final_norm_w):
    depth = norm1_w.shape[0]
    assert depth == 1, "the final norm is fused into the layer's last kernel"
    l = 0
    rpad = LANES - N_EXPERTS - N_GROUPS_E
    p = dict(
        norm1_w=norm1_w[l], w_in=_pack_w_in(w_in[l]),
        w_attn_br=w_attn_br[l].astype(BF16), w_ssd_br=w_ssd_br[l].astype(BF16), b_gate=b_gate[l],
        w_out=w_out[l].astype(BF16), conv_w=conv_w[l], conv_b=conv_b[l], dt_bias=dt_bias[l], A_log=A_log[l],
        D_skip=D_skip[l], ssd_norm_w=ssd_norm_w[l], norm2_w=norm2_w[l],
        w_router=jnp.pad(jnp.concatenate([w_router_fine[l], w_router_coarse[l]], axis=1),
                         ((0, 0), (0, rpad))).astype(BF16),
        b_router=jnp.pad(jnp.concatenate([b_router_fine[l], b_router_coarse[l]]), (0, rpad)).reshape(1, LANES),
        w_eg=w_expert_gate[l], w_eu=w_expert_up[l], w_ed=w_expert_down[l], final_norm_w=final_norm_w,
    )
    g_p, kv_p, c_p, st_p = _layer(x_prompt, 0, None, None, None, p)
    bufs = (cache_kv_w128[l], cache_kv_w512[l], cache_kv_w2048[l])
    g_s, kv_s, c_s, st_s = _layer(x_sample, PAST_LEN, bufs, state_conv[l], state_ssm[l], p)
    y_p, y_s = _moe_and_final([g_p, g_s], p["w_eg"], p["w_eu"], p["w_ed"], p["final_norm_w"])
    y_p = y_p.reshape(x_prompt.shape)
    y_s = y_s.reshape(x_sample.shape)
    lead = lambda t: t[None]
    return (y_p, y_s, lead(kv_p[0]), lead(kv_p[1]), lead(kv_p[2]), lead(c_p), lead(st_p),
            lead(kv_s[0]), lead(kv_s[1]), lead(kv_s[2]), lead(c_s), lead(st_s))
```

```python
import functools

import jax
import jax.numpy as jnp
from jax import lax
from jax.experimental import pallas as pl
from jax.experimental.pallas import tpu as pltpu

F32 = jnp.float32
BF16 = jnp.bfloat16
I32 = jnp.int32

D_MODEL = 1024
HEAD_DIM = 64
ATTN_GROUPS = ((128, 1), (512, 4), (2048, 16))
SPAN = 128
HEADS_PER_GROUP = 4
GROUP_W = HEADS_PER_GROUP * HEAD_DIM
ATTN_W = GROUP_W * len(ATTN_GROUPS)
ROPE_THETA = 10000.0
PAST_LEN = 8192
SSD_INNER = 1024
SSD_HEADS = 16
SSD_STATE = 128
SSD_CONV_DIM = 1536
SSD_CHUNK = 128
SSD_CHUNKS_PER_STEP = 4
N_GROUPS_E = 4
EXPERTS_PER_GROUP = 8
N_EXPERTS = 32
D_FF = 512
EPS = 1e-6
LANES = 128
SUBLANES = 8
NEG = -1e30

_OFF_Q, _OFF_K, _OFF_V, _OFF_Z, _OFF_XBC, _OFF_DT, _OFF_GA, _OFF_GS = 0, 768, 1536, 2304, 3328, 4864, 4880, 5904
_W_IN_COLS = 6928

ROW_TILE = 512
MERGE_SPLIT = 1
INPROJ_TILE = 512
EXPERT_TILE = 512
VMEM_LIMIT = 56 * 1024 * 1024


def _cparams(sem, **kw):
    return pltpu.CompilerParams(dimension_semantics=sem, vmem_limit_bytes=VMEM_LIMIT, **kw)


def _sigmoid(x):
    return 1.0 / (1.0 + jnp.exp(-x))


ROW_CHUNKS = D_MODEL // LANES


def _store_row_tiles(ref, val):
    rows = val.shape[0]
    for c in range(ROW_CHUNKS):
        ref[pl.ds(c, rows, stride=ROW_CHUNKS), :] = val[:, c * LANES:(c + 1) * LANES]


def _load_row_tiles(ref, first, rows):
    return jnp.concatenate(
        [ref[pl.ds(first * ROW_CHUNKS + c, rows, stride=ROW_CHUNKS), :] for c in range(ROW_CHUNKS)], axis=1)


def _nt_dot(a, b):
    return lax.dot_general(a, b, (((1,), (1,)), ((), ())), preferred_element_type=F32)


def _store_by_class(o_ref, scr_ref, val, dil):
    if dil == 1:
        o_ref[0, 0] = val
        return
    rows = val.shape[0]
    for half in range(GROUP_W // LANES):
        lanes = slice(half * LANES, (half + 1) * LANES)
        scr_ref[half] = val[:, lanes]
        for r in range(dil):
            o_ref[0, r, :, lanes] = scr_ref[half, pl.ds(r, rows // dil, stride=dil), :]


def _inproj_kernel(x_ref, nw_ref, w_ref, cos_ref, sin_ref, *refs, class_major):
    if class_major:
        qkv_refs, (z_ref, xbc_ref, ga_ref, gs_ref, dt_ref, scr_ref) = refs[:9], refs[9:]
    else:
        qkv_refs, (z_ref, xbc_ref, ga_ref, gs_ref, dt_ref) = refs[:3], refs[3:]
    x = x_ref[...]
    xn = x * lax.rsqrt(jnp.mean(x * x, axis=-1, keepdims=True) + EPS) * nw_ref[...]
    xb = xn.astype(BF16)
    cos = cos_ref[...]
    sin = sin_ref[...]
    lane = lax.broadcasted_iota(I32, cos.shape, 1)
    first_half = (lane % HEAD_DIM) < (HEAD_DIM // 2)

    def mm(lo, hi):
        return _nt_dot(xb, w_ref[lo:hi, :])

    def rope(uc):
        ur = jnp.where(first_half, pltpu.roll(uc, LANES - HEAD_DIM // 2, 1), pltpu.roll(uc, HEAD_DIM // 2, 1))
        return uc * cos + ur * sin

    def emit(which, base, roped):
        u = mm(base, base + ATTN_W)
        chunks = [u[:, c * LANES:(c + 1) * LANES] for c in range(ATTN_W // LANES)]
        if roped:
            chunks = [rope(uc) for uc in chunks]
        if class_major:
            for gi, (_, dil) in enumerate(ATTN_GROUPS):
                val = jnp.concatenate(chunks[2 * gi:2 * gi + 2], axis=1)
                _store_by_class(qkv_refs[3 * which + gi], scr_ref.at[3 * which + gi], val, dil)
        else:
            for c, uc in enumerate(chunks):
                qkv_refs[which][:, c * LANES:(c + 1) * LANES] = uc

    emit(0, _OFF_Q, True)
    emit(1, _OFF_K, True)
    emit(2, _OFF_V, False)
    z_ref[...] = mm(_OFF_Z, _OFF_XBC)
    xbc_ref[...] = mm(_OFF_XBC, _OFF_DT)
    ga_ref[...] = mm(_OFF_GA, _OFF_GS)
    gs_ref[...] = mm(_OFF_GS, _W_IN_COLS)
    dt_ref[...] = jnp.where(lane < SSD_HEADS, mm(_OFF_DT, _OFF_DT + LANES), 0.0)


def _rope_tables(pos):
    half = HEAD_DIM // 2
    inv_freq = ROPE_THETA ** (-jnp.arange(half, dtype=F32) * (2.0 / HEAD_DIM))
    ang = pos.astype(F32)[:, None] * inv_freq[None, :]
    lane = jnp.arange(LANES)
    sign = jnp.where((lane % HEAD_DIM) < half, -1.0, 1.0).astype(F32)
    widen = lambda t: jnp.tile(t, (1, LANES // half))
    return widen(jnp.cos(ang)), widen(jnp.sin(ang)) * sign[None, :]


def _inproj(x2d, batch, seq_len, pos0, norm_w, w_packed, class_major):
    n = x2d.shape[0]
    tm = min(INPROJ_TILE, n)
    cos, sin = _rope_tables(pos0 + jnp.arange(seq_len, dtype=I32))
    if seq_len < tm:
        cos = jnp.tile(cos, (tm // seq_len, 1))
        sin = jnp.tile(sin, (tm // seq_len, 1))
    tab_blocks = cos.shape[0] // tm
    row = lambda i: (i, 0)
    fixed = lambda i: (0, 0)
    tab = lambda i: (i % tab_blocks, 0)
    widths = (SSD_INNER, SSD_CONV_DIM, D_MODEL, D_MODEL, LANES)
    out_specs = [pl.BlockSpec((tm, w), row) for w in widths]
    out_shape = [jax.ShapeDtypeStruct((n, w), F32) for w in widths]
    scratch = []
    if class_major:
        assert seq_len % tm == 0
        per_seq = seq_len // tm
        cls = lambda i: (i // per_seq, 0, i % per_seq, 0)
        dils = [d for _, d in ATTN_GROUPS] * 3
        out_specs = [pl.BlockSpec((1, d, tm // d, GROUP_W), cls) for d in dils] + out_specs
        out_shape = [jax.ShapeDtypeStruct((batch, d, seq_len // d, GROUP_W), F32) for d in dils] + out_shape
        scratch = [pltpu.VMEM((len(dils), GROUP_W // LANES, tm, LANES), F32)]
    else:
        out_specs = [pl.BlockSpec((tm, ATTN_W), row)] * 3 + out_specs
        out_shape = [jax.ShapeDtypeStruct((n, ATTN_W), F32)] * 3 + out_shape
    res = pl.pallas_call(
        functools.partial(_inproj_kernel, class_major=class_major),
        grid=(n // tm,),
        in_specs=[pl.BlockSpec((tm, D_MODEL), row),
                  pl.BlockSpec((1, D_MODEL), fixed),
                  pl.BlockSpec((_W_IN_COLS, D_MODEL), fixed, pipeline_mode=pl.Buffered(1)),
                  pl.BlockSpec((tm, LANES), tab),
                  pl.BlockSpec((tm, LANES), tab)],
        out_specs=out_specs,
        out_shape=out_shape,
        scratch_shapes=scratch,
        compiler_params=_cparams(("parallel",)),
        name="inproj",
    )(x2d, norm_w.reshape(1, D_MODEL), w_packed, cos, sin)
    if class_major:
        return (tuple(res[0:3]), tuple(res[3:6]), tuple(res[6:9])) + tuple(res[9:])
    return tuple(res)


ATTN_QBLOCKS = 4


def _attn_prompt_kernel(q_ref, kp_ref, kc_ref, vp_ref, vc_ref, o_ref, l_ref, k_scr, v_scr, s_scr, p_scr, r_scr,
                        *, qblocks):
    c = pl.program_id(2)
    i = lax.broadcasted_iota(I32, (SPAN, 2 * SPAN), 0)
    j = lax.broadcasted_iota(I32, (SPAN, 2 * SPAN), 1)
    band = (j >= i) & (j <= i + SPAN)
    first_band = band & ((j >= SPAN) | (c > 0))
    k_scr[0:SPAN, :] = kp_ref[0, 0].astype(BF16)
    k_scr[SPAN:, :] = kc_ref[0, 0].astype(BF16)
    v_scr[0:SPAN, :] = vp_ref[0, 0].astype(BF16)
    v_scr[SPAN:, :] = vc_ref[0, 0].astype(BF16)
    units = [(b, h) for b in range(qblocks) for h in range(HEADS_PER_GROUP)]

    for u, (b, h) in enumerate(units):
        hs = slice(h * HEAD_DIM, (h + 1) * HEAD_DIM)
        qh = (q_ref[0, 0, b * SPAN:(b + 1) * SPAN, hs] * (HEAD_DIM ** -0.5)).astype(BF16)
        s_scr[u] = _nt_dot(qh, k_scr[b * SPAN:(b + 2) * SPAN, hs])

    for u, (b, h) in enumerate(units):
        hs = slice(h * HEAD_DIM, (h + 1) * HEAD_DIM)
        s = jnp.where(first_band if b == 0 else band, s_scr[u], NEG)
        m = jnp.max(jnp.maximum(s[:, :SPAN], s[:, SPAN:]), axis=-1, keepdims=True)
        p = jnp.exp(s - m)
        den = jnp.sum(p[:, :SPAN] + p[:, SPAN:], axis=-1, keepdims=True)
        p_scr[u] = p.astype(BF16)
        r_scr[u] = jnp.broadcast_to(1.0 / den, (SPAN, HEAD_DIM))
        l_ref[0, 0, b * SPAN:(b + 1) * SPAN, hs] = jnp.broadcast_to(m + jnp.log(den), (SPAN, HEAD_DIM))

    for u, (b, h) in enumerate(units):
        hs = slice(h * HEAD_DIM, (h + 1) * HEAD_DIM)
        acc = jnp.dot(p_scr[u], v_scr[b * SPAN:(b + 2) * SPAN, hs], preferred_element_type=F32)
        o_ref[0, 0, b * SPAN:(b + 1) * SPAN, hs] = acc * r_scr[u]


def _attn_prompt(q, k, v, dil):
    batch, _, n_cls, _ = q.shape
    qblocks = min(ATTN_QBLOCKS, n_cls // SPAN)
    tq = qblocks * SPAN
    assert n_cls % tq == 0
    cur = lambda b, r, c: (b, r, c, 0)
    prev = lambda b, r, c: (b, r, jnp.maximum(c * qblocks - 1, 0), 0)
    big = pl.BlockSpec((1, 1, tq, GROUP_W), cur)
    small = pl.BlockSpec((1, 1, SPAN, GROUP_W), prev)
    return pl.pallas_call(
        functools.partial(_attn_prompt_kernel, qblocks=qblocks),
        grid=(batch, dil, n_cls // tq),
        in_specs=[big, small, big, small, big],
        out_specs=[big, big],
        out_shape=[jax.ShapeDtypeStruct(q.shape, F32)] * 2,
        scratch_shapes=[pltpu.VMEM((SPAN + tq, GROUP_W), BF16),
                        pltpu.VMEM((SPAN + tq, GROUP_W), BF16),
                        pltpu.VMEM((qblocks * HEADS_PER_GROUP, SPAN, 2 * SPAN), F32),
                        pltpu.VMEM((qblocks * HEADS_PER_GROUP, SPAN, 2 * SPAN), BF16),
                        pltpu.VMEM((qblocks * HEADS_PER_GROUP, SPAN, HEAD_DIM), F32)],
        compiler_params=_cparams(("parallel", "parallel", "arbitrary")),
        name=f"attn_prompt_d{dil}",
    )(q, k, k, v, v)


STEP_ROWS_PER_CALL = 2048


def _attn_step_kernel(q_ref, kn_ref, vn_ref, buf_ref, o_ref, l_ref, nbuf_ref, *, buf_len, dil, n_new, seqs):
    t = lax.broadcasted_iota(I32, (n_new, buf_len), 0)
    i = lax.broadcasted_iota(I32, (n_new, buf_len), 1)
    delta = buf_len + t - i
    valid_b = delta <= dil * SPAN
    first_new = LANES - n_new
    tn = lax.broadcasted_iota(I32, (n_new, LANES), 0)
    un = lax.broadcasted_iota(I32, (n_new, LANES), 1) - first_new
    dn = tn - un
    valid_n = (dn >= 0) & (un >= 0)
    if dil > 1:
        valid_b = valid_b & ((delta & (dil - 1)) == 0)
        valid_n = valid_n & ((dn & (dil - 1)) == 0)

    def to_columns(x):
        xp = jnp.concatenate([x, jnp.zeros((LANES - n_new, GROUP_W), F32)], axis=0)
        xt = jnp.concatenate([xp[:, c * LANES:(c + 1) * LANES].T for c in range(GROUP_W // LANES)], axis=0)
        return pltpu.roll(xt, first_new, 1)

    is_new = lax.broadcasted_iota(I32, (HEAD_DIM, LANES), 1) >= first_new
    tail = slice(buf_len - LANES, buf_len)
    for b, h in [(b, h) for b in range(seqs) for h in range(HEADS_PER_GROUP)]:
        if h == 0:
            q = q_ref[b] * (HEAD_DIM ** -0.5)
            new_cols = (to_columns(kn_ref[b]), to_columns(vn_ref[b]))
        hs = slice(h * HEAD_DIM, (h + 1) * HEAD_DIM)
        qh = q[:, hs].astype(BF16)
        kt = buf_ref[b, 0, h]
        vt = buf_ref[b, 1, h]
        knt = new_cols[0][hs]
        vnt = new_cols[1][hs]
        sb = jnp.where(valid_b, jnp.dot(qh, kt.astype(BF16), preferred_element_type=F32), NEG)
        sn = jnp.where(valid_n, jnp.dot(qh, knt.astype(BF16), preferred_element_type=F32), NEG)
        m = jnp.maximum(jnp.max(sb, axis=-1, keepdims=True), jnp.max(sn, axis=-1, keepdims=True))
        pb = jnp.exp(sb - m)
        pn = jnp.exp(sn - m)
        den = jnp.sum(pb, axis=-1, keepdims=True) + jnp.sum(pn, axis=-1, keepdims=True)
        acc = _nt_dot(pb.astype(BF16), vt.astype(BF16)) + _nt_dot(pn.astype(BF16), vnt.astype(BF16))
        o_ref[b, :, hs] = acc / den
        l_ref[b, :, hs] = jnp.broadcast_to(m + jnp.log(den), (n_new, HEAD_DIM))
        for kv, (old, new) in enumerate(((kt, knt), (vt, vnt))):
            shifted = pltpu.roll(old, buf_len - n_new, 1)
            nbuf_ref[b, kv, h] = shifted
            nbuf_ref[b, kv, h, :, tail] = jnp.where(is_new, new, shifted[:, tail])


def _attn_step(q, k, v, buf, batch, n_new, gi, dil):
    buf_len = buf.shape[1]
    assert dil & (dil - 1) == 0 and buf_len >= dil * SPAN and n_new % SUBLANES == 0 and buf_len % LANES == 0
    view = lambda t: t.reshape(batch, n_new, ATTN_W)
    seqs = max(1, min(batch, STEP_ROWS_PER_CALL // buf_len))
    assert batch % seqs == 0
    tok = pl.BlockSpec((seqs, n_new, GROUP_W), lambda b: (b, 0, gi))
    full = pl.BlockSpec((seqs, 2, HEADS_PER_GROUP, HEAD_DIM, buf_len), lambda b: (b, 0, 0, 0, 0))
    osp = pl.BlockSpec((seqs, n_new, GROUP_W), lambda b: (b, 0, 0))
    o, l, nbuf = pl.pallas_call(
        functools.partial(_attn_step_kernel, buf_len=buf_len, dil=dil, n_new=n_new, seqs=seqs),
        grid=(batch // seqs,),
        in_specs=[tok, tok, tok, full],
        out_specs=[osp, osp, full],
        out_shape=[jax.ShapeDtypeStruct((batch, n_new, GROUP_W), F32)] * 2
        + [jax.ShapeDtypeStruct((batch, 2, HEADS_PER_GROUP, HEAD_DIM, buf_len), F32)],
        compiler_params=_cparams(("parallel",)),
        name=f"attn_step_d{dil}",
    )(view(q), view(k), view(v), jnp.transpose(buf, (0, 2, 3, 4, 1)))
    return (o.reshape(batch * n_new, GROUP_W), l.reshape(batch * n_new, GROUP_W),
            jnp.transpose(nbuf, (0, 4, 1, 2, 3)))


def _split3(a):
    a1 = a.astype(BF16)
    r1 = a - a1.astype(F32)
    a2 = r1.astype(BF16)
    a3 = (r1 - a2.astype(F32)).astype(BF16)
    return a1, a2, a3


def _ssd_kernel(*refs, n_valid, n_steps, per_step, separate):
    for sub in range(per_step):
        _ssd_chunk(*refs, n_valid=n_valid, n_steps=n_steps, per_step=per_step, separate=separate, sub=sub)


def _ssd_chunk(xbc_ref, z_ref, dt_ref, cst_ref, h0_ref, cw_ref, cb_ref, dtb_ref, alog_ref, dfull_ref, nw_ref,
               tri_ref, expand_ref, y_ref, cout_ref, hout_ref, xpad_ref, h_ref,
               *, n_valid, n_steps, per_step, separate, sub):
    seq = sub if separate else 0
    c = pl.program_id(1)
    lc = SSD_CHUNK
    pad = SUBLANES
    n_slabs = SSD_CONV_DIM // LANES
    tok = slice(sub * n_valid, (sub + 1) * n_valid)

    def carry_rows():
        for j in range(n_slabs):
            xpad_ref[j, 0:pad, :] = xpad_ref[j, lc:lc + pad, :]

    def start_sequence():
        for j in range(n_slabs):
            xpad_ref[j, 0:pad, :] = cst_ref[seq, :, j * LANES:(j + 1) * LANES]
        h_ref[...] = h0_ref[seq]

    if separate:
        start_sequence()
    elif sub == 0:
        pl.when(c == 0)(start_sequence)
        pl.when(c > 0)(carry_rows)
    else:
        carry_rows()

    if n_valid == lc:
        z = z_ref[tok, :]
        dtr = dt_ref[tok, :]
    else:
        fill = lambda w: jnp.zeros((lc - n_valid, w), F32)
        z = jnp.concatenate([z_ref[tok, :], fill(SSD_INNER)], axis=0)
        dtr = jnp.concatenate([dt_ref[tok, :], fill(LANES)], axis=0)

    slabs = []
    for j in range(n_slabs):
        cols = slice(j * LANES, (j + 1) * LANES)
        xpad_ref[j, pad:pad + n_valid, :] = xbc_ref[tok, cols]
        if n_valid < lc:
            xpad_ref[j, pad + n_valid:pad + lc, :] = jnp.zeros((lc - n_valid, LANES), F32)
        xc = cb_ref[:, cols]
        for tap in range(4):
            xc = xc + xpad_ref[j, pl.ds(pad - 3 + tap, lc), :] * cw_ref[tap:tap + 1, cols]
        slabs.append(xc * _sigmoid(xc))
        cout_ref[seq, :, cols] = xpad_ref[j, pl.ds(pad + n_valid - 3, 3), :]
    xs = jnp.concatenate(slabs[:SSD_INNER // LANES], axis=1)
    bm = slabs[SSD_INNER // LANES:SSD_INNER // LANES + 2]
    cm = slabs[SSD_INNER // LANES + 2:]

    row = lax.broadcasted_iota(I32, (lc, lc), 0)
    col = lax.broadcasted_iota(I32, (lc, lc), 1)
    causal = row >= col
    dtv = dtr + dtb_ref[...]
    dt = jnp.maximum(dtv, 0.0) + jnp.log1p(jnp.exp(-jnp.abs(dtv)))
    if n_valid < lc:
        dt = jnp.where(row < n_valid, dt, 0.0)
    a = dt * (-jnp.exp(alog_ref[...]))
    tri = tri_ref[...]
    a_cs = sum(jnp.dot(tri, p, preferred_element_type=F32) for p in _split3(a))
    a_cs_t = a_cs.T
    expand = expand_ref[...]
    a_full = sum(jnp.dot(p, expand, preferred_element_type=F32) for p in _split3(a_cs))
    dt_full = sum(jnp.dot(p, expand, preferred_element_type=F32) for p in _split3(dt))
    xdt = xs * dt_full
    xd = xdt * jnp.exp(a_full[lc - 1:lc, :] - a_full)
    grow = jnp.exp(a_full)
    xdt_b = xdt.astype(BF16)
    xd_t = jnp.concatenate([xd[:, k * LANES:(k + 1) * LANES].T for k in range(SSD_INNER // LANES)],
                           axis=0).astype(BF16)
    lane = lax.broadcasted_iota(I32, (lc, LANES), 1)
    low_half = lane < HEAD_DIM
    zero_b = jnp.zeros((lc, LANES), BF16)
    half = SSD_INNER // 2

    y_parts = []
    for g in range(2):
        bg = bm[g].astype(BF16)
        cg = cm[g].astype(BF16)
        cb = jnp.where(causal, _nt_dot(cg, bg), 0.0)
        h_grp = h_ref[g * half:(g + 1) * half, :]
        y_off = _nt_dot(cg, h_grp.astype(BF16)) * grow[:, g * half:(g + 1) * half]
        for pair in range(4):
            e0 = g * 8 + pair * 2
            rows = slice(e0 * HEAD_DIM, (e0 + 2) * HEAD_DIM)
            ms, keep = [], []
            for k in range(2):
                e = e0 + k
                seg = a_cs[:, e:e + 1] - a_cs_t[e:e + 1, :]
                ms.append((cb * jnp.exp(jnp.minimum(seg, 0.0))).astype(BF16))
                keep.append(jnp.broadcast_to(jnp.exp(a_cs[lc - 1:lc, e:e + 1]), (HEAD_DIM, SSD_STATE)))
            pair_b = xdt_b[:, rows]
            rhs = jnp.concatenate([jnp.where(low_half, pair_b, zero_b), jnp.where(low_half, zero_b, pair_b)], axis=0)
            y_parts.append(jnp.dot(jnp.concatenate(ms, axis=1), rhs, preferred_element_type=F32)
                           + y_off[:, pair * LANES:(pair + 1) * LANES])
            st = jnp.dot(xd_t[rows, :], bg, preferred_element_type=F32)
            h_ref[rows, :] = h_grp[pair * LANES:(pair + 1) * LANES, :] * jnp.concatenate(keep, axis=0) + st

    y = jnp.concatenate(y_parts, axis=1) + xs * dfull_ref[...]
    gate = y * (z * _sigmoid(z))
    for g in range(2):
        gg = gate[:, g * half:(g + 1) * half]
        gg = gg * lax.rsqrt(jnp.mean(gg * gg, axis=-1, keepdims=True) + EPS)
        y_ref[tok, g * half:(g + 1) * half] = (gg * nw_ref[:, g * half:(g + 1) * half])[0:n_valid]

    if separate:
        hout_ref[seq] = h_ref[...]
    elif sub == per_step - 1:
        @pl.when(c == n_steps - 1)
        def _():
            hout_ref[0] = h_ref[...]


def _ssd(xbc, z, dt_raw, conv_state, ssm_state, batch, seq, conv_w, conv_b, dt_bias, a_log, d_skip, norm_w):
    n_valid = min(seq, SSD_CHUNK)
    n_chunks = seq // n_valid
    assert n_valid % SUBLANES == 0 and seq % n_valid == 0
    padl = lambda t: jnp.pad(t.reshape(1, SSD_HEADS), ((0, 0), (0, LANES - SSD_HEADS)))
    cst = jnp.pad(conv_state, ((0, 0), (SUBLANES - 3, 0), (0, 0)))
    tri = (jnp.arange(SSD_CHUNK)[:, None] >= jnp.arange(SSD_CHUNK)[None, :]).astype(BF16)
    expand = (jnp.arange(LANES)[:, None] == jnp.arange(SSD_INNER)[None, :] // HEAD_DIM).astype(BF16)
    separate = n_chunks == 1 and batch % SSD_CHUNKS_PER_STEP == 0
    per_step = SSD_CHUNKS_PER_STEP if (separate or n_chunks % SSD_CHUNKS_PER_STEP == 0) else 1
    n_steps = 1 if separate else n_chunks // per_step
    n_outer = batch // per_step if separate else batch
    per_state = per_step if separate else 1
    step_rows = per_step * n_valid
    tokrow = lambda b, c: (b * n_steps + c, 0)
    fixed = lambda b, c: (0, 0)
    per_b3 = lambda b, c: (b, 0, 0)
    state = pl.BlockSpec((per_state, SSD_INNER, SSD_STATE), per_b3)
    kern = functools.partial(_ssd_kernel, n_valid=n_valid, n_steps=n_steps, per_step=per_step, separate=separate)
    y, cout, hout = pl.pallas_call(
        kern,
        grid=(n_outer, n_steps),
        in_specs=[pl.BlockSpec((step_rows, SSD_CONV_DIM), tokrow),
                  pl.BlockSpec((step_rows, SSD_INNER), tokrow),
                  pl.BlockSpec((step_rows, LANES), tokrow),
                  pl.BlockSpec((per_state, SUBLANES, SSD_CONV_DIM), per_b3),
                  state,
                  pl.BlockSpec((4, SSD_CONV_DIM), fixed),
                  pl.BlockSpec((1, SSD_CONV_DIM), fixed),
                  pl.BlockSpec((1, LANES), fixed),
                  pl.BlockSpec((1, LANES), fixed),
                  pl.BlockSpec((1, SSD_INNER), fixed),
                  pl.BlockSpec((1, SSD_INNER), fixed),
                  pl.BlockSpec((SSD_CHUNK, SSD_CHUNK), fixed),
                  pl.BlockSpec((LANES, SSD_INNER), fixed)],
        out_specs=[pl.BlockSpec((step_rows, SSD_INNER), tokrow),
                   pl.BlockSpec((per_state, 3, SSD_CONV_DIM), per_b3),
                   state],
        out_shape=[jax.ShapeDtypeStruct((batch * seq, SSD_INNER), F32),
                   jax.ShapeDtypeStruct((batch, 3, SSD_CONV_DIM), F32),
                   jax.ShapeDtypeStruct((batch, SSD_INNER, SSD_STATE), F32)],
        scratch_shapes=[pltpu.VMEM((SSD_CONV_DIM // LANES, SSD_CHUNK + SUBLANES, LANES), F32),
                        pltpu.VMEM((SSD_INNER, SSD_STATE), F32)],
        compiler_params=_cparams(("parallel", "arbitrary")),
        name="ssd",
    )(xbc, z, dt_raw, cst, ssm_state.reshape(batch, SSD_INNER, SSD_STATE), conv_w, conv_b.reshape(1, SSD_CONV_DIM),
      padl(dt_bias), padl(a_log), jnp.repeat(d_skip, HEAD_DIM).reshape(1, SSD_INNER), norm_w.reshape(1, SSD_INNER),
      tri, expand)
    return y, cout, hout.reshape(batch, SSD_HEADS, HEAD_DIM, SSD_STATE)


def _merge_kernel(x_ref, o0_ref, o1_ref, o2_ref, l0_ref, l1_ref, l2_ref, ys_ref, ga_ref, gs_ref,
                  wab_ref, wsb_ref, bg_ref, wo_ref, n2_ref, wr_ref, br_ref,
                  h_ref, hn_ref, ri_ref, rf_ref, cnt_ref, carry_ref, *scr, class_major):
    step = pl.program_id(0)

    @pl.when(step == 0)
    def _():
        carry_ref[...] = jnp.zeros_like(carry_ref)

    def load(ref, gi, slab):
        if not class_major:
            return ref[...]
        dil = ATTN_GROUPS[gi][1]
        if dil == 1:
            return ref[0, 0]
        rows = ref.shape[1] * ref.shape[2]
        halves = []
        for half in range(GROUP_W // LANES):
            for r in range(dil):
                scr[0][slab, half, pl.ds(r, rows // dil, stride=dil), :] = ref[0, r, :, half * LANES:(half + 1) * LANES]
            halves.append(scr[0][slab, half])
        return jnp.concatenate(halves, axis=1)

    lses = (load(l0_ref, 0, 0), load(l1_ref, 1, 0), load(l2_ref, 2, 1))
    outs = (load(o0_ref, 0, 0), load(o1_ref, 1, 2), load(o2_ref, 2, 3))
    tm = x_ref.shape[0]
    sub = tm // MERGE_SPLIT
    lane = lax.broadcasted_iota(I32, (sub, LANES), 1)
    big = jnp.int32(LANES)
    r = lax.broadcasted_iota(I32, (sub, sub), 0)
    s = lax.broadcasted_iota(I32, (sub, sub), 1)
    earlier = (r > s).astype(BF16)

    def top(vals):
        v = jnp.max(vals, axis=-1, keepdims=True)
        idx = jnp.min(jnp.where(vals == v, lane, big), axis=-1, keepdims=True)
        return v, idx

    carry = carry_ref[0:1, :]
    for part in range(MERGE_SPLIT):
        rows = slice(part * sub, (part + 1) * sub)
        l0, l1, l2 = (t[rows] for t in lses)
        m = jnp.maximum(jnp.maximum(l0, l1), l2)
        w0, w1, w2 = jnp.exp(l0 - m), jnp.exp(l1 - m), jnp.exp(l2 - m)
        y_attn = (w0 * outs[0][rows] + w1 * outs[1][rows] + w2 * outs[2][rows]) / (w0 + w1 + w2)
        pa = jnp.dot(y_attn.astype(BF16), wab_ref[...], preferred_element_type=F32)
        ps = jnp.dot(ys_ref[rows, :].astype(BF16), wsb_ref[...], preferred_element_type=F32)
        merged = (_sigmoid(ga_ref[rows, :] + bg_ref[0:1, :]) * pa + _sigmoid(gs_ref[rows, :] + bg_ref[1:2, :]) * ps)
        h = x_ref[rows, :] + jnp.dot(merged.astype(BF16), wo_ref[...], preferred_element_type=F32)
        h_ref[rows, :] = h
        hn = h * lax.rsqrt(jnp.mean(h * h, axis=-1, keepdims=True) + EPS) * n2_ref[...]
        hnb = hn.astype(BF16)
        _store_row_tiles(hn_ref.at[pl.ds(part * sub * ROW_CHUNKS, sub * ROW_CHUNKS), :], hn)

        logits = jnp.dot(hnb, wr_ref[...], preferred_element_type=F32) + br_ref[...]
        is_coarse = (lane >= N_EXPERTS) & (lane < N_EXPERTS + N_GROUPS_E)
        lc = jnp.where(is_coarse, logits, NEG)
        mc, ic = top(lc)
        p_grp = 1.0 / jnp.sum(jnp.exp(lc - mc), axis=-1, keepdims=True)
        lo = (ic - N_EXPERTS) * EXPERTS_PER_GROUP
        lf = jnp.where((lane >= lo) & (lane < lo + EXPERTS_PER_GROUP), logits, NEG)
        v1, i1 = top(lf)
        v2, i2 = top(jnp.where(lane == i1, NEG, lf))
        e2 = jnp.exp(v2 - v1)
        g1 = p_grp / (1.0 + e2)
        g2 = p_grp * e2 / (1.0 + e2)

        oh1 = lane == i1
        oh2 = lane == i2
        cnt = oh1.astype(F32) + oh2.astype(F32)
        before = jnp.dot(earlier, cnt.astype(BF16), preferred_element_type=F32) + carry
        r1 = jnp.sum(jnp.where(oh1, before, 0.0), axis=-1, keepdims=True)
        r2 = jnp.sum(jnp.where(oh2, before, 0.0), axis=-1, keepdims=True)
        carry = carry + jnp.sum(cnt, axis=0, keepdims=True)

        ri = jnp.where(lane == 0, i1, jnp.where(lane == 1, i2, 0))
        ri = jnp.where(lane == 2, r1.astype(I32), jnp.where(lane == 3, r2.astype(I32), ri))
        ri_ref[rows, :] = ri
        rf_ref[rows, :] = jnp.where(lane == 0, g1, jnp.where(lane == 1, g2, 0.0))

    carry_ref[...] = jnp.broadcast_to(carry, carry_ref.shape)
    cnt_ref[...] = jnp.broadcast_to(carry, cnt_ref.shape).astype(I32)


def _merge(x2d, outs, lses, y_ssd, g_a, g_s, wab, wsb, b_gate, wo, norm2_w, w_router, b_router, class_major):
    n = x2d.shape[0]
    tm = min(ROW_TILE, n)
    row = lambda i: (i, 0)
    fixed = lambda i: (0, 0)
    wide = pl.BlockSpec((tm, D_MODEL), row)
    info = pl.BlockSpec((tm, LANES), row)
    if class_major:
        per_seq = outs[0].shape[2] // tm
        cls = lambda i: (i // per_seq, 0, i % per_seq, 0)
        grps = [pl.BlockSpec((1, d, tm // d, GROUP_W), cls) for _, d in ATTN_GROUPS]
        scratch = [pltpu.VMEM((4, GROUP_W // LANES, tm, LANES), F32)]
    else:
        grps = [pl.BlockSpec((tm, GROUP_W), row)] * 3
        scratch = []
    return pl.pallas_call(
        functools.partial(_merge_kernel, class_major=class_major),
        grid=(n // tm,),
        in_specs=[wide, *grps, *grps, wide, wide, wide,
                  pl.BlockSpec((GROUP_W, D_MODEL), fixed),
                  pl.BlockSpec((SSD_INNER, D_MODEL), fixed),
                  pl.BlockSpec((2, D_MODEL), fixed),
                  pl.BlockSpec((D_MODEL, D_MODEL), fixed),
                  pl.BlockSpec((1, D_MODEL), fixed),
                  pl.BlockSpec((D_MODEL, LANES), fixed),
                  pl.BlockSpec((1, LANES), fixed)],
        out_specs=[wide, pl.BlockSpec((tm * ROW_CHUNKS, LANES), row), info, info,
                   pl.BlockSpec((SUBLANES, LANES), fixed)],
        out_shape=[jax.ShapeDtypeStruct((n, D_MODEL), F32),
                   jax.ShapeDtypeStruct((n * ROW_CHUNKS, LANES), F32),
                   jax.ShapeDtypeStruct((n, LANES), I32),
                   jax.ShapeDtypeStruct((n, LANES), F32),
                   jax.ShapeDtypeStruct((SUBLANES, LANES), I32)],
        scratch_shapes=[pltpu.VMEM((SUBLANES, LANES), F32)] + scratch,
        compiler_params=_cparams(("arbitrary",)),
        name="merge_out",
    )(x2d, *outs, *lses, y_ssd, g_a, g_s, wab, wsb, b_gate, wo, norm2_w.reshape(1, D_MODEL), w_router, b_router)


GATHER_TILE = 512
DISPATCH_TILE = 512
INDEX_BATCH = 16


def _dispatch_kernel(dest_ref, pad_ref, hn_hbm, *rest, tm, first_group):
    if first_group:
        xs_hbm, ring, in_sem, sem, zeros_ref, pad_sem = rest
    else:
        _, xs_hbm, ring, in_sem, sem = rest
    i = pl.program_id(0)
    n_steps = pl.num_programs(0)
    slot = i % 3

    def load(tile):
        into = tile % 3
        return pltpu.make_async_copy(hn_hbm.at[pl.ds(tile * tm * ROW_CHUNKS, tm * ROW_CHUNKS), :], ring.at[into],
                                     in_sem.at[into])

    def wait_scatter(which):
        for _ in range(2):
            pltpu.make_async_copy(ring.at[which], xs_hbm.at[pl.ds(0, tm * ROW_CHUNKS), :], sem.at[which]).wait()

    @pl.when(i == 0)
    def _():
        load(0).start()

        @pl.when(n_steps > 1)
        def _():
            load(1).start()

    if first_group:
        @pl.when(i == 0)
        def _():
            zeros_ref[...] = jnp.zeros_like(zeros_ref)

            def fill(first_slot, size):
                first = pl.multiple_of(first_slot * ROW_CHUNKS, ROW_CHUNKS)
                return pltpu.make_async_copy(zeros_ref.at[pl.ds(0, size * ROW_CHUNKS), :],
                                             xs_hbm.at[pl.ds(first, size * ROW_CHUNKS), :], pad_sem)

            fills = []
            for e in range(N_EXPERTS):
                start, count = pad_ref[e], pad_ref[N_EXPERTS + e]
                size = EXPERT_TILE
                while size >= 1:
                    fills.append(((count & size) != 0, start + (count & ~(2 * size - 1)), size))
                    size //= 2
            tail_first, tail_tiles = pad_ref[2 * N_EXPERTS], pad_ref[2 * N_EXPERTS + 1]
            for t in range(N_EXPERTS):
                fills.append((t < tail_tiles, (tail_first + t) * EXPERT_TILE, EXPERT_TILE))
            for pred, first_slot, size in fills:
                pl.when(pred)(lambda first_slot=first_slot, size=size: fill(first_slot, size).start())
            for pred, first_slot, size in fills:
                pl.when(pred)(lambda first_slot=first_slot, size=size: fill(first_slot, size).wait())

    load(i).wait()
    for j0 in range(0, 2 * tm, INDEX_BATCH):
        slots = [dest_ref[i * 2 * tm + j0 + u] for u in range(INDEX_BATCH)]
        for u, dst in enumerate(slots):
            tok = (j0 + u) // 2
            first = pl.multiple_of(dst * ROW_CHUNKS, ROW_CHUNKS)
            pltpu.make_async_copy(ring.at[slot, pl.ds(tok * ROW_CHUNKS, ROW_CHUNKS), :],
                                  xs_hbm.at[pl.ds(first, ROW_CHUNKS), :], sem.at[slot]).start(priority=u % 2)

    @pl.when(i > 0)
    def _():
        wait_scatter((i + 2) % 3)

    @pl.when(i + 2 < n_steps)
    def _():
        load(i + 2).start()

    @pl.when(i == n_steps - 1)
    def _():
        wait_scatter(slot)


def _dispatch(hn, dest_tiles, pad_info, xs, n_slots, tm):
    n = hn.shape[0] // ROW_CHUNKS
    first_group = xs is None
    anyspec = pl.BlockSpec(memory_space=pl.ANY)
    scratch = [pltpu.VMEM((3, tm * ROW_CHUNKS, LANES), F32), pltpu.SemaphoreType.DMA((3,)),
               pltpu.SemaphoreType.DMA((3,))]
    if first_group:
        scratch += [pltpu.VMEM((EXPERT_TILE * ROW_CHUNKS, LANES), F32), pltpu.SemaphoreType.DMA(())]
    return pl.pallas_call(
        functools.partial(_dispatch_kernel, tm=tm, first_group=first_group),
        grid_spec=pltpu.PrefetchScalarGridSpec(
            num_scalar_prefetch=2,
            grid=(n // tm,),
            in_specs=[anyspec] + ([] if first_group else [anyspec]),
            out_specs=anyspec,
            scratch_shapes=scratch),
        out_shape=jax.ShapeDtypeStruct((n_slots * ROW_CHUNKS, LANES), F32),
        input_output_aliases={} if first_group else {3: 0},
        compiler_params=_cparams(("arbitrary",), disable_bounds_checks=True, has_side_effects=True),
        name="dispatch",
    )(dest_tiles, pad_info, hn, *([] if first_group else [xs]))


def _expert_kernel(te_ref, nu_ref, first_ref, slot_ref, next_ref, x_ref, wg_hbm, wu_hbm, wd_hbm, o_ref,
                   wg32, wu32, wd32, wgb, wub, wdb, sem):
    i = pl.program_id(0)

    def fetch(expert, slot):
        return [pltpu.make_async_copy(src.at[expert], dst.at[slot], sem.at[slot])
                for src, dst in ((wg_hbm, wg32), (wu_hbm, wu32), (wd_hbm, wd32))]

    @pl.when((i < nu_ref[0]) & (first_ref[i] == 1))
    def _():
        slot = slot_ref[i]

        @pl.when(i == 0)
        def _():
            for c in fetch(te_ref[0], slot):
                c.start()

        for c in fetch(te_ref[i], slot):
            c.wait()
        wgb[...] = wg32[slot].astype(BF16)
        wub[...] = wu32[slot].astype(BF16)
        wdb[...] = wd32[slot].astype(BF16)

        @pl.when(next_ref[i] >= 0)
        def _():
            for c in fetch(next_ref[i], 1 - slot):
                c.start()

    @pl.when(i < nu_ref[0])
    def _():
        x = _load_row_tiles(x_ref, 0, EXPERT_TILE).astype(BF16)
        hg = jnp.dot(x, wgb[...], preferred_element_type=F32)
        hu = jnp.dot(x, wub[...], preferred_element_type=F32)
        hb = (hg * _sigmoid(hg)) * hu
        _store_row_tiles(o_ref, jnp.dot(hb.astype(BF16), wdb[...], preferred_element_type=F32))

    @pl.when(i >= nu_ref[0])
    def _():
        o_ref[...] = jnp.zeros_like(o_ref)


def _experts(xs, tile_expert, n_used, run_first, run_slot, run_next, w_eg, w_eu, w_ed):
    n_slots = xs.shape[0] // ROW_CHUNKS
    n_tiles = n_slots // EXPERT_TILE
    row = lambda i, *_: (i, 0)
    used_row = lambda i, te, nu, *_: (jnp.minimum(i, nu[0] - 1), 0)
    anyspec = pl.BlockSpec(memory_space=pl.ANY)
    return pl.pallas_call(
        _expert_kernel,
        grid_spec=pltpu.PrefetchScalarGridSpec(
            num_scalar_prefetch=5,
            grid=(n_tiles,),
            in_specs=[pl.BlockSpec((EXPERT_TILE * ROW_CHUNKS, LANES), used_row), anyspec, anyspec, anyspec],
            out_specs=pl.BlockSpec((EXPERT_TILE * ROW_CHUNKS, LANES), row),
            scratch_shapes=[pltpu.VMEM((2, D_MODEL, D_FF), F32),
                            pltpu.VMEM((2, D_MODEL, D_FF), F32),
                            pltpu.VMEM((2, D_FF, D_MODEL), F32),
                            pltpu.VMEM((D_MODEL, D_FF), BF16),
                            pltpu.VMEM((D_MODEL, D_FF), BF16),
                            pltpu.VMEM((D_FF, D_MODEL), BF16),
                            pltpu.SemaphoreType.DMA((2,))]),
        out_shape=jax.ShapeDtypeStruct((n_slots * ROW_CHUNKS, LANES), F32),
        compiler_params=_cparams(("arbitrary",)),
        name="experts",
    )(tile_expert, n_used, run_first, run_slot, run_next, xs, w_eg, w_eu, w_ed)


FINAL_BLOCKS = 8


def _final_kernel(dest_ref, h_ref, rf_ref, fw_ref, out_hbm, o_ref, ybuf_a, ybuf_b, sem, *, tm):
    i = pl.program_id(0)
    last = pl.num_programs(0) - 1
    bufs = (ybuf_a, ybuf_b)
    nblk = FINAL_BLOCKS if tm % (FINAL_BLOCKS * SUBLANES) == 0 else 1
    tb = tm // nblk

    def gather(tile, into, lo, hi):
        for j0 in range(lo, hi, INDEX_BATCH):
            slots = [dest_ref[tile * 2 * tm + j0 + u] for u in range(INDEX_BATCH)]
            for u, src in enumerate(slots):
                first = pl.multiple_of(src * ROW_CHUNKS, ROW_CHUNKS)
                row = ((j0 + u) % 2) * tm + (j0 + u) // 2
                pltpu.make_async_copy(out_hbm.at[pl.ds(first, ROW_CHUNKS), :],
                                      bufs[into].at[pl.ds(row * ROW_CHUNKS, ROW_CHUNKS), :],
                                      sem.at[into]).start(priority=u % 2)

    def wait_rows(which):
        pltpu.make_async_copy(out_hbm.at[pl.ds(0, 2 * tm * ROW_CHUNKS), :], bufs[which], sem.at[which]).wait()

    @pl.when(i == 0)
    def _():
        gather(0, 0, 0, 2 * tm)

    def step(cur):
        wait_rows(cur)
        nxt = jnp.minimum(i + 1, last)
        for blk in range(nblk):
            gather(nxt, 1 - cur, 2 * blk * tb, 2 * (blk + 1) * tb)
            rows = slice(blk * tb, (blk + 1) * tb)
            g = rf_ref[rows, :]
            moe = (_load_row_tiles(bufs[cur], blk * tb, tb) * g[:, 0:1]
                   + _load_row_tiles(bufs[cur], tm + blk * tb, tb) * g[:, 1:2])
            h = h_ref[rows, :] + moe
            o_ref[rows, :] = h * lax.rsqrt(jnp.mean(h * h, axis=-1, keepdims=True) + EPS) * fw_ref[...]

        @pl.when(i == last)
        def _():
            wait_rows(1 - cur)

    for cur in range(2):
        pl.when(i % 2 == cur)(functools.partial(step, cur))


def _final(h, out, dest_tiles, rf, final_w):
    n = h.shape[0]
    tm = min(GATHER_TILE, n)
    row = lambda i, d: (i, 0)
    wide = pl.BlockSpec((tm, D_MODEL), row)
    return pl.pallas_call(
        functools.partial(_final_kernel, tm=tm),
        grid_spec=pltpu.PrefetchScalarGridSpec(
            num_scalar_prefetch=1,
            grid=(n // tm,),
            in_specs=[wide, pl.BlockSpec((tm, LANES), row), pl.BlockSpec((1, D_MODEL), lambda i, d: (0, 0)),
                      pl.BlockSpec(memory_space=pl.ANY)],
            out_specs=wide,
            scratch_shapes=[pltpu.VMEM((2 * tm * ROW_CHUNKS, LANES), F32), pltpu.VMEM((2 * tm * ROW_CHUNKS, LANES), F32),
                            pltpu.SemaphoreType.DMA((2,))]),
        out_shape=jax.ShapeDtypeStruct((n, D_MODEL), F32),
        compiler_params=_cparams(("arbitrary",), disable_bounds_checks=True),
        name="final",
    )(dest_tiles, h, rf, final_w.reshape(1, D_MODEL), out)


def _moe_and_final(groups, w_eg, w_eu, w_ed, final_w):
    group_counts = [g[4][0, :N_EXPERTS] for g in groups]
    counts = sum(group_counts)
    padded = ((counts + EXPERT_TILE - 1) // EXPERT_TILE) * EXPERT_TILE
    ends = jnp.cumsum(padded)
    starts = ends - padded
    n_assign = sum(2 * g[0].shape[0] for g in groups)
    n_tiles = n_assign // EXPERT_TILE + N_EXPERTS
    n_slots = n_tiles * EXPERT_TILE
    experts = jnp.arange(N_EXPERTS, dtype=I32)
    tile_start = jnp.arange(n_tiles, dtype=I32) * EXPERT_TILE
    tile_expert = jnp.minimum(jnp.sum((ends[None, :] <= tile_start[:, None]).astype(I32), axis=1), N_EXPERTS - 1)
    n_used = (ends[-1] // EXPERT_TILE).astype(I32).reshape(1)

    later = sum(g[0].shape[0] for g in groups[1:])
    assert later <= EXPERT_TILE, "fill counts must stay below 2 * EXPERT_TILE"
    n_used_tiles = ends[-1] // EXPERT_TILE
    pad_info = jnp.concatenate([starts + group_counts[0], padded - group_counts[0],
                                jnp.stack([n_used_tiles, n_tiles - n_used_tiles])]).astype(I32)
    xs = None
    dests = []
    base = starts
    for (h, hn, ri, rf, _), cnt in zip(groups, group_counts):
        n = h.shape[0]
        dest = jnp.sum(jnp.where(ri[:, 0:2, None] == experts, base, 0), axis=-1) + ri[:, 2:4]
        dests.append(dest.reshape(-1))
        xs = _dispatch(hn, dests[-1], pad_info, xs, n_slots, min(DISPATCH_TILE, n))
        base = base + cnt
    used = padded > 0
    run_index = jnp.cumsum(used.astype(I32)) - 1
    later_used = used[None, :] & (experts[None, :] > experts[:, None])
    next_used = jnp.min(jnp.where(later_used, experts[None, :], N_EXPERTS), axis=1)
    next_used = jnp.where(next_used < N_EXPERTS, next_used, -1).astype(I32)
    pick = lambda table: jnp.sum(jnp.where(tile_expert[:, None] == experts, table, 0), axis=-1).astype(I32)
    run_first = (tile_start == pick(starts)).astype(I32)
    out = _experts(xs, tile_expert, n_used, run_first, pick(run_index) % 2, pick(next_used), w_eg, w_eu, w_ed)
    return [_final(h, out, dest_tiles, rf, final_w) for (h, _, _, rf, _), dest_tiles in zip(groups, dests)]


def _layer(x, pos0, kv_bufs, conv_state, ssm_state, p):
    batch, seq, _ = x.shape
    n = batch * seq
    x2d = x.reshape(n, D_MODEL)
    prompt = kv_bufs is None
    q, k, v, z, xbc, g_a, g_s, dt_raw = _inproj(x2d, batch, seq, pos0, p["norm1_w"], p["w_in"], prompt)

    outs, lses, new_kv = [], [], []
    for gi, (window, dil) in enumerate(ATTN_GROUPS):
        if prompt:
            o, l = _attn_prompt(q[gi], k[gi], v[gi], dil)
            keep = min(window, seq) // dil
            tail = lambda t: t[:, :, seq // dil - keep:, :].transpose(0, 2, 1, 3).reshape(batch, keep * dil, GROUP_W)
            nbuf = jnp.stack([tail(k[gi]), tail(v[gi])], axis=2).reshape(
                batch, keep * dil, 2, HEADS_PER_GROUP, HEAD_DIM)
        else:
            o, l, nbuf = _attn_step(q, k, v, kv_bufs[gi], batch, seq, gi, dil)
        outs.append(o)
        lses.append(l)
        new_kv.append(nbuf)

    if conv_state is None:
        conv_state = jnp.zeros((batch, 3, SSD_CONV_DIM), F32)
        ssm_state = jnp.zeros((batch, SSD_HEADS, HEAD_DIM, SSD_STATE), F32)
    y_ssd, new_conv, new_ssm = _ssd(xbc, z, dt_raw, conv_state, ssm_state, batch, seq, p["conv_w"], p["conv_b"],
                                    p["dt_bias"], p["A_log"], p["D_skip"], p["ssd_norm_w"])

    h, hn, ri, rf, counts = _merge(x2d, outs, lses, y_ssd, g_a, g_s, p["w_attn_br"], p["w_ssd_br"], p["b_gate"],
                                   p["w_out"], p["norm2_w"], p["w_router"], p["b_router"], prompt)
    return (h, hn, ri, rf, counts), new_kv, new_conv, new_ssm


def _pack_w_in(w_in):
    return w_in.T.astype(BF16)


def kernel(x_prompt, x_sample, cache_kv_w128, cache_kv_w512, cache_kv_w2048, state_conv, state_ssm, norm1_w, w_in, w_attn_br, w_ssd_br, b_gate, w_out, conv_w, conv_b, dt_bias, A_log, D_skip, ssd_norm_w, norm2_w, w_router_coarse, b_router_coarse, w_router_fine, b_router_fine, w_expert_gate, w_expert_up, w_expert_down, final_norm_w):
    depth = norm1_w.shape[0]
    assert depth == 1, "the final norm is fused into the layer's last kernel"
    l = 0
    rpad = LANES - N_EXPERTS - N_GROUPS_E
    p = dict(
        norm1_w=norm1_w[l], w_in=_pack_w_in(w_in[l]),
        w_attn_br=w_attn_br[l].astype(BF16), w_ssd_br=w_ssd_br[l].astype(BF16), b_gate=b_gate[l],
        w_out=w_out[l].astype(BF16), conv_w=conv_w[l], conv_b=conv_b[l], dt_bias=dt_bias[l], A_log=A_log[l],
        D_skip=D_skip[l], ssd_norm_w=ssd_norm_w[l], norm2_w=norm2_w[l],
        w_router=jnp.pad(jnp.concatenate([w_router_fine[l], w_router_coarse[l]], axis=1),
                         ((0, 0), (0, rpad))).astype(BF16),
        b_router=jnp.pad(jnp.concatenate([b_router_fine[l], b_router_coarse[l]]), (0, rpad)).reshape(1, LANES),
        w_eg=w_expert_gate[l], w_eu=w_expert_up[l], w_ed=w_expert_down[l], final_norm_w=final_norm_w,
    )
    g_p, kv_p, c_p, st_p = _layer(x_prompt, 0, None, None, None, p)
    bufs = (cache_kv_w128[l], cache_kv_w512[l], cache_kv_w2048[l])
    g_s, kv_s, c_s, st_s = _layer(x_sample, PAST_LEN, bufs, state_conv[l], state_ssm[l], p)
    y_p, y_s = _moe_and_final([g_p, g_s], p["w_eg"], p["w_eu"], p["w_ed"], p["final_norm_w"])
    y_p = y_p.reshape(x_prompt.shape)
    y_s = y_s.reshape(x_sample.shape)
    lead = lambda t: t[None]
    return (y_p, y_s, lead(kv_p[0]), lead(kv_p[1]), lead(kv_p[2]), lead(c_p), lead(st_p),
            lead(kv_s[0]), lead(kv_s[1]), lead(kv_s[2]), lead(c_s), lead(st_s))
```

```python
import functools

import jax
import jax.numpy as jnp
from jax import lax
from jax.experimental import pallas as pl
from jax.experimental.pallas import tpu as pltpu

F32 = jnp.float32
BF16 = jnp.bfloat16
I32 = jnp.int32

D_MODEL = 1024
HEAD_DIM = 64
ATTN_GROUPS = ((128, 1), (512, 4), (2048, 16))
SPAN = 128
HEADS_PER_GROUP = 4
GROUP_W = HEADS_PER_GROUP * HEAD_DIM
ATTN_W = GROUP_W * len(ATTN_GROUPS)
ROPE_THETA = 10000.0
PAST_LEN = 8192
SSD_INNER = 1024
SSD_HEADS = 16
SSD_STATE = 128
SSD_CONV_DIM = 1536
SSD_CHUNK = 128
SSD_CHUNKS_PER_STEP = 4
N_GROUPS_E = 4
EXPERTS_PER_GROUP = 8
N_EXPERTS = 32
D_FF = 512
EPS = 1e-6
LANES = 128
SUBLANES = 8
NEG = -1e30

_OFF_Q, _OFF_K, _OFF_V, _OFF_Z, _OFF_XBC, _OFF_DT, _OFF_GA, _OFF_GS = 0, 768, 1536, 2304, 3328, 4864, 4880, 5904
_W_IN_COLS = 6928

ROW_TILE = 512
MERGE_SPLIT = 1
INPROJ_TILE = 512
EXPERT_TILE = 512
VMEM_LIMIT = 56 * 1024 * 1024


def _cparams(sem, **kw):
    return pltpu.CompilerParams(dimension_semantics=sem, vmem_limit_bytes=VMEM_LIMIT, **kw)


def _sigmoid(x):
    return 1.0 / (1.0 + jnp.exp(-x))


ROW_CHUNKS = D_MODEL // LANES


def _store_row_tiles(ref, val):
    rows = val.shape[0]
    for c in range(ROW_CHUNKS):
        ref[pl.ds(c, rows, stride=ROW_CHUNKS), :] = val[:, c * LANES:(c + 1) * LANES]


def _load_row_tiles(ref, first, rows):
    return jnp.concatenate(
        [ref[pl.ds(first * ROW_CHUNKS + c, rows, stride=ROW_CHUNKS), :] for c in range(ROW_CHUNKS)], axis=1)


def _nt_dot(a, b):
    return lax.dot_general(a, b, (((1,), (1,)), ((), ())), preferred_element_type=F32)


def _store_by_class(o_ref, scr_ref, val, dil):
    if dil == 1:
        o_ref[0, 0] = val
        return
    rows = val.shape[0]
    for half in range(GROUP_W // LANES):
        lanes = slice(half * LANES, (half + 1) * LANES)
        scr_ref[half] = val[:, lanes]
        for r in range(dil):
            o_ref[0, r, :, lanes] = scr_ref[half, pl.ds(r, rows // dil, stride=dil), :]


def _inproj_kernel(x_ref, nw_ref, w_ref, cos_ref, sin_ref, *refs, class_major):
    if class_major:
        qkv_refs, (z_ref, xbc_ref, ga_ref, gs_ref, dt_ref, scr_ref) = refs[:9], refs[9:]
    else:
        qkv_refs, (z_ref, xbc_ref, ga_ref, gs_ref, dt_ref) = refs[:3], refs[3:]
    x = x_ref[...]
    xn = x * lax.rsqrt(jnp.mean(x * x, axis=-1, keepdims=True) + EPS) * nw_ref[...]
    xb = xn.astype(BF16)
    cos = cos_ref[...]
    sin = sin_ref[...]
    lane = lax.broadcasted_iota(I32, cos.shape, 1)
    first_half = (lane % HEAD_DIM) < (HEAD_DIM // 2)

    def mm(lo, hi):
        return _nt_dot(xb, w_ref[lo:hi, :])

    def rope(uc):
        ur = jnp.where(first_half, pltpu.roll(uc, LANES - HEAD_DIM // 2, 1), pltpu.roll(uc, HEAD_DIM // 2, 1))
        return uc * cos + ur * sin

    def emit(which, base, roped):
        u = mm(base, base + ATTN_W)
        chunks = [u[:, c * LANES:(c + 1) * LANES] for c in range(ATTN_W // LANES)]
        if roped:
            chunks = [rope(uc) for uc in chunks]
        if class_major:
            for gi, (_, dil) in enumerate(ATTN_GROUPS):
                val = jnp.concatenate(chunks[2 * gi:2 * gi + 2], axis=1)
                _store_by_class(qkv_refs[3 * which + gi], scr_ref.at[3 * which + gi], val, dil)
        else:
            for c, uc in enumerate(chunks):
                qkv_refs[which][:, c * LANES:(c + 1) * LANES] = uc

    emit(0, _OFF_Q, True)
    emit(1, _OFF_K, True)
    emit(2, _OFF_V, False)
    z_ref[...] = mm(_OFF_Z, _OFF_XBC)
    xbc_ref[...] = mm(_OFF_XBC, _OFF_DT)
    ga_ref[...] = mm(_OFF_GA, _OFF_GS)
    gs_ref[...] = mm(_OFF_GS, _W_IN_COLS)
    dt_ref[...] = jnp.where(lane < SSD_HEADS, mm(_OFF_DT, _OFF_DT + LANES), 0.0)


def _rope_tables(pos):
    half = HEAD_DIM // 2
    inv_freq = ROPE_THETA ** (-jnp.arange(half, dtype=F32) * (2.0 / HEAD_DIM))
    ang = pos.astype(F32)[:, None] * inv_freq[None, :]
    lane = jnp.arange(LANES)
    sign = jnp.where((lane % HEAD_DIM) < half, -1.0, 1.0).astype(F32)
    widen = lambda t: jnp.tile(t, (1, LANES // half))
    return widen(jnp.cos(ang)), widen(jnp.sin(ang)) * sign[None, :]


def _inproj(x2d, batch, seq_len, pos0, norm_w, w_packed, class_major):
    n = x2d.shape[0]
    tm = min(INPROJ_TILE, n)
    cos, sin = _rope_tables(pos0 + jnp.arange(seq_len, dtype=I32))
    if seq_len < tm:
        cos = jnp.tile(cos, (tm // seq_len, 1))
        sin = jnp.tile(sin, (tm // seq_len, 1))
    tab_blocks = cos.shape[0] // tm
    row = lambda i: (i, 0)
    fixed = lambda i: (0, 0)
    tab = lambda i: (i % tab_blocks, 0)
    widths = (SSD_INNER, SSD_CONV_DIM, D_MODEL, D_MODEL, LANES)
    out_specs = [pl.BlockSpec((tm, w), row) for w in widths]
    out_shape = [jax.ShapeDtypeStruct((n, w), F32) for w in widths]
    scratch = []
    if class_major:
        assert seq_len % tm == 0
        per_seq = seq_len // tm
        cls = lambda i: (i // per_seq, 0, i % per_seq, 0)
        dils = [d for _, d in ATTN_GROUPS] * 3
        out_specs = [pl.BlockSpec((1, d, tm // d, GROUP_W), cls) for d in dils] + out_specs
        out_shape = [jax.ShapeDtypeStruct((batch, d, seq_len // d, GROUP_W), F32) for d in dils] + out_shape
        scratch = [pltpu.VMEM((len(dils), GROUP_W // LANES, tm, LANES), F32)]
    else:
        out_specs = [pl.BlockSpec((tm, ATTN_W), row)] * 3 + out_specs
        out_shape = [jax.ShapeDtypeStruct((n, ATTN_W), F32)] * 3 + out_shape
    res = pl.pallas_call(
        functools.partial(_inproj_kernel, class_major=class_major),
        grid=(n // tm,),
        in_specs=[pl.BlockSpec((tm, D_MODEL), row),
                  pl.BlockSpec((1, D_MODEL), fixed),
                  pl.BlockSpec((_W_IN_COLS, D_MODEL), fixed, pipeline_mode=pl.Buffered(1)),
                  pl.BlockSpec((tm, LANES), tab),
                  pl.BlockSpec((tm, LANES), tab)],
        out_specs=out_specs,
        out_shape=out_shape,
        scratch_shapes=scratch,
        compiler_params=_cparams(("parallel",)),
        name="inproj",
    )(x2d, norm_w.reshape(1, D_MODEL), w_packed, cos, sin)
    if class_major:
        return (tuple(res[0:3]), tuple(res[3:6]), tuple(res[6:9])) + tuple(res[9:])
    return tuple(res)


ATTN_QBLOCKS = 4
ATTN_LAG = 16


def _attn_prompt_kernel(q_ref, kp_ref, kc_ref, vp_ref, vc_ref, o_ref, l_ref, k_scr, v_scr, s_scr, p_scr, r_scr,
                        *, qblocks):
    c = pl.program_id(2)
    i = lax.broadcasted_iota(I32, (SPAN, 2 * SPAN), 0)
    j = lax.broadcasted_iota(I32, (SPAN, 2 * SPAN), 1)
    band = (j >= i) & (j <= i + SPAN)
    first_band = band & ((j >= SPAN) | (c > 0))
    k_scr[0:SPAN, :] = kp_ref[0, 0].astype(BF16)
    k_scr[SPAN:, :] = kc_ref[0, 0].astype(BF16)
    v_scr[0:SPAN, :] = vp_ref[0, 0].astype(BF16)
    v_scr[SPAN:, :] = vc_ref[0, 0].astype(BF16)
    units = [(b, h) for b in range(qblocks) for h in range(HEADS_PER_GROUP)]

    def scores(u):
        b, h = units[u]
        hs = slice(h * HEAD_DIM, (h + 1) * HEAD_DIM)
        qh = (q_ref[0, 0, b * SPAN:(b + 1) * SPAN, hs] * (HEAD_DIM ** -0.5)).astype(BF16)
        s_scr[u] = _nt_dot(qh, k_scr[b * SPAN:(b + 2) * SPAN, hs])

    def softmax(u):
        b, h = units[u]
        hs = slice(h * HEAD_DIM, (h + 1) * HEAD_DIM)
        s = jnp.where(first_band if b == 0 else band, s_scr[u], NEG)
        m = jnp.max(jnp.maximum(s[:, :SPAN], s[:, SPAN:]), axis=-1, keepdims=True)
        p = jnp.exp(s - m)
        den = jnp.sum(p[:, :SPAN] + p[:, SPAN:], axis=-1, keepdims=True)
        p_scr[u] = p.astype(BF16)
        r_scr[u] = jnp.broadcast_to(1.0 / den, (SPAN, HEAD_DIM))
        l_ref[0, 0, b * SPAN:(b + 1) * SPAN, hs] = jnp.broadcast_to(m + jnp.log(den), (SPAN, HEAD_DIM))

    def weighted_values(u):
        b, h = units[u]
        hs = slice(h * HEAD_DIM, (h + 1) * HEAD_DIM)
        acc = jnp.dot(p_scr[u], v_scr[b * SPAN:(b + 2) * SPAN, hs], preferred_element_type=F32)
        o_ref[0, 0, b * SPAN:(b + 1) * SPAN, hs] = acc * r_scr[u]

    n = len(units)
    for t in range(n + 2 * ATTN_LAG):
        if t < n:
            scores(t)
        if 0 <= t - ATTN_LAG < n:
            softmax(t - ATTN_LAG)
        if 0 <= t - 2 * ATTN_LAG < n:
            weighted_values(t - 2 * ATTN_LAG)


def _attn_prompt(q, k, v, dil):
    batch, _, n_cls, _ = q.shape
    qblocks = min(ATTN_QBLOCKS, n_cls // SPAN)
    tq = qblocks * SPAN
    assert n_cls % tq == 0
    cur = lambda b, r, c: (b, r, c, 0)
    prev = lambda b, r, c: (b, r, jnp.maximum(c * qblocks - 1, 0), 0)
    big = pl.BlockSpec((1, 1, tq, GROUP_W), cur)
    small = pl.BlockSpec((1, 1, SPAN, GROUP_W), prev)
    return pl.pallas_call(
        functools.partial(_attn_prompt_kernel, qblocks=qblocks),
        grid=(batch, dil, n_cls // tq),
        in_specs=[big, small, big, small, big],
        out_specs=[big, big],
        out_shape=[jax.ShapeDtypeStruct(q.shape, F32)] * 2,
        scratch_shapes=[pltpu.VMEM((SPAN + tq, GROUP_W), BF16),
                        pltpu.VMEM((SPAN + tq, GROUP_W), BF16),
                        pltpu.VMEM((qblocks * HEADS_PER_GROUP, SPAN, 2 * SPAN), F32),
                        pltpu.VMEM((qblocks * HEADS_PER_GROUP, SPAN, 2 * SPAN), BF16),
                        pltpu.VMEM((qblocks * HEADS_PER_GROUP, SPAN, HEAD_DIM), F32)],
        compiler_params=_cparams(("parallel", "parallel", "arbitrary")),
        name=f"attn_prompt_d{dil}",
    )(q, k, k, v, v)


STEP_ROWS_PER_CALL = 2048


def _attn_step_kernel(q_ref, kn_ref, vn_ref, buf_ref, o_ref, l_ref, nbuf_ref, *, buf_len, dil, n_new, seqs):
    t = lax.broadcasted_iota(I32, (n_new, buf_len), 0)
    i = lax.broadcasted_iota(I32, (n_new, buf_len), 1)
    delta = buf_len + t - i
    valid_b = delta <= dil * SPAN
    first_new = LANES - n_new
    tn = lax.broadcasted_iota(I32, (n_new, LANES), 0)
    un = lax.broadcasted_iota(I32, (n_new, LANES), 1) - first_new
    dn = tn - un
    valid_n = (dn >= 0) & (un >= 0)
    if dil > 1:
        valid_b = valid_b & ((delta & (dil - 1)) == 0)
        valid_n = valid_n & ((dn & (dil - 1)) == 0)

    def to_columns(x):
        xp = jnp.concatenate([x, jnp.zeros((LANES - n_new, GROUP_W), F32)], axis=0)
        xt = jnp.concatenate([xp[:, c * LANES:(c + 1) * LANES].T for c in range(GROUP_W // LANES)], axis=0)
        return pltpu.roll(xt, first_new, 1)

    is_new = lax.broadcasted_iota(I32, (HEAD_DIM, LANES), 1) >= first_new
    tail = slice(buf_len - LANES, buf_len)
    for b, h in [(b, h) for b in range(seqs) for h in range(HEADS_PER_GROUP)]:
        if h == 0:
            q = q_ref[b] * (HEAD_DIM ** -0.5)
            new_cols = (to_columns(kn_ref[b]), to_columns(vn_ref[b]))
        hs = slice(h * HEAD_DIM, (h + 1) * HEAD_DIM)
        qh = q[:, hs].astype(BF16)
        kt = buf_ref[b, 0, h]
        vt = buf_ref[b, 1, h]
        knt = new_cols[0][hs]
        vnt = new_cols[1][hs]
        sb = jnp.where(valid_b, jnp.dot(qh, kt.astype(BF16), preferred_element_type=F32), NEG)
        sn = jnp.where(valid_n, jnp.dot(qh, knt.astype(BF16), preferred_element_type=F32), NEG)
        m = jnp.maximum(jnp.max(sb, axis=-1, keepdims=True), jnp.max(sn, axis=-1, keepdims=True))
        pb = jnp.exp(sb - m)
        pn = jnp.exp(sn - m)
        den = jnp.sum(pb, axis=-1, keepdims=True) + jnp.sum(pn, axis=-1, keepdims=True)
        acc = _nt_dot(pb.astype(BF16), vt.astype(BF16)) + _nt_dot(pn.astype(BF16), vnt.astype(BF16))
        o_ref[b, :, hs] = acc / den
        l_ref[b, :, hs] = jnp.broadcast_to(m + jnp.log(den), (n_new, HEAD_DIM))
        for kv, (old, new) in enumerate(((kt, knt), (vt, vnt))):
            shifted = pltpu.roll(old, buf_len - n_new, 1)
            nbuf_ref[b, kv, h] = shifted
            nbuf_ref[b, kv, h, :, tail] = jnp.where(is_new, new, shifted[:, tail])


def _attn_step(q, k, v, buf, batch, n_new, gi, dil):
    buf_len = buf.shape[1]
    assert dil & (dil - 1) == 0 and buf_len >= dil * SPAN and n_new % SUBLANES == 0 and buf_len % LANES == 0
    view = lambda t: t.reshape(batch, n_new, ATTN_W)
    seqs = max(1, min(batch, STEP_ROWS_PER_CALL // buf_len))
    assert batch % seqs == 0
    tok = pl.BlockSpec((seqs, n_new, GROUP_W), lambda b: (b, 0, gi))
    full = pl.BlockSpec((seqs, 2, HEADS_PER_GROUP, HEAD_DIM, buf_len), lambda b: (b, 0, 0, 0, 0))
    osp = pl.BlockSpec((seqs, n_new, GROUP_W), lambda b: (b, 0, 0))
    o, l, nbuf = pl.pallas_call(
        functools.partial(_attn_step_kernel, buf_len=buf_len, dil=dil, n_new=n_new, seqs=seqs),
        grid=(batch // seqs,),
        in_specs=[tok, tok, tok, full],
        out_specs=[osp, osp, full],
        out_shape=[jax.ShapeDtypeStruct((batch, n_new, GROUP_W), F32)] * 2
        + [jax.ShapeDtypeStruct((batch, 2, HEADS_PER_GROUP, HEAD_DIM, buf_len), F32)],
        compiler_params=_cparams(("parallel",)),
        name=f"attn_step_d{dil}",
    )(view(q), view(k), view(v), jnp.transpose(buf, (0, 2, 3, 4, 1)))
    return (o.reshape(batch * n_new, GROUP_W), l.reshape(batch * n_new, GROUP_W),
            jnp.transpose(nbuf, (0, 4, 1, 2, 3)))


def _split3(a):
    a1 = a.astype(BF16)
    r1 = a - a1.astype(F32)
    a2 = r1.astype(BF16)
    a3 = (r1 - a2.astype(F32)).astype(BF16)
    return a1, a2, a3


def _ssd_kernel(*refs, n_valid, n_steps, per_step, separate):
    for sub in range(per_step):
        _ssd_chunk(*refs, n_valid=n_valid, n_steps=n_steps, per_step=per_step, separate=separate, sub=sub)


def _ssd_chunk(xbc_ref, z_ref, dt_ref, cst_ref, h0_ref, cw_ref, cb_ref, dtb_ref, alog_ref, dfull_ref, nw_ref,
               tri_ref, expand_ref, y_ref, cout_ref, hout_ref, xpad_ref, h_ref,
               *, n_valid, n_steps, per_step, separate, sub):
    seq = sub if separate else 0
    c = pl.program_id(1)
    lc = SSD_CHUNK
    pad = SUBLANES
    n_slabs = SSD_CONV_DIM // LANES
    tok = slice(sub * n_valid, (sub + 1) * n_valid)

    def carry_rows():
        for j in range(n_slabs):
            xpad_ref[j, 0:pad, :] = xpad_ref[j, lc:lc + pad, :]

    def start_sequence():
        for j in range(n_slabs):
            xpad_ref[j, 0:pad, :] = cst_ref[seq, :, j * LANES:(j + 1) * LANES]
        h_ref[...] = h0_ref[seq]

    if separate:
        start_sequence()
    elif sub == 0:
        pl.when(c == 0)(start_sequence)
        pl.when(c > 0)(carry_rows)
    else:
        carry_rows()

    if n_valid == lc:
        z = z_ref[tok, :]
        dtr = dt_ref[tok, :]
    else:
        fill = lambda w: jnp.zeros((lc - n_valid, w), F32)
        z = jnp.concatenate([z_ref[tok, :], fill(SSD_INNER)], axis=0)
        dtr = jnp.concatenate([dt_ref[tok, :], fill(LANES)], axis=0)

    slabs = []
    for j in range(n_slabs):
        cols = slice(j * LANES, (j + 1) * LANES)
        xpad_ref[j, pad:pad + n_valid, :] = xbc_ref[tok, cols]
        if n_valid < lc:
            xpad_ref[j, pad + n_valid:pad + lc, :] = jnp.zeros((lc - n_valid, LANES), F32)
        xc = cb_ref[:, cols]
        for tap in range(4):
            xc = xc + xpad_ref[j, pl.ds(pad - 3 + tap, lc), :] * cw_ref[tap:tap + 1, cols]
        slabs.append(xc * _sigmoid(xc))
        cout_ref[seq, :, cols] = xpad_ref[j, pl.ds(pad + n_valid - 3, 3), :]
    xs = jnp.concatenate(slabs[:SSD_INNER // LANES], axis=1)
    bm = slabs[SSD_INNER // LANES:SSD_INNER // LANES + 2]
    cm = slabs[SSD_INNER // LANES + 2:]

    row = lax.broadcasted_iota(I32, (lc, lc), 0)
    col = lax.broadcasted_iota(I32, (lc, lc), 1)
    causal = row >= col
    dtv = dtr + dtb_ref[...]
    dt = jnp.maximum(dtv, 0.0) + jnp.log1p(jnp.exp(-jnp.abs(dtv)))
    if n_valid < lc:
        dt = jnp.where(row < n_valid, dt, 0.0)
    a = dt * (-jnp.exp(alog_ref[...]))
    tri = tri_ref[...]
    a_cs = sum(jnp.dot(tri, p, preferred_element_type=F32) for p in _split3(a))
    a_cs_t = a_cs.T
    expand = expand_ref[...]
    a_full = sum(jnp.dot(p, expand, preferred_element_type=F32) for p in _split3(a_cs))
    dt_full = sum(jnp.dot(p, expand, preferred_element_type=F32) for p in _split3(dt))
    xdt = xs * dt_full
    xd = xdt * jnp.exp(a_full[lc - 1:lc, :] - a_full)
    grow = jnp.exp(a_full)
    xdt_b = xdt.astype(BF16)
    xd_t = jnp.concatenate([xd[:, k * LANES:(k + 1) * LANES].T for k in range(SSD_INNER // LANES)],
                           axis=0).astype(BF16)
    lane = lax.broadcasted_iota(I32, (lc, LANES), 1)
    low_half = lane < HEAD_DIM
    zero_b = jnp.zeros((lc, LANES), BF16)
    half = SSD_INNER // 2

    y_parts = []
    for g in range(2):
        bg = bm[g].astype(BF16)
        cg = cm[g].astype(BF16)
        cb = jnp.where(causal, _nt_dot(cg, bg), 0.0)
        h_grp = h_ref[g * half:(g + 1) * half, :]
        y_off = _nt_dot(cg, h_grp.astype(BF16)) * grow[:, g * half:(g + 1) * half]
        for pair in range(4):
            e0 = g * 8 + pair * 2
            rows = slice(e0 * HEAD_DIM, (e0 + 2) * HEAD_DIM)
            ms, keep = [], []
            for k in range(2):
                e = e0 + k
                seg = a_cs[:, e:e + 1] - a_cs_t[e:e + 1, :]
                ms.append((cb * jnp.exp(jnp.minimum(seg, 0.0))).astype(BF16))
                keep.append(jnp.broadcast_to(jnp.exp(a_cs[lc - 1:lc, e:e + 1]), (HEAD_DIM, SSD_STATE)))
            pair_b = xdt_b[:, rows]
            rhs = jnp.concatenate([jnp.where(low_half, pair_b, zero_b), jnp.where(low_half, zero_b, pair_b)], axis=0)
            y_parts.append(jnp.dot(jnp.concatenate(ms, axis=1), rhs, preferred_element_type=F32)
                           + y_off[:, pair * LANES:(pair + 1) * LANES])
            st = jnp.dot(xd_t[rows, :], bg, preferred_element_type=F32)
            h_ref[rows, :] = h_grp[pair * LANES:(pair + 1) * LANES, :] * jnp.concatenate(keep, axis=0) + st

    y = jnp.concatenate(y_parts, axis=1) + xs * dfull_ref[...]
    gate = y * (z * _sigmoid(z))
    for g in range(2):
        gg = gate[:, g * half:(g + 1) * half]
        gg = gg * lax.rsqrt(jnp.mean(gg * gg, axis=-1, keepdims=True) + EPS)
        y_ref[tok, g * half:(g + 1) * half] = (gg * nw_ref[:, g * half:(g + 1) * half])[0:n_valid]

    if separate:
        hout_ref[seq] = h_ref[...]
    elif sub == per_step - 1:
        @pl.when(c == n_steps - 1)
        def _():
            hout_ref[0] = h_ref[...]


def _ssd(xbc, z, dt_raw, conv_state, ssm_state, batch, seq, conv_w, conv_b, dt_bias, a_log, d_skip, norm_w):
    n_valid = min(seq, SSD_CHUNK)
    n_chunks = seq // n_valid
    assert n_valid % SUBLANES == 0 and seq % n_valid == 0
    padl = lambda t: jnp.pad(t.reshape(1, SSD_HEADS), ((0, 0), (0, LANES - SSD_HEADS)))
    cst = jnp.pad(conv_state, ((0, 0), (SUBLANES - 3, 0), (0, 0)))
    tri = (jnp.arange(SSD_CHUNK)[:, None] >= jnp.arange(SSD_CHUNK)[None, :]).astype(BF16)
    expand = (jnp.arange(LANES)[:, None] == jnp.arange(SSD_INNER)[None, :] // HEAD_DIM).astype(BF16)
    separate = n_chunks == 1 and batch % SSD_CHUNKS_PER_STEP == 0
    per_step = SSD_CHUNKS_PER_STEP if (separate or n_chunks % SSD_CHUNKS_PER_STEP == 0) else 1
    n_steps = 1 if separate else n_chunks // per_step
    n_outer = batch // per_step if separate else batch
    per_state = per_step if separate else 1
    step_rows = per_step * n_valid
    tokrow = lambda b, c: (b * n_steps + c, 0)
    fixed = lambda b, c: (0, 0)
    per_b3 = lambda b, c: (b, 0, 0)
    state = pl.BlockSpec((per_state, SSD_INNER, SSD_STATE), per_b3)
    kern = functools.partial(_ssd_kernel, n_valid=n_valid, n_steps=n_steps, per_step=per_step, separate=separate)
    y, cout, hout = pl.pallas_call(
        kern,
        grid=(n_outer, n_steps),
        in_specs=[pl.BlockSpec((step_rows, SSD_CONV_DIM), tokrow),
                  pl.BlockSpec((step_rows, SSD_INNER), tokrow),
                  pl.BlockSpec((step_rows, LANES), tokrow),
                  pl.BlockSpec((per_state, SUBLANES, SSD_CONV_DIM), per_b3),
                  state,
                  pl.BlockSpec((4, SSD_CONV_DIM), fixed),
                  pl.BlockSpec((1, SSD_CONV_DIM), fixed),
                  pl.BlockSpec((1, LANES), fixed),
                  pl.BlockSpec((1, LANES), fixed),
                  pl.BlockSpec((1, SSD_INNER), fixed),
                  pl.BlockSpec((1, SSD_INNER), fixed),
                  pl.BlockSpec((SSD_CHUNK, SSD_CHUNK), fixed),
                  pl.BlockSpec((LANES, SSD_INNER), fixed)],
        out_specs=[pl.BlockSpec((step_rows, SSD_INNER), tokrow),
                   pl.BlockSpec((per_state, 3, SSD_CONV_DIM), per_b3),
                   state],
        out_shape=[jax.ShapeDtypeStruct((batch * seq, SSD_INNER), F32),
                   jax.ShapeDtypeStruct((batch, 3, SSD_CONV_DIM), F32),
                   jax.ShapeDtypeStruct((batch, SSD_INNER, SSD_STATE), F32)],
        scratch_shapes=[pltpu.VMEM((SSD_CONV_DIM // LANES, SSD_CHUNK + SUBLANES, LANES), F32),
                        pltpu.VMEM((SSD_INNER, SSD_STATE), F32)],
        compiler_params=_cparams(("parallel", "arbitrary")),
        name="ssd",
    )(xbc, z, dt_raw, cst, ssm_state.reshape(batch, SSD_INNER, SSD_STATE), conv_w, conv_b.reshape(1, SSD_CONV_DIM),
      padl(dt_bias), padl(a_log), jnp.repeat(d_skip, HEAD_DIM).reshape(1, SSD_INNER), norm_w.reshape(1, SSD_INNER),
      tri, expand)
    return y, cout, hout.reshape(batch, SSD_HEADS, HEAD_DIM, SSD_STATE)


def _merge_kernel(x_ref, o0_ref, o1_ref, o2_ref, l0_ref, l1_ref, l2_ref, ys_ref, ga_ref, gs_ref,
                  wab_ref, wsb_ref, bg_ref, wo_ref, n2_ref, wr_ref, br_ref,
                  h_ref, hn_ref, ri_ref, rf_ref, cnt_ref, carry_ref, *scr, class_major):
    step = pl.program_id(0)

    @pl.when(step == 0)
    def _():
        carry_ref[...] = jnp.zeros_like(carry_ref)

    def load(ref, gi, slab):
        if not class_major:
            return ref[...]
        dil = ATTN_GROUPS[gi][1]
        if dil == 1:
            return ref[0, 0]
        rows = ref.shape[1] * ref.shape[2]
        halves = []
        for half in range(GROUP_W // LANES):
            for r in range(dil):
                scr[0][slab, half, pl.ds(r, rows // dil, stride=dil), :] = ref[0, r, :, half * LANES:(half + 1) * LANES]
            halves.append(scr[0][slab, half])
        return jnp.concatenate(halves, axis=1)

    lses = (load(l0_ref, 0, 0), load(l1_ref, 1, 0), load(l2_ref, 2, 1))
    outs = (load(o0_ref, 0, 0), load(o1_ref, 1, 2), load(o2_ref, 2, 3))
    tm = x_ref.shape[0]
    sub = tm // MERGE_SPLIT
    lane = lax.broadcasted_iota(I32, (sub, LANES), 1)
    big = jnp.int32(LANES)
    r = lax.broadcasted_iota(I32, (sub, sub), 0)
    s = lax.broadcasted_iota(I32, (sub, sub), 1)
    earlier = (r > s).astype(BF16)

    def top(vals):
        v = jnp.max(vals, axis=-1, keepdims=True)
        idx = jnp.min(jnp.where(vals == v, lane, big), axis=-1, keepdims=True)
        return v, idx

    carry = carry_ref[0:1, :]
    for part in range(MERGE_SPLIT):
        rows = slice(part * sub, (part + 1) * sub)
        l0, l1, l2 = (t[rows] for t in lses)
        m = jnp.maximum(jnp.maximum(l0, l1), l2)
        w0, w1, w2 = jnp.exp(l0 - m), jnp.exp(l1 - m), jnp.exp(l2 - m)
        y_attn = (w0 * outs[0][rows] + w1 * outs[1][rows] + w2 * outs[2][rows]) / (w0 + w1 + w2)
        pa = jnp.dot(y_attn.astype(BF16), wab_ref[...], preferred_element_type=F32)
        ps = jnp.dot(ys_ref[rows, :].astype(BF16), wsb_ref[...], preferred_element_type=F32)
        merged = (_sigmoid(ga_ref[rows, :] + bg_ref[0:1, :]) * pa + _sigmoid(gs_ref[rows, :] + bg_ref[1:2, :]) * ps)
        h = x_ref[rows, :] + jnp.dot(merged.astype(BF16), wo_ref[...], preferred_element_type=F32)
        h_ref[rows, :] = h
        hn = h * lax.rsqrt(jnp.mean(h * h, axis=-1, keepdims=True) + EPS) * n2_ref[...]
        hnb = hn.astype(BF16)
        _store_row_tiles(hn_ref.at[pl.ds(part * sub * ROW_CHUNKS, sub * ROW_CHUNKS), :], hn)

        logits = jnp.dot(hnb, wr_ref[...], preferred_element_type=F32) + br_ref[...]
        is_coarse = (lane >= N_EXPERTS) & (lane < N_EXPERTS + N_GROUPS_E)
        lc = jnp.where(is_coarse, logits, NEG)
        mc, ic = top(lc)
        p_grp = 1.0 / jnp.sum(jnp.exp(lc - mc), axis=-1, keepdims=True)
        lo = (ic - N_EXPERTS) * EXPERTS_PER_GROUP
        lf = jnp.where((lane >= lo) & (lane < lo + EXPERTS_PER_GROUP), logits, NEG)
        v1, i1 = top(lf)
        v2, i2 = top(jnp.where(lane == i1, NEG, lf))
        e2 = jnp.exp(v2 - v1)
        g1 = p_grp / (1.0 + e2)
        g2 = p_grp * e2 / (1.0 + e2)

        oh1 = lane == i1
        oh2 = lane == i2
        cnt = oh1.astype(F32) + oh2.astype(F32)
        before = jnp.dot(earlier, cnt.astype(BF16), preferred_element_type=F32) + carry
        r1 = jnp.sum(jnp.where(oh1, before, 0.0), axis=-1, keepdims=True)
        r2 = jnp.sum(jnp.where(oh2, before, 0.0), axis=-1, keepdims=True)
        carry = carry + jnp.sum(cnt, axis=0, keepdims=True)

        ri = jnp.where(lane == 0, i1, jnp.where(lane == 1, i2, 0))
        ri = jnp.where(lane == 2, r1.astype(I32), jnp.where(lane == 3, r2.astype(I32), ri))
        ri_ref[rows, :] = ri
        rf_ref[rows, :] = jnp.where(lane == 0, g1, jnp.where(lane == 1, g2, 0.0))

    carry_ref[...] = jnp.broadcast_to(carry, carry_ref.shape)
    cnt_ref[...] = jnp.broadcast_to(carry, cnt_ref.shape).astype(I32)


def _merge(x2d, outs, lses, y_ssd, g_a, g_s, wab, wsb, b_gate, wo, norm2_w, w_router, b_router, class_major):
    n = x2d.shape[0]
    tm = min(ROW_TILE, n)
    row = lambda i: (i, 0)
    fixed = lambda i: (0, 0)
    wide = pl.BlockSpec((tm, D_MODEL), row)
    info = pl.BlockSpec((tm, LANES), row)
    if class_major:
        per_seq = outs[0].shape[2] // tm
        cls = lambda i: (i // per_seq, 0, i % per_seq, 0)
        grps = [pl.BlockSpec((1, d, tm // d, GROUP_W), cls) for _, d in ATTN_GROUPS]
        scratch = [pltpu.VMEM((4, GROUP_W // LANES, tm, LANES), F32)]
    else:
        grps = [pl.BlockSpec((tm, GROUP_W), row)] * 3
        scratch = []
    return pl.pallas_call(
        functools.partial(_merge_kernel, class_major=class_major),
        grid=(n // tm,),
        in_specs=[wide, *grps, *grps, wide, wide, wide,
                  pl.BlockSpec((GROUP_W, D_MODEL), fixed),
                  pl.BlockSpec((SSD_INNER, D_MODEL), fixed),
                  pl.BlockSpec((2, D_MODEL), fixed),
                  pl.BlockSpec((D_MODEL, D_MODEL), fixed),
                  pl.BlockSpec((1, D_MODEL), fixed),
                  pl.BlockSpec((D_MODEL, LANES), fixed),
                  pl.BlockSpec((1, LANES), fixed)],
        out_specs=[wide, pl.BlockSpec((tm * ROW_CHUNKS, LANES), row), info, info,
                   pl.BlockSpec((SUBLANES, LANES), fixed)],
        out_shape=[jax.ShapeDtypeStruct((n, D_MODEL), F32),
                   jax.ShapeDtypeStruct((n * ROW_CHUNKS, LANES), F32),
                   jax.ShapeDtypeStruct((n, LANES), I32),
                   jax.ShapeDtypeStruct((n, LANES), F32),
                   jax.ShapeDtypeStruct((SUBLANES, LANES), I32)],
        scratch_shapes=[pltpu.VMEM((SUBLANES, LANES), F32)] + scratch,
        compiler_params=_cparams(("arbitrary",)),
        name="merge_out",
    )(x2d, *outs, *lses, y_ssd, g_a, g_s, wab, wsb, b_gate, wo, norm2_w.reshape(1, D_MODEL), w_router, b_router)


GATHER_TILE = 512
DISPATCH_TILE = 1024
INDEX_BATCH = 16


def _dispatch_kernel(dest_ref, pad_ref, hn_hbm, *rest, tm, first_group):
    if first_group:
        xs_hbm, ring, in_sem, sem, zeros_ref, pad_sem = rest
    else:
        _, xs_hbm, ring, in_sem, sem = rest
    i = pl.program_id(0)
    n_steps = pl.num_programs(0)
    slot = i % 3

    def load(tile):
        into = tile % 3
        return pltpu.make_async_copy(hn_hbm.at[pl.ds(tile * tm * ROW_CHUNKS, tm * ROW_CHUNKS), :], ring.at[into],
                                     in_sem.at[into])

    def wait_scatter(which):
        for _ in range(2):
            pltpu.make_async_copy(ring.at[which], xs_hbm.at[pl.ds(0, tm * ROW_CHUNKS), :], sem.at[which]).wait()

    @pl.when(i == 0)
    def _():
        load(0).start()

        @pl.when(n_steps > 1)
        def _():
            load(1).start()

    if first_group:
        @pl.when(i == 0)
        def _():
            zeros_ref[...] = jnp.zeros_like(zeros_ref)

            def fill(first_slot, size):
                first = pl.multiple_of(first_slot * ROW_CHUNKS, ROW_CHUNKS)
                return pltpu.make_async_copy(zeros_ref.at[pl.ds(0, size * ROW_CHUNKS), :],
                                             xs_hbm.at[pl.ds(first, size * ROW_CHUNKS), :], pad_sem)

            fills = []
            for e in range(N_EXPERTS):
                start, count = pad_ref[e], pad_ref[N_EXPERTS + e]
                size = EXPERT_TILE
                while size >= 1:
                    fills.append(((count & size) != 0, start + (count & ~(2 * size - 1)), size))
                    size //= 2
            tail_first, tail_tiles = pad_ref[2 * N_EXPERTS], pad_ref[2 * N_EXPERTS + 1]
            for t in range(N_EXPERTS):
                fills.append((t < tail_tiles, (tail_first + t) * EXPERT_TILE, EXPERT_TILE))
            for pred, first_slot, size in fills:
                pl.when(pred)(lambda first_slot=first_slot, size=size: fill(first_slot, size).start())
            for pred, first_slot, size in fills:
                pl.when(pred)(lambda first_slot=first_slot, size=size: fill(first_slot, size).wait())

    load(i).wait()
    for j0 in range(0, 2 * tm, INDEX_BATCH):
        slots = [dest_ref[i * 2 * tm + j0 + u] for u in range(INDEX_BATCH)]
        for u, dst in enumerate(slots):
            tok = (j0 + u) // 2
            first = pl.multiple_of(dst * ROW_CHUNKS, ROW_CHUNKS)
            pltpu.make_async_copy(ring.at[slot, pl.ds(tok * ROW_CHUNKS, ROW_CHUNKS), :],
                                  xs_hbm.at[pl.ds(first, ROW_CHUNKS), :], sem.at[slot]).start(priority=u % 2)

    @pl.when(i > 0)
    def _():
        wait_scatter((i + 2) % 3)

    @pl.when(i + 2 < n_steps)
    def _():
        load(i + 2).start()

    @pl.when(i == n_steps - 1)
    def _():
        wait_scatter(slot)


def _dispatch(hn, dest_tiles, pad_info, xs, n_slots, tm):
    n = hn.shape[0] // ROW_CHUNKS
    first_group = xs is None
    anyspec = pl.BlockSpec(memory_space=pl.ANY)
    scratch = [pltpu.VMEM((3, tm * ROW_CHUNKS, LANES), F32), pltpu.SemaphoreType.DMA((3,)),
               pltpu.SemaphoreType.DMA((3,))]
    if first_group:
        scratch += [pltpu.VMEM((EXPERT_TILE * ROW_CHUNKS, LANES), F32), pltpu.SemaphoreType.DMA(())]
    return pl.pallas_call(
        functools.partial(_dispatch_kernel, tm=tm, first_group=first_group),
        grid_spec=pltpu.PrefetchScalarGridSpec(
            num_scalar_prefetch=2,
            grid=(n // tm,),
            in_specs=[anyspec] + ([] if first_group else [anyspec]),
            out_specs=anyspec,
            scratch_shapes=scratch),
        out_shape=jax.ShapeDtypeStruct((n_slots * ROW_CHUNKS, LANES), F32),
        input_output_aliases={} if first_group else {3: 0},
        compiler_params=_cparams(("arbitrary",), disable_bounds_checks=True, has_side_effects=True),
        name="dispatch",
    )(dest_tiles, pad_info, hn, *([] if first_group else [xs]))


def _expert_kernel(te_ref, nu_ref, first_ref, slot_ref, next_ref, x_ref, wg_hbm, wu_hbm, wd_hbm, o_ref,
                   wg32, wu32, wd32, wgb, wub, wdb, sem):
    i = pl.program_id(0)

    def fetch(expert, slot):
        return [pltpu.make_async_copy(src.at[expert], dst.at[slot], sem.at[slot])
                for src, dst in ((wg_hbm, wg32), (wu_hbm, wu32), (wd_hbm, wd32))]

    @pl.when((i < nu_ref[0]) & (first_ref[i] == 1))
    def _():
        slot = slot_ref[i]

        @pl.when(i == 0)
        def _():
            for c in fetch(te_ref[0], slot):
                c.start()

        for c in fetch(te_ref[i], slot):
            c.wait()
        wgb[...] = wg32[slot].astype(BF16)
        wub[...] = wu32[slot].astype(BF16)
        wdb[...] = wd32[slot].astype(BF16)

        @pl.when(next_ref[i] >= 0)
        def _():
            for c in fetch(next_ref[i], 1 - slot):
                c.start()

    @pl.when(i < nu_ref[0])
    def _():
        x = _load_row_tiles(x_ref, 0, EXPERT_TILE).astype(BF16)
        hg = jnp.dot(x, wgb[...], preferred_element_type=F32)
        hu = jnp.dot(x, wub[...], preferred_element_type=F32)
        hb = (hg * _sigmoid(hg)) * hu
        _store_row_tiles(o_ref, jnp.dot(hb.astype(BF16), wdb[...], preferred_element_type=F32))

    @pl.when(i >= nu_ref[0])
    def _():
        o_ref[...] = jnp.zeros_like(o_ref)


def _experts(xs, tile_expert, n_used, run_first, run_slot, run_next, w_eg, w_eu, w_ed):
    n_slots = xs.shape[0] // ROW_CHUNKS
    n_tiles = n_slots // EXPERT_TILE
    row = lambda i, *_: (i, 0)
    used_row = lambda i, te, nu, *_: (jnp.minimum(i, nu[0] - 1), 0)
    anyspec = pl.BlockSpec(memory_space=pl.ANY)
    return pl.pallas_call(
        _expert_kernel,
        grid_spec=pltpu.PrefetchScalarGridSpec(
            num_scalar_prefetch=5,
            grid=(n_tiles,),
            in_specs=[pl.BlockSpec((EXPERT_TILE * ROW_CHUNKS, LANES), used_row), anyspec, anyspec, anyspec],
            out_specs=pl.BlockSpec((EXPERT_TILE * ROW_CHUNKS, LANES), row),
            scratch_shapes=[pltpu.VMEM((2, D_MODEL, D_FF), F32),
                            pltpu.VMEM((2, D_MODEL, D_FF), F32),
                            pltpu.VMEM((2, D_FF, D_MODEL), F32),
                            pltpu.VMEM((D_MODEL, D_FF), BF16),
                            pltpu.VMEM((D_MODEL, D_FF), BF16),
                            pltpu.VMEM((D_FF, D_MODEL), BF16),
                            pltpu.SemaphoreType.DMA((2,))]),
        out_shape=jax.ShapeDtypeStruct((n_slots * ROW_CHUNKS, LANES), F32),
        compiler_params=_cparams(("arbitrary",)),
        name="experts",
    )(tile_expert, n_used, run_first, run_slot, run_next, xs, w_eg, w_eu, w_ed)


FINAL_BLOCKS = 8


def _final_kernel(dest_ref, h_ref, rf_ref, fw_ref, out_hbm, o_ref, ybuf_a, ybuf_b, sem, *, tm):
    i = pl.program_id(0)
    last = pl.num_programs(0) - 1
    bufs = (ybuf_a, ybuf_b)
    nblk = FINAL_BLOCKS if tm % (FINAL_BLOCKS * SUBLANES) == 0 else 1
    tb = tm // nblk

    def gather(tile, into, lo, hi):
        for j0 in range(lo, hi, INDEX_BATCH):
            slots = [dest_ref[tile * 2 * tm + j0 + u] for u in range(INDEX_BATCH)]
            for u, src in enumerate(slots):
                first = pl.multiple_of(src * ROW_CHUNKS, ROW_CHUNKS)
                row = ((j0 + u) % 2) * tm + (j0 + u) // 2
                pltpu.make_async_copy(out_hbm.at[pl.ds(first, ROW_CHUNKS), :],
                                      bufs[into].at[pl.ds(row * ROW_CHUNKS, ROW_CHUNKS), :],
                                      sem.at[into]).start(priority=u % 2)

    def wait_rows(which):
        pltpu.make_async_copy(out_hbm.at[pl.ds(0, 2 * tm * ROW_CHUNKS), :], bufs[which], sem.at[which]).wait()

    @pl.when(i == 0)
    def _():
        gather(0, 0, 0, 2 * tm)

    def step(cur):
        wait_rows(cur)
        nxt = jnp.minimum(i + 1, last)
        for blk in range(nblk):
            gather(nxt, 1 - cur, 2 * blk * tb, 2 * (blk + 1) * tb)
            rows = slice(blk * tb, (blk + 1) * tb)
            g = rf_ref[rows, :]
            moe = (_load_row_tiles(bufs[cur], blk * tb, tb) * g[:, 0:1]
                   + _load_row_tiles(bufs[cur], tm + blk * tb, tb) * g[:, 1:2])
            h = h_ref[rows, :] + moe
            o_ref[rows, :] = h * lax.rsqrt(jnp.mean(h * h, axis=-1, keepdims=True) + EPS) * fw_ref[...]

        @pl.when(i == last)
        def _():
            wait_rows(1 - cur)

    for cur in range(2):
        pl.when(i % 2 == cur)(functools.partial(step, cur))


def _final(h, out, dest_tiles, rf, final_w):
    n = h.shape[0]
    tm = min(GATHER_TILE, n)
    row = lambda i, d: (i, 0)
    wide = pl.BlockSpec((tm, D_MODEL), row)
    return pl.pallas_call(
        functools.partial(_final_kernel, tm=tm),
        grid_spec=pltpu.PrefetchScalarGridSpec(
            num_scalar_prefetch=1,
            grid=(n // tm,),
            in_specs=[wide, pl.BlockSpec((tm, LANES), row), pl.BlockSpec((1, D_MODEL), lambda i, d: (0, 0)),
                      pl.BlockSpec(memory_space=pl.ANY)],
            out_specs=wide,
            scratch_shapes=[pltpu.VMEM((2 * tm * ROW_CHUNKS, LANES), F32), pltpu.VMEM((2 * tm * ROW_CHUNKS, LANES), F32),
                            pltpu.SemaphoreType.DMA((2,))]),
        out_shape=jax.ShapeDtypeStruct((n, D_MODEL), F32),
        compiler_params=_cparams(("arbitrary",), disable_bounds_checks=True),
        name="final",
    )(dest_tiles, h, rf, final_w.reshape(1, D_MODEL), out)


def _moe_and_final(groups, w_eg, w_eu, w_ed, final_w):
    group_counts = [g[4][0, :N_EXPERTS] for g in groups]
    counts = sum(group_counts)
    padded = ((counts + EXPERT_TILE - 1) // EXPERT_TILE) * EXPERT_TILE
    ends = jnp.cumsum(padded)
    starts = ends - padded
    n_assign = sum(2 * g[0].shape[0] for g in groups)
    n_tiles = n_assign // EXPERT_TILE + N_EXPERTS
    n_slots = n_tiles * EXPERT_TILE
    experts = jnp.arange(N_EXPERTS, dtype=I32)
    tile_start = jnp.arange(n_tiles, dtype=I32) * EXPERT_TILE
    tile_expert = jnp.minimum(jnp.sum((ends[None, :] <= tile_start[:, None]).astype(I32), axis=1), N_EXPERTS - 1)
    n_used = (ends[-1] // EXPERT_TILE).astype(I32).reshape(1)

    later = sum(g[0].shape[0] for g in groups[1:])
    assert later <= EXPERT_TILE, "fill counts must stay below 2 * EXPERT_TILE"
    n_used_tiles = ends[-1] // EXPERT_TILE
    pad_info = jnp.concatenate([starts + group_counts[0], padded - group_counts[0],
                                jnp.stack([n_used_tiles, n_tiles - n_used_tiles])]).astype(I32)
    xs = None
    dests = []
    base = starts
    for (h, hn, ri, rf, _), cnt in zip(groups, group_counts):
        n = h.shape[0]
        dest = jnp.sum(jnp.where(ri[:, 0:2, None] == experts, base, 0), axis=-1) + ri[:, 2:4]
        dests.append(dest.reshape(-1))
        xs = _dispatch(hn, dests[-1], pad_info, xs, n_slots, min(DISPATCH_TILE, n))
        base = base + cnt
    used = padded > 0
    run_index = jnp.cumsum(used.astype(I32)) - 1
    later_used = used[None, :] & (experts[None, :] > experts[:, None])
    next_used = jnp.min(jnp.where(later_used, experts[None, :], N_EXPERTS), axis=1)
    next_used = jnp.where(next_used < N_EXPERTS, next_used, -1).astype(I32)
    pick = lambda table: jnp.sum(jnp.where(tile_expert[:, None] == experts, table, 0), axis=-1).astype(I32)
    run_first = (tile_start == pick(starts)).astype(I32)
    out = _experts(xs, tile_expert, n_used, run_first, pick(run_index) % 2, pick(next_used), w_eg, w_eu, w_ed)
    return [_final(h, out, dest_tiles, rf, final_w) for (h, _, _, rf, _), dest_tiles in zip(groups, dests)]


def _layer(x, pos0, kv_bufs, conv_state, ssm_state, p):
    batch, seq, _ = x.shape
    n = batch * seq
    x2d = x.reshape(n, D_MODEL)
    prompt = kv_bufs is None
    q, k, v, z, xbc, g_a, g_s, dt_raw = _inproj(x2d, batch, seq, pos0, p["norm1_w"], p["w_in"], prompt)

    outs, lses, new_kv = [], [], []
    for gi, (window, dil) in enumerate(ATTN_GROUPS):
        if prompt:
            o, l = _attn_prompt(q[gi], k[gi], v[gi], dil)
            keep = min(window, seq) // dil
            tail = lambda t: t[:, :, seq // dil - keep:, :].transpose(0, 3, 2, 1).reshape(
                batch, HEADS_PER_GROUP, HEAD_DIM, keep * dil)
            nbuf = jnp.stack([tail(k[gi]), tail(v[gi])], axis=1).transpose(0, 4, 1, 2, 3)
        else:
            o, l, nbuf = _attn_step(q, k, v, kv_bufs[gi], batch, seq, gi, dil)
        outs.append(o)
        lses.append(l)
        new_kv.append(nbuf)

    if conv_state is None:
        conv_state = jnp.zeros((batch, 3, SSD_CONV_DIM), F32)
        ssm_state = jnp.zeros((batch, SSD_HEADS, HEAD_DIM, SSD_STATE), F32)
    y_ssd, new_conv, new_ssm = _ssd(xbc, z, dt_raw, conv_state, ssm_state, batch, seq, p["conv_w"], p["conv_b"],
                                    p["dt_bias"], p["A_log"], p["D_skip"], p["ssd_norm_w"])

    h, hn, ri, rf, counts = _merge(x2d, outs, lses, y_ssd, g_a, g_s, p["w_attn_br"], p["w_ssd_br"], p["b_gate"],
                                   p["w_out"], p["norm2_w"], p["w_router"], p["b_router"], prompt)
    return (h, hn, ri, rf, counts), new_kv, new_conv, new_ssm


def _pack_w_in(w_in):
    return w_in.T.astype(BF16)


def kernel(x_prompt, x_sample, cache_kv_w128, cache_kv_w512, cache_kv_w2048, state_conv, state_ssm, norm1_w, w_in, w_attn_br, w_ssd_br, b_gate, w_out, conv_w, conv_b, dt_bias, A_log, D_skip, ssd_norm_w, norm2_w, w_router_coarse, b_router_coarse, w_router_fine, b_router_fine, w_expert_gate, w_expert_up, w_expert_down, final_norm_w):
    depth = norm1_w.shape[0]
    assert depth == 1, "the final norm is fused into the layer's last kernel"
    l = 0
    rpad = LANES - N_EXPERTS - N_GROUPS_E
    p = dict(
        norm1_w=norm1_w[l], w_in=_pack_w_in(w_in[l]),
        w_attn_br=w_attn_br[l].astype(BF16), w_ssd_br=w_ssd_br[l].astype(BF16), b_gate=b_gate[l],
        w_out=w_out[l].astype(BF16), conv_w=conv_w[l], conv_b=conv_b[l], dt_bias=dt_bias[l], A_log=A_log[l],
        D_skip=D_skip[l], ssd_norm_w=ssd_norm_w[l], norm2_w=norm2_w[l],
        w_router=jnp.pad(jnp.concatenate([w_router_fine[l], w_router_coarse[l]], axis=1),
                         ((0, 0), (0, rpad))).astype(BF16),
        b_router=jnp.pad(jnp.concatenate([b_router_fine[l], b_router_coarse[l]]), (0, rpad)).reshape(1, LANES),
        w_eg=w_expert_gate[l], w_eu=w_expert_up[l], w_ed=w_expert_down[l], final_norm_w=final_norm_w,
    )
    g_p, kv_p, c_p, st_p = _layer(x_prompt, 0, None, None, None, p)
    bufs = (cache_kv_w128[l], cache_kv_w512[l], cache_kv_w2048[l])
    g_s, kv_s, c_s, st_s = _layer(x_sample, PAST_LEN, bufs, state_conv[l], state_ssm[l], p)
    y_p, y_s = _moe_and_final([g_p, g_s], p["w_eg"], p["w_eu"], p["w_ed"], p["final_norm_w"])
    y_p = y_p.reshape(x_prompt.shape)
    y_s = y_s.reshape(x_sample.shape)
    lead = lambda t: t[None]
    return (y_p, y_s, lead(kv_p[0]), lead(kv_p[1]), lead(kv_p[2]), lead(c_p), lead(st_p),
            lead(kv_s[0]), lead(kv_s[1]), lead(kv_s[2]), lead(c_s), lead(st_s))
```

```python
import functools

import jax
import jax.numpy as jnp
from jax import lax
from jax.experimental import pallas as pl
from jax.experimental.pallas import tpu as pltpu

F32 = jnp.float32
BF16 = jnp.bfloat16
I32 = jnp.int32

D_MODEL = 1024
HEAD_DIM = 64
ATTN_GROUPS = ((128, 1), (512, 4), (2048, 16))
SPAN = 128
HEADS_PER_GROUP = 4
GROUP_W = HEADS_PER_GROUP * HEAD_DIM
ATTN_W = GROUP_W * len(ATTN_GROUPS)
ROPE_THETA = 10000.0
PAST_LEN = 8192
SSD_INNER = 1024
SSD_HEADS = 16
SSD_STATE = 128
SSD_CONV_DIM = 1536
SSD_CHUNK = 128
SSD_CHUNKS_PER_STEP = 8
N_GROUPS_E = 4
EXPERTS_PER_GROUP = 8
N_EXPERTS = 32
D_FF = 512
EPS = 1e-6
LANES = 128
SUBLANES = 8
NEG = -1e30

_OFF_Q, _OFF_K, _OFF_V, _OFF_Z, _OFF_XBC, _OFF_DT, _OFF_GA, _OFF_GS = 0, 768, 1536, 2304, 3328, 4864, 4880, 5904
_W_IN_COLS = 6928

ROW_TILE = 512
MERGE_SPLIT = 1
INPROJ_TILE = 512
EXPERT_TILE = 512
VMEM_LIMIT = 56 * 1024 * 1024


def _cparams(sem, **kw):
    return pltpu.CompilerParams(dimension_semantics=sem, vmem_limit_bytes=VMEM_LIMIT, **kw)


def _sigmoid(x):
    return 1.0 / (1.0 + jnp.exp(-x))


ROW_CHUNKS = D_MODEL // LANES


def _store_row_tiles(ref, val):
    rows = val.shape[0]
    for c in range(ROW_CHUNKS):
        ref[pl.ds(c, rows, stride=ROW_CHUNKS), :] = val[:, c * LANES:(c + 1) * LANES]


def _load_row_tiles(ref, first, rows):
    return jnp.concatenate(
        [ref[pl.ds(first * ROW_CHUNKS + c, rows, stride=ROW_CHUNKS), :] for c in range(ROW_CHUNKS)], axis=1)


def _nt_dot(a, b):
    return lax.dot_general(a, b, (((1,), (1,)), ((), ())), preferred_element_type=F32)


def _store_by_class(o_ref, scr_ref, val, dil):
    if dil == 1:
        o_ref[0, 0] = val
        return
    rows = val.shape[0]
    for half in range(GROUP_W // LANES):
        lanes = slice(half * LANES, (half + 1) * LANES)
        scr_ref[half] = val[:, lanes]
        for r in range(dil):
            o_ref[0, r, :, lanes] = scr_ref[half, pl.ds(r, rows // dil, stride=dil), :]


def _inproj_kernel(x_ref, nw_ref, w_ref, cos_ref, sin_ref, *refs, class_major):
    if class_major:
        qkv_refs, (z_ref, xbc_ref, ga_ref, gs_ref, dt_ref, scr_ref) = refs[:9], refs[9:]
    else:
        qkv_refs, (z_ref, xbc_ref, ga_ref, gs_ref, dt_ref) = refs[:3], refs[3:]
    x = x_ref[...]
    xn = x * lax.rsqrt(jnp.mean(x * x, axis=-1, keepdims=True) + EPS) * nw_ref[...]
    xb = xn.astype(BF16)
    cos = cos_ref[...]
    sin = sin_ref[...]
    lane = lax.broadcasted_iota(I32, cos.shape, 1)
    first_half = (lane % HEAD_DIM) < (HEAD_DIM // 2)

    def mm(lo, hi):
        return _nt_dot(xb, w_ref[lo:hi, :])

    def rope(uc):
        ur = jnp.where(first_half, pltpu.roll(uc, LANES - HEAD_DIM // 2, 1), pltpu.roll(uc, HEAD_DIM // 2, 1))
        return uc * cos + ur * sin

    def emit(which, base, roped):
        u = mm(base, base + ATTN_W)
        chunks = [u[:, c * LANES:(c + 1) * LANES] for c in range(ATTN_W // LANES)]
        if roped:
            chunks = [rope(uc) for uc in chunks]
        if class_major:
            for gi, (_, dil) in enumerate(ATTN_GROUPS):
                val = jnp.concatenate(chunks[2 * gi:2 * gi + 2], axis=1)
                _store_by_class(qkv_refs[3 * which + gi], scr_ref.at[3 * which + gi], val, dil)
        else:
            for c, uc in enumerate(chunks):
                qkv_refs[which][:, c * LANES:(c + 1) * LANES] = uc

    emit(0, _OFF_Q, True)
    emit(1, _OFF_K, True)
    emit(2, _OFF_V, False)
    z_ref[...] = mm(_OFF_Z, _OFF_XBC)
    xbc_ref[...] = mm(_OFF_XBC, _OFF_DT)
    ga_ref[...] = mm(_OFF_GA, _OFF_GS)
    gs_ref[...] = mm(_OFF_GS, _W_IN_COLS)
    dt_ref[...] = jnp.where(lane < SSD_HEADS, mm(_OFF_DT, _OFF_DT + LANES), 0.0)


def _rope_tables(pos):
    half = HEAD_DIM // 2
    inv_freq = ROPE_THETA ** (-jnp.arange(half, dtype=F32) * (2.0 / HEAD_DIM))
    ang = pos.astype(F32)[:, None] * inv_freq[None, :]
    lane = jnp.arange(LANES)
    sign = jnp.where((lane % HEAD_DIM) < half, -1.0, 1.0).astype(F32)
    widen = lambda t: jnp.tile(t, (1, LANES // half))
    return widen(jnp.cos(ang)), widen(jnp.sin(ang)) * sign[None, :]


def _inproj(x2d, batch, seq_len, pos0, norm_w, w_packed, class_major):
    n = x2d.shape[0]
    tm = min(INPROJ_TILE, n)
    cos, sin = _rope_tables(pos0 + jnp.arange(seq_len, dtype=I32))
    if seq_len < tm:
        cos = jnp.tile(cos, (tm // seq_len, 1))
        sin = jnp.tile(sin, (tm // seq_len, 1))
    tab_blocks = cos.shape[0] // tm
    row = lambda i: (i, 0)
    fixed = lambda i: (0, 0)
    tab = lambda i: (i % tab_blocks, 0)
    widths = (SSD_INNER, SSD_CONV_DIM, D_MODEL, D_MODEL, LANES)
    out_specs = [pl.BlockSpec((tm, w), row) for w in widths]
    out_shape = [jax.ShapeDtypeStruct((n, w), F32) for w in widths]
    scratch = []
    if class_major:
        assert seq_len % tm == 0
        per_seq = seq_len // tm
        cls = lambda i: (i // per_seq, 0, i % per_seq, 0)
        dils = [d for _, d in ATTN_GROUPS] * 3
        out_specs = [pl.BlockSpec((1, d, tm // d, GROUP_W), cls) for d in dils] + out_specs
        out_shape = [jax.ShapeDtypeStruct((batch, d, seq_len // d, GROUP_W), F32) for d in dils] + out_shape
        scratch = [pltpu.VMEM((len(dils), GROUP_W // LANES, tm, LANES), F32)]
    else:
        out_specs = [pl.BlockSpec((tm, ATTN_W), row)] * 3 + out_specs
        out_shape = [jax.ShapeDtypeStruct((n, ATTN_W), F32)] * 3 + out_shape
    res = pl.pallas_call(
        functools.partial(_inproj_kernel, class_major=class_major),
        grid=(n // tm,),
        in_specs=[pl.BlockSpec((tm, D_MODEL), row),
                  pl.BlockSpec((1, D_MODEL), fixed),
                  pl.BlockSpec((_W_IN_COLS, D_MODEL), fixed, pipeline_mode=pl.Buffered(1)),
                  pl.BlockSpec((tm, LANES), tab),
                  pl.BlockSpec((tm, LANES), tab)],
        out_specs=out_specs,
        out_shape=out_shape,
        scratch_shapes=scratch,
        compiler_params=_cparams(("parallel",)),
        name="inproj",
    )(x2d, norm_w.reshape(1, D_MODEL), w_packed, cos, sin)
    if class_major:
        return (tuple(res[0:3]), tuple(res[3:6]), tuple(res[6:9])) + tuple(res[9:])
    return tuple(res)


ATTN_QBLOCKS = 4
ATTN_LAG = 16


def _attn_prompt_kernel(q_ref, kp_ref, kc_ref, vp_ref, vc_ref, o_ref, l_ref, k_scr, v_scr, s_scr, p_scr, r_scr,
                        *, qblocks):
    c = pl.program_id(2)
    i = lax.broadcasted_iota(I32, (SPAN, 2 * SPAN), 0)
    j = lax.broadcasted_iota(I32, (SPAN, 2 * SPAN), 1)
    band = (j >= i) & (j <= i + SPAN)
    first_band = band & ((j >= SPAN) | (c > 0))
    k_scr[0:SPAN, :] = kp_ref[0, 0].astype(BF16)
    k_scr[SPAN:, :] = kc_ref[0, 0].astype(BF16)
    v_scr[0:SPAN, :] = vp_ref[0, 0].astype(BF16)
    v_scr[SPAN:, :] = vc_ref[0, 0].astype(BF16)
    units = [(b, h) for b in range(qblocks) for h in range(HEADS_PER_GROUP)]

    def scores(u):
        b, h = units[u]
        hs = slice(h * HEAD_DIM, (h + 1) * HEAD_DIM)
        qh = (q_ref[0, 0, b * SPAN:(b + 1) * SPAN, hs] * (HEAD_DIM ** -0.5)).astype(BF16)
        s_scr[u] = _nt_dot(qh, k_scr[b * SPAN:(b + 2) * SPAN, hs])

    def softmax(u):
        b, h = units[u]
        hs = slice(h * HEAD_DIM, (h + 1) * HEAD_DIM)
        s = jnp.where(first_band if b == 0 else band, s_scr[u], NEG)
        m = jnp.max(jnp.maximum(s[:, :SPAN], s[:, SPAN:]), axis=-1, keepdims=True)
        p = jnp.exp(s - m)
        den = jnp.sum(p[:, :SPAN] + p[:, SPAN:], axis=-1, keepdims=True)
        p_scr[u] = p.astype(BF16)
        r_scr[u] = jnp.broadcast_to(1.0 / den, (SPAN, HEAD_DIM))
        l_ref[0, 0, b * SPAN:(b + 1) * SPAN, hs] = jnp.broadcast_to(m + jnp.log(den), (SPAN, HEAD_DIM))

    def weighted_values(u):
        b, h = units[u]
        hs = slice(h * HEAD_DIM, (h + 1) * HEAD_DIM)
        acc = jnp.dot(p_scr[u], v_scr[b * SPAN:(b + 2) * SPAN, hs], preferred_element_type=F32)
        o_ref[0, 0, b * SPAN:(b + 1) * SPAN, hs] = acc * r_scr[u]

    n = len(units)
    for t in range(n + 2 * ATTN_LAG):
        if t < n:
            scores(t)
        if 0 <= t - ATTN_LAG < n:
            softmax(t - ATTN_LAG)
        if 0 <= t - 2 * ATTN_LAG < n:
            weighted_values(t - 2 * ATTN_LAG)


def _attn_prompt(q, k, v, dil):
    batch, _, n_cls, _ = q.shape
    qblocks = min(ATTN_QBLOCKS, n_cls // SPAN)
    tq = qblocks * SPAN
    assert n_cls % tq == 0
    cur = lambda b, r, c: (b, r, c, 0)
    prev = lambda b, r, c: (b, r, jnp.maximum(c * qblocks - 1, 0), 0)
    big = pl.BlockSpec((1, 1, tq, GROUP_W), cur)
    small = pl.BlockSpec((1, 1, SPAN, GROUP_W), prev)
    return pl.pallas_call(
        functools.partial(_attn_prompt_kernel, qblocks=qblocks),
        grid=(batch, dil, n_cls // tq),
        in_specs=[big, small, big, small, big],
        out_specs=[big, big],
        out_shape=[jax.ShapeDtypeStruct(q.shape, F32)] * 2,
        scratch_shapes=[pltpu.VMEM((SPAN + tq, GROUP_W), BF16),
                        pltpu.VMEM((SPAN + tq, GROUP_W), BF16),
                        pltpu.VMEM((qblocks * HEADS_PER_GROUP, SPAN, 2 * SPAN), F32),
                        pltpu.VMEM((qblocks * HEADS_PER_GROUP, SPAN, 2 * SPAN), BF16),
                        pltpu.VMEM((qblocks * HEADS_PER_GROUP, SPAN, HEAD_DIM), F32)],
        compiler_params=_cparams(("parallel", "parallel", "arbitrary")),
        name=f"attn_prompt_d{dil}",
    )(q, k, k, v, v)


STEP_ROWS_PER_CALL = 2048


def _attn_step_kernel(q_ref, kn_ref, vn_ref, buf_ref, o_ref, l_ref, nbuf_ref, *, buf_len, dil, n_new, seqs):
    t = lax.broadcasted_iota(I32, (n_new, buf_len), 0)
    i = lax.broadcasted_iota(I32, (n_new, buf_len), 1)
    delta = buf_len + t - i
    valid_b = delta <= dil * SPAN
    first_new = LANES - n_new
    tn = lax.broadcasted_iota(I32, (n_new, LANES), 0)
    un = lax.broadcasted_iota(I32, (n_new, LANES), 1) - first_new
    dn = tn - un
    valid_n = (dn >= 0) & (un >= 0)
    if dil > 1:
        valid_b = valid_b & ((delta & (dil - 1)) == 0)
        valid_n = valid_n & ((dn & (dil - 1)) == 0)

    def to_columns(x):
        xp = jnp.concatenate([x, jnp.zeros((LANES - n_new, GROUP_W), F32)], axis=0)
        xt = jnp.concatenate([xp[:, c * LANES:(c + 1) * LANES].T for c in range(GROUP_W // LANES)], axis=0)
        return pltpu.roll(xt, first_new, 1)

    is_new = lax.broadcasted_iota(I32, (HEAD_DIM, LANES), 1) >= first_new
    tail = slice(buf_len - LANES, buf_len)
    for b, h in [(b, h) for b in range(seqs) for h in range(HEADS_PER_GROUP)]:
        if h == 0:
            q = q_ref[b] * (HEAD_DIM ** -0.5)
            new_cols = (to_columns(kn_ref[b]), to_columns(vn_ref[b]))
        hs = slice(h * HEAD_DIM, (h + 1) * HEAD_DIM)
        qh = q[:, hs].astype(BF16)
        kt = buf_ref[b, 0, h]
        vt = buf_ref[b, 1, h]
        knt = new_cols[0][hs]
        vnt = new_cols[1][hs]
        sb = jnp.where(valid_b, jnp.dot(qh, kt.astype(BF16), preferred_element_type=F32), NEG)
        sn = jnp.where(valid_n, jnp.dot(qh, knt.astype(BF16), preferred_element_type=F32), NEG)
        m = jnp.maximum(jnp.max(sb, axis=-1, keepdims=True), jnp.max(sn, axis=-1, keepdims=True))
        pb = jnp.exp(sb - m)
        pn = jnp.exp(sn - m)
        den = jnp.sum(pb, axis=-1, keepdims=True) + jnp.sum(pn, axis=-1, keepdims=True)
        acc = _nt_dot(pb.astype(BF16), vt.astype(BF16)) + _nt_dot(pn.astype(BF16), vnt.astype(BF16))
        o_ref[b, :, hs] = acc / den
        l_ref[b, :, hs] = jnp.broadcast_to(m + jnp.log(den), (n_new, HEAD_DIM))
        for kv, (old, new) in enumerate(((kt, knt), (vt, vnt))):
            shifted = pltpu.roll(old, buf_len - n_new, 1)
            nbuf_ref[b, kv, h] = shifted
            nbuf_ref[b, kv, h, :, tail] = jnp.where(is_new, new, shifted[:, tail])


def _attn_step(q, k, v, buf, batch, n_new, gi, dil):
    buf_len = buf.shape[1]
    assert dil & (dil - 1) == 0 and buf_len >= dil * SPAN and n_new % SUBLANES == 0 and buf_len % LANES == 0
    view = lambda t: t.reshape(batch, n_new, ATTN_W)
    seqs = max(1, min(batch, STEP_ROWS_PER_CALL // buf_len))
    assert batch % seqs == 0
    tok = pl.BlockSpec((seqs, n_new, GROUP_W), lambda b: (b, 0, gi))
    full = pl.BlockSpec((seqs, 2, HEADS_PER_GROUP, HEAD_DIM, buf_len), lambda b: (b, 0, 0, 0, 0))
    osp = pl.BlockSpec((seqs, n_new, GROUP_W), lambda b: (b, 0, 0))
    o, l, nbuf = pl.pallas_call(
        functools.partial(_attn_step_kernel, buf_len=buf_len, dil=dil, n_new=n_new, seqs=seqs),
        grid=(batch // seqs,),
        in_specs=[tok, tok, tok, full],
        out_specs=[osp, osp, full],
        out_shape=[jax.ShapeDtypeStruct((batch, n_new, GROUP_W), F32)] * 2
        + [jax.ShapeDtypeStruct((batch, 2, HEADS_PER_GROUP, HEAD_DIM, buf_len), F32)],
        compiler_params=_cparams(("parallel",)),
        name=f"attn_step_d{dil}",
    )(view(q), view(k), view(v), jnp.transpose(buf, (0, 2, 3, 4, 1)))
    return (o.reshape(batch * n_new, GROUP_W), l.reshape(batch * n_new, GROUP_W),
            jnp.transpose(nbuf, (0, 4, 1, 2, 3)))


def _split3(a):
    a1 = a.astype(BF16)
    r1 = a - a1.astype(F32)
    a2 = r1.astype(BF16)
    a3 = (r1 - a2.astype(F32)).astype(BF16)
    return a1, a2, a3


def _ssd_kernel(*refs, n_valid, n_steps, per_step, separate):
    for sub in range(per_step):
        _ssd_chunk(*refs, n_valid=n_valid, n_steps=n_steps, per_step=per_step, separate=separate, sub=sub)


def _ssd_chunk(xbc_ref, z_ref, dt_ref, cst_ref, h0_ref, cw_ref, cb_ref, dtb_ref, alog_ref, dfull_ref, nw_ref,
               tri_ref, expand_ref, y_ref, cout_ref, hout_ref, xpad_ref, h_ref,
               *, n_valid, n_steps, per_step, separate, sub):
    seq = sub if separate else 0
    c = pl.program_id(1)
    lc = SSD_CHUNK
    pad = SUBLANES
    n_slabs = SSD_CONV_DIM // LANES
    tok = slice(sub * n_valid, (sub + 1) * n_valid)

    def carry_rows():
        for j in range(n_slabs):
            xpad_ref[j, 0:pad, :] = xpad_ref[j, lc:lc + pad, :]

    def start_sequence():
        for j in range(n_slabs):
            xpad_ref[j, 0:pad, :] = cst_ref[seq, :, j * LANES:(j + 1) * LANES]
        h_ref[...] = h0_ref[seq]

    if separate:
        start_sequence()
    elif sub == 0:
        pl.when(c == 0)(start_sequence)
        pl.when(c > 0)(carry_rows)
    else:
        carry_rows()

    if n_valid == lc:
        z = z_ref[tok, :]
        dtr = dt_ref[tok, :]
    else:
        fill = lambda w: jnp.zeros((lc - n_valid, w), F32)
        z = jnp.concatenate([z_ref[tok, :], fill(SSD_INNER)], axis=0)
        dtr = jnp.concatenate([dt_ref[tok, :], fill(LANES)], axis=0)

    slabs = []
    for j in range(n_slabs):
        cols = slice(j * LANES, (j + 1) * LANES)
        xpad_ref[j, pad:pad + n_valid, :] = xbc_ref[tok, cols]
        if n_valid < lc:
            xpad_ref[j, pad + n_valid:pad + lc, :] = jnp.zeros((lc - n_valid, LANES), F32)
        xc = cb_ref[:, cols]
        for tap in range(4):
            xc = xc + xpad_ref[j, pl.ds(pad - 3 + tap, lc), :] * cw_ref[tap:tap + 1, cols]
        slabs.append(xc * _sigmoid(xc))
        cout_ref[seq, :, cols] = xpad_ref[j, pl.ds(pad + n_valid - 3, 3), :]
    xs = jnp.concatenate(slabs[:SSD_INNER // LANES], axis=1)
    bm = slabs[SSD_INNER // LANES:SSD_INNER // LANES + 2]
    cm = slabs[SSD_INNER // LANES + 2:]

    row = lax.broadcasted_iota(I32, (lc, lc), 0)
    col = lax.broadcasted_iota(I32, (lc, lc), 1)
    causal = row >= col
    dtv = dtr + dtb_ref[...]
    dt = jnp.maximum(dtv, 0.0) + jnp.log1p(jnp.exp(-jnp.abs(dtv)))
    if n_valid < lc:
        dt = jnp.where(row < n_valid, dt, 0.0)
    a = dt * (-jnp.exp(alog_ref[...]))
    tri = tri_ref[...]
    a_cs = sum(jnp.dot(tri, p, preferred_element_type=F32) for p in _split3(a))
    a_cs_t = a_cs.T
    expand = expand_ref[...]
    a_full = sum(jnp.dot(p, expand, preferred_element_type=F32) for p in _split3(a_cs))
    dt_full = sum(jnp.dot(p, expand, preferred_element_type=F32) for p in _split3(dt))
    xdt = xs * dt_full
    xd = xdt * jnp.exp(a_full[lc - 1:lc, :] - a_full)
    grow = jnp.exp(a_full)
    xdt_b = xdt.astype(BF16)
    xd_t = jnp.concatenate([xd[:, k * LANES:(k + 1) * LANES].T for k in range(SSD_INNER // LANES)],
                           axis=0).astype(BF16)
    lane = lax.broadcasted_iota(I32, (lc, LANES), 1)
    low_half = lane < HEAD_DIM
    zero_b = jnp.zeros((lc, LANES), BF16)
    half = SSD_INNER // 2

    y_parts = []
    for g in range(2):
        bg = bm[g].astype(BF16)
        cg = cm[g].astype(BF16)
        cb = jnp.where(causal, _nt_dot(cg, bg), 0.0)
        h_grp = h_ref[g * half:(g + 1) * half, :]
        y_off = _nt_dot(cg, h_grp.astype(BF16)) * grow[:, g * half:(g + 1) * half]
        for pair in range(4):
            e0 = g * 8 + pair * 2
            rows = slice(e0 * HEAD_DIM, (e0 + 2) * HEAD_DIM)
            ms, keep = [], []
            for k in range(2):
                e = e0 + k
                seg = a_cs[:, e:e + 1] - a_cs_t[e:e + 1, :]
                ms.append((cb * jnp.exp(jnp.minimum(seg, 0.0))).astype(BF16))
                keep.append(jnp.broadcast_to(jnp.exp(a_cs[lc - 1:lc, e:e + 1]), (HEAD_DIM, SSD_STATE)))
            pair_b = xdt_b[:, rows]
            rhs = jnp.concatenate([jnp.where(low_half, pair_b, zero_b), jnp.where(low_half, zero_b, pair_b)], axis=0)
            y_parts.append(jnp.dot(jnp.concatenate(ms, axis=1), rhs, preferred_element_type=F32)
                           + y_off[:, pair * LANES:(pair + 1) * LANES])
            st = jnp.dot(xd_t[rows, :], bg, preferred_element_type=F32)
            h_ref[rows, :] = h_grp[pair * LANES:(pair + 1) * LANES, :] * jnp.concatenate(keep, axis=0) + st

    y = jnp.concatenate(y_parts, axis=1) + xs * dfull_ref[...]
    gate = y * (z * _sigmoid(z))
    for g in range(2):
        gg = gate[:, g * half:(g + 1) * half]
        gg = gg * lax.rsqrt(jnp.mean(gg * gg, axis=-1, keepdims=True) + EPS)
        y_ref[tok, g * half:(g + 1) * half] = (gg * nw_ref[:, g * half:(g + 1) * half])[0:n_valid]

    if separate:
        hout_ref[seq] = h_ref[...]
    elif sub == per_step - 1:
        @pl.when(c == n_steps - 1)
        def _():
            hout_ref[0] = h_ref[...]


def _ssd(xbc, z, dt_raw, conv_state, ssm_state, batch, seq, conv_w, conv_b, dt_bias, a_log, d_skip, norm_w):
    n_valid = min(seq, SSD_CHUNK)
    n_chunks = seq // n_valid
    assert n_valid % SUBLANES == 0 and seq % n_valid == 0
    padl = lambda t: jnp.pad(t.reshape(1, SSD_HEADS), ((0, 0), (0, LANES - SSD_HEADS)))
    cst = jnp.pad(conv_state, ((0, 0), (SUBLANES - 3, 0), (0, 0)))
    tri = (jnp.arange(SSD_CHUNK)[:, None] >= jnp.arange(SSD_CHUNK)[None, :]).astype(BF16)
    expand = (jnp.arange(LANES)[:, None] == jnp.arange(SSD_INNER)[None, :] // HEAD_DIM).astype(BF16)
    separate = n_chunks == 1 and batch % SSD_CHUNKS_PER_STEP == 0
    per_step = SSD_CHUNKS_PER_STEP if (separate or n_chunks % SSD_CHUNKS_PER_STEP == 0) else 1
    n_steps = 1 if separate else n_chunks // per_step
    n_outer = batch // per_step if separate else batch
    per_state = per_step if separate else 1
    step_rows = per_step * n_valid
    tokrow = lambda b, c: (b * n_steps + c, 0)
    fixed = lambda b, c: (0, 0)
    per_b3 = lambda b, c: (b, 0, 0)
    state = pl.BlockSpec((per_state, SSD_INNER, SSD_STATE), per_b3)
    kern = functools.partial(_ssd_kernel, n_valid=n_valid, n_steps=n_steps, per_step=per_step, separate=separate)
    y, cout, hout = pl.pallas_call(
        kern,
        grid=(n_outer, n_steps),
        in_specs=[pl.BlockSpec((step_rows, SSD_CONV_DIM), tokrow),
                  pl.BlockSpec((step_rows, SSD_INNER), tokrow),
                  pl.BlockSpec((step_rows, LANES), tokrow),
                  pl.BlockSpec((per_state, SUBLANES, SSD_CONV_DIM), per_b3),
                  state,
                  pl.BlockSpec((4, SSD_CONV_DIM), fixed),
                  pl.BlockSpec((1, SSD_CONV_DIM), fixed),
                  pl.BlockSpec((1, LANES), fixed),
                  pl.BlockSpec((1, LANES), fixed),
                  pl.BlockSpec((1, SSD_INNER), fixed),
                  pl.BlockSpec((1, SSD_INNER), fixed),
                  pl.BlockSpec((SSD_CHUNK, SSD_CHUNK), fixed),
                  pl.BlockSpec((LANES, SSD_INNER), fixed)],
        out_specs=[pl.BlockSpec((step_rows, SSD_INNER), tokrow),
                   pl.BlockSpec((per_state, 3, SSD_CONV_DIM), per_b3),
                   state],
        out_shape=[jax.ShapeDtypeStruct((batch * seq, SSD_INNER), F32),
                   jax.ShapeDtypeStruct((batch, 3, SSD_CONV_DIM), F32),
                   jax.ShapeDtypeStruct((batch, SSD_INNER, SSD_STATE), F32)],
        scratch_shapes=[pltpu.VMEM((SSD_CONV_DIM // LANES, SSD_CHUNK + SUBLANES, LANES), F32),
                        pltpu.VMEM((SSD_INNER, SSD_STATE), F32)],
        compiler_params=_cparams(("parallel", "arbitrary")),
        name="ssd",
    )(xbc, z, dt_raw, cst, ssm_state.reshape(batch, SSD_INNER, SSD_STATE), conv_w, conv_b.reshape(1, SSD_CONV_DIM),
      padl(dt_bias), padl(a_log), jnp.repeat(d_skip, HEAD_DIM).reshape(1, SSD_INNER), norm_w.reshape(1, SSD_INNER),
      tri, expand)
    return y, cout, hout.reshape(batch, SSD_HEADS, HEAD_DIM, SSD_STATE)


def _merge_kernel(x_ref, o0_ref, o1_ref, o2_ref, l0_ref, l1_ref, l2_ref, ys_ref, ga_ref, gs_ref,
                  wab_ref, wsb_ref, bg_ref, wo_ref, n2_ref, wr_ref, br_ref,
                  h_ref, hn_ref, ri_ref, rf_ref, cnt_ref, carry_ref, *scr, class_major):
    step = pl.program_id(0)

    @pl.when(step == 0)
    def _():
        carry_ref[...] = jnp.zeros_like(carry_ref)

    def load(ref, gi, slab):
        if not class_major:
            return ref[...]
        dil = ATTN_GROUPS[gi][1]
        if dil == 1:
            return ref[0, 0]
        rows = ref.shape[1] * ref.shape[2]
        halves = []
        for half in range(GROUP_W // LANES):
            for r in range(dil):
                scr[0][slab, half, pl.ds(r, rows // dil, stride=dil), :] = ref[0, r, :, half * LANES:(half + 1) * LANES]
            halves.append(scr[0][slab, half])
        return jnp.concatenate(halves, axis=1)

    lses = (load(l0_ref, 0, 0), load(l1_ref, 1, 0), load(l2_ref, 2, 1))
    outs = (load(o0_ref, 0, 0), load(o1_ref, 1, 2), load(o2_ref, 2, 3))
    tm = x_ref.shape[0]
    sub = tm // MERGE_SPLIT
    lane = lax.broadcasted_iota(I32, (sub, LANES), 1)
    big = jnp.int32(LANES)
    r = lax.broadcasted_iota(I32, (sub, sub), 0)
    s = lax.broadcasted_iota(I32, (sub, sub), 1)
    earlier = (r > s).astype(BF16)

    def top(vals):
        v = jnp.max(vals, axis=-1, keepdims=True)
        idx = jnp.min(jnp.where(vals == v, lane, big), axis=-1, keepdims=True)
        return v, idx

    carry = carry_ref[0:1, :]
    for part in range(MERGE_SPLIT):
        rows = slice(part * sub, (part + 1) * sub)
        l0, l1, l2 = (t[rows] for t in lses)
        m = jnp.maximum(jnp.maximum(l0, l1), l2)
        w0, w1, w2 = jnp.exp(l0 - m), jnp.exp(l1 - m), jnp.exp(l2 - m)
        y_attn = (w0 * outs[0][rows] + w1 * outs[1][rows] + w2 * outs[2][rows]) / (w0 + w1 + w2)
        pa = jnp.dot(y_attn.astype(BF16), wab_ref[...], preferred_element_type=F32)
        ps = jnp.dot(ys_ref[rows, :].astype(BF16), wsb_ref[...], preferred_element_type=F32)
        merged = (_sigmoid(ga_ref[rows, :] + bg_ref[0:1, :]) * pa + _sigmoid(gs_ref[rows, :] + bg_ref[1:2, :]) * ps)
        h = x_ref[rows, :] + jnp.dot(merged.astype(BF16), wo_ref[...], preferred_element_type=F32)
        h_ref[rows, :] = h
        hn = h * lax.rsqrt(jnp.mean(h * h, axis=-1, keepdims=True) + EPS) * n2_ref[...]
        hnb = hn.astype(BF16)
        _store_row_tiles(hn_ref.at[pl.ds(part * sub * ROW_CHUNKS, sub * ROW_CHUNKS), :], hn)

        logits = jnp.dot(hnb, wr_ref[...], preferred_element_type=F32) + br_ref[...]
        is_coarse = (lane >= N_EXPERTS) & (lane < N_EXPERTS + N_GROUPS_E)
        lc = jnp.where(is_coarse, logits, NEG)
        mc, ic = top(lc)
        p_grp = 1.0 / jnp.sum(jnp.exp(lc - mc), axis=-1, keepdims=True)
        lo = (ic - N_EXPERTS) * EXPERTS_PER_GROUP
        lf = jnp.where((lane >= lo) & (lane < lo + EXPERTS_PER_GROUP), logits, NEG)
        v1, i1 = top(lf)
        v2, i2 = top(jnp.where(lane == i1, NEG, lf))
        e2 = jnp.exp(v2 - v1)
        g1 = p_grp / (1.0 + e2)
        g2 = p_grp * e2 / (1.0 + e2)

        oh1 = lane == i1
        oh2 = lane == i2
        cnt = oh1.astype(F32) + oh2.astype(F32)
        before = jnp.dot(earlier, cnt.astype(BF16), preferred_element_type=F32) + carry
        r1 = jnp.sum(jnp.where(oh1, before, 0.0), axis=-1, keepdims=True)
        r2 = jnp.sum(jnp.where(oh2, before, 0.0), axis=-1, keepdims=True)
        carry = carry + jnp.sum(cnt, axis=0, keepdims=True)

        ri = jnp.where(lane == 0, i1, jnp.where(lane == 1, i2, 0))
        ri = jnp.where(lane == 2, r1.astype(I32), jnp.where(lane == 3, r2.astype(I32), ri))
        ri_ref[rows, :] = ri
        rf_ref[rows, :] = jnp.where(lane == 0, g1, jnp.where(lane == 1, g2, 0.0))

    carry_ref[...] = jnp.broadcast_to(carry, carry_ref.shape)
    cnt_ref[...] = jnp.broadcast_to(carry, cnt_ref.shape).astype(I32)


def _merge(x2d, outs, lses, y_ssd, g_a, g_s, wab, wsb, b_gate, wo, norm2_w, w_router, b_router, class_major):
    n = x2d.shape[0]
    tm = min(ROW_TILE, n)
    row = lambda i: (i, 0)
    fixed = lambda i: (0, 0)
    wide = pl.BlockSpec((tm, D_MODEL), row)
    info = pl.BlockSpec((tm, LANES), row)
    if class_major:
        per_seq = outs[0].shape[2] // tm
        cls = lambda i: (i // per_seq, 0, i % per_seq, 0)
        grps = [pl.BlockSpec((1, d, tm // d, GROUP_W), cls) for _, d in ATTN_GROUPS]
        scratch = [pltpu.VMEM((4, GROUP_W // LANES, tm, LANES), F32)]
    else:
        grps = [pl.BlockSpec((tm, GROUP_W), row)] * 3
        scratch = []
    return pl.pallas_call(
        functools.partial(_merge_kernel, class_major=class_major),
        grid=(n // tm,),
        in_specs=[wide, *grps, *grps, wide, wide, wide,
                  pl.BlockSpec((GROUP_W, D_MODEL), fixed),
                  pl.BlockSpec((SSD_INNER, D_MODEL), fixed),
                  pl.BlockSpec((2, D_MODEL), fixed),
                  pl.BlockSpec((D_MODEL, D_MODEL), fixed),
                  pl.BlockSpec((1, D_MODEL), fixed),
                  pl.BlockSpec((D_MODEL, LANES), fixed),
                  pl.BlockSpec((1, LANES), fixed)],
        out_specs=[wide, pl.BlockSpec((tm * ROW_CHUNKS, LANES), row), info, info,
                   pl.BlockSpec((SUBLANES, LANES), fixed)],
        out_shape=[jax.ShapeDtypeStruct((n, D_MODEL), F32),
                   jax.ShapeDtypeStruct((n * ROW_CHUNKS, LANES), F32),
                   jax.ShapeDtypeStruct((n, LANES), I32),
                   jax.ShapeDtypeStruct((n, LANES), F32),
                   jax.ShapeDtypeStruct((SUBLANES, LANES), I32)],
        scratch_shapes=[pltpu.VMEM((SUBLANES, LANES), F32)] + scratch,
        compiler_params=_cparams(("arbitrary",)),
        name="merge_out",
    )(x2d, *outs, *lses, y_ssd, g_a, g_s, wab, wsb, b_gate, wo, norm2_w.reshape(1, D_MODEL), w_router, b_router)


GATHER_TILE = 512
DISPATCH_TILE = 1024
INDEX_BATCH = 16


def _dispatch_kernel(dest_ref, pad_ref, hn_hbm, *rest, tm, first_group):
    if first_group:
        xs_hbm, ring, in_sem, sem, zeros_ref, pad_sem = rest
    else:
        _, xs_hbm, ring, in_sem, sem = rest
    i = pl.program_id(0)
    n_steps = pl.num_programs(0)
    slot = i % 3

    def load(tile):
        into = tile % 3
        return pltpu.make_async_copy(hn_hbm.at[pl.ds(tile * tm * ROW_CHUNKS, tm * ROW_CHUNKS), :], ring.at[into],
                                     in_sem.at[into])

    def wait_scatter(which):
        for _ in range(2):
            pltpu.make_async_copy(ring.at[which], xs_hbm.at[pl.ds(0, tm * ROW_CHUNKS), :], sem.at[which]).wait()

    @pl.when(i == 0)
    def _():
        load(0).start()

        @pl.when(n_steps > 1)
        def _():
            load(1).start()

    if first_group:
        @pl.when(i == 0)
        def _():
            zeros_ref[...] = jnp.zeros_like(zeros_ref)

            def fill(first_slot, size):
                first = pl.multiple_of(first_slot * ROW_CHUNKS, ROW_CHUNKS)
                return pltpu.make_async_copy(zeros_ref.at[pl.ds(0, size * ROW_CHUNKS), :],
                                             xs_hbm.at[pl.ds(first, size * ROW_CHUNKS), :], pad_sem)

            fills = []
            for e in range(N_EXPERTS):
                start, count = pad_ref[e], pad_ref[N_EXPERTS + e]
                size = EXPERT_TILE
                while size >= 1:
                    fills.append(((count & size) != 0, start + (count & ~(2 * size - 1)), size))
                    size //= 2
            tail_first, tail_tiles = pad_ref[2 * N_EXPERTS], pad_ref[2 * N_EXPERTS + 1]
            for t in range(N_EXPERTS):
                fills.append((t < tail_tiles, (tail_first + t) * EXPERT_TILE, EXPERT_TILE))
            for pred, first_slot, size in fills:
                pl.when(pred)(lambda first_slot=first_slot, size=size: fill(first_slot, size).start())
            for pred, first_slot, size in fills:
                pl.when(pred)(lambda first_slot=first_slot, size=size: fill(first_slot, size).wait())

    load(i).wait()
    for j0 in range(0, 2 * tm, INDEX_BATCH):
        slots = [dest_ref[i * 2 * tm + j0 + u] for u in range(INDEX_BATCH)]
        for u, dst in enumerate(slots):
            tok = (j0 + u) // 2
            first = pl.multiple_of(dst * ROW_CHUNKS, ROW_CHUNKS)
            pltpu.make_async_copy(ring.at[slot, pl.ds(tok * ROW_CHUNKS, ROW_CHUNKS), :],
                                  xs_hbm.at[pl.ds(first, ROW_CHUNKS), :], sem.at[slot]).start(priority=u % 2)

    @pl.when(i > 0)
    def _():
        wait_scatter((i + 2) % 3)

    @pl.when(i + 2 < n_steps)
    def _():
        load(i + 2).start()

    @pl.when(i == n_steps - 1)
    def _():
        wait_scatter(slot)


def _dispatch(hn, dest_tiles, pad_info, xs, n_slots, tm):
    n = hn.shape[0] // ROW_CHUNKS
    first_group = xs is None
    anyspec = pl.BlockSpec(memory_space=pl.ANY)
    scratch = [pltpu.VMEM((3, tm * ROW_CHUNKS, LANES), F32), pltpu.SemaphoreType.DMA((3,)),
               pltpu.SemaphoreType.DMA((3,))]
    if first_group:
        scratch += [pltpu.VMEM((EXPERT_TILE * ROW_CHUNKS, LANES), F32), pltpu.SemaphoreType.DMA(())]
    return pl.pallas_call(
        functools.partial(_dispatch_kernel, tm=tm, first_group=first_group),
        grid_spec=pltpu.PrefetchScalarGridSpec(
            num_scalar_prefetch=2,
            grid=(n // tm,),
            in_specs=[anyspec] + ([] if first_group else [anyspec]),
            out_specs=anyspec,
            scratch_shapes=scratch),
        out_shape=jax.ShapeDtypeStruct((n_slots * ROW_CHUNKS, LANES), F32),
        input_output_aliases={} if first_group else {3: 0},
        compiler_params=_cparams(("arbitrary",), disable_bounds_checks=True, has_side_effects=True),
        name="dispatch",
    )(dest_tiles, pad_info, hn, *([] if first_group else [xs]))


def _expert_kernel(te_ref, nu_ref, first_ref, slot_ref, next_ref, x_ref, wg_hbm, wu_hbm, wd_hbm, o_ref,
                   wg32, wu32, wd32, wgb, wub, wdb, sem):
    i = pl.program_id(0)

    def fetch(expert, slot):
        return [pltpu.make_async_copy(src.at[expert], dst.at[slot], sem.at[slot])
                for src, dst in ((wg_hbm, wg32), (wu_hbm, wu32), (wd_hbm, wd32))]

    @pl.when((i < nu_ref[0]) & (first_ref[i] == 1))
    def _():
        slot = slot_ref[i]

        @pl.when(i == 0)
        def _():
            for c in fetch(te_ref[0], slot):
                c.start()

        for c in fetch(te_ref[i], slot):
            c.wait()
        wgb[...] = wg32[slot].astype(BF16)
        wub[...] = wu32[slot].astype(BF16)
        wdb[...] = wd32[slot].astype(BF16)

        @pl.when(next_ref[i] >= 0)
        def _():
            for c in fetch(next_ref[i], 1 - slot):
                c.start()

    @pl.when(i < nu_ref[0])
    def _():
        x = _load_row_tiles(x_ref, 0, EXPERT_TILE).astype(BF16)
        hg = jnp.dot(x, wgb[...], preferred_element_type=F32)
        hu = jnp.dot(x, wub[...], preferred_element_type=F32)
        hb = (hg * _sigmoid(hg)) * hu
        _store_row_tiles(o_ref, jnp.dot(hb.astype(BF16), wdb[...], preferred_element_type=F32))

    @pl.when(i >= nu_ref[0])
    def _():
        o_ref[...] = jnp.zeros_like(o_ref)


def _experts(xs, tile_expert, n_used, run_first, run_slot, run_next, w_eg, w_eu, w_ed):
    n_slots = xs.shape[0] // ROW_CHUNKS
    n_tiles = n_slots // EXPERT_TILE
    row = lambda i, *_: (i, 0)
    used_row = lambda i, te, nu, *_: (jnp.minimum(i, nu[0] - 1), 0)
    anyspec = pl.BlockSpec(memory_space=pl.ANY)
    return pl.pallas_call(
        _expert_kernel,
        grid_spec=pltpu.PrefetchScalarGridSpec(
            num_scalar_prefetch=5,
            grid=(n_tiles,),
            in_specs=[pl.BlockSpec((EXPERT_TILE * ROW_CHUNKS, LANES), used_row), anyspec, anyspec, anyspec],
            out_specs=pl.BlockSpec((EXPERT_TILE * ROW_CHUNKS, LANES), row),
            scratch_shapes=[pltpu.VMEM((2, D_MODEL, D_FF), F32),
                            pltpu.VMEM((2, D_MODEL, D_FF), F32),
                            pltpu.VMEM((2, D_FF, D_MODEL), F32),
                            pltpu.VMEM((D_MODEL, D_FF), BF16),
                            pltpu.VMEM((D_MODEL, D_FF), BF16),
                            pltpu.VMEM((D_FF, D_MODEL), BF16),
                            pltpu.SemaphoreType.DMA((2,))]),
        out_shape=jax.ShapeDtypeStruct((n_slots * ROW_CHUNKS, LANES), F32),
        compiler_params=_cparams(("arbitrary",)),
        name="experts",
    )(tile_expert, n_used, run_first, run_slot, run_next, xs, w_eg, w_eu, w_ed)


FINAL_BLOCKS = 8


def _final_kernel(dest_ref, h_ref, rf_ref, fw_ref, out_hbm, o_ref, ybuf_a, ybuf_b, sem, *, tm):
    i = pl.program_id(0)
    last = pl.num_programs(0) - 1
    bufs = (ybuf_a, ybuf_b)
    nblk = FINAL_BLOCKS if tm % (FINAL_BLOCKS * SUBLANES) == 0 else 1
    tb = tm // nblk

    def gather(tile, into, lo, hi):
        for j0 in range(lo, hi, INDEX_BATCH):
            slots = [dest_ref[tile * 2 * tm + j0 + u] for u in range(INDEX_BATCH)]
            for u, src in enumerate(slots):
                first = pl.multiple_of(src * ROW_CHUNKS, ROW_CHUNKS)
                row = ((j0 + u) % 2) * tm + (j0 + u) // 2
                pltpu.make_async_copy(out_hbm.at[pl.ds(first, ROW_CHUNKS), :],
                                      bufs[into].at[pl.ds(row * ROW_CHUNKS, ROW_CHUNKS), :],
                                      sem.at[into]).start(priority=u % 2)

    def wait_rows(which):
        pltpu.make_async_copy(out_hbm.at[pl.ds(0, 2 * tm * ROW_CHUNKS), :], bufs[which], sem.at[which]).wait()

    @pl.when(i == 0)
    def _():
        gather(0, 0, 0, 2 * tm)

    def step(cur):
        wait_rows(cur)
        nxt = jnp.minimum(i + 1, last)
        for blk in range(nblk):
            gather(nxt, 1 - cur, 2 * blk * tb, 2 * (blk + 1) * tb)
            rows = slice(blk * tb, (blk + 1) * tb)
            g = rf_ref[rows, :]
            moe = (_load_row_tiles(bufs[cur], blk * tb, tb) * g[:, 0:1]
                   + _load_row_tiles(bufs[cur], tm + blk * tb, tb) * g[:, 1:2])
            h = h_ref[rows, :] + moe
            o_ref[rows, :] = h * lax.rsqrt(jnp.mean(h * h, axis=-1, keepdims=True) + EPS) * fw_ref[...]

        @pl.when(i == last)
        def _():
            wait_rows(1 - cur)

    for cur in range(2):
        pl.when(i % 2 == cur)(functools.partial(step, cur))


def _final(h, out, dest_tiles, rf, final_w):
    n = h.shape[0]
    tm = min(GATHER_TILE, n)
    row = lambda i, d: (i, 0)
    wide = pl.BlockSpec((tm, D_MODEL), row)
    return pl.pallas_call(
        functools.partial(_final_kernel, tm=tm),
        grid_spec=pltpu.PrefetchScalarGridSpec(
            num_scalar_prefetch=1,
            grid=(n // tm,),
            in_specs=[wide, pl.BlockSpec((tm, LANES), row), pl.BlockSpec((1, D_MODEL), lambda i, d: (0, 0)),
                      pl.BlockSpec(memory_space=pl.ANY)],
            out_specs=wide,
            scratch_shapes=[pltpu.VMEM((2 * tm * ROW_CHUNKS, LANES), F32), pltpu.VMEM((2 * tm * ROW_CHUNKS, LANES), F32),
                            pltpu.SemaphoreType.DMA((2,))]),
        out_shape=jax.ShapeDtypeStruct((n, D_MODEL), F32),
        compiler_params=_cparams(("arbitrary",), disable_bounds_checks=True),
        name="final",
    )(dest_tiles, h, rf, final_w.reshape(1, D_MODEL), out)


def _moe_and_final(groups, w_eg, w_eu, w_ed, final_w):
    group_counts = [g[4][0, :N_EXPERTS] for g in groups]
    counts = sum(group_counts)
    padded = ((counts + EXPERT_TILE - 1) // EXPERT_TILE) * EXPERT_TILE
    ends = jnp.cumsum(padded)
    starts = ends - padded
    n_assign = sum(2 * g[0].shape[0] for g in groups)
    n_tiles = n_assign // EXPERT_TILE + N_EXPERTS
    n_slots = n_tiles * EXPERT_TILE
    experts = jnp.arange(N_EXPERTS, dtype=I32)
    tile_start = jnp.arange(n_tiles, dtype=I32) * EXPERT_TILE
    tile_expert = jnp.minimum(jnp.sum((ends[None, :] <= tile_start[:, None]).astype(I32), axis=1), N_EXPERTS - 1)
    n_used = (ends[-1] // EXPERT_TILE).astype(I32).reshape(1)

    later = sum(g[0].shape[0] for g in groups[1:])
    assert later <= EXPERT_TILE, "fill counts must stay below 2 * EXPERT_TILE"
    n_used_tiles = ends[-1] // EXPERT_TILE
    pad_info = jnp.concatenate([starts + group_counts[0], padded - group_counts[0],
                                jnp.stack([n_used_tiles, n_tiles - n_used_tiles])]).astype(I32)
    xs = None
    dests = []
    base = starts
    for (h, hn, ri, rf, _), cnt in zip(groups, group_counts):
        n = h.shape[0]
        dest = jnp.sum(jnp.where(ri[:, 0:2, None] == experts, base, 0), axis=-1) + ri[:, 2:4]
        dests.append(dest.reshape(-1))
        xs = _dispatch(hn, dests[-1], pad_info, xs, n_slots, min(DISPATCH_TILE, n))
        base = base + cnt
    used = padded > 0
    run_index = jnp.cumsum(used.astype(I32)) - 1
    later_used = used[None, :] & (experts[None, :] > experts[:, None])
    next_used = jnp.min(jnp.where(later_used, experts[None, :], N_EXPERTS), axis=1)
    next_used = jnp.where(next_used < N_EXPERTS, next_used, -1).astype(I32)
    pick = lambda table: jnp.sum(jnp.where(tile_expert[:, None] == experts, table, 0), axis=-1).astype(I32)
    run_first = (tile_start == pick(starts)).astype(I32)
    out = _experts(xs, tile_expert, n_used, run_first, pick(run_index) % 2, pick(next_used), w_eg, w_eu, w_ed)
    return [_final(h, out, dest_tiles, rf, final_w) for (h, _, _, rf, _), dest_tiles in zip(groups, dests)]


def _layer(x, pos0, kv_bufs, conv_state, ssm_state, p):
    batch, seq, _ = x.shape
    n = batch * seq
    x2d = x.reshape(n, D_MODEL)
    prompt = kv_bufs is None
    q, k, v, z, xbc, g_a, g_s, dt_raw = _inproj(x2d, batch, seq, pos0, p["norm1_w"], p["w_in"], prompt)

    outs, lses, new_kv = [], [], []
    for gi, (window, dil) in enumerate(ATTN_GROUPS):
        if prompt:
            o, l = _attn_prompt(q[gi], k[gi], v[gi], dil)
            keep = min(window, seq) // dil
            tail = lambda t: t[:, :, seq // dil - keep:, :].transpose(0, 3, 2, 1).reshape(
                batch, HEADS_PER_GROUP, HEAD_DIM, keep * dil)
            nbuf = jnp.stack([tail(k[gi]), tail(v[gi])], axis=1).transpose(0, 4, 1, 2, 3)
        else:
            o, l, nbuf = _attn_step(q, k, v, kv_bufs[gi], batch, seq, gi, dil)
        outs.append(o)
        lses.append(l)
        new_kv.append(nbuf)

    if conv_state is None:
        conv_state = jnp.zeros((batch, 3, SSD_CONV_DIM), F32)
        ssm_state = jnp.zeros((batch, SSD_HEADS, HEAD_DIM, SSD_STATE), F32)
    y_ssd, new_conv, new_ssm = _ssd(xbc, z, dt_raw, conv_state, ssm_state, batch, seq, p["conv_w"], p["conv_b"],
                                    p["dt_bias"], p["A_log"], p["D_skip"], p["ssd_norm_w"])

    h, hn, ri, rf, counts = _merge(x2d, outs, lses, y_ssd, g_a, g_s, p["w_attn_br"], p["w_ssd_br"], p["b_gate"],
                                   p["w_out"], p["norm2_w"], p["w_router"], p["b_router"], prompt)
    return (h, hn, ri, rf, counts), new_kv, new_conv, new_ssm


def _pack_w_in(w_in):
    return w_in.T.astype(BF16)


def kernel(x_prompt, x_sample, cache_kv_w128, cache_kv_w512, cache_kv_w2048, state_conv, state_ssm, norm1_w, w_in, w_attn_br, w_ssd_br, b_gate, w_out, conv_w, conv_b, dt_bias, A_log, D_skip, ssd_norm_w, norm2_w, w_router_coarse, b_router_coarse, w_router_fine, b_router_fine, w_expert_gate, w_expert_up, w_expert_down, final_norm_w):
    depth = norm1_w.shape[0]
    assert depth == 1, "the final norm is fused into the layer's last kernel"
    l = 0
    rpad = LANES - N_EXPERTS - N_GROUPS_E
    p = dict(
        norm1_w=norm1_w[l], w_in=_pack_w_in(w_in[l]),
        w_attn_br=w_attn_br[l].astype(BF16), w_ssd_br=w_ssd_br[l].astype(BF16), b_gate=b_gate[l],
        w_out=w_out[l].astype(BF16), conv_w=conv_w[l], conv_b=conv_b[l], dt_bias=dt_bias[l], A_log=A_log[l],
        D_skip=D_skip[l], ssd_norm_w=ssd_norm_w[l], norm2_w=norm2_w[l],
        w_router=jnp.pad(jnp.concatenate([w_router_fine[l], w_router_coarse[l]], axis=1),
                         ((0, 0), (0, rpad))).astype(BF16),
        b_router=jnp.pad(jnp.concatenate([b_router_fine[l], b_router_coarse[l]]), (0, rpad)).reshape(1, LANES),
        w_eg=w_expert_gate[l], w_eu=w_expert_up[l], w_ed=w_expert_down[l], final_norm_w=final_norm_w,
    )
    g_p, kv_p, c_p, st_p = _layer(x_prompt, 0, None, None, None, p)
    bufs = (cache_kv_w128[l], cache_kv_w512[l], cache_kv_w2048[l])
    g_s, kv_s, c_s, st_s = _layer(x_sample, PAST_LEN, bufs, state_conv[l], state_ssm[l], p)
    y_p, y_s = _moe_and_final([g_p, g_s], p["w_eg"], p["w_eu"], p["w_ed"], p["final_norm_w"])
    y_p = y_p.reshape(x_prompt.shape)
    y_s = y_s.reshape(x_sample.shape)
    lead = lambda t: t[None]
    return (y_p, y_s, lead(kv_p[0]), lead(kv_p[1]), lead(kv_p[2]), lead(c_p), lead(st_p),
            lead(kv_s[0]), lead(kv_s[1]), lead(kv_s[2]), lead(c_s), lead(st_s))
```
